```python
import math
import jax
import jax.numpy as jnp
from jax import lax
import numpy as np

D_MODEL = 1024
BATCH = 16
SEQ = 4096
DEPTH = 1
DEC_BATCH = 128
DEC_SEQ = 4
PAST_LEN = 8192
PAGE_SIZE = 128

EPS = 1e-6
NEG_INF = -1e30
FORCE_SCORE = 1e4

GDN_HEADS = 4
GDN_DK = 128
GDN_DV = 128
GDN_CONV = 4
GDN_CHUNK = 64
NSA_HEADS = 8
NSA_KV_HEADS = 2
NSA_HD = 64
NSA_GROUP = NSA_HEADS // NSA_KV_HEADS
CMP_BLOCK = 32
SEL_BLOCK = 64
SEL_TOPK = 16
WINDOW = 512
SEL_QBLOCK = 64
WIN_QBLOCK = 128
N_KV_STREAMS = 4
REL_BUCKETS = 32
REL_MAX_DIST = 2048
PEER_HEADS = 8
PEER_NKEYS = 128
PEER_EXPERTS = PEER_NKEYS * PEER_NKEYS
PEER_QDIM = 128
PEER_TOPK = 16
PEER_TBLOCK = 256

GDN_QK = GDN_HEADS * GDN_DK
GDN_V = GDN_HEADS * GDN_DV
GDN_CONV_CH = 2 * GDN_QK + GDN_V
NSA_Q = NSA_HEADS * NSA_HD
NSA_KV = NSA_KV_HEADS * NSA_HD
MIX_WIDTH = GDN_V + NSA_Q
OFF_CONV = 0
OFF_Z = OFF_CONV + GDN_CONV_CH
OFF_B = OFF_Z + GDN_V
OFF_A = OFF_B + GDN_HEADS
OFF_NQ = OFF_A + GDN_HEADS
OFF_NKV = OFF_NQ + NSA_Q
OFF_NWIN = OFF_NKV + N_KV_STREAMS * NSA_KV
OFF_NG = OFF_NWIN + 2 * NSA_KV
IN_DIM = OFF_NG + 3 * NSA_HEADS

kernel_name = 'hybrid_gdn_nsa_peer_step'


def rmsnorm(x, g):
    xf = x.astype(jnp.float32)
    y = xf * lax.rsqrt(jnp.mean(xf * xf, axis=-1, keepdims=True) + EPS)
    return (y * g.astype(jnp.float32)).astype(x.dtype)


def l2norm(x):
    xf = x.astype(jnp.float32)
    return (xf * lax.rsqrt(jnp.sum(xf * xf, axis=-1, keepdims=True) + EPS)).astype(x.dtype)


def masked_softmax(logits, valid):
    l = jnp.where(valid, logits.astype(jnp.float32), NEG_INF)
    m = jnp.max(l, axis=-1, keepdims=True)
    p = jnp.where(valid, jnp.exp(l - m), 0.0)
    return p / jnp.maximum(jnp.sum(p, axis=-1, keepdims=True), 1e-30)


def rel_bucket(dist):
    exact = REL_BUCKETS // 2
    d = jnp.maximum(dist, 0)
    df = jnp.maximum(d, 1).astype(jnp.float32)
    large = exact + (jnp.log(df / exact) / math.log(REL_MAX_DIST / exact)
                     * (REL_BUCKETS - exact)).astype(jnp.int32)
    large = jnp.minimum(large, REL_BUCKETS - 1)
    return jnp.where(d < exact, d, large)


def map_blocks(fn, xs, block, axis):
    n = xs[0].shape[axis]
    if n <= block:
        return fn(*xs)
    nb = -(-n // block)
    pad = nb * block - n

    def split(a):
        a = jnp.pad(a, [(0, pad) if i == axis else (0, 0) for i in range(a.ndim)])
        a = a.reshape(a.shape[:axis] + (nb, block) + a.shape[axis + 1:])
        return jnp.moveaxis(a, axis, 0)

    out = lax.map(lambda args: fn(*args), tuple(split(a) for a in xs))
    out = jnp.moveaxis(out, 0, axis)
    out = out.reshape(out.shape[:axis] + (nb * block,) + out.shape[axis + 2:])
    return lax.slice_in_dim(out, 0, n, axis=axis)


def causal_conv(xc, buf, w):
    L = xc.shape[1]
    xp = jnp.concatenate([buf.astype(xc.dtype), xc], axis=1)
    y = xp[:, 0:L] * w[0]
    for j in range(1, GDN_CONV):
        y = y + xp[:, j:j + L] * w[j]
    return jax.nn.silu(y), xp[:, L:]


def gated_delta_chunked(q, k, v, g, beta, s0):
    B, L, H, DK = q.shape
    C = GDN_CHUNK
    n = -(-L // C)
    pad = n * C - L

    def chunks(a):
        a = jnp.pad(a, [(0, 0), (0, pad)] + [(0, 0)] * (a.ndim - 2))
        a = a.reshape((B, n, C, H) + a.shape[3:])
        return jnp.moveaxis(a, 3, 2)

    qc, kc, vc, gc, bc = (chunks(a) for a in (q, k, v, g, beta))
    gc = jnp.cumsum(gc, axis=-1)
    causal = jnp.tril(jnp.ones((C, C), bool))
    strict = jnp.tril(jnp.ones((C, C), bool), -1)
    diff = gc[..., :, None] - gc[..., None, :]
    decay = jnp.where(causal, jnp.exp(jnp.where(causal, diff, 0.0)), 0.0)
    kb = kc * bc[..., None]
    lmat = jnp.where(strict, jnp.einsum('bnhid,bnhjd->bnhij', kb, kc) * decay, 0.0)
    eye = jnp.eye(C, dtype=jnp.float32)
    tmat = lax.linalg.triangular_solve(eye + lmat, jnp.broadcast_to(eye, lmat.shape),
                                       left_side=True, lower=True)
    u = tmat @ (vc * bc[..., None])
    w = tmat @ (kb * jnp.exp(gc)[..., None])
    a_intra = jnp.einsum('bnhid,bnhjd->bnhij', qc, kc) * decay
    q_dec = qc * jnp.exp(gc)[..., None]
    k_dec = kc * jnp.exp(gc[..., -1:] - gc)[..., None]
    g_last = jnp.exp(gc[..., -1])

    def step(s, xs):
        u_i, w_i, a_i, qd_i, kd_i, gl_i = xs
        v_new = u_i - w_i @ s
        o_i = qd_i @ s + a_i @ v_new
        s = s * gl_i[..., None, None] + jnp.swapaxes(kd_i, -1, -2) @ v_new
        return s, o_i

    xs = tuple(jnp.moveaxis(a, 1, 0) for a in (u, w, a_intra, q_dec, k_dec, g_last))
    s_fin, o = lax.scan(step, s0, xs)
    o = jnp.moveaxis(jnp.moveaxis(o, 0, 1), 2, 3).reshape(B, n * C, H, v.shape[-1])[:, :L]
    return o, s_fin


def nsa_compressed(q, qpos, k_all, v_all, rel_tab):
    B, Lp, G, D = k_all.shape
    Q = q.shape[1]
    nc = Lp // CMP_BLOCK
    kc = k_all.reshape(B, nc, CMP_BLOCK, G, D).mean(axis=2)
    vc = v_all.reshape(B, nc, CMP_BLOCK, G, D).mean(axis=2)
    dist = qpos[:, None] - (jnp.arange(nc) * CMP_BLOCK + CMP_BLOCK - 1)[None, :]
    bias = jnp.moveaxis(rel_tab[rel_bucket(dist)].reshape(Q, nc, G, NSA_GROUP), 1, 3)
    logits = jnp.einsum('bqgpd,bngd->bqgpn', q, kc).astype(jnp.float32) * (NSA_HD ** -0.5) + bias
    p = masked_softmax(logits, (dist >= 0)[:, None, None, :])
    o = jnp.einsum('bqgpn,bngd->bqgpd', p.astype(vc.dtype), vc)
    return o, p.sum(axis=3)


def select_blocks(imp, qpos):
    B, Q, G, nc = imp.shape
    nsb = nc * CMP_BLOCK // SEL_BLOCK
    score = imp.reshape(B, Q, G, nsb, SEL_BLOCK // CMP_BLOCK).sum(axis=-1)
    j = jnp.arange(nsb)[None, :]
    cur = (qpos // SEL_BLOCK)[:, None]
    forced = (j == 0) | (j == cur) | (j == cur - 1)
    future = j * SEL_BLOCK > qpos[:, None]
    score = jnp.where(future[:, None, :], -1.0, jnp.where(forced[:, None, :], FORCE_SCORE, score))
    _, idx = lax.top_k(score, min(SEL_TOPK, nsb))
    return idx


def nsa_selected(q, qpos, idx, k_all, v_all, rel_tab):
    B, Lp, G, D = k_all.shape
    Qb = q.shape[1]
    K = idx.shape[-1]
    nsb = Lp // SEL_BLOCK
    kb = k_all.reshape(B, nsb, SEL_BLOCK, G, D)
    vb = v_all.reshape(B, nsb, SEL_BLOCK, G, D)
    bi = jnp.arange(B)[:, None, None, None]
    gi = jnp.arange(G)[None, None, :, None]
    ksel = kb[bi, idx, :, gi, :]
    vsel = vb[bi, idx, :, gi, :]
    kpos = idx[..., None] * SEL_BLOCK + jnp.arange(SEL_BLOCK)
    dist = qpos[None, :, None, None, None] - kpos
    tab = rel_tab.reshape(REL_BUCKETS, G, NSA_GROUP)
    bias = tab[rel_bucket(dist), jnp.arange(G)[:, None, None]]
    bias = jnp.moveaxis(bias, -1, 3).reshape(B, Qb, G, NSA_GROUP, K * SEL_BLOCK)
    logits = jnp.einsum('bqgpd,bqgksd->bqgpks', q, ksel).astype(jnp.float32)
    logits = logits.reshape(B, Qb, G, NSA_GROUP, K * SEL_BLOCK) * (NSA_HD ** -0.5) + bias
    p = masked_softmax(logits, (dist >= 0).reshape(B, Qb, G, 1, K * SEL_BLOCK))
    return jnp.einsum('bqgpn,bqgnd->bqgpd', p.astype(vsel.dtype),
                      vsel.reshape(B, Qb, G, K * SEL_BLOCK, D))


def nsa_window(q, li, pos0, wb, k_seq, v_seq, rel_tab):
    Qb = q.shape[1]
    Lk = k_seq.shape[1]
    kw = wb + Qb
    start = jnp.minimum(li[0], Lk - kw)
    ks = lax.dynamic_slice_in_dim(k_seq, start, kw, axis=1)
    vs = lax.dynamic_slice_in_dim(v_seq, start, kw, axis=1)
    qpos = pos0 + li
    kpos = pos0 - wb + start + jnp.arange(kw)
    dist = qpos[:, None] - kpos[None, :]
    valid = (dist >= 0) & (dist <= WINDOW) & (kpos >= 0)[None, :]
    G = k_seq.shape[2]
    bias = jnp.moveaxis(rel_tab[rel_bucket(dist)].reshape(Qb, kw, G, NSA_GROUP), 1, 3)
    logits = jnp.einsum('bqgpd,bkgd->bqgpk', q, ks).astype(jnp.float32) * (NSA_HD ** -0.5) + bias
    p = masked_softmax(logits, valid[:, None, None, :])
    return jnp.einsum('bqgpk,bkgd->bqgpd', p.astype(vs.dtype), vs)


def peer(xn, w_q, sub_keys, u, v):
    B, L, dm = xn.shape
    xt = xn.reshape(B * L, dm)
    t = xt.shape[0]
    q = (xt @ w_q).reshape(t, PEER_HEADS, 2, PEER_QDIM // 2)
    s = jnp.einsum('thcd,hcnd->thcn', q, sub_keys).astype(jnp.float32)
    s1, i1 = lax.top_k(s[:, :, 0], PEER_TOPK)
    s2, i2 = lax.top_k(s[:, :, 1], PEER_TOPK)
    cand = (s1[..., :, None] + s2[..., None, :]).reshape(t, PEER_HEADS, PEER_TOPK * PEER_TOPK)
    cand_id = (i1[..., :, None] * PEER_NKEYS + i2[..., None, :]).reshape(t, PEER_HEADS, PEER_TOPK * PEER_TOPK)
    top, pos = lax.top_k(cand, PEER_TOPK)
    eid = jnp.take_along_axis(cand_id, pos, axis=-1)
    gate = jax.nn.softmax(top, axis=-1)

    def experts(xb, eb, gb):
        act = jax.nn.gelu(jnp.einsum('td,thkd->thk', xb, u[eb]).astype(jnp.float32), approximate=False)
        return jnp.einsum('thk,thkd->td', (gb * act).astype(xb.dtype), v[eb])

    out = map_blocks(experts, (xt, eid, gate), PEER_TBLOCK, 0)
    return out.reshape(B, L, dm)


def decoder_layer(x, pos0, conv_buf, s0, kv_past, win_buf, rel_bias,
                  norm1, w_in, gdn_conv_w, gdn_a_log, gdn_dt_bias, gdn_norm,
                  nsa_q_norm, nsa_k_norm, w_o, norm2, peer_wq, peer_subkeys, peer_u, peer_v):
    B, L, _ = x.shape
    dt = x.dtype
    G, P, D = NSA_KV_HEADS, NSA_GROUP, NSA_HD
    xn = rmsnorm(x, norm1)
    h = xn @ w_in

    conv_out, new_conv = causal_conv(h[..., OFF_CONV:OFF_Z], conv_buf, gdn_conv_w)
    qg = l2norm(conv_out[..., :GDN_QK].reshape(B, L, GDN_HEADS, GDN_DK)) * (GDN_DK ** -0.5)
    kg = l2norm(conv_out[..., GDN_QK:2 * GDN_QK].reshape(B, L, GDN_HEADS, GDN_DK))
    vg = conv_out[..., 2 * GDN_QK:].reshape(B, L, GDN_HEADS, GDN_DV)
    beta = jax.nn.sigmoid(h[..., OFF_B:OFF_A].astype(jnp.float32))
    gdec = -jnp.exp(gdn_a_log.astype(jnp.float32)) * jax.nn.softplus(
        h[..., OFF_A:OFF_NQ].astype(jnp.float32) + gdn_dt_bias.astype(jnp.float32))
    o_g, s_new = gated_delta_chunked(qg.astype(jnp.float32), kg.astype(jnp.float32),
                                     vg.astype(jnp.float32), gdec, beta, s0.astype(jnp.float32))
    z = h[..., OFF_Z:OFF_B].reshape(B, L, GDN_HEADS, GDN_DV).astype(jnp.float32)
    o_g = (rmsnorm(o_g, gdn_norm) * jax.nn.silu(z)).reshape(B, L, GDN_V).astype(dt)

    li = jnp.arange(L, dtype=jnp.int32)[None]
    qpos = pos0 + li[0]
    qn = rmsnorm(h[..., OFF_NQ:OFF_NKV].reshape(B, L, NSA_HEADS, D), nsa_q_norm).reshape(B, L, G, P, D)
    raw = h[..., OFF_NKV:OFF_NWIN].reshape(B, L, N_KV_STREAMS, G, D)
    kv_new = jnp.stack([rmsnorm(raw[:, :, 0], nsa_k_norm[0]), raw[:, :, 1],
                        rmsnorm(raw[:, :, 2], nsa_k_norm[1]), raw[:, :, 3]], axis=2)
    raw_w = h[..., OFF_NWIN:OFF_NG].reshape(B, L, 2, G, D)
    win_new = jnp.stack([rmsnorm(raw_w[:, :, 0], nsa_k_norm[2]), raw_w[:, :, 1]], axis=2)
    gates = jax.nn.sigmoid(h[..., OFF_NG:].reshape(B, L, G, P, 3).astype(jnp.float32))

    lt = kv_past.shape[1] + L
    pad = -(-lt // SEL_BLOCK) * SEL_BLOCK - lt
    kv_all = jnp.concatenate([kv_past.astype(dt), kv_new,
                              jnp.zeros((B, pad, N_KV_STREAMS, G, D), dt)], axis=1)
    o_cmp, imp = nsa_compressed(qn, qpos, kv_all[:, :, 0], kv_all[:, :, 1], rel_bias)
    idx = select_blocks(imp, qpos)
    ks_all, vs_all = kv_all[:, :, 2], kv_all[:, :, 3]
    o_slc = map_blocks(lambda qb, ib, lb: nsa_selected(qb, pos0 + lb[0], ib, ks_all, vs_all, rel_bias),
                       (qn, idx, li), SEL_QBLOCK, 1)
    wseq = jnp.concatenate([win_buf.astype(dt), win_new], axis=1)
    wb = win_buf.shape[1]
    wk, wv = wseq[:, :, 0], wseq[:, :, 1]
    o_win = map_blocks(lambda qb, lb: nsa_window(qb, lb[0], pos0, wb, wk, wv, rel_bias),
                       (qn, li), WIN_QBLOCK, 1)
    new_win = wseq[:, wseq.shape[1] - wb:]
    o_n = (gates[..., 0:1] * o_cmp + gates[..., 1:2] * o_slc + gates[..., 2:3] * o_win)
    o_n = o_n.reshape(B, L, NSA_Q).astype(dt)

    hres = x + jnp.concatenate([o_g, o_n], axis=-1) @ w_o
    y = hres + peer(rmsnorm(hres, norm2), peer_wq, peer_subkeys, peer_u, peer_v)
    return y, kv_new, new_win, new_conv, s_new.astype(dt)


def setup_inputs(seed: int = 0) -> dict:
    key = jax.random.key(seed)
    ks = jax.random.split(key, 24)
    n_pages = PAST_LEN // PAGE_SIZE
    n_used = DEC_BATCH * n_pages
    n_pool = n_used + n_used // 4
    win_rows = min(WINDOW, PAST_LEN)

    def normal(k, shape, scale):
        return jax.random.normal(k, shape, jnp.float32) * scale

    def gain(k, shape):
        return 1.0 + 0.02 * jax.random.normal(k, shape, jnp.float32)

    page_table = jax.random.permutation(ks[3], n_pool)[:n_used].reshape(DEC_BATCH, n_pages).astype(jnp.int32)
    dt_init = jnp.exp(jax.random.uniform(ks[9], (DEPTH, GDN_HEADS), jnp.float32,
                                         math.log(1e-3), math.log(1e-1)))
    return {
        'x_prompt': normal(ks[0], (BATCH, SEQ, D_MODEL), 1.0),
        'x_sample': normal(ks[1], (DEC_BATCH, DEC_SEQ, D_MODEL), 1.0),
        'cache_nsa_kv': normal(ks[2], (DEPTH, n_pool, PAGE_SIZE, N_KV_STREAMS, NSA_KV_HEADS, NSA_HD), 1.0),
        'page_table': page_table,
        'state_win_kv': normal(ks[4], (DEPTH, DEC_BATCH, win_rows, 2, NSA_KV_HEADS, NSA_HD), 1.0),
        'state_conv': normal(ks[5], (DEPTH, DEC_BATCH, GDN_CONV - 1, GDN_CONV_CH), 1.0),
        'state_gdn': normal(ks[6], (DEPTH, DEC_BATCH, GDN_HEADS, GDN_DK, GDN_DV), 0.5),
        'norm1': gain(ks[7], (DEPTH, D_MODEL)),
        'w_in': normal(ks[10], (DEPTH, D_MODEL, IN_DIM), D_MODEL ** -0.5),
        'gdn_conv_w': normal(ks[11], (DEPTH, GDN_CONV, GDN_CONV_CH), 0.5),
        'gdn_a_log': jnp.log(jax.random.uniform(ks[8], (DEPTH, GDN_HEADS), jnp.float32, 1.0, 16.0)),
        'gdn_dt_bias': jnp.log(jnp.expm1(dt_init)),
        'gdn_norm': gain(ks[12], (DEPTH, GDN_DV)),
        'nsa_q_norm': gain(ks[13], (DEPTH, NSA_HD)),
        'nsa_k_norm': gain(ks[14], (DEPTH, 3, NSA_HD)),
        'rel_bias': normal(ks[15], (REL_BUCKETS, NSA_HEADS), 0.5),
        'w_o': normal(ks[16], (DEPTH, MIX_WIDTH, D_MODEL), MIX_WIDTH ** -0.5),
        'norm2': gain(ks[17], (DEPTH, D_MODEL)),
        'peer_wq': normal(ks[18], (DEPTH, D_MODEL, PEER_HEADS * PEER_QDIM), D_MODEL ** -0.5),
        'peer_subkeys': normal(ks[19], (DEPTH, PEER_HEADS, 2, PEER_NKEYS, PEER_QDIM // 2), (PEER_QDIM // 2) ** -0.5),
        'peer_u': normal(ks[20], (DEPTH, PEER_EXPERTS, D_MODEL), D_MODEL ** -0.5),
        'peer_v': normal(ks[21], (DEPTH, PEER_EXPERTS, D_MODEL), 0.5),
    }


def reference(x_prompt, x_sample, cache_nsa_kv, page_table, state_win_kv, state_conv, state_gdn,
              norm1, w_in, gdn_conv_w, gdn_a_log, gdn_dt_bias, gdn_norm, nsa_q_norm, nsa_k_norm,
              rel_bias, w_o, norm2, peer_wq, peer_subkeys, peer_u, peer_v):
    b = x_prompt.shape[0]
    db = x_sample.shape[0]
    n_pages = page_table.shape[1]
    past_len = n_pages * cache_nsa_kv.shape[2]
    dt = x_prompt.dtype
    kv_tail = (N_KV_STREAMS, NSA_KV_HEADS, NSA_HD)
    y_prompt, y_sample = x_prompt, x_sample
    kvp_l, winp_l, convp_l, gdnp_l = [], [], [], []
    kvs_l, wins_l, convs_l, gdns_l = [], [], [], []
    for l in range(DEPTH):
        lw = (norm1[l], w_in[l], gdn_conv_w[l], gdn_a_log[l], gdn_dt_bias[l], gdn_norm[l],
              nsa_q_norm[l], nsa_k_norm[l], w_o[l], norm2[l], peer_wq[l], peer_subkeys[l],
              peer_u[l], peer_v[l])
        y_prompt, kvp, winp, convp, gdnp = decoder_layer(
            y_prompt, 0,
            jnp.zeros((b, GDN_CONV - 1, GDN_CONV_CH), dt),
            jnp.zeros((b, GDN_HEADS, GDN_DK, GDN_DV), dt),
            jnp.zeros((b, 0) + kv_tail, dt),
            jnp.zeros((b, WINDOW, 2, NSA_KV_HEADS, NSA_HD), dt),
            rel_bias, *lw)
        kv_past = cache_nsa_kv[l][page_table].reshape((db, past_len) + kv_tail)
        y_sample, kvs, wins, convs, gdns = decoder_layer(
            y_sample, past_len, state_conv[l], state_gdn[l], kv_past, state_win_kv[l],
            rel_bias, *lw)
        kvp_l.append(kvp); winp_l.append(winp); convp_l.append(convp); gdnp_l.append(gdnp)
        kvs_l.append(kvs); wins_l.append(wins); convs_l.append(convs); gdns_l.append(gdns)
    kv_rows_prompt = jnp.stack(kvp_l)
    win_prompt = jnp.stack(winp_l)
    conv_prompt = jnp.stack(convp_l)
    gdn_prompt = jnp.stack(gdnp_l)
    kv_rows_sample = jnp.stack(kvs_l)
    win_sample = jnp.stack(wins_l)
    conv_sample = jnp.stack(convs_l)
    gdn_sample = jnp.stack(gdns_l)
    return (y_prompt, y_sample, kv_rows_prompt, win_prompt, conv_prompt, gdn_prompt,
            kv_rows_sample, win_sample, conv_sample, gdn_sample)
```

```python
import functools
import math

import numpy as np
import jax
import jax.numpy as jnp
from jax import lax
from jax.experimental import pallas as pl
from jax.experimental.pallas import tpu as pltpu

F32 = jnp.float32
BF16 = jnp.bfloat16
I32 = jnp.int32

D_MODEL = 1024
EPS = 1e-6
NEG_INF = -1e30
FORCE_SCORE = 1e4
GDN_HEADS = 4
GDN_DK = 128
GDN_CHUNK = 64
GDN_CH = 1536
NSA_HEADS = 8
NSA_GROUPS = 2
NSA_P = 4
NSA_HD = 64
CMP_BLOCK = 32
SEL_BLOCK = 64
SEL_TOPK = 16
WINDOW = 512
REL_BUCKETS = 32
PAGE = 128
PAGES_PER_STEP = 4
PEER_HEADS = 8
PEER_NKEYS = 128
PEER_TOPK = 16
PEER_HALF = 64
PEER_SEL = PEER_HEADS * PEER_TOPK
PEER_NCAND = 80
OFF_B = 2048
OFF_NQ = 2056
OFF_NG = 3336
LANE = 128
VMEM_LIMIT = 56 * 1024 * 1024
REMOVED = -3.0e38
SQRT_HALF = 0.7071067811865476


def _cparams(sem):
    return pltpu.CompilerParams(dimension_semantics=sem, vmem_limit_bytes=VMEM_LIMIT)


def _dot(a, b):
    return jnp.dot(a, b, preferred_element_type=F32)


def _dot_nt(a, b):
    return lax.dot_general(a, b, (((1,), (1,)), ((), ())), preferred_element_type=F32)


def _dot_tn(a, b):
    return lax.dot_general(a, b, (((0,), (0,)), ((), ())), preferred_element_type=F32)


def _split2(x):
    hi = x.astype(BF16)
    lo = (x - hi.astype(F32)).astype(BF16)
    return hi, lo


def _split3(x):
    hi = x.astype(BF16)
    r = x - hi.astype(F32)
    mid = r.astype(BF16)
    lo = (r - mid.astype(F32)).astype(BF16)
    return hi, mid, lo


def _dot_exact_lhs(m01, x):
    hi, mid, lo = _split3(x)
    return _dot(m01, hi) + (_dot(m01, mid) + _dot(m01, lo))


def _mm3(a, b):
    ah, al = _split2(a)
    bh, bl = _split2(b)
    return _dot(ah, bh) + (_dot(ah, bl) + _dot(al, bh))


def _iota_f(shape, axis):
    return lax.broadcasted_iota(I32, shape, axis).astype(F32)


def _rel_bucket(dist):
    d = jnp.maximum(dist, 0)
    df = jnp.maximum(d, 1).astype(F32)
    large = 16 + (jnp.log(df / 16.0) / math.log(128.0) * 16.0).astype(I32)
    large = jnp.minimum(large, REL_BUCKETS - 1)
    return jnp.where(d < 16, d, large)


def _bias_from_bucket(bucket, tab_ref, head, lo=0, hi=REL_BUCKETS - 1):
    if isinstance(lo, int) and isinstance(hi, int):
        b = jnp.zeros(bucket.shape, F32)
        for k in range(lo, hi + 1):
            b = jnp.where(bucket == k, tab_ref[k, head], b)
        return b

    def body(k, b):
        return jnp.where(bucket == k, tab_ref[k, head], b)

    return lax.fori_loop(lo, hi + 1, body, jnp.zeros(bucket.shape, F32))


def _np_bucket(d):
    d = max(int(d), 0)
    if d < 16:
        return d
    return min(16 + int(math.log(max(d, 1) / 16.0) / math.log(128.0) * 16.0), REL_BUCKETS - 1)


def _topk_mask(s, k, axis):
    n = s.shape[axis]
    ids = _iota_f(s.shape, axis)
    sel = jnp.zeros(s.shape, F32)
    for _ in range(k):
        m = jnp.max(s, axis=axis, keepdims=True)
        idx = jnp.min(jnp.where(s == m, ids, float(n)), axis=axis, keepdims=True)
        hit = ids == idx
        sel = jnp.where(hit, 1.0, sel)
        s = jnp.where(hit, REMOVED, s)
    return sel


def _seg_rmsnorm(v, gain, seg):
    sq = v * v
    hi, lo = _split2(sq)
    ssum = _dot(hi, seg) + _dot(lo, seg)
    return v * lax.rsqrt(ssum * (1.0 / NSA_HD) + EPS) * gain


def _inproj_kernel(x_ref, g1_ref, wm_ref, ws_ref, seg_ref, qg_ref, kg_ref,
                   hc_ref, z_ref, sm_ref, q_ref, kv_ref, win_ref):
    x = x_ref[...]
    ms = jnp.mean(x * x, axis=-1, keepdims=True)
    xn = (x * lax.rsqrt(ms + EPS) * g1_ref[...]).astype(BF16)
    h = _dot(xn, wm_ref[...])
    sm_ref[...] = _dot(xn, ws_ref[...])
    hc_ref[...] = h[:, :GDN_CH]
    z_ref[...] = h[:, GDN_CH:2048]
    seg = seg_ref[...]
    qg = qg_ref[...]
    kg = kg_ref[...]
    for i in range(4):
        q_ref[:, i * LANE:(i + 1) * LANE] = _seg_rmsnorm(h[:, 2048 + i * LANE:2048 + (i + 1) * LANE], qg, seg)
    kv_ref[:, 0:128] = _seg_rmsnorm(h[:, 2560:2688], kg[0:1], seg)
    kv_ref[:, 128:256] = h[:, 2688:2816]
    kv_ref[:, 256:384] = _seg_rmsnorm(h[:, 2816:2944], kg[1:2], seg)
    kv_ref[:, 384:512] = h[:, 2944:3072]
    win_ref[:, 0:128] = _seg_rmsnorm(h[:, 3072:3200], kg[2:3], seg)
    win_ref[:, 128:256] = h[:, 3200:3328]


def _in_proj(x, norm1, wm, ws, seg, qg, kg, tm):
    t = x.shape[0]
    assert t % tm == 0
    row = lambda w: pl.BlockSpec((tm, w), lambda i: (i, 0))
    full = lambda a: pl.BlockSpec(a.shape, lambda i: (0,) * a.ndim)
    widths = (GDN_CH, 512, LANE, 512, 512, 256)
    return pl.pallas_call(
        _inproj_kernel,
        grid=(t // tm,),
        in_specs=[row(D_MODEL), full(norm1), full(wm), full(ws), full(seg), full(qg), full(kg)],
        out_specs=[row(w) for w in widths],
        out_shape=[jax.ShapeDtypeStruct((t, w), F32) for w in widths],
        compiler_params=_cparams(("parallel",)),
        name="in_proj",
    )(x, norm1, wm, ws, seg, qg, kg)


def _inv_unit_lower(lmat, c):
    ri = lax.broadcasted_iota(I32, (c, c), 0)
    ci = lax.broadcasted_iota(I32, (c, c), 1)
    eye = jnp.where(ri == ci, 1.0, 0.0).astype(F32)
    n = -lmat
    p = eye + n
    m = _mm3(n, n)
    span = 2
    while True:
        p = p + _mm3(p, m)
        span *= 2
        if span >= c:
            break
        m = _mm3(m, m)
    return p


def _gdn_kernel(hc_ref, z_ref, sm_ref, cw_ref, al_ref, dtb_ref, gn_ref, conv0_ref, s0_ref,
                og_ref, sfin_ref, xbuf, s_scr, *, c, l_valid, n_chunks):
    ci = pl.program_id(1)

    @pl.when(ci == 0)
    def _():
        xbuf[0:8, :] = jnp.zeros((8, GDN_CH), F32)
        xbuf[5:8, :] = conv0_ref[...]
        s_scr[...] = s0_ref[...]

    xbuf[8:8 + c, :] = hc_ref[...]
    w = cw_ref[...]
    y = (xbuf[5:5 + c, :] * w[0:1] + xbuf[6:6 + c, :] * w[1:2]
         + xbuf[7:7 + c, :] * w[2:3] + xbuf[8:8 + c, :] * w[3:4])
    tail = xbuf[5 + c:8 + c, :]
    xbuf[5:8, :] = tail
    y = y * jax.nn.sigmoid(y)

    sm = sm_ref[...]
    rowid = ci * c + lax.broadcasted_iota(I32, (c, 1), 0)
    rvalid = rowid < l_valid
    beta_all = jnp.where(rvalid, jax.nn.sigmoid(sm), 0.0)
    sp_in = sm + dtb_ref[...]
    softplus = jnp.maximum(sp_in, 0.0) + jnp.log1p(jnp.exp(-jnp.abs(sp_in)))
    g_all = jnp.where(rvalid, -jnp.exp(al_ref[...]) * softplus, 0.0)

    ri = lax.broadcasted_iota(I32, (c, c), 0)
    cj = lax.broadcasted_iota(I32, (c, c), 1)
    causal = ri >= cj
    strict = ri > cj
    tri = jnp.where(causal, 1.0, 0.0).astype(BF16)
    triu = jnp.where(ri <= cj, 1.0, 0.0).astype(BF16)
    g_hi, g_mid, g_lo = _split3(g_all)
    gcum = _dot(tri, g_hi) + (_dot(tri, g_mid) + _dot(tri, g_lo))
    gcum_t = _dot_tn(g_hi, triu) + (_dot_tn(g_mid, triu) + _dot_tn(g_lo, triu))
    gn = gn_ref[...]

    for h in range(GDN_HEADS):
        qh = y[:, h * 128:(h + 1) * 128]
        kh = y[:, 512 + h * 128:512 + (h + 1) * 128]
        vh = y[:, 1024 + h * 128:1024 + (h + 1) * 128]
        qh = qh * lax.rsqrt(jnp.sum(qh * qh, axis=-1, keepdims=True) + EPS) * (GDN_DK ** -0.5)
        kh = kh * lax.rsqrt(jnp.sum(kh * kh, axis=-1, keepdims=True) + EPS)
        qh = jnp.where(rvalid, qh, 0.0)
        kh = jnp.where(rvalid, kh, 0.0)
        vh = jnp.where(rvalid, vh, 0.0)
        beta = beta_all[:, h:h + 1]
        gc = gcum[:, 4 + h:5 + h]
        gct = gcum_t[4 + h:5 + h, :]
        diff = gc - gct
        decay = jnp.where(causal, jnp.exp(jnp.where(causal, diff, 0.0)), 0.0)
        kb = kh * beta
        khb = kh.astype(BF16)
        lmat = jnp.where(strict, _dot_nt(kb.astype(BF16), khb) * decay, 0.0)
        tmat = _inv_unit_lower(lmat, c).astype(BF16)
        eg = jnp.exp(gc)
        u = _dot(tmat, (vh * beta).astype(BF16))
        wmat = _dot(tmat, (kb * eg).astype(BF16))
        a_intra = _dot_nt(qh.astype(BF16), khb) * decay
        gc_last = gc[c - 1:c, :]
        q_dec = qh * eg
        k_dec = kh * jnp.exp(gc_last - gc)
        g_last = jnp.exp(gc_last)
        s = s_scr[h]
        sb = s.astype(BF16)
        v_new = u - _dot(wmat.astype(BF16), sb)
        vnb = v_new.astype(BF16)
        o = _dot(q_dec.astype(BF16), sb) + _dot(a_intra.astype(BF16), vnb)
        s_scr[h] = s * g_last + _dot_tn(k_dec.astype(BF16), vnb)
        o = o * lax.rsqrt(jnp.mean(o * o, axis=-1, keepdims=True) + EPS) * gn
        zh = z_ref[:, h * 128:(h + 1) * 128]
        og_ref[:, h * 128:(h + 1) * 128] = o * (zh * jax.nn.sigmoid(zh))

    @pl.when(ci == n_chunks - 1)
    def _():
        sfin_ref[...] = s_scr[...]


def _gdn(hc, z, sm, conv_w, al_vec, dtb_vec, gnorm, conv0, s0, c, l_valid):
    b, lp, _ = hc.shape
    assert lp % c == 0
    n_chunks = lp // c
    full = lambda a: pl.BlockSpec(a.shape, lambda i, j: (0,) * a.ndim)
    seq = lambda w: pl.BlockSpec((None, c, w), lambda i, j: (i, j, 0))
    kern = functools.partial(_gdn_kernel, c=c, l_valid=l_valid, n_chunks=n_chunks)
    return pl.pallas_call(
        kern,
        grid=(b, n_chunks),
        in_specs=[seq(GDN_CH), seq(512), seq(LANE), full(conv_w), full(al_vec), full(dtb_vec), full(gnorm),
                  pl.BlockSpec((None, 3, GDN_CH), lambda i, j: (i, 0, 0)),
                  pl.BlockSpec((None, GDN_HEADS, 128, 128), lambda i, j: (i, 0, 0, 0))],
        out_specs=[seq(512), pl.BlockSpec((None, GDN_HEADS, 128, 128), lambda i, j: (i, 0, 0, 0))],
        out_shape=[jax.ShapeDtypeStruct((b, lp, 512), F32),
                   jax.ShapeDtypeStruct((b, GDN_HEADS, 128, 128), F32)],
        scratch_shapes=[pltpu.VMEM((c + 8, GDN_CH), F32), pltpu.VMEM((GDN_HEADS, 128, 128), F32)],
        compiler_params=_cparams(("parallel", "arbitrary")),
        name="gdn",
    )(hc, z, sm, conv_w, al_vec, dtb_vec, gnorm, conv0, s0)


def _head_q128(q, h):
    g = h // NSA_P
    piece = q[:, (h // 2) * LANE:(h // 2 + 1) * LANE]
    lane = lax.broadcasted_iota(I32, piece.shape, 1)
    keep = (lane >= NSA_HD) if h % 2 == 1 else (lane < NSA_HD)
    qm = jnp.where(keep, piece, 0.0)
    if h % 2 != g:
        qm = pltpu.roll(qm, NSA_HD, 1)
    return qm


def _select_scores(imp, qpos, nsb):
    score = imp[:, :nsb] + imp[:, nsb:]
    j = lax.broadcasted_iota(I32, score.shape, 1)
    cur = qpos // SEL_BLOCK
    forced = (j == 0) | (j == cur) | (j == cur - 1)
    future = j * SEL_BLOCK > qpos
    return jnp.where(future, -1.0, jnp.where(forced, FORCE_SCORE, score))


def _cmp_attention(tab_ref, q, kcv, qpos, ocmp_ref):
    nc = kcv.shape[0]
    nsb = nc // 2
    kc = kcv[:, 0:128].astype(BF16)
    vc = kcv[:, 128:256].astype(BF16)
    lane = lax.broadcasted_iota(I32, (1, nc), 1)
    blk = jnp.where(lane < nsb, 2 * lane, 2 * (lane - nsb) + 1)
    dist = qpos - (blk * CMP_BLOCK + CMP_BLOCK - 1)
    valid = dist >= 0
    bucket = _rel_bucket(dist)
    scores = []
    for g in range(NSA_GROUPS):
        imp = jnp.zeros((q.shape[0], nc), F32)
        for p in range(NSA_P):
            h = g * NSA_P + p
            qm = _head_q128(q, h).astype(BF16)
            logits = _dot_nt(qm, kc) * (NSA_HD ** -0.5) + _bias_from_bucket(bucket, tab_ref, h)
            l = jnp.where(valid, logits, NEG_INF)
            m = jnp.max(l, axis=-1, keepdims=True)
            pr = jnp.where(valid, jnp.exp(l - m), 0.0)
            pr = pr / jnp.maximum(jnp.sum(pr, axis=-1, keepdims=True), 1e-30)
            ocmp_ref[h] = _dot(pr.astype(BF16), vc)
            imp = imp + pr
        scores.append(_select_scores(imp, qpos, nsb))
    return scores


def _flash_update(h, qm, kt, vt, valid, bias, m_scr, l_scr, acc_scr):
    s = _dot_nt(qm, kt) * (NSA_HD ** -0.5) + bias
    l = jnp.where(valid, s, NEG_INF)
    m_old = m_scr[h]
    m_new = jnp.maximum(m_old, jnp.max(l, axis=-1, keepdims=True))
    pr = jnp.where(valid, jnp.exp(l - m_new), 0.0)
    alpha = jnp.exp(m_old - m_new)
    l_scr[h] = alpha * l_scr[h] + jnp.sum(pr, axis=-1, keepdims=True)
    acc_scr[h] = alpha * acc_scr[h] + _dot(pr.astype(BF16), vt)
    m_scr[h] = m_new


def _flash_init(m_scr, l_scr, acc_scr):
    m_scr[...] = jnp.full(m_scr.shape, NEG_INF, F32)
    l_scr[...] = jnp.zeros(l_scr.shape, F32)
    acc_scr[...] = jnp.zeros(acc_scr.shape, F32)


def _flash_finish(o_ref, l_scr, acc_scr):
    for h in range(NSA_HEADS):
        o_ref[h] = acc_scr[h] / jnp.maximum(l_scr[h], 1e-30)


def _cmp_prompt_kernel(tab_ref, q_ref, kv_ref, mavg_ref, ocmp_ref, sel_ref, kc_scr, *, tq, nsb):
    qi = pl.program_id(1)

    @pl.when(qi == 0)
    def _():
        kc_scr[...] = _dot_exact_lhs(mavg_ref[...], kv_ref[...]) * (1.0 / CMP_BLOCK)

    qpos = qi * tq + lax.broadcasted_iota(I32, (tq, 1), 0)
    scores = _cmp_attention(tab_ref, q_ref[...], kc_scr[...], qpos, ocmp_ref)
    st = jnp.concatenate(scores, axis=1).T
    k = min(SEL_TOPK, nsb)
    sel_t = jnp.concatenate([_topk_mask(st[:nsb], k, 0), _topk_mask(st[nsb:], k, 0)], axis=0)
    sel_ref[...] = sel_t.T


def _cmp_prompt(tab, qn, kvn, mavg, tq):
    b, l, _ = qn.shape
    nc = l // CMP_BLOCK
    assert l % tq == 0 and l % SEL_BLOCK == 0
    kern = functools.partial(_cmp_prompt_kernel, tq=tq, nsb=nc // 2)
    return pl.pallas_call(
        kern,
        grid=(b, l // tq),
        in_specs=[pl.BlockSpec(memory_space=pltpu.SMEM),
                  pl.BlockSpec((None, tq, 512), lambda i, j: (i, j, 0)),
                  pl.BlockSpec((None, l, 256), lambda i, j: (i, 0, 0)),
                  pl.BlockSpec(mavg.shape, lambda i, j: (0, 0))],
        out_specs=[pl.BlockSpec((None, NSA_HEADS, tq, LANE), lambda i, j: (i, 0, j, 0)),
                   pl.BlockSpec((None, tq, nc), lambda i, j: (i, j, 0))],
        out_shape=[jax.ShapeDtypeStruct((b, NSA_HEADS, l, LANE), F32),
                   jax.ShapeDtypeStruct((b, l, nc), F32)],
        scratch_shapes=[pltpu.VMEM((nc, 256), F32)],
        compiler_params=_cparams(("parallel", "arbitrary")),
        name="cmp_prompt",
    )(tab, qn, kvn, mavg)


def _flash_prompt_kernel(tab_ref, blo_ref, bhi_ref, q_ref, k_ref, v_ref, *rest, t, windowed):
    if windowed:
        o_ref, m_scr, l_scr, acc_scr = rest
    else:
        sel_ref, e_ref, o_ref, m_scr, l_scr, acc_scr = rest
        selb = sel_ref[...].astype(BF16)
    qi = pl.program_id(1)
    q = q_ref[...]
    qms = [_head_q128(q, h).astype(BF16) for h in range(NSA_HEADS)]
    qpos = qi * t + lax.broadcasted_iota(I32, (t, 1), 0)
    _flash_init(m_scr, l_scr, acc_scr)
    k_lo = jnp.maximum(qi - WINDOW // t, 0) if windowed else 0

    def body(ki, carry):
        k0 = pl.multiple_of(ki * t, t)
        kt = k_ref[pl.ds(k0, t), :].astype(BF16)
        vt = v_ref[pl.ds(k0, t), :].astype(BF16)
        kpos = k0 + lax.broadcasted_iota(I32, (1, t), 1)
        dist = qpos - kpos
        ok = dist >= 0
        if windowed:
            ok = ok & (dist <= WINDOW)
        bucket = _rel_bucket(dist)
        blo = blo_ref[qi - ki]
        bhi = bhi_ref[qi - ki]
        for g in range(NSA_GROUPS):
            valid = ok if windowed else ok & (_dot(selb, e_ref[g, ki]) > 0.5)
            for p in range(NSA_P):
                h = g * NSA_P + p
                bias = _bias_from_bucket(bucket, tab_ref, h, blo, bhi)
                _flash_update(h, qms[h], kt, vt, valid, bias, m_scr, l_scr, acc_scr)
        return carry

    lax.fori_loop(k_lo, qi + 1, body, 0)
    _flash_finish(o_ref, l_scr, acc_scr)


def _bucket_bounds(n_tiles, t):
    lo = np.zeros((n_tiles,), np.int32)
    hi = np.zeros((n_tiles,), np.int32)
    for d in range(n_tiles):
        lo[d] = max(_np_bucket(d * t - (t - 1)) - 1, 0)
        hi[d] = min(_np_bucket(d * t + (t - 1)) + 1, REL_BUCKETS - 1)
    return jnp.asarray(lo), jnp.asarray(hi)


def _flash_prompt(tab, qn, kv_arr, k_blk, v_blk, sel, emat, t, windowed):
    b, l, _ = qn.shape
    assert l % t == 0 and t == LANE
    blo, bhi = _bucket_bounds(l // t, t)
    kern = functools.partial(_flash_prompt_kernel, t=t, windowed=windowed)
    smem = pl.BlockSpec(memory_space=pltpu.SMEM)
    in_specs = [smem, smem, smem,
                pl.BlockSpec((None, t, 512), lambda i, j: (i, j, 0)),
                pl.BlockSpec((None, l, LANE), lambda i, j: (i, 0, k_blk)),
                pl.BlockSpec((None, l, LANE), lambda i, j: (i, 0, v_blk))]
    args = [tab, blo, bhi, qn, kv_arr, kv_arr]
    if not windowed:
        in_specs += [pl.BlockSpec((None, t, sel.shape[-1]), lambda i, j: (i, j, 0)),
                     pl.BlockSpec(emat.shape, lambda i, j: (0, 0, 0, 0))]
        args += [sel, emat]
    return pl.pallas_call(
        kern,
        grid=(b, l // t),
        in_specs=in_specs,
        out_specs=pl.BlockSpec((None, NSA_HEADS, t, LANE), lambda i, j: (i, 0, j, 0)),
        out_shape=jax.ShapeDtypeStruct((b, NSA_HEADS, l, LANE), F32),
        scratch_shapes=[pltpu.VMEM((NSA_HEADS, t, 1), F32), pltpu.VMEM((NSA_HEADS, t, 1), F32),
                        pltpu.VMEM((NSA_HEADS, t, LANE), F32)],
        compiler_params=_cparams(("parallel", "arbitrary")),
        name="win_prompt" if windowed else "slc_prompt",
    )(*args)


def _cmp_sample_kernel(pt_ref, tab_ref, p0_ref, p1_ref, p2_ref, p3_ref, q_ref, avg_ref,
                       ocmp_ref, sel_ref, kc_scr, *, n_steps, past_len):
    del pt_ref
    st = pl.program_id(1)
    half = n_steps * 2 * PAGES_PER_STEP
    pages = jnp.concatenate([p0_ref[...], p1_ref[...], p2_ref[...], p3_ref[...]], axis=0)
    means = _dot_exact_lhs(avg_ref[...], pages) * (1.0 / CMP_BLOCK)
    off = pl.multiple_of(st * 8, 8)
    kc_scr[pl.ds(off, 8), :] = means[0:8]
    kc_scr[pl.ds(half + off, 8), :] = means[8:16]

    @pl.when(st == n_steps - 1)
    def _():
        qpos = past_len + lax.broadcasted_iota(I32, (8, 1), 0)
        scores = _cmp_attention(tab_ref, q_ref[...], kc_scr[...], qpos, ocmp_ref)
        k = min(SEL_TOPK, half + 1) - 1
        for g in range(NSA_GROUPS):
            sel_ref[g] = _topk_mask(scores[g], k, 1)


def _page_spec(r, lane_blk):
    return pl.BlockSpec((None, PAGE, 256), lambda i, j, pt: (pt[i, j * PAGES_PER_STEP + r], 0, lane_blk))


def _cmp_sample(page_table, tab, cache, q8, avg, past_len):
    db, n_pages = page_table.shape
    assert n_pages % PAGES_PER_STEP == 0
    n_steps = n_pages // PAGES_PER_STEP
    nsb = past_len // SEL_BLOCK
    kern = functools.partial(_cmp_sample_kernel, n_steps=n_steps, past_len=past_len)
    grid_spec = pltpu.PrefetchScalarGridSpec(
        num_scalar_prefetch=1,
        grid=(db, n_steps),
        in_specs=[pl.BlockSpec(memory_space=pltpu.SMEM)]
        + [_page_spec(r, 0) for r in range(PAGES_PER_STEP)]
        + [pl.BlockSpec((None, 8, 512), lambda i, j, pt: (i, 0, 0)),
           pl.BlockSpec(avg.shape, lambda i, j, pt: (0, 0))],
        out_specs=[pl.BlockSpec((None, NSA_HEADS, 8, LANE), lambda i, j, pt: (i, 0, 0, 0)),
                   pl.BlockSpec((None, NSA_GROUPS, 8, nsb), lambda i, j, pt: (i, 0, 0, 0))],
        scratch_shapes=[pltpu.VMEM((2 * nsb, 256), F32)],
    )
    return pl.pallas_call(
        kern,
        grid_spec=grid_spec,
        out_shape=[jax.ShapeDtypeStruct((db, NSA_HEADS, 8, LANE), F32),
                   jax.ShapeDtypeStruct((db, NSA_GROUPS, 8, nsb), F32)],
        compiler_params=_cparams(("parallel", "arbitrary")),
        name="cmp_sample",
    )(page_table, tab, cache, cache, cache, cache, q8, avg)


def _slc_sample_kernel(pt_ref, tab_ref, p0_ref, p1_ref, p2_ref, p3_ref, q_ref, new_ref, sel_ref, e_ref,
                       o_ref, m_scr, l_scr, acc_scr, *, n_steps, past_len):
    del pt_ref
    st = pl.program_id(1)

    @pl.when(st == 0)
    def _():
        _flash_init(m_scr, l_scr, acc_scr)

    q = q_ref[...]
    qms = [_head_q128(q, h).astype(BF16) for h in range(NSA_HEADS)]
    qpos = past_len + lax.broadcasted_iota(I32, (8, 1), 0)
    selb = [sel_ref[g].astype(BF16) for g in range(NSA_GROUPS)]

    def tile(kv, k0, page):
        kt = kv[:, 0:128].astype(BF16)
        vt = kv[:, 128:256].astype(BF16)
        dist = qpos - (k0 + lax.broadcasted_iota(I32, (1, PAGE), 1))
        ok = dist >= 0
        bucket = _rel_bucket(dist)
        for g in range(NSA_GROUPS):
            valid = ok if page is None else ok & (_dot(selb[g], e_ref[page]) > 0.5)
            for p in range(NSA_P):
                h = g * NSA_P + p
                _flash_update(h, qms[h], kt, vt, valid, _bias_from_bucket(bucket, tab_ref, h),
                              m_scr, l_scr, acc_scr)

    for r, p_ref in enumerate((p0_ref, p1_ref, p2_ref, p3_ref)):
        page = st * PAGES_PER_STEP + r
        tile(p_ref[...], page * PAGE, page)

    @pl.when(st == n_steps - 1)
    def _():
        tile(new_ref[...], past_len, None)
        _flash_finish(o_ref, l_scr, acc_scr)


def _slc_sample(page_table, tab, cache, q8, new_kv, sel, emat, past_len):
    db, n_pages = page_table.shape
    n_steps = n_pages // PAGES_PER_STEP
    kern = functools.partial(_slc_sample_kernel, n_steps=n_steps, past_len=past_len)
    grid_spec = pltpu.PrefetchScalarGridSpec(
        num_scalar_prefetch=1,
        grid=(db, n_steps),
        in_specs=[pl.BlockSpec(memory_space=pltpu.SMEM)]
        + [_page_spec(r, 1) for r in range(PAGES_PER_STEP)]
        + [pl.BlockSpec((None, 8, 512), lambda i, j, pt: (i, 0, 0)),
           pl.BlockSpec((None, PAGE, 256), lambda i, j, pt: (i, 0, 1)),
           pl.BlockSpec((None,) + sel.shape[1:], lambda i, j, pt: (i, 0, 0, 0)),
           pl.BlockSpec(emat.shape, lambda i, j, pt: (0, 0, 0))],
        out_specs=pl.BlockSpec((None, NSA_HEADS, 8, LANE), lambda i, j, pt: (i, 0, 0, 0)),
        scratch_shapes=[pltpu.VMEM((NSA_HEADS, 8, 1), F32), pltpu.VMEM((NSA_HEADS, 8, 1), F32),
                        pltpu.VMEM((NSA_HEADS, 8, LANE), F32)],
    )
    return pl.pallas_call(
        kern,
        grid_spec=grid_spec,
        out_shape=jax.ShapeDtypeStruct((db, NSA_HEADS, 8, LANE), F32),
        compiler_params=_cparams(("parallel", "arbitrary")),
        name="slc_sample",
    )(page_table, tab, cache, cache, cache, cache, q8, new_kv, sel, emat)


def _win_sample_kernel(tab_ref, q_ref, w_ref, o_ref, *, past_len, wb, lq):
    q = q_ref[...]
    qpos = past_len + lax.broadcasted_iota(I32, (8, 1), 0)
    wseq = w_ref[...]
    n = wseq.shape[0]
    kw = wseq[:, 0:128].astype(BF16)
    vw = wseq[:, 128:256].astype(BF16)
    j = lax.broadcasted_iota(I32, (1, n), 1)
    kpos = past_len - wb + j
    dist = qpos - kpos
    valid = (dist >= 0) & (dist <= WINDOW) & (kpos >= 0) & (j < wb + lq)
    bucket = _rel_bucket(dist)
    for h in range(NSA_HEADS):
        qm = _head_q128(q, h).astype(BF16)
        s = _dot_nt(qm, kw) * (NSA_HD ** -0.5) + _bias_from_bucket(bucket, tab_ref, h)
        l = jnp.where(valid, s, NEG_INF)
        m = jnp.max(l, axis=-1, keepdims=True)
        pr = jnp.where(valid, jnp.exp(l - m), 0.0)
        pr = pr / jnp.maximum(jnp.sum(pr, axis=-1, keepdims=True), 1e-30)
        o_ref[h] = _dot(pr.astype(BF16), vw)


def _win_sample(tab, q8, wseq, past_len, wb, lq):
    db, n, _ = wseq.shape
    kern = functools.partial(_win_sample_kernel, past_len=past_len, wb=wb, lq=lq)
    return pl.pallas_call(
        kern,
        grid=(db,),
        in_specs=[pl.BlockSpec(memory_space=pltpu.SMEM),
                  pl.BlockSpec((None, 8, 512), lambda i: (i, 0, 0)),
                  pl.BlockSpec((None, n, 256), lambda i: (i, 0, 0))],
        out_specs=pl.BlockSpec((None, NSA_HEADS, 8, LANE), lambda i: (i, 0, 0, 0)),
        out_shape=jax.ShapeDtypeStruct((db, NSA_HEADS, 8, LANE), F32),
        compiler_params=_cparams(("parallel",)),
        name="win_sample",
    )(tab, q8, wseq)


def _outproj_kernel(x_ref, og_ref, oc_ref, os_ref, ow_ref, sm_ref, wg_ref, wn_ref, n2_ref,
                    hres_ref, xn_ref):
    gates = jax.nn.sigmoid(sm_ref[...])
    acc = x_ref[...] + _dot(og_ref[...].astype(BF16), wg_ref[...])
    for h in range(NSA_HEADS):
        c = 8 + 3 * h
        on = (gates[:, c:c + 1] * oc_ref[h] + gates[:, c + 1:c + 2] * os_ref[h]
              + gates[:, c + 2:c + 3] * ow_ref[h])
        acc = acc + _dot(on.astype(BF16), wn_ref[h])
    hres_ref[...] = acc
    ms = jnp.mean(acc * acc, axis=-1, keepdims=True)
    xn_ref[...] = acc * lax.rsqrt(ms + EPS) * n2_ref[...]


def _out_proj(x, og, ocmp, oslc, owin, sm, wg, wn, norm2, tm):
    b, l, _ = x.shape
    assert l % tm == 0
    seq = lambda w: pl.BlockSpec((None, tm, w), lambda i, j: (i, j, 0))
    heads = pl.BlockSpec((None, NSA_HEADS, tm, LANE), lambda i, j: (i, 0, j, 0))
    full = lambda a: pl.BlockSpec(a.shape, lambda i, j: (0,) * a.ndim)
    return pl.pallas_call(
        _outproj_kernel,
        grid=(b, l // tm),
        in_specs=[seq(D_MODEL), seq(512), heads, heads, heads, seq(LANE), full(wg), full(wn), full(norm2)],
        out_specs=[seq(D_MODEL), seq(D_MODEL)],
        out_shape=[jax.ShapeDtypeStruct((b, l, D_MODEL), F32)] * 2,
        compiler_params=_cparams(("parallel", "parallel")),
        name="out_proj",
    )(x, og, ocmp, oslc, owin, sm, wg, wn, norm2)


def _peer_topk_kernel(x_ref, wqt_ref, keys_ref, cflat_ref, eid_ref, gate_ref, qt_scr, sv_scr, si_scr, top_scr,
                      *, tm):
    qt_scr[...] = _dot_nt(wqt_ref[...], x_ref[...].astype(BF16))
    rows = _iota_f((PEER_NKEYS, tm), 0)
    cflat = jnp.broadcast_to(cflat_ref[...], (PEER_NCAND, tm))

    def head_body(h, carry):
        for c in range(2):
            off = pl.multiple_of(h * (2 * PEER_HALF) + c * PEER_HALF, PEER_HALF)
            qs = qt_scr[pl.ds(off, PEER_HALF), :].astype(BF16)
            s = _dot(keys_ref[h, c], qs)

            def round_body(r, s):
                m = jnp.max(s, axis=0, keepdims=True)
                idx = jnp.min(jnp.where(s == m, rows, float(PEER_NKEYS)), axis=0, keepdims=True)
                sv_scr[c, pl.ds(r, 1), :] = m
                si_scr[c, pl.ds(r, 1), :] = idx
                return jnp.where(rows == idx, REMOVED, s)

            lax.fori_loop(0, PEER_TOPK, round_body, s)
        s1 = sv_scr[0]
        s2 = sv_scr[1]
        i1 = si_scr[0] * float(PEER_NKEYS)
        i2 = si_scr[1]
        cand = [s1[0:1] + s2]
        eidc = [i1[0:1] + i2]
        for a in range(1, 8):
            cand.append(s1[a:a + 1] + s2[0:8])
            eidc.append(i1[a:a + 1] + i2[0:8])
        cand.append(s1[8:16] + s2[0:1])
        eidc.append(i1[8:16] + i2[0:1])
        cand = jnp.where(cflat >= 0.0, jnp.concatenate(cand, axis=0), REMOVED)
        eidc = jnp.concatenate(eidc, axis=0)

        def round2(r, cand):
            m = jnp.max(cand, axis=0, keepdims=True)
            f = jnp.min(jnp.where(cand == m, cflat, 1e9), axis=0, keepdims=True)
            hit = cflat == f
            top_scr[pl.ds(r, 1), :] = m
            eid_ref[h, pl.ds(r, 1), :] = jnp.sum(jnp.where(hit, eidc, 0.0), axis=0, keepdims=True).astype(I32)
            return jnp.where(hit, REMOVED, cand)

        lax.fori_loop(0, PEER_TOPK, round2, cand)
        top = top_scr[...]
        e = jnp.exp(top - jnp.max(top, axis=0, keepdims=True))
        gate_ref[h] = e / jnp.sum(e, axis=0, keepdims=True)
        return carry

    lax.fori_loop(0, PEER_HEADS, head_body, 0)


def _peer_cflat():
    rows = [(0, b) for b in range(16)]
    for a in range(1, 8):
        rows += [(a, b) for b in range(8)]
    rows += [(a, 0) for a in range(8, 16)]
    flat = [a * 16 + b if (a + 1) * (b + 1) <= PEER_TOPK else -1 for a, b in rows]
    assert len(flat) == PEER_NCAND
    return jnp.asarray(np.array(flat, np.float32).reshape(PEER_NCAND, 1))


def _peer_topk(xn, wqt, keys, tm):
    t = xn.shape[0]
    assert t % tm == 0
    cflat = _peer_cflat()
    kern = functools.partial(_peer_topk_kernel, tm=tm)
    full = lambda a: pl.BlockSpec(a.shape, lambda i: (0,) * a.ndim)
    out_spec = pl.BlockSpec((PEER_HEADS, PEER_TOPK, tm), lambda i: (0, 0, i))
    return pl.pallas_call(
        kern,
        grid=(t // tm,),
        in_specs=[pl.BlockSpec((tm, D_MODEL), lambda i: (i, 0)), full(wqt), full(keys), full(cflat)],
        out_specs=[out_spec, out_spec],
        out_shape=[jax.ShapeDtypeStruct((PEER_HEADS, PEER_TOPK, t), I32),
                   jax.ShapeDtypeStruct((PEER_HEADS, PEER_TOPK, t), F32)],
        scratch_shapes=[pltpu.VMEM((D_MODEL, tm), F32), pltpu.VMEM((2, PEER_TOPK, tm), F32),
                        pltpu.VMEM((2, PEER_TOPK, tm), F32), pltpu.VMEM((PEER_TOPK, tm), F32)],
        compiler_params=_cparams(("parallel",)),
        name="peer_topk",
    )(xn, wqt, keys, cflat)


PEER_SLOTS = 3


def _peer_expert_kernel(eid_ref, x_ref, gate_ref, hres_ref, uv_ref, y_ref, buf, sem, *, tt):
    def row_copy(e, slot, k):
        return pltpu.make_async_copy(uv_ref.at[pl.ds(e, 1), :], buf.at[slot, pl.ds(k, 1), :], sem.at[slot])

    def issue(t, slot):
        base = t * PEER_SEL
        for k in range(PEER_SEL):
            row_copy(eid_ref[base + k], slot, k).start()

    def wait(slot):
        for k in range(PEER_SEL):
            row_copy(0, slot, k).wait()

    for t0 in range(PEER_SLOTS - 1):
        issue(t0, t0)

    def body(t, carry):
        slot = t % PEER_SLOTS
        nxt = t + (PEER_SLOTS - 1)

        @pl.when(nxt < tt)
        def _():
            issue(nxt, nxt % PEER_SLOTS)

        wait(slot)
        ub = buf[slot, :, 0:D_MODEL].astype(BF16)
        vb = buf[slot, :, D_MODEL:2 * D_MODEL].astype(BF16)
        x8 = jnp.broadcast_to(x_ref[pl.ds(t, 1), :].astype(BF16), (8, D_MODEL))
        act = _dot_nt(x8, ub)[0:1]
        w = gate_ref[pl.ds(t, 1), :] * (0.5 * act * (1.0 + lax.erf(act * SQRT_HALF)))
        w8 = jnp.broadcast_to(w.astype(BF16), (8, PEER_SEL))
        y_ref[pl.ds(t, 1), :] = hres_ref[pl.ds(t, 1), :] + _dot(w8, vb)[0:1]
        return carry

    lax.fori_loop(0, tt, body, 0)


def _peer_experts(eid_flat, xn, gate, hres, uv, tt):
    t = xn.shape[0]
    assert t % tt == 0 and tt >= PEER_SLOTS
    kern = functools.partial(_peer_expert_kernel, tt=tt)
    row = lambda w: pl.BlockSpec((tt, w), lambda i: (i, 0))
    return pl.pallas_call(
        kern,
        grid=(t // tt,),
        in_specs=[pl.BlockSpec((tt * PEER_SEL,), lambda i: (i,), memory_space=pltpu.SMEM),
                  row(D_MODEL), row(PEER_SEL), row(D_MODEL),
                  pl.BlockSpec(memory_space=pl.ANY)],
        out_specs=row(D_MODEL),
        out_shape=jax.ShapeDtypeStruct((t, D_MODEL), F32),
        scratch_shapes=[pltpu.VMEM((PEER_SLOTS, PEER_SEL, 2 * D_MODEL), F32),
                        pltpu.SemaphoreType.DMA((PEER_SLOTS,))],
        compiler_params=_cparams(("arbitrary",)),
        name="peer_experts",
    )(eid_flat, xn, gate, hres, uv)


def _prep_params(norm1, w_in, gdn_conv_w, gdn_a_log, gdn_dt_bias, gdn_norm, nsa_q_norm, nsa_k_norm,
                 rel_bias, w_o, norm2, peer_wq, peer_subkeys, peer_u, peer_v):
    w = w_in[0]
    p = {}
    p["norm1"] = norm1[0][None]
    p["wm"] = jnp.concatenate([w[:, :OFF_B], w[:, OFF_NQ:OFF_NG]], axis=1).astype(BF16)
    p["ws"] = jnp.concatenate([w[:, OFF_B:OFF_NQ], w[:, OFF_NG:], jnp.zeros((D_MODEL, LANE - 32), F32)],
                              axis=1).astype(BF16)
    li = jnp.arange(LANE)
    p["seg"] = (li[:, None] // NSA_HD == li[None, :] // NSA_HD).astype(BF16)
    p["qg"] = jnp.tile(nsa_q_norm[0], 2)[None]
    p["kg"] = jnp.tile(nsa_k_norm[0], (1, 2))
    p["conv_w"] = gdn_conv_w[0]
    p["al_vec"] = jnp.zeros((1, LANE), F32).at[0, 4:8].set(gdn_a_log[0])
    p["dtb_vec"] = jnp.zeros((1, LANE), F32).at[0, 4:8].set(gdn_dt_bias[0])
    p["gnorm"] = gdn_norm[0][None]
    p["tab"] = rel_bias
    wo = w_o[0]
    p["wg"] = wo[:512].astype(BF16)
    wn = jnp.zeros((NSA_HEADS, LANE, D_MODEL), F32)
    for h in range(NSA_HEADS):
        g = h // NSA_P
        wn = wn.at[h, g * NSA_HD:(g + 1) * NSA_HD].set(wo[512 + h * NSA_HD:512 + (h + 1) * NSA_HD])
    p["wn"] = wn.astype(BF16)
    p["norm2"] = norm2[0][None]
    p["wqt"] = peer_wq[0].T.astype(BF16)
    p["keys"] = peer_subkeys[0].astype(BF16)
    p["uv"] = jnp.concatenate([peer_u[0], peer_v[0]], axis=1)
    return p


def _perm_avg_matrix(n_blocks, n_rows):
    half = n_blocks // 2
    r = jnp.arange(n_blocks)
    blk = jnp.where(r < half, 2 * r, 2 * (r - half) + 1)
    return (jnp.arange(n_rows)[None, :] // CMP_BLOCK == blk[:, None]).astype(BF16)


def _token_mixer_tail(p, x, og, ocmp, oslc, owin, sm, tm_out, tm_topk, tt):
    b, l, _ = x.shape
    hres, xn2 = _out_proj(x, og, ocmp, oslc, owin, sm, p["wg"], p["wn"], p["norm2"], tm_out)
    t = b * l
    xn2 = xn2.reshape(t, D_MODEL)
    eid, gate = _peer_topk(xn2, p["wqt"], p["keys"], tm_topk)
    eid_flat = eid.reshape(PEER_SEL, t).T.reshape(t * PEER_SEL)
    gate_tok = gate.reshape(PEER_SEL, t).T
    y = _peer_experts(eid_flat, xn2, gate_tok, hres.reshape(t, D_MODEL), p["uv"], tt)
    return y.reshape(b, l, D_MODEL)


def kernel(x_prompt, x_sample, cache_nsa_kv, page_table, state_win_kv, state_conv, state_gdn, norm1, w_in, gdn_conv_w, gdn_a_log, gdn_dt_bias, gdn_norm, nsa_q_norm, nsa_k_norm, rel_bias, w_o, norm2, peer_wq, peer_subkeys, peer_u, peer_v):
    assert w_in.shape[0] == 1, "single layer"
    p = _prep_params(norm1, w_in, gdn_conv_w, gdn_a_log, gdn_dt_bias, gdn_norm, nsa_q_norm, nsa_k_norm,
                     rel_bias, w_o, norm2, peer_wq, peer_subkeys, peer_u, peer_v)
    b, l, _ = x_prompt.shape
    db, lq, _ = x_sample.shape
    n_pages = page_table.shape[1]
    past_len = n_pages * PAGE
    wb = state_win_kv.shape[2]
    assert cache_nsa_kv.shape[2] == PAGE and l >= WINDOW and l >= 3 and lq >= 3
    assert lq < CMP_BLOCK and lq <= 8 and wb == WINDOW and past_len >= wb

    tp = b * l
    tm = 256 if tp % 256 == 0 else LANE
    hc, z, sm, qn, kvn, winn = _in_proj(x_prompt.reshape(tp, D_MODEL), p["norm1"], p["wm"], p["ws"],
                                        p["seg"], p["qg"], p["kg"], tm)
    hc3, z3, sm3 = hc.reshape(b, l, GDN_CH), z.reshape(b, l, 512), sm.reshape(b, l, LANE)
    qn3, kvn3, winn3 = qn.reshape(b, l, 512), kvn.reshape(b, l, 512), winn.reshape(b, l, 256)
    og, gdn_p = _gdn(hc3, z3, sm3, p["conv_w"], p["al_vec"], p["dtb_vec"], p["gnorm"],
                     jnp.zeros((b, 3, GDN_CH), F32), jnp.zeros((b, GDN_HEADS, 128, 128), F32),
                     GDN_CHUNK, l)
    nc = l // CMP_BLOCK
    nsb = l // SEL_BLOCK
    ocmp, sel = _cmp_prompt(p["tab"], qn3, kvn3, _perm_avg_matrix(nc, l), tm)
    nk = l // LANE
    col = jnp.arange(2 * nsb)
    key_blk = (jnp.arange(nk)[:, None] * LANE + jnp.arange(LANE)[None, :]) // SEL_BLOCK
    emat = jnp.stack([(col[None, :, None] == g * nsb + key_blk[:, None, :]) for g in range(NSA_GROUPS)]
                     ).astype(BF16)
    oslc = _flash_prompt(p["tab"], qn3, kvn3, 2, 3, sel, emat, LANE, False)
    owin = _flash_prompt(p["tab"], qn3, winn3, 0, 1, None, None, LANE, True)
    y_prompt = _token_mixer_tail(p, x_prompt, og, ocmp, oslc, owin, sm3, tm, tm, 64)

    ts = db * lq
    hc_s, z_s, sm_s, qn_s, kvn_s, winn_s = _in_proj(x_sample.reshape(ts, D_MODEL), p["norm1"], p["wm"], p["ws"],
                                                    p["seg"], p["qg"], p["kg"], min(ts, 256))
    pad_rows = lambda a, n: jnp.pad(a.reshape(db, lq, a.shape[-1]), ((0, 0), (0, n - lq), (0, 0)))
    og_s, gdn_s = _gdn(pad_rows(hc_s, GDN_CHUNK), pad_rows(z_s, GDN_CHUNK), pad_rows(sm_s, GDN_CHUNK),
                       p["conv_w"], p["al_vec"], p["dtb_vec"], p["gnorm"], state_conv[0], state_gdn[0],
                       GDN_CHUNK, lq)
    cache3 = cache_nsa_kv[0].reshape(cache_nsa_kv.shape[1], PAGE, 512)
    q8 = pad_rows(qn_s, 8)
    avg = _perm_avg_matrix(4 * PAGES_PER_STEP, PAGES_PER_STEP * PAGE)
    ocmp_s, sel_s = _cmp_sample(page_table, p["tab"], cache3, q8, avg, past_len)
    nsb_s = past_len // SEL_BLOCK
    key_blk_s = (jnp.arange(n_pages)[:, None] * PAGE + jnp.arange(PAGE)[None, :]) // SEL_BLOCK
    emat_s = (jnp.arange(nsb_s)[None, :, None] == key_blk_s[:, None, :]).astype(BF16)
    oslc_s = _slc_sample(page_table, p["tab"], cache3, q8, pad_rows(kvn_s, PAGE), sel_s, emat_s, past_len)
    wseq = jnp.concatenate([state_win_kv[0].reshape(db, wb, 256), pad_rows(winn_s, LANE)], axis=1)
    owin_s = _win_sample(p["tab"], q8, wseq, past_len, wb, lq)
    flat_heads = lambda o: o[:, :, :lq].transpose(1, 0, 2, 3).reshape(1, NSA_HEADS, ts, LANE)
    y_sample = _token_mixer_tail(p, x_sample.reshape(1, ts, D_MODEL), og_s[:, :lq].reshape(1, ts, 512),
                                 flat_heads(ocmp_s), flat_heads(oslc_s), flat_heads(owin_s),
                                 sm_s.reshape(1, ts, LANE), min(ts, 256), min(ts, 256), 64)

    kv_tail = (4, NSA_GROUPS, NSA_HD)
    return (y_prompt,
            y_sample.reshape(db, lq, D_MODEL),
            kvn.reshape((1, b, l) + kv_tail),
            winn3[:, l - WINDOW:].reshape(1, b, WINDOW, 2, NSA_GROUPS, NSA_HD),
            hc3[:, l - 3:][None],
            gdn_p[None],
            kvn_s.reshape((1, db, lq) + kv_tail),
            wseq[:, lq:lq + wb].reshape(1, db, wb, 2, NSA_GROUPS, NSA_HD),
            hc_s.reshape(db, lq, GDN_CH)[:, lq - 3:][None],
            gdn_s[None])
```

```python
import functools
import math

import numpy as np
import jax
import jax.numpy as jnp
from jax import lax
from jax.experimental import pallas as pl
from jax.experimental.pallas import tpu as pltpu

F32 = jnp.float32
BF16 = jnp.bfloat16
I32 = jnp.int32

D_MODEL = 1024
EPS = 1e-6
NEG_INF = -1e30
FORCE_SCORE = 1e4
GDN_HEADS = 4
GDN_DK = 128
GDN_CHUNK = 64
GDN_CH = 1536
NSA_HEADS = 8
NSA_GROUPS = 2
NSA_P = 4
NSA_HD = 64
CMP_BLOCK = 32
SEL_BLOCK = 64
SEL_TOPK = 16
WINDOW = 512
REL_BUCKETS = 32
PAGE = 128
PAGES_PER_STEP = 4
PEER_HEADS = 8
PEER_NKEYS = 128
PEER_TOPK = 16
PEER_HALF = 64
PEER_SEL = PEER_HEADS * PEER_TOPK
PEER_NCAND = 80
OFF_B = 2048
OFF_NQ = 2056
OFF_NG = 3336
LANE = 128
VMEM_LIMIT = 56 * 1024 * 1024
REMOVED = -3.0e38
SQRT_HALF = 0.7071067811865476


def _cparams(sem):
    return pltpu.CompilerParams(dimension_semantics=sem, vmem_limit_bytes=VMEM_LIMIT)


def _dot(a, b):
    return jnp.dot(a, b, preferred_element_type=F32)


def _dot_nt(a, b):
    return lax.dot_general(a, b, (((1,), (1,)), ((), ())), preferred_element_type=F32)


def _dot_tn(a, b):
    return lax.dot_general(a, b, (((0,), (0,)), ((), ())), preferred_element_type=F32)


def _split2(x):
    hi = x.astype(BF16)
    lo = (x - hi.astype(F32)).astype(BF16)
    return hi, lo


def _split3(x):
    hi = x.astype(BF16)
    r = x - hi.astype(F32)
    mid = r.astype(BF16)
    lo = (r - mid.astype(F32)).astype(BF16)
    return hi, mid, lo


def _dot_exact_lhs(m01, x):
    hi, mid, lo = _split3(x)
    return _dot(m01, hi) + (_dot(m01, mid) + _dot(m01, lo))


def _mm3(a, b):
    ah, al = _split2(a)
    bh, bl = _split2(b)
    return _dot(ah, bh) + (_dot(ah, bl) + _dot(al, bh))


def _iota_f(shape, axis):
    return lax.broadcasted_iota(I32, shape, axis).astype(F32)


def _rel_bucket(dist):
    d = jnp.maximum(dist, 0)
    df = jnp.maximum(d, 1).astype(F32)
    large = 16 + (jnp.log(df / 16.0) / math.log(128.0) * 16.0).astype(I32)
    large = jnp.minimum(large, REL_BUCKETS - 1)
    return jnp.where(d < 16, d, large)


def _bias_from_bucket(bucket, tab_ref, head, lo=0, hi=REL_BUCKETS - 1):
    if isinstance(lo, int) and isinstance(hi, int):
        b = jnp.zeros(bucket.shape, F32)
        for k in range(lo, hi + 1):
            b = jnp.where(bucket == k, tab_ref[k, head], b)
        return b

    def body(k, b):
        return jnp.where(bucket == k, tab_ref[k, head], b)

    return lax.fori_loop(lo, hi + 1, body, jnp.zeros(bucket.shape, F32))


def _topk_mask(s, k, axis):
    n = s.shape[axis]
    ids = _iota_f(s.shape, axis)
    sel = jnp.zeros(s.shape, F32)
    for _ in range(k):
        m = jnp.max(s, axis=axis, keepdims=True)
        idx = jnp.min(jnp.where(s == m, ids, float(n)), axis=axis, keepdims=True)
        hit = ids == idx
        sel = jnp.where(hit, 1.0, sel)
        s = jnp.where(hit, REMOVED, s)
    return sel


def _seg_rmsnorm(v, gain, seg):
    sq = v * v
    hi, lo = _split2(sq)
    ssum = _dot(hi, seg) + _dot(lo, seg)
    return v * lax.rsqrt(ssum * (1.0 / NSA_HD) + EPS) * gain


def _inproj_kernel(x_ref, g1_ref, wm_ref, ws_ref, seg_ref, qg_ref, kg_ref,
                   hc_ref, z_ref, sm_ref, q_ref, kv_ref, win_ref):
    x = x_ref[...]
    ms = jnp.mean(x * x, axis=-1, keepdims=True)
    xn = (x * lax.rsqrt(ms + EPS) * g1_ref[...]).astype(BF16)
    h = _dot(xn, wm_ref[...])
    sm_ref[...] = _dot(xn, ws_ref[...])
    hc_ref[...] = h[:, :GDN_CH]
    z_ref[...] = h[:, GDN_CH:2048]
    seg = seg_ref[...]
    qg = qg_ref[...]
    kg = kg_ref[...]
    for i in range(4):
        q_ref[:, i * LANE:(i + 1) * LANE] = _seg_rmsnorm(h[:, 2048 + i * LANE:2048 + (i + 1) * LANE], qg, seg)
    kv_ref[:, 0:128] = _seg_rmsnorm(h[:, 2560:2688], kg[0:1], seg)
    kv_ref[:, 128:256] = h[:, 2688:2816]
    kv_ref[:, 256:384] = _seg_rmsnorm(h[:, 2816:2944], kg[1:2], seg)
    kv_ref[:, 384:512] = h[:, 2944:3072]
    win_ref[:, 0:128] = _seg_rmsnorm(h[:, 3072:3200], kg[2:3], seg)
    win_ref[:, 128:256] = h[:, 3200:3328]


def _in_proj(x, norm1, wm, ws, seg, qg, kg, tm):
    t = x.shape[0]
    assert t % tm == 0
    row = lambda w: pl.BlockSpec((tm, w), lambda i: (i, 0))
    full = lambda a: pl.BlockSpec(a.shape, lambda i: (0,) * a.ndim)
    widths = (GDN_CH, 512, LANE, 512, 512, 256)
    return pl.pallas_call(
        _inproj_kernel,
        grid=(t // tm,),
        in_specs=[row(D_MODEL), full(norm1), full(wm), full(ws), full(seg), full(qg), full(kg)],
        out_specs=[row(w) for w in widths],
        out_shape=[jax.ShapeDtypeStruct((t, w), F32) for w in widths],
        compiler_params=_cparams(("parallel",)),
        name="in_proj",
    )(x, norm1, wm, ws, seg, qg, kg)


def _inv_unit_lower(lmat, c):
    ri = lax.broadcasted_iota(I32, (c, c), 0)
    ci = lax.broadcasted_iota(I32, (c, c), 1)
    eye = jnp.where(ri == ci, 1.0, 0.0).astype(F32)
    n = -lmat
    p = eye + n
    m = _mm3(n, n)
    span = 2
    while True:
        p = p + _mm3(p, m)
        span *= 2
        if span >= c:
            break
        m = _mm3(m, m)
    return p


def _gdn_kernel(hc_ref, z_ref, sm_ref, cw_ref, al_ref, dtb_ref, gn_ref, conv0_ref, s0_ref,
                og_ref, sfin_ref, xbuf, s_scr, *, c, l_valid, n_chunks):
    ci = pl.program_id(1)

    @pl.when(ci == 0)
    def _():
        xbuf[0:8, :] = jnp.zeros((8, GDN_CH), F32)
        xbuf[5:8, :] = conv0_ref[...]
        s_scr[...] = s0_ref[...]

    xbuf[8:8 + c, :] = hc_ref[...]
    w = cw_ref[...]
    y = (xbuf[5:5 + c, :] * w[0:1] + xbuf[6:6 + c, :] * w[1:2]
         + xbuf[7:7 + c, :] * w[2:3] + xbuf[8:8 + c, :] * w[3:4])
    tail = xbuf[5 + c:8 + c, :]
    xbuf[5:8, :] = tail
    y = y * jax.nn.sigmoid(y)

    sm = sm_ref[...]
    rowid = ci * c + lax.broadcasted_iota(I32, (c, 1), 0)
    rvalid = rowid < l_valid
    beta_all = jnp.where(rvalid, jax.nn.sigmoid(sm), 0.0)
    sp_in = sm + dtb_ref[...]
    softplus = jnp.maximum(sp_in, 0.0) + jnp.log1p(jnp.exp(-jnp.abs(sp_in)))
    g_all = jnp.where(rvalid, -jnp.exp(al_ref[...]) * softplus, 0.0)

    ri = lax.broadcasted_iota(I32, (c, c), 0)
    cj = lax.broadcasted_iota(I32, (c, c), 1)
    causal = ri >= cj
    strict = ri > cj
    tri = jnp.where(causal, 1.0, 0.0).astype(BF16)
    triu = jnp.where(ri <= cj, 1.0, 0.0).astype(BF16)
    g_hi, g_mid, g_lo = _split3(g_all)
    gcum = _dot(tri, g_hi) + (_dot(tri, g_mid) + _dot(tri, g_lo))
    gcum_t = _dot_tn(g_hi, triu) + (_dot_tn(g_mid, triu) + _dot_tn(g_lo, triu))
    gn = gn_ref[...]

    for h in range(GDN_HEADS):
        qh = y[:, h * 128:(h + 1) * 128]
        kh = y[:, 512 + h * 128:512 + (h + 1) * 128]
        vh = y[:, 1024 + h * 128:1024 + (h + 1) * 128]
        qh = qh * lax.rsqrt(jnp.sum(qh * qh, axis=-1, keepdims=True) + EPS) * (GDN_DK ** -0.5)
        kh = kh * lax.rsqrt(jnp.sum(kh * kh, axis=-1, keepdims=True) + EPS)
        qh = jnp.where(rvalid, qh, 0.0)
        kh = jnp.where(rvalid, kh, 0.0)
        vh = jnp.where(rvalid, vh, 0.0)
        beta = beta_all[:, h:h + 1]
        gc = gcum[:, 4 + h:5 + h]
        gct = gcum_t[4 + h:5 + h, :]
        diff = gc - gct
        decay = jnp.where(causal, jnp.exp(jnp.where(causal, diff, 0.0)), 0.0)
        kb = kh * beta
        khb = kh.astype(BF16)
        lmat = jnp.where(strict, _dot_nt(kb.astype(BF16), khb) * decay, 0.0)
        tmat = _inv_unit_lower(lmat, c).astype(BF16)
        eg = jnp.exp(gc)
        u = _dot(tmat, (vh * beta).astype(BF16))
        wmat = _dot(tmat, (kb * eg).astype(BF16))
        a_intra = _dot_nt(qh.astype(BF16), khb) * decay
        gc_last = gc[c - 1:c, :]
        q_dec = qh * eg
        k_dec = kh * jnp.exp(gc_last - gc)
        g_last = jnp.exp(gc_last)
        s = s_scr[h]
        sb = s.astype(BF16)
        v_new = u - _dot(wmat.astype(BF16), sb)
        vnb = v_new.astype(BF16)
        o = _dot(q_dec.astype(BF16), sb) + _dot(a_intra.astype(BF16), vnb)
        s_scr[h] = s * g_last + _dot_tn(k_dec.astype(BF16), vnb)
        o = o * lax.rsqrt(jnp.mean(o * o, axis=-1, keepdims=True) + EPS) * gn
        zh = z_ref[:, h * 128:(h + 1) * 128]
        og_ref[:, h * 128:(h + 1) * 128] = o * (zh * jax.nn.sigmoid(zh))

    @pl.when(ci == n_chunks - 1)
    def _():
        sfin_ref[...] = s_scr[...]


def _gdn(hc, z, sm, conv_w, al_vec, dtb_vec, gnorm, conv0, s0, c, l_valid):
    b, lp, _ = hc.shape
    assert lp % c == 0
    n_chunks = lp // c
    full = lambda a: pl.BlockSpec(a.shape, lambda i, j: (0,) * a.ndim)
    seq = lambda w: pl.BlockSpec((None, c, w), lambda i, j: (i, j, 0))
    kern = functools.partial(_gdn_kernel, c=c, l_valid=l_valid, n_chunks=n_chunks)
    return pl.pallas_call(
        kern,
        grid=(b, n_chunks),
        in_specs=[seq(GDN_CH), seq(512), seq(LANE), full(conv_w), full(al_vec), full(dtb_vec), full(gnorm),
                  pl.BlockSpec((None, 3, GDN_CH), lambda i, j: (i, 0, 0)),
                  pl.BlockSpec((None, GDN_HEADS, 128, 128), lambda i, j: (i, 0, 0, 0))],
        out_specs=[seq(512), pl.BlockSpec((None, GDN_HEADS, 128, 128), lambda i, j: (i, 0, 0, 0))],
        out_shape=[jax.ShapeDtypeStruct((b, lp, 512), F32),
                   jax.ShapeDtypeStruct((b, GDN_HEADS, 128, 128), F32)],
        scratch_shapes=[pltpu.VMEM((c + 8, GDN_CH), F32), pltpu.VMEM((GDN_HEADS, 128, 128), F32)],
        compiler_params=_cparams(("parallel", "arbitrary")),
        name="gdn",
    )(hc, z, sm, conv_w, al_vec, dtb_vec, gnorm, conv0, s0)


def _head_q128(q, h):
    g = h // NSA_P
    piece = q[:, (h // 2) * LANE:(h // 2 + 1) * LANE]
    lane = lax.broadcasted_iota(I32, piece.shape, 1)
    keep = (lane >= NSA_HD) if h % 2 == 1 else (lane < NSA_HD)
    qm = jnp.where(keep, piece, 0.0)
    if h % 2 != g:
        qm = pltpu.roll(qm, NSA_HD, 1)
    return qm


def _select_scores(imp, qpos, nsb):
    score = imp[:, :nsb] + imp[:, nsb:]
    j = lax.broadcasted_iota(I32, score.shape, 1)
    cur = qpos // SEL_BLOCK
    forced = (j == 0) | (j == cur) | (j == cur - 1)
    future = j * SEL_BLOCK > qpos
    return jnp.where(future, -1.0, jnp.where(forced, FORCE_SCORE, score))


def _cmp_attention(tab_ref, q, kcv, qpos, ocmp_ref):
    nc = kcv.shape[0]
    nsb = nc // 2
    kc = kcv[:, 0:128].astype(BF16)
    vc = kcv[:, 128:256].astype(BF16)
    lane = lax.broadcasted_iota(I32, (1, nc), 1)
    blk = jnp.where(lane < nsb, 2 * lane, 2 * (lane - nsb) + 1)
    dist = qpos - (blk * CMP_BLOCK + CMP_BLOCK - 1)
    valid = dist >= 0
    bucket = _rel_bucket(dist)
    scores = []
    for g in range(NSA_GROUPS):
        imp = jnp.zeros((q.shape[0], nc), F32)
        for p in range(NSA_P):
            h = g * NSA_P + p
            qm = _head_q128(q, h).astype(BF16)
            logits = _dot_nt(qm, kc) * (NSA_HD ** -0.5) + _bias_from_bucket(bucket, tab_ref, h)
            l = jnp.where(valid, logits, NEG_INF)
            m = jnp.max(l, axis=-1, keepdims=True)
            pr = jnp.where(valid, jnp.exp(l - m), 0.0)
            pr = pr / jnp.maximum(jnp.sum(pr, axis=-1, keepdims=True), 1e-30)
            ocmp_ref[h] = _dot(pr.astype(BF16), vc)
            imp = imp + pr
        scores.append(_select_scores(imp, qpos, nsb))
    return scores


def _flash_init(m_scr, l_scr, acc_scr):
    m_scr[...] = jnp.full(m_scr.shape, NEG_INF, F32)
    l_scr[...] = jnp.zeros(l_scr.shape, F32)
    acc_scr[...] = jnp.zeros(acc_scr.shape, F32)


def _cmp_prompt_kernel(tab_ref, q_ref, kv_ref, mavg_ref, ocmp_ref, sel_ref, kc_scr, *, tq, nsb):
    qi = pl.program_id(1)

    @pl.when(qi == 0)
    def _():
        kc_scr[...] = _dot_exact_lhs(mavg_ref[...], kv_ref[...]) * (1.0 / CMP_BLOCK)

    qpos = qi * tq + lax.broadcasted_iota(I32, (tq, 1), 0)
    scores = _cmp_attention(tab_ref, q_ref[...], kc_scr[...], qpos, ocmp_ref)
    st = jnp.concatenate(scores, axis=1).T
    k = min(SEL_TOPK, nsb)
    sel_ref[...] = jnp.concatenate([_topk_mask(st[:nsb], k, 0), _topk_mask(st[nsb:], k, 0)], axis=0)


def _cmp_prompt(tab, qn, kvn, mavg, tq):
    b, l, _ = qn.shape
    nc = l // CMP_BLOCK
    assert l % tq == 0 and l % SEL_BLOCK == 0
    kern = functools.partial(_cmp_prompt_kernel, tq=tq, nsb=nc // 2)
    return pl.pallas_call(
        kern,
        grid=(b, l // tq),
        in_specs=[pl.BlockSpec(memory_space=pltpu.SMEM),
                  pl.BlockSpec((None, tq, 512), lambda i, j: (i, j, 0)),
                  pl.BlockSpec((None, l, 256), lambda i, j: (i, 0, 0)),
                  pl.BlockSpec(mavg.shape, lambda i, j: (0, 0))],
        out_specs=[pl.BlockSpec((None, NSA_HEADS, tq, LANE), lambda i, j: (i, 0, j, 0)),
                   pl.BlockSpec((None, nc, tq), lambda i, j: (i, 0, j))],
        out_shape=[jax.ShapeDtypeStruct((b, NSA_HEADS, l, LANE), F32),
                   jax.ShapeDtypeStruct((b, nc, l), F32)],
        scratch_shapes=[pltpu.VMEM((nc, 256), F32)],
        compiler_params=_cparams(("parallel", "arbitrary")),
        name="cmp_prompt",
    )(tab, qn, kvn, mavg)


def _flash_prompt_kernel(tab_ref, q_ref, k_ref, v_ref, *rest, t, windowed, nsb, n_dist):
    if windowed:
        o_ref, bias_scr, m_scr, l_scr, acc_scr = rest
    else:
        sel_ref, o_ref, bias_scr, m_scr, l_scr, acc_scr = rest
    qi = pl.program_id(1)
    lane_i = lax.broadcasted_iota(I32, (1, t), 1)
    sub_j = lax.broadcasted_iota(I32, (t, 1), 0)

    @pl.when((pl.program_id(0) == 0) & (qi == 0))
    def _():
        def build(d, carry):
            bucket = _rel_bucket(d * t + lane_i - sub_j)
            for h in range(NSA_HEADS):
                bias_scr[h, d] = _bias_from_bucket(bucket, tab_ref, h)
            return carry

        lax.fori_loop(0, n_dist, build, 0)

    q = q_ref[...]
    qts = [_head_q128(q, h).T.astype(BF16) for h in range(NSA_HEADS)]
    qpos = qi * t + lane_i
    _flash_init(m_scr, l_scr, acc_scr)
    k_lo = jnp.maximum(qi - WINDOW // t, 0) if windowed else 0

    def body(ki, carry):
        k0 = pl.multiple_of(ki * t, t)
        kt = k_ref[pl.ds(k0, t), :].astype(BF16)
        vtt = v_ref[pl.ds(k0, t), :].T.astype(BF16)
        dist = qpos - (k0 + sub_j)
        ok = dist >= 0
        if windowed:
            ok = ok & (dist <= WINDOW)
        for g in range(NSA_GROUPS):
            if windowed:
                valid = ok
            else:
                blk = g * nsb + ki * (t // SEL_BLOCK)
                rows = [sel_ref[pl.ds(blk + r, 1), :] for r in range(t // SEL_BLOCK)]
                selm = rows[-1]
                for r in range(t // SEL_BLOCK - 2, -1, -1):
                    selm = jnp.where(sub_j < (r + 1) * SEL_BLOCK, rows[r], selm)
                valid = ok & (selm > 0.5)
            for p in range(NSA_P):
                h = g * NSA_P + p
                s = _dot(kt, qts[h]) * (NSA_HD ** -0.5) + bias_scr[h, qi - ki]
                l = jnp.where(valid, s, NEG_INF)
                m_old = m_scr[h]
                m_new = jnp.maximum(m_old, jnp.max(l, axis=0, keepdims=True))
                pr = jnp.where(valid, jnp.exp(l - m_new), 0.0)
                alpha = jnp.exp(m_old - m_new)
                l_scr[h] = alpha * l_scr[h] + jnp.sum(pr, axis=0, keepdims=True)
                acc_scr[h] = alpha * acc_scr[h] + _dot(vtt, pr.astype(BF16))
                m_scr[h] = m_new
        return carry

    lax.fori_loop(k_lo, qi + 1, body, 0)
    for h in range(NSA_HEADS):
        o_ref[h] = (acc_scr[h] / jnp.maximum(l_scr[h], 1e-30)).T


def _flash_prompt(tab, qn, kv_arr, k_blk, v_blk, sel, t, windowed):
    b, l, _ = qn.shape
    assert l % t == 0 and t == LANE
    n_dist = WINDOW // t + 1 if windowed else l // t
    kern = functools.partial(_flash_prompt_kernel, t=t, windowed=windowed, nsb=l // SEL_BLOCK, n_dist=n_dist)
    in_specs = [pl.BlockSpec(memory_space=pltpu.SMEM),
                pl.BlockSpec((None, t, 512), lambda i, j: (i, j, 0)),
                pl.BlockSpec((None, l, LANE), lambda i, j: (i, 0, k_blk)),
                pl.BlockSpec((None, l, LANE), lambda i, j: (i, 0, v_blk))]
    args = [tab, qn, kv_arr, kv_arr]
    if not windowed:
        in_specs += [pl.BlockSpec((None, sel.shape[1], t), lambda i, j: (i, 0, j))]
        args += [sel]
    return pl.pallas_call(
        kern,
        grid=(b, l // t),
        in_specs=in_specs,
        out_specs=pl.BlockSpec((None, NSA_HEADS, t, LANE), lambda i, j: (i, 0, j, 0)),
        out_shape=jax.ShapeDtypeStruct((b, NSA_HEADS, l, LANE), F32),
        scratch_shapes=[pltpu.VMEM((NSA_HEADS, n_dist, t, t), F32),
                        pltpu.VMEM((NSA_HEADS, 1, t), F32), pltpu.VMEM((NSA_HEADS, 1, t), F32),
                        pltpu.VMEM((NSA_HEADS, LANE, t), F32)],
        compiler_params=_cparams(("arbitrary", "arbitrary")),
        name="win_prompt" if windowed else "slc_prompt",
    )(*args)


def _cmp_sample_kernel(pt_ref, tab_ref, p0_ref, p1_ref, p2_ref, p3_ref, q_ref, avg_ref,
                       ocmp_ref, sel_ref, kc_scr, *, n_steps, past_len):
    del pt_ref
    st = pl.program_id(1)
    half = n_steps * 2 * PAGES_PER_STEP
    pages = jnp.concatenate([p0_ref[...], p1_ref[...], p2_ref[...], p3_ref[...]], axis=0)
    means = _dot_exact_lhs(avg_ref[...], pages) * (1.0 / CMP_BLOCK)
    off = pl.multiple_of(st * 8, 8)
    kc_scr[pl.ds(off, 8), :] = means[0:8]
    kc_scr[pl.ds(half + off, 8), :] = means[8:16]

    @pl.when(st == n_steps - 1)
    def _():
        qpos = past_len + lax.broadcasted_iota(I32, (8, 1), 0)
        scores = _cmp_attention(tab_ref, q_ref[...], kc_scr[...], qpos, ocmp_ref)
        k = min(SEL_TOPK, half + 1) - 1
        for g in range(NSA_GROUPS):
            sel_ref[g] = _topk_mask(scores[g], k, 1)


def _page_spec(r, lane_blk):
    return pl.BlockSpec((None, PAGE, 256), lambda i, j, pt: (pt[i, j * PAGES_PER_STEP + r], 0, lane_blk))


def _cmp_sample(page_table, tab, cache, q8, avg, past_len):
    db, n_pages = page_table.shape
    assert n_pages % PAGES_PER_STEP == 0
    n_steps = n_pages // PAGES_PER_STEP
    nsb = past_len // SEL_BLOCK
    kern = functools.partial(_cmp_sample_kernel, n_steps=n_steps, past_len=past_len)
    grid_spec = pltpu.PrefetchScalarGridSpec(
        num_scalar_prefetch=1,
        grid=(db, n_steps),
        in_specs=[pl.BlockSpec(memory_space=pltpu.SMEM)]
        + [_page_spec(r, 0) for r in range(PAGES_PER_STEP)]
        + [pl.BlockSpec((None, 8, 512), lambda i, j, pt: (i, 0, 0)),
           pl.BlockSpec(avg.shape, lambda i, j, pt: (0, 0))],
        out_specs=[pl.BlockSpec((None, NSA_HEADS, 8, LANE), lambda i, j, pt: (i, 0, 0, 0)),
                   pl.BlockSpec((None, NSA_GROUPS, 8, nsb), lambda i, j, pt: (i, 0, 0, 0))],
        scratch_shapes=[pltpu.VMEM((2 * nsb, 256), F32)],
    )
    return pl.pallas_call(
        kern,
        grid_spec=grid_spec,
        out_shape=[jax.ShapeDtypeStruct((db, NSA_HEADS, 8, LANE), F32),
                   jax.ShapeDtypeStruct((db, NSA_GROUPS, 8, nsb), F32)],
        compiler_params=_cparams(("parallel", "arbitrary")),
        name="cmp_sample",
    )(page_table, tab, cache, cache, cache, cache, q8, avg)


def _slc_sample_kernel(pt_ref, tab_ref, q_ref, new_ref, sel_ref, e_ref, cache_ref, o_ref, kv_buf, sem,
                       *, n_pages, past_len):
    b = pl.program_id(0)

    def page_copy(pg):
        return pltpu.make_async_copy(cache_ref.at[pt_ref[b, pg], :, pl.ds(256, 256)],
                                     kv_buf.at[pl.ds(pg * PAGE, PAGE), :], sem.at[0])

    for pg in range(n_pages):
        page_copy(pg).start()

    q = q_ref[...]
    qpos = past_len + lax.broadcasted_iota(I32, (8, 1), 0)
    dist_p = qpos - lax.broadcasted_iota(I32, (1, past_len), 1)
    bucket_p = _rel_bucket(dist_p)
    dist_n = qpos - (past_len + lax.broadcasted_iota(I32, (1, PAGE), 1))
    ok_n = dist_n >= 0
    bucket_n = _rel_bucket(dist_n)
    knew = new_ref[:, 0:128].astype(BF16)
    vnew = new_ref[:, 128:256].astype(BF16)

    for pg in range(n_pages):
        page_copy(pg).wait()

    kall = kv_buf[:, 0:128].astype(BF16)
    vall = kv_buf[:, 128:256].astype(BF16)
    for g in range(NSA_GROUPS):
        qg = jnp.concatenate([_head_q128(q, g * NSA_P + p) for p in range(NSA_P)], axis=0).astype(BF16)
        s_p = _dot_nt(qg, kall) * (NSA_HD ** -0.5)
        s_n = _dot_nt(qg, knew) * (NSA_HD ** -0.5)
        mask_p = _dot(sel_ref[g].astype(BF16), e_ref[...]) > 0.5
        pps, pns, dens = [], [], []
        for p in range(NSA_P):
            h = g * NSA_P + p
            rows = slice(8 * p, 8 * p + 8)
            lp = jnp.where(mask_p, s_p[rows] + _bias_from_bucket(bucket_p, tab_ref, h), NEG_INF)
            ln = jnp.where(ok_n, s_n[rows] + _bias_from_bucket(bucket_n, tab_ref, h), NEG_INF)
            m = jnp.maximum(jnp.max(lp, axis=-1, keepdims=True), jnp.max(ln, axis=-1, keepdims=True))
            pp = jnp.where(mask_p, jnp.exp(lp - m), 0.0)
            pn = jnp.where(ok_n, jnp.exp(ln - m), 0.0)
            dens.append(jnp.sum(pp, axis=-1, keepdims=True) + jnp.sum(pn, axis=-1, keepdims=True))
            pps.append(pp.astype(BF16))
            pns.append(pn.astype(BF16))
        o = _dot(jnp.concatenate(pps, axis=0), vall) + _dot(jnp.concatenate(pns, axis=0), vnew)
        for p in range(NSA_P):
            o_ref[g * NSA_P + p] = o[8 * p:8 * p + 8] / jnp.maximum(dens[p], 1e-30)


def _slc_sample(page_table, tab, cache, q8, new_kv, sel, emat, past_len):
    db, n_pages = page_table.shape
    kern = functools.partial(_slc_sample_kernel, n_pages=n_pages, past_len=past_len)
    grid_spec = pltpu.PrefetchScalarGridSpec(
        num_scalar_prefetch=1,
        grid=(db,),
        in_specs=[pl.BlockSpec(memory_space=pltpu.SMEM),
                  pl.BlockSpec((None, 8, 512), lambda i, pt: (i, 0, 0)),
                  pl.BlockSpec((None, PAGE, 256), lambda i, pt: (i, 0, 1)),
                  pl.BlockSpec((None,) + sel.shape[1:], lambda i, pt: (i, 0, 0, 0)),
                  pl.BlockSpec(emat.shape, lambda i, pt: (0, 0)),
                  pl.BlockSpec(memory_space=pl.ANY)],
        out_specs=pl.BlockSpec((None, NSA_HEADS, 8, LANE), lambda i, pt: (i, 0, 0, 0)),
        scratch_shapes=[pltpu.VMEM((n_pages * PAGE, 256), F32), pltpu.SemaphoreType.DMA((1,))],
    )
    return pl.pallas_call(
        kern,
        grid_spec=grid_spec,
        out_shape=jax.ShapeDtypeStruct((db, NSA_HEADS, 8, LANE), F32),
        compiler_params=_cparams(("arbitrary",)),
        name="slc_sample",
    )(page_table, tab, q8, new_kv, sel, emat, cache)


def _win_sample_kernel(tab_ref, q_ref, w_ref, o_ref, *, past_len, wb, lq):
    q = q_ref[...]
    qpos = past_len + lax.broadcasted_iota(I32, (8, 1), 0)
    wseq = w_ref[...]
    n = wseq.shape[0]
    kw = wseq[:, 0:128].astype(BF16)
    vw = wseq[:, 128:256].astype(BF16)
    j = lax.broadcasted_iota(I32, (1, n), 1)
    kpos = past_len - wb + j
    dist = qpos - kpos
    valid = (dist >= 0) & (dist <= WINDOW) & (kpos >= 0) & (j < wb + lq)
    bucket = _rel_bucket(dist)
    for h in range(NSA_HEADS):
        qm = _head_q128(q, h).astype(BF16)
        s = _dot_nt(qm, kw) * (NSA_HD ** -0.5) + _bias_from_bucket(bucket, tab_ref, h)
        l = jnp.where(valid, s, NEG_INF)
        m = jnp.max(l, axis=-1, keepdims=True)
        pr = jnp.where(valid, jnp.exp(l - m), 0.0)
        pr = pr / jnp.maximum(jnp.sum(pr, axis=-1, keepdims=True), 1e-30)
        o_ref[h] = _dot(pr.astype(BF16), vw)


def _win_sample(tab, q8, wseq, past_len, wb, lq):
    db, n, _ = wseq.shape
    kern = functools.partial(_win_sample_kernel, past_len=past_len, wb=wb, lq=lq)
    return pl.pallas_call(
        kern,
        grid=(db,),
        in_specs=[pl.BlockSpec(memory_space=pltpu.SMEM),
                  pl.BlockSpec((None, 8, 512), lambda i: (i, 0, 0)),
                  pl.BlockSpec((None, n, 256), lambda i: (i, 0, 0))],
        out_specs=pl.BlockSpec((None, NSA_HEADS, 8, LANE), lambda i: (i, 0, 0, 0)),
        out_shape=jax.ShapeDtypeStruct((db, NSA_HEADS, 8, LANE), F32),
        compiler_params=_cparams(("parallel",)),
        name="win_sample",
    )(tab, q8, wseq)


def _outproj_kernel(x_ref, og_ref, oc_ref, os_ref, ow_ref, sm_ref, wg_ref, wn_ref, n2_ref,
                    hres_ref, xn_ref):
    gates = jax.nn.sigmoid(sm_ref[...])
    acc = x_ref[...] + _dot(og_ref[...].astype(BF16), wg_ref[...])
    for h in range(NSA_HEADS):
        c = 8 + 3 * h
        on = (gates[:, c:c + 1] * oc_ref[h] + gates[:, c + 1:c + 2] * os_ref[h]
              + gates[:, c + 2:c + 3] * ow_ref[h])
        acc = acc + _dot(on.astype(BF16), wn_ref[h])
    hres_ref[...] = acc
    ms = jnp.mean(acc * acc, axis=-1, keepdims=True)
    xn_ref[...] = acc * lax.rsqrt(ms + EPS) * n2_ref[...]


def _out_proj(x, og, ocmp, oslc, owin, sm, wg, wn, norm2, tm):
    b, l, _ = x.shape
    assert l % tm == 0
    seq = lambda w: pl.BlockSpec((None, tm, w), lambda i, j: (i, j, 0))
    heads = pl.BlockSpec((None, NSA_HEADS, tm, LANE), lambda i, j: (i, 0, j, 0))
    full = lambda a: pl.BlockSpec(a.shape, lambda i, j: (0,) * a.ndim)
    return pl.pallas_call(
        _outproj_kernel,
        grid=(b, l // tm),
        in_specs=[seq(D_MODEL), seq(512), heads, heads, heads, seq(LANE), full(wg), full(wn), full(norm2)],
        out_specs=[seq(D_MODEL), seq(D_MODEL)],
        out_shape=[jax.ShapeDtypeStruct((b, l, D_MODEL), F32)] * 2,
        compiler_params=_cparams(("parallel", "parallel")),
        name="out_proj",
    )(x, og, ocmp, oslc, owin, sm, wg, wn, norm2)


def _peer_topk_kernel(x_ref, wqt_ref, keys_ref, cflat_ref, eid_ref, gate_ref, qt_scr, sv_scr, si_scr, top_scr,
                      *, tm):
    qt_scr[...] = _dot_nt(wqt_ref[...], x_ref[...].astype(BF16))
    rows = _iota_f((PEER_NKEYS, tm), 0)
    cflat = jnp.broadcast_to(cflat_ref[...], (PEER_NCAND, tm))

    def head_body(h, carry):
        for c in range(2):
            off = pl.multiple_of(h * (2 * PEER_HALF) + c * PEER_HALF, PEER_HALF)
            qs = qt_scr[pl.ds(off, PEER_HALF), :].astype(BF16)
            s = _dot(keys_ref[h, c], qs)

            def round_body(r, s):
                m = jnp.max(s, axis=0, keepdims=True)
                idx = jnp.min(jnp.where(s == m, rows, float(PEER_NKEYS)), axis=0, keepdims=True)
                sv_scr[c, pl.ds(r, 1), :] = m
                si_scr[c, pl.ds(r, 1), :] = idx
                return jnp.where(rows == idx, REMOVED, s)

            lax.fori_loop(0, PEER_TOPK, round_body, s)
        s1 = sv_scr[0]
        s2 = sv_scr[1]
        i1 = si_scr[0] * float(PEER_NKEYS)
        i2 = si_scr[1]
        cand = [s1[0:1] + s2]
        eidc = [i1[0:1] + i2]
        for a in range(1, 8):
            cand.append(s1[a:a + 1] + s2[0:8])
            eidc.append(i1[a:a + 1] + i2[0:8])
        cand.append(s1[8:16] + s2[0:1])
        eidc.append(i1[8:16] + i2[0:1])
        cand = jnp.where(cflat >= 0.0, jnp.concatenate(cand, axis=0), REMOVED)
        eidc = jnp.concatenate(eidc, axis=0)

        def round2(r, cand):
            m = jnp.max(cand, axis=0, keepdims=True)
            f = jnp.min(jnp.where(cand == m, cflat, 1e9), axis=0, keepdims=True)
            hit = cflat == f
            top_scr[pl.ds(r, 1), :] = m
            eid_ref[h, pl.ds(r, 1), :] = jnp.sum(jnp.where(hit, eidc, 0.0), axis=0, keepdims=True).astype(I32)
            return jnp.where(hit, REMOVED, cand)

        lax.fori_loop(0, PEER_TOPK, round2, cand)
        top = top_scr[...]
        e = jnp.exp(top - jnp.max(top, axis=0, keepdims=True))
        gate_ref[h] = e / jnp.sum(e, axis=0, keepdims=True)
        return carry

    lax.fori_loop(0, PEER_HEADS, head_body, 0)


def _peer_cflat():
    rows = [(0, b) for b in range(16)]
    for a in range(1, 8):
        rows += [(a, b) for b in range(8)]
    rows += [(a, 0) for a in range(8, 16)]
    flat = [a * 16 + b if (a + 1) * (b + 1) <= PEER_TOPK else -1 for a, b in rows]
    assert len(flat) == PEER_NCAND
    return jnp.asarray(np.array(flat, np.float32).reshape(PEER_NCAND, 1))


def _peer_topk(xn, wqt, keys, tm):
    t = xn.shape[0]
    assert t % tm == 0
    cflat = _peer_cflat()
    kern = functools.partial(_peer_topk_kernel, tm=tm)
    full = lambda a: pl.BlockSpec(a.shape, lambda i: (0,) * a.ndim)
    out_spec = pl.BlockSpec((PEER_HEADS, PEER_TOPK, tm), lambda i: (0, 0, i))
    return pl.pallas_call(
        kern,
        grid=(t // tm,),
        in_specs=[pl.BlockSpec((tm, D_MODEL), lambda i: (i, 0)), full(wqt), full(keys), full(cflat)],
        out_specs=[out_spec, out_spec],
        out_shape=[jax.ShapeDtypeStruct((PEER_HEADS, PEER_TOPK, t), I32),
                   jax.ShapeDtypeStruct((PEER_HEADS, PEER_TOPK, t), F32)],
        scratch_shapes=[pltpu.VMEM((D_MODEL, tm), F32), pltpu.VMEM((2, PEER_TOPK, tm), F32),
                        pltpu.VMEM((2, PEER_TOPK, tm), F32), pltpu.VMEM((PEER_TOPK, tm), F32)],
        compiler_params=_cparams(("parallel",)),
        name="peer_topk",
    )(xn, wqt, keys, cflat)


PEER_SLOTS = 3
PEER_CHUNKS = D_MODEL // LANE


def _peer_expert_kernel(eid_ref, x_ref, gate_ref, hres_ref, uv_ref, y_ref, *scratch, tt):
    bufs = scratch[:PEER_SLOTS]
    sem = scratch[PEER_SLOTS]
    ahead = PEER_SLOTS - 1

    def row_copy(e, slot, k):
        return pltpu.make_async_copy(uv_ref.at[e], bufs[slot].at[pl.ds(k * PEER_CHUNKS, PEER_CHUNKS), :],
                                     sem.at[slot])

    def issue(t, slot, part=None):
        base = t * PEER_SEL
        i, n = (0, 1) if part is None else part
        for k in range(i * PEER_SEL // n, (i + 1) * PEER_SEL // n):
            row_copy(eid_ref[base + k], slot, k).start()

    def wait(slot):
        for k in range(PEER_SEL):
            row_copy(0, slot, k).wait()

    def chunk_words(slot, c):
        return bufs[slot][pl.ds(c, PEER_SEL, stride=PEER_CHUNKS), :]

    def evaluate(t, slot, prefetch):
        wait(slot)
        nparts = 2 * PEER_CHUNKS

        def start_part(i):
            if prefetch:
                issue(t + ahead, (slot + ahead) % PEER_SLOTS, (i, nparts))

        xb = x_ref[pl.ds(t, 1), :].astype(BF16).astype(F32)
        acc = jnp.zeros((PEER_SEL, LANE), F32)
        for c in range(PEER_CHUNKS):
            start_part(c)
            uf = pltpu.bitcast(chunk_words(slot, c) & jnp.int32(-65536), F32)
            acc = acc + uf * xb[:, c * LANE:(c + 1) * LANE]
        act = jnp.sum(acc.T, axis=0, keepdims=True)
        w = gate_ref[pl.ds(t, 1), :] * (0.5 * act * (1.0 + lax.erf(act * SQRT_HALF)))
        wb = w.astype(BF16).astype(F32)
        w2 = jnp.broadcast_to(wb, (LANE, PEER_SEL)).T
        outs = []
        for c in range(PEER_CHUNKS):
            start_part(PEER_CHUNKS + c)
            vf = pltpu.bitcast(chunk_words(slot, c) << 16, F32)
            outs.append(jnp.sum(vf * w2, axis=0, keepdims=True))
        y_ref[pl.ds(t, 1), :] = hres_ref[pl.ds(t, 1), :] + jnp.concatenate(outs, axis=1)

    for t0 in range(ahead):
        issue(t0, t0)
    n_main = (tt - ahead) // PEER_SLOTS * PEER_SLOTS

    def body(i, carry):
        for r in range(PEER_SLOTS):
            evaluate(i * PEER_SLOTS + r, r, True)
        return carry

    lax.fori_loop(0, n_main // PEER_SLOTS, body, 0)
    for t0 in range(n_main, tt):
        evaluate(t0, t0 % PEER_SLOTS, t0 + ahead < tt)


def _pack_expert_rows(u, v):
    bits = lambda a: lax.bitcast_convert_type(a.astype(BF16), jnp.uint16).astype(jnp.uint32)
    words = (bits(u) << 16) | bits(v)
    return lax.bitcast_convert_type(words, I32).reshape(-1, PEER_CHUNKS, LANE)


def _peer_experts(eid_flat, xn, gate, hres, uv, tt):
    t = xn.shape[0]
    assert t % tt == 0 and tt >= PEER_SLOTS
    kern = functools.partial(_peer_expert_kernel, tt=tt)
    row = lambda w: pl.BlockSpec((tt, w), lambda i: (i, 0))
    return pl.pallas_call(
        kern,
        grid=(t // tt,),
        in_specs=[pl.BlockSpec((tt * PEER_SEL,), lambda i: (i,), memory_space=pltpu.SMEM),
                  row(D_MODEL), row(PEER_SEL), row(D_MODEL),
                  pl.BlockSpec(memory_space=pl.ANY)],
        out_specs=row(D_MODEL),
        out_shape=jax.ShapeDtypeStruct((t, D_MODEL), F32),
        scratch_shapes=[pltpu.VMEM((PEER_SEL * PEER_CHUNKS, LANE), I32) for _ in range(PEER_SLOTS)]
        + [pltpu.SemaphoreType.DMA((PEER_SLOTS,))],
        compiler_params=_cparams(("arbitrary",)),
        name="peer_experts",
    )(eid_flat, xn, gate, hres, uv)


def _prep_params(norm1, w_in, gdn_conv_w, gdn_a_log, gdn_dt_bias, gdn_norm, nsa_q_norm, nsa_k_norm,
                 rel_bias, w_o, norm2, peer_wq, peer_subkeys, peer_u, peer_v):
    w = w_in[0]
    p = {}
    p["norm1"] = norm1[0][None]
    p["wm"] = jnp.concatenate([w[:, :OFF_B], w[:, OFF_NQ:OFF_NG]], axis=1).astype(BF16)
    p["ws"] = jnp.concatenate([w[:, OFF_B:OFF_NQ], w[:, OFF_NG:], jnp.zeros((D_MODEL, LANE - 32), F32)],
                              axis=1).astype(BF16)
    li = jnp.arange(LANE)
    p["seg"] = (li[:, None] // NSA_HD == li[None, :] // NSA_HD).astype(BF16)
    p["qg"] = jnp.tile(nsa_q_norm[0], 2)[None]
    p["kg"] = jnp.tile(nsa_k_norm[0], (1, 2))
    p["conv_w"] = gdn_conv_w[0]
    p["al_vec"] = jnp.zeros((1, LANE), F32).at[0, 4:8].set(gdn_a_log[0])
    p["dtb_vec"] = jnp.zeros((1, LANE), F32).at[0, 4:8].set(gdn_dt_bias[0])
    p["gnorm"] = gdn_norm[0][None]
    p["tab"] = rel_bias
    wo = w_o[0]
    p["wg"] = wo[:512].astype(BF16)
    wn = jnp.zeros((NSA_HEADS, LANE, D_MODEL), F32)
    for h in range(NSA_HEADS):
        g = h // NSA_P
        wn = wn.at[h, g * NSA_HD:(g + 1) * NSA_HD].set(wo[512 + h * NSA_HD:512 + (h + 1) * NSA_HD])
    p["wn"] = wn.astype(BF16)
    p["norm2"] = norm2[0][None]
    p["wqt"] = peer_wq[0].T.astype(BF16)
    p["keys"] = peer_subkeys[0].astype(BF16)
    p["uv"] = _pack_expert_rows(peer_u[0], peer_v[0])
    return p


def _perm_avg_matrix(n_blocks, n_rows):
    half = n_blocks // 2
    r = jnp.arange(n_blocks)
    blk = jnp.where(r < half, 2 * r, 2 * (r - half) + 1)
    return (jnp.arange(n_rows)[None, :] // CMP_BLOCK == blk[:, None]).astype(BF16)


def _token_mixer_tail(p, x, og, ocmp, oslc, owin, sm, tm_out, tm_topk, tt):
    b, l, _ = x.shape
    hres, xn2 = _out_proj(x, og, ocmp, oslc, owin, sm, p["wg"], p["wn"], p["norm2"], tm_out)
    t = b * l
    xn2 = xn2.reshape(t, D_MODEL)
    eid, gate = _peer_topk(xn2, p["wqt"], p["keys"], tm_topk)
    eid_flat = eid.reshape(PEER_SEL, t).T.reshape(t * PEER_SEL)
    gate_tok = gate.reshape(PEER_SEL, t).T
    y = _peer_experts(eid_flat, xn2, gate_tok, hres.reshape(t, D_MODEL), p["uv"], tt)
    return y.reshape(b, l, D_MODEL)


def kernel(x_prompt, x_sample, cache_nsa_kv, page_table, state_win_kv, state_conv, state_gdn, norm1, w_in, gdn_conv_w, gdn_a_log, gdn_dt_bias, gdn_norm, nsa_q_norm, nsa_k_norm, rel_bias, w_o, norm2, peer_wq, peer_subkeys, peer_u, peer_v):
    assert w_in.shape[0] == 1, "single layer"
    p = _prep_params(norm1, w_in, gdn_conv_w, gdn_a_log, gdn_dt_bias, gdn_norm, nsa_q_norm, nsa_k_norm,
                     rel_bias, w_o, norm2, peer_wq, peer_subkeys, peer_u, peer_v)
    b, l, _ = x_prompt.shape
    db, lq, _ = x_sample.shape
    n_pages = page_table.shape[1]
    past_len = n_pages * PAGE
    wb = state_win_kv.shape[2]
    assert cache_nsa_kv.shape[2] == PAGE and l >= WINDOW and l >= 3 and lq >= 3
    assert lq < CMP_BLOCK and lq <= 8 and wb == WINDOW and past_len >= wb

    tp = b * l
    tm = 256 if tp % 256 == 0 else LANE
    hc, z, sm, qn, kvn, winn = _in_proj(x_prompt.reshape(tp, D_MODEL), p["norm1"], p["wm"], p["ws"],
                                        p["seg"], p["qg"], p["kg"], tm)
    hc3, z3, sm3 = hc.reshape(b, l, GDN_CH), z.reshape(b, l, 512), sm.reshape(b, l, LANE)
    qn3, kvn3, winn3 = qn.reshape(b, l, 512), kvn.reshape(b, l, 512), winn.reshape(b, l, 256)
    og, gdn_p = _gdn(hc3, z3, sm3, p["conv_w"], p["al_vec"], p["dtb_vec"], p["gnorm"],
                     jnp.zeros((b, 3, GDN_CH), F32), jnp.zeros((b, GDN_HEADS, 128, 128), F32),
                     GDN_CHUNK, l)
    nc = l // CMP_BLOCK
    ocmp, sel = _cmp_prompt(p["tab"], qn3, kvn3, _perm_avg_matrix(nc, l), tm)
    oslc = _flash_prompt(p["tab"], qn3, kvn3, 2, 3, sel, LANE, False)
    owin = _flash_prompt(p["tab"], qn3, winn3, 0, 1, None, LANE, True)
    y_prompt = _token_mixer_tail(p, x_prompt, og, ocmp, oslc, owin, sm3, tm, tm, 64)

    ts = db * lq
    hc_s, z_s, sm_s, qn_s, kvn_s, winn_s = _in_proj(x_sample.reshape(ts, D_MODEL), p["norm1"], p["wm"], p["ws"],
                                                    p["seg"], p["qg"], p["kg"], min(ts, 256))
    pad_rows = lambda a, n: jnp.pad(a.reshape(db, lq, a.shape[-1]), ((0, 0), (0, n - lq), (0, 0)))
    og_s, gdn_s = _gdn(pad_rows(hc_s, GDN_CHUNK), pad_rows(z_s, GDN_CHUNK), pad_rows(sm_s, GDN_CHUNK),
                       p["conv_w"], p["al_vec"], p["dtb_vec"], p["gnorm"], state_conv[0], state_gdn[0],
                       GDN_CHUNK, lq)
    cache3 = cache_nsa_kv[0].reshape(cache_nsa_kv.shape[1], PAGE, 512)
    q8 = pad_rows(qn_s, 8)
    avg = _perm_avg_matrix(4 * PAGES_PER_STEP, PAGES_PER_STEP * PAGE)
    ocmp_s, sel_s = _cmp_sample(page_table, p["tab"], cache3, q8, avg, past_len)
    nsb_s = past_len // SEL_BLOCK
    emat_s = (jnp.arange(nsb_s)[:, None] == jnp.arange(past_len)[None, :] // SEL_BLOCK).astype(BF16)
    oslc_s = _slc_sample(page_table, p["tab"], cache3, q8, pad_rows(kvn_s, PAGE), sel_s, emat_s, past_len)
    wseq = jnp.concatenate([state_win_kv[0].reshape(db, wb, 256), pad_rows(winn_s, LANE)], axis=1)
    owin_s = _win_sample(p["tab"], q8, wseq, past_len, wb, lq)
    flat_heads = lambda o: o[:, :, :lq].transpose(1, 0, 2, 3).reshape(1, NSA_HEADS, ts, LANE)
    y_sample = _token_mixer_tail(p, x_sample.reshape(1, ts, D_MODEL), og_s[:, :lq].reshape(1, ts, 512),
                                 flat_heads(ocmp_s), flat_heads(oslc_s), flat_heads(owin_s),
                                 sm_s.reshape(1, ts, LANE), min(ts, 256), min(ts, 256), 64)

    kv_tail = (4, NSA_GROUPS, NSA_HD)
    return (y_prompt,
            y_sample.reshape(db, lq, D_MODEL),
            kvn.reshape((1, b, l) + kv_tail),
            winn3[:, l - WINDOW:].reshape(1, b, WINDOW, 2, NSA_GROUPS, NSA_HD),
            hc3[:, l - 3:][None],
            gdn_p[None],
            kvn_s.reshape((1, db, lq) + kv_tail),
            wseq[:, lq:lq + wb].reshape(1, db, wb, 2, NSA_GROUPS, NSA_HD),
            hc_s.reshape(db, lq, GDN_CH)[:, lq - 3:][None],
            gdn_s[None])
```

```python
import functools
import math

import numpy as np
import jax
import jax.numpy as jnp
from jax import lax
from jax.experimental import pallas as pl
from jax.experimental.pallas import tpu as pltpu

F32 = jnp.float32
BF16 = jnp.bfloat16
I32 = jnp.int32

D_MODEL = 1024
EPS = 1e-6
NEG_INF = -1e30
FORCE_SCORE = 1e4
GDN_HEADS = 4
GDN_DK = 128
GDN_CHUNK = 64
GDN_CH = 1536
NSA_HEADS = 8
NSA_GROUPS = 2
NSA_P = 4
NSA_HD = 64
CMP_BLOCK = 32
SEL_BLOCK = 64
SEL_TOPK = 16
WINDOW = 512
REL_BUCKETS = 32
PAGE = 128
PAGES_PER_STEP = 4
PEER_HEADS = 8
PEER_NKEYS = 128
PEER_TOPK = 16
PEER_HALF = 64
PEER_SEL = PEER_HEADS * PEER_TOPK
PEER_NCAND = 80
OFF_B = 2048
OFF_NQ = 2056
OFF_NG = 3336
LANE = 128
VMEM_LIMIT = 56 * 1024 * 1024
REMOVED = -3.0e38
SQRT_HALF = 0.7071067811865476


def _cparams(sem):
    return pltpu.CompilerParams(dimension_semantics=sem, vmem_limit_bytes=VMEM_LIMIT)


def _dot(a, b):
    return jnp.dot(a, b, preferred_element_type=F32)


def _dot_nt(a, b):
    return lax.dot_general(a, b, (((1,), (1,)), ((), ())), preferred_element_type=F32)


def _dot_tn(a, b):
    return lax.dot_general(a, b, (((0,), (0,)), ((), ())), preferred_element_type=F32)


def _split2(x):
    hi = x.astype(BF16)
    lo = (x - hi.astype(F32)).astype(BF16)
    return hi, lo


def _split3(x):
    hi = x.astype(BF16)
    r = x - hi.astype(F32)
    mid = r.astype(BF16)
    lo = (r - mid.astype(F32)).astype(BF16)
    return hi, mid, lo


def _dot_exact_lhs(m01, x):
    hi, mid, lo = _split3(x)
    return _dot(m01, hi) + (_dot(m01, mid) + _dot(m01, lo))


def _mm3(a, b):
    ah, al = _split2(a)
    bh, bl = _split2(b)
    return _dot(ah, bh) + (_dot(ah, bl) + _dot(al, bh))


def _iota_f(shape, axis):
    return lax.broadcasted_iota(I32, shape, axis).astype(F32)


def _rel_bucket(dist):
    d = jnp.maximum(dist, 0)
    df = jnp.maximum(d, 1).astype(F32)
    large = 16 + (jnp.log(df / 16.0) / math.log(128.0) * 16.0).astype(I32)
    large = jnp.minimum(large, REL_BUCKETS - 1)
    return jnp.where(d < 16, d, large)


def _bias_from_bucket(bucket, tab_ref, head, lo=0, hi=REL_BUCKETS - 1):
    if isinstance(lo, int) and isinstance(hi, int):
        b = jnp.zeros(bucket.shape, F32)
        for k in range(lo, hi + 1):
            b = jnp.where(bucket == k, tab_ref[k, head], b)
        return b

    def body(k, b):
        return jnp.where(bucket == k, tab_ref[k, head], b)

    return lax.fori_loop(lo, hi + 1, body, jnp.zeros(bucket.shape, F32))


def _topk_mask(s, k, axis):
    n = s.shape[axis]
    ids = _iota_f(s.shape, axis)
    sel = jnp.zeros(s.shape, F32)
    for _ in range(k):
        m = jnp.max(s, axis=axis, keepdims=True)
        idx = jnp.min(jnp.where(s == m, ids, float(n)), axis=axis, keepdims=True)
        hit = ids == idx
        sel = jnp.where(hit, 1.0, sel)
        s = jnp.where(hit, REMOVED, s)
    return sel


def _seg_rmsnorm(v, gain, seg):
    sq = v * v
    hi, lo = _split2(sq)
    ssum = _dot(hi, seg) + _dot(lo, seg)
    return v * lax.rsqrt(ssum * (1.0 / NSA_HD) + EPS) * gain


def _inproj_kernel(x_ref, g1_ref, wm_ref, ws_ref, seg_ref, qg_ref, kg_ref,
                   hc_ref, z_ref, sm_ref, q_ref, kv_ref, win_ref):
    x = x_ref[...]
    ms = jnp.mean(x * x, axis=-1, keepdims=True)
    xn = (x * lax.rsqrt(ms + EPS) * g1_ref[...]).astype(BF16)
    h = _dot(xn, wm_ref[...])
    sm_ref[...] = _dot(xn, ws_ref[...])
    hc_ref[...] = h[:, :GDN_CH]
    z_ref[...] = h[:, GDN_CH:2048]
    seg = seg_ref[...]
    qg = qg_ref[...]
    kg = kg_ref[...]
    for i in range(4):
        q_ref[:, i * LANE:(i + 1) * LANE] = _seg_rmsnorm(h[:, 2048 + i * LANE:2048 + (i + 1) * LANE], qg, seg)
    kv_ref[:, 0:128] = _seg_rmsnorm(h[:, 2560:2688], kg[0:1], seg)
    kv_ref[:, 128:256] = h[:, 2688:2816]
    kv_ref[:, 256:384] = _seg_rmsnorm(h[:, 2816:2944], kg[1:2], seg)
    kv_ref[:, 384:512] = h[:, 2944:3072]
    win_ref[:, 0:128] = _seg_rmsnorm(h[:, 3072:3200], kg[2:3], seg)
    win_ref[:, 128:256] = h[:, 3200:3328]


def _in_proj(x, norm1, wm, ws, seg, qg, kg, tm):
    t = x.shape[0]
    assert t % tm == 0
    row = lambda w: pl.BlockSpec((tm, w), lambda i: (i, 0))
    full = lambda a: pl.BlockSpec(a.shape, lambda i: (0,) * a.ndim)
    widths = (GDN_CH, 512, LANE, 512, 512, 256)
    return pl.pallas_call(
        _inproj_kernel,
        grid=(t // tm,),
        in_specs=[row(D_MODEL), full(norm1), full(wm), full(ws), full(seg), full(qg), full(kg)],
        out_specs=[row(w) for w in widths],
        out_shape=[jax.ShapeDtypeStruct((t, w), F32) for w in widths],
        compiler_params=_cparams(("parallel",)),
        name="in_proj",
    )(x, norm1, wm, ws, seg, qg, kg)


def _inv_unit_lower(lmat, c):
    ri = lax.broadcasted_iota(I32, (c, c), 0)
    ci = lax.broadcasted_iota(I32, (c, c), 1)
    eye = jnp.where(ri == ci, 1.0, 0.0).astype(F32)
    n = -lmat
    p = eye + n
    m = _mm3(n, n)
    span = 2
    while True:
        p = p + _mm3(p, m)
        span *= 2
        if span >= c:
            break
        m = _mm3(m, m)
    return p


def _gdn_kernel(hc_ref, z_ref, sm_ref, cw_ref, al_ref, dtb_ref, gn_ref, conv0_ref, s0_ref,
                og_ref, sfin_ref, xbuf, s_scr, *, c, l_valid, n_chunks):
    ci = pl.program_id(1)

    @pl.when(ci == 0)
    def _():
        xbuf[0:8, :] = jnp.zeros((8, GDN_CH), F32)
        xbuf[5:8, :] = conv0_ref[...]
        s_scr[...] = s0_ref[...]

    xbuf[8:8 + c, :] = hc_ref[...]
    w = cw_ref[...]
    y = (xbuf[5:5 + c, :] * w[0:1] + xbuf[6:6 + c, :] * w[1:2]
         + xbuf[7:7 + c, :] * w[2:3] + xbuf[8:8 + c, :] * w[3:4])
    tail = xbuf[5 + c:8 + c, :]
    xbuf[5:8, :] = tail
    y = y * jax.nn.sigmoid(y)

    sm = sm_ref[...]
    rowid = ci * c + lax.broadcasted_iota(I32, (c, 1), 0)
    rvalid = rowid < l_valid
    beta_all = jnp.where(rvalid, jax.nn.sigmoid(sm), 0.0)
    sp_in = sm + dtb_ref[...]
    softplus = jnp.maximum(sp_in, 0.0) + jnp.log1p(jnp.exp(-jnp.abs(sp_in)))
    g_all = jnp.where(rvalid, -jnp.exp(al_ref[...]) * softplus, 0.0)

    ri = lax.broadcasted_iota(I32, (c, c), 0)
    cj = lax.broadcasted_iota(I32, (c, c), 1)
    causal = ri >= cj
    strict = ri > cj
    tri = jnp.where(causal, 1.0, 0.0).astype(BF16)
    triu = jnp.where(ri <= cj, 1.0, 0.0).astype(BF16)
    g_hi, g_mid, g_lo = _split3(g_all)
    gcum = _dot(tri, g_hi) + (_dot(tri, g_mid) + _dot(tri, g_lo))
    gcum_t = _dot_tn(g_hi, triu) + (_dot_tn(g_mid, triu) + _dot_tn(g_lo, triu))
    gn = gn_ref[...]

    for h in range(GDN_HEADS):
        qh = y[:, h * 128:(h + 1) * 128]
        kh = y[:, 512 + h * 128:512 + (h + 1) * 128]
        vh = y[:, 1024 + h * 128:1024 + (h + 1) * 128]
        qh = qh * lax.rsqrt(jnp.sum(qh * qh, axis=-1, keepdims=True) + EPS) * (GDN_DK ** -0.5)
        kh = kh * lax.rsqrt(jnp.sum(kh * kh, axis=-1, keepdims=True) + EPS)
        qh = jnp.where(rvalid, qh, 0.0)
        kh = jnp.where(rvalid, kh, 0.0)
        vh = jnp.where(rvalid, vh, 0.0)
        beta = beta_all[:, h:h + 1]
        gc = gcum[:, 4 + h:5 + h]
        gct = gcum_t[4 + h:5 + h, :]
        diff = gc - gct
        decay = jnp.where(causal, jnp.exp(jnp.where(causal, diff, 0.0)), 0.0)
        kb = kh * beta
        khb = kh.astype(BF16)
        lmat = jnp.where(strict, _dot_nt(kb.astype(BF16), khb) * decay, 0.0)
        tmat = _inv_unit_lower(lmat, c).astype(BF16)
        eg = jnp.exp(gc)
        u = _dot(tmat, (vh * beta).astype(BF16))
        wmat = _dot(tmat, (kb * eg).astype(BF16))
        a_intra = _dot_nt(qh.astype(BF16), khb) * decay
        gc_last = gc[c - 1:c, :]
        q_dec = qh * eg
        k_dec = kh * jnp.exp(gc_last - gc)
        g_last = jnp.exp(gc_last)
        s = s_scr[h]
        sb = s.astype(BF16)
        v_new = u - _dot(wmat.astype(BF16), sb)
        vnb = v_new.astype(BF16)
        o = _dot(q_dec.astype(BF16), sb) + _dot(a_intra.astype(BF16), vnb)
        s_scr[h] = s * g_last + _dot_tn(k_dec.astype(BF16), vnb)
        o = o * lax.rsqrt(jnp.mean(o * o, axis=-1, keepdims=True) + EPS) * gn
        zh = z_ref[:, h * 128:(h + 1) * 128]
        og_ref[:, h * 128:(h + 1) * 128] = o * (zh * jax.nn.sigmoid(zh))

    @pl.when(ci == n_chunks - 1)
    def _():
        sfin_ref[...] = s_scr[...]


def _gdn(hc, z, sm, conv_w, al_vec, dtb_vec, gnorm, conv0, s0, c, l_valid):
    b, lp, _ = hc.shape
    assert lp % c == 0
    n_chunks = lp // c
    full = lambda a: pl.BlockSpec(a.shape, lambda i, j: (0,) * a.ndim)
    seq = lambda w: pl.BlockSpec((None, c, w), lambda i, j: (i, j, 0))
    kern = functools.partial(_gdn_kernel, c=c, l_valid=l_valid, n_chunks=n_chunks)
    return pl.pallas_call(
        kern,
        grid=(b, n_chunks),
        in_specs=[seq(GDN_CH), seq(512), seq(LANE), full(conv_w), full(al_vec), full(dtb_vec), full(gnorm),
                  pl.BlockSpec((None, 3, GDN_CH), lambda i, j: (i, 0, 0)),
                  pl.BlockSpec((None, GDN_HEADS, 128, 128), lambda i, j: (i, 0, 0, 0))],
        out_specs=[seq(512), pl.BlockSpec((None, GDN_HEADS, 128, 128), lambda i, j: (i, 0, 0, 0))],
        out_shape=[jax.ShapeDtypeStruct((b, lp, 512), F32),
                   jax.ShapeDtypeStruct((b, GDN_HEADS, 128, 128), F32)],
        scratch_shapes=[pltpu.VMEM((c + 8, GDN_CH), F32), pltpu.VMEM((GDN_HEADS, 128, 128), F32)],
        compiler_params=_cparams(("parallel", "arbitrary")),
        name="gdn",
    )(hc, z, sm, conv_w, al_vec, dtb_vec, gnorm, conv0, s0)


def _head_q128(q, h):
    g = h // NSA_P
    piece = q[:, (h // 2) * LANE:(h // 2 + 1) * LANE]
    lane = lax.broadcasted_iota(I32, piece.shape, 1)
    keep = (lane >= NSA_HD) if h % 2 == 1 else (lane < NSA_HD)
    qm = jnp.where(keep, piece, 0.0)
    if h % 2 != g:
        qm = pltpu.roll(qm, NSA_HD, 1)
    return qm


def _select_scores(imp, qpos, nsb):
    score = imp[:, :nsb] + imp[:, nsb:]
    j = lax.broadcasted_iota(I32, score.shape, 1)
    cur = qpos // SEL_BLOCK
    forced = (j == 0) | (j == cur) | (j == cur - 1)
    future = j * SEL_BLOCK > qpos
    return jnp.where(future, -1.0, jnp.where(forced, FORCE_SCORE, score))


def _cmp_attention(tab_ref, q, kcv, qpos, ocmp_ref):
    nc = kcv.shape[0]
    nsb = nc // 2
    kc = kcv[:, 0:128].astype(BF16)
    vc = kcv[:, 128:256].astype(BF16)
    lane = lax.broadcasted_iota(I32, (1, nc), 1)
    blk = jnp.where(lane < nsb, 2 * lane, 2 * (lane - nsb) + 1)
    dist = qpos - (blk * CMP_BLOCK + CMP_BLOCK - 1)
    valid = dist >= 0
    bucket = _rel_bucket(dist)
    scores = []
    for g in range(NSA_GROUPS):
        imp = jnp.zeros((q.shape[0], nc), F32)
        for p in range(NSA_P):
            h = g * NSA_P + p
            qm = _head_q128(q, h).astype(BF16)
            logits = _dot_nt(qm, kc) * (NSA_HD ** -0.5) + _bias_from_bucket(bucket, tab_ref, h)
            l = jnp.where(valid, logits, NEG_INF)
            m = jnp.max(l, axis=-1, keepdims=True)
            pr = jnp.where(valid, jnp.exp(l - m), 0.0)
            pr = pr / jnp.maximum(jnp.sum(pr, axis=-1, keepdims=True), 1e-30)
            ocmp_ref[h] = _dot(pr.astype(BF16), vc)
            imp = imp + pr
        scores.append(_select_scores(imp, qpos, nsb))
    return scores


def _flash_init(m_scr, l_scr, acc_scr):
    m_scr[...] = jnp.full(m_scr.shape, NEG_INF, F32)
    l_scr[...] = jnp.zeros(l_scr.shape, F32)
    acc_scr[...] = jnp.zeros(acc_scr.shape, F32)


def _cmp_prompt_kernel(tab_ref, q_ref, kv_ref, mavg_ref, ocmp_ref, sel_ref, kc_scr, *, tq, nsb):
    qi = pl.program_id(1)

    @pl.when(qi == 0)
    def _():
        kc_scr[...] = _dot_exact_lhs(mavg_ref[...], kv_ref[...]) * (1.0 / CMP_BLOCK)

    qpos = qi * tq + lax.broadcasted_iota(I32, (tq, 1), 0)
    scores = _cmp_attention(tab_ref, q_ref[...], kc_scr[...], qpos, ocmp_ref)
    st = jnp.concatenate(scores, axis=1).T
    k = min(SEL_TOPK, nsb)
    sel_ref[...] = jnp.concatenate([_topk_mask(st[:nsb], k, 0), _topk_mask(st[nsb:], k, 0)], axis=0)


def _cmp_prompt(tab, qn, kvn, mavg, tq):
    b, l, _ = qn.shape
    nc = l // CMP_BLOCK
    assert l % tq == 0 and l % SEL_BLOCK == 0
    kern = functools.partial(_cmp_prompt_kernel, tq=tq, nsb=nc // 2)
    return pl.pallas_call(
        kern,
        grid=(b, l // tq),
        in_specs=[pl.BlockSpec(memory_space=pltpu.SMEM),
                  pl.BlockSpec((None, tq, 512), lambda i, j: (i, j, 0)),
                  pl.BlockSpec((None, l, 256), lambda i, j: (i, 0, 0)),
                  pl.BlockSpec(mavg.shape, lambda i, j: (0, 0))],
        out_specs=[pl.BlockSpec((None, NSA_HEADS, tq, LANE), lambda i, j: (i, 0, j, 0)),
                   pl.BlockSpec((None, nc, tq), lambda i, j: (i, 0, j))],
        out_shape=[jax.ShapeDtypeStruct((b, NSA_HEADS, l, LANE), F32),
                   jax.ShapeDtypeStruct((b, nc, l), F32)],
        scratch_shapes=[pltpu.VMEM((nc, 256), F32)],
        compiler_params=_cparams(("parallel", "arbitrary")),
        name="cmp_prompt",
    )(tab, qn, kvn, mavg)


def _flash_prompt_kernel(tab_ref, q_ref, k_ref, v_ref, *rest, t, windowed, nsb, n_dist):
    if windowed:
        o_ref, bias_scr, m_scr, l_scr, acc_scr = rest
    else:
        sel_ref, o_ref, bias_scr, m_scr, l_scr, acc_scr = rest
    qi = pl.program_id(1)
    lane_i = lax.broadcasted_iota(I32, (1, t), 1)
    sub_j = lax.broadcasted_iota(I32, (t, 1), 0)

    @pl.when((pl.program_id(0) == 0) & (qi == 0))
    def _():
        def build(d, carry):
            bucket = _rel_bucket(d * t + lane_i - sub_j)
            for h in range(NSA_HEADS):
                bias_scr[h, d] = _bias_from_bucket(bucket, tab_ref, h)
            return carry

        lax.fori_loop(0, n_dist, build, 0)

    q = q_ref[...]
    qts = [_head_q128(q, h).T.astype(BF16) for h in range(NSA_HEADS)]
    qpos = qi * t + lane_i
    _flash_init(m_scr, l_scr, acc_scr)
    k_lo = jnp.maximum(qi - WINDOW // t, 0) if windowed else 0

    def body(ki, carry):
        k0 = pl.multiple_of(ki * t, t)
        kt = k_ref[pl.ds(k0, t), :].astype(BF16)
        vtt = v_ref[pl.ds(k0, t), :].T.astype(BF16)
        dist = qpos - (k0 + sub_j)
        ok = dist >= 0
        if windowed:
            ok = ok & (dist <= WINDOW)
        for g in range(NSA_GROUPS):
            if windowed:
                valid = ok
            else:
                blk = g * nsb + ki * (t // SEL_BLOCK)
                rows = [sel_ref[pl.ds(blk + r, 1), :] for r in range(t // SEL_BLOCK)]
                selm = rows[-1]
                for r in range(t // SEL_BLOCK - 2, -1, -1):
                    selm = jnp.where(sub_j < (r + 1) * SEL_BLOCK, rows[r], selm)
                valid = ok & (selm > 0.5)
            for p in range(NSA_P):
                h = g * NSA_P + p
                s = _dot(kt, qts[h]) * (NSA_HD ** -0.5) + bias_scr[h, qi - ki]
                l = jnp.where(valid, s, NEG_INF)
                m_old = m_scr[h]
                m_new = jnp.maximum(m_old, jnp.max(l, axis=0, keepdims=True))
                pr = jnp.where(valid, jnp.exp(l - m_new), 0.0)
                alpha = jnp.exp(m_old - m_new)
                l_scr[h] = alpha * l_scr[h] + jnp.sum(pr, axis=0, keepdims=True)
                acc_scr[h] = alpha * acc_scr[h] + _dot(vtt, pr.astype(BF16))
                m_scr[h] = m_new
        return carry

    lax.fori_loop(k_lo, qi + 1, body, 0)
    for h in range(NSA_HEADS):
        o_ref[h] = (acc_scr[h] / jnp.maximum(l_scr[h], 1e-30)).T


def _flash_prompt(tab, qn, kv_arr, k_blk, v_blk, sel, t, windowed):
    b, l, _ = qn.shape
    assert l % t == 0 and t == LANE
    n_dist = WINDOW // t + 1 if windowed else l // t
    kern = functools.partial(_flash_prompt_kernel, t=t, windowed=windowed, nsb=l // SEL_BLOCK, n_dist=n_dist)
    in_specs = [pl.BlockSpec(memory_space=pltpu.SMEM),
                pl.BlockSpec((None, t, 512), lambda i, j: (i, j, 0)),
                pl.BlockSpec((None, l, LANE), lambda i, j: (i, 0, k_blk)),
                pl.BlockSpec((None, l, LANE), lambda i, j: (i, 0, v_blk))]
    args = [tab, qn, kv_arr, kv_arr]
    if not windowed:
        in_specs += [pl.BlockSpec((None, sel.shape[1], t), lambda i, j: (i, 0, j))]
        args += [sel]
    return pl.pallas_call(
        kern,
        grid=(b, l // t),
        in_specs=in_specs,
        out_specs=pl.BlockSpec((None, NSA_HEADS, t, LANE), lambda i, j: (i, 0, j, 0)),
        out_shape=jax.ShapeDtypeStruct((b, NSA_HEADS, l, LANE), F32),
        scratch_shapes=[pltpu.VMEM((NSA_HEADS, n_dist, t, t), F32),
                        pltpu.VMEM((NSA_HEADS, 1, t), F32), pltpu.VMEM((NSA_HEADS, 1, t), F32),
                        pltpu.VMEM((NSA_HEADS, LANE, t), F32)],
        compiler_params=_cparams(("arbitrary", "arbitrary")),
        name="win_prompt" if windowed else "slc_prompt",
    )(*args)


def _cmp_sample_kernel(pt_ref, tab_ref, p0_ref, p1_ref, p2_ref, p3_ref, q_ref, avg_ref,
                       ocmp_ref, sel_ref, kc_scr, *, n_steps, past_len):
    del pt_ref
    st = pl.program_id(1)
    half = n_steps * 2 * PAGES_PER_STEP
    pages = jnp.concatenate([p0_ref[...], p1_ref[...], p2_ref[...], p3_ref[...]], axis=0)
    means = _dot_exact_lhs(avg_ref[...], pages) * (1.0 / CMP_BLOCK)
    off = pl.multiple_of(st * 8, 8)
    kc_scr[pl.ds(off, 8), :] = means[0:8]
    kc_scr[pl.ds(half + off, 8), :] = means[8:16]

    @pl.when(st == n_steps - 1)
    def _():
        qpos = past_len + lax.broadcasted_iota(I32, (8, 1), 0)
        scores = _cmp_attention(tab_ref, q_ref[...], kc_scr[...], qpos, ocmp_ref)
        k = min(SEL_TOPK, half + 1) - 1
        for g in range(NSA_GROUPS):
            sel_ref[g] = _topk_mask(scores[g], k, 1)


def _page_spec(r, lane_blk):
    return pl.BlockSpec((None, PAGE, 256), lambda i, j, pt: (pt[i, j * PAGES_PER_STEP + r], 0, lane_blk))


def _cmp_sample(page_table, tab, cache, q8, avg, past_len):
    db, n_pages = page_table.shape
    assert n_pages % PAGES_PER_STEP == 0
    n_steps = n_pages // PAGES_PER_STEP
    nsb = past_len // SEL_BLOCK
    kern = functools.partial(_cmp_sample_kernel, n_steps=n_steps, past_len=past_len)
    grid_spec = pltpu.PrefetchScalarGridSpec(
        num_scalar_prefetch=1,
        grid=(db, n_steps),
        in_specs=[pl.BlockSpec(memory_space=pltpu.SMEM)]
        + [_page_spec(r, 0) for r in range(PAGES_PER_STEP)]
        + [pl.BlockSpec((None, 8, 512), lambda i, j, pt: (i, 0, 0)),
           pl.BlockSpec(avg.shape, lambda i, j, pt: (0, 0))],
        out_specs=[pl.BlockSpec((None, NSA_HEADS, 8, LANE), lambda i, j, pt: (i, 0, 0, 0)),
                   pl.BlockSpec((None, NSA_GROUPS, 8, nsb), lambda i, j, pt: (i, 0, 0, 0))],
        scratch_shapes=[pltpu.VMEM((2 * nsb, 256), F32)],
    )
    return pl.pallas_call(
        kern,
        grid_spec=grid_spec,
        out_shape=[jax.ShapeDtypeStruct((db, NSA_HEADS, 8, LANE), F32),
                   jax.ShapeDtypeStruct((db, NSA_GROUPS, 8, nsb), F32)],
        compiler_params=_cparams(("parallel", "arbitrary")),
        name="cmp_sample",
    )(page_table, tab, cache, cache, cache, cache, q8, avg)


def _slc_sample_kernel(pt_ref, tab_ref, q_ref, new_ref, sel_ref, e_ref, cache_ref, o_ref, kv_buf, sem,
                       *, n_pages, past_len):
    b = pl.program_id(0)

    def page_copy(pg):
        return pltpu.make_async_copy(cache_ref.at[pt_ref[b, pg], :, pl.ds(256, 256)],
                                     kv_buf.at[pl.ds(pg * PAGE, PAGE), :], sem.at[0])

    for pg in range(n_pages):
        page_copy(pg).start()

    q = q_ref[...]
    qpos = past_len + lax.broadcasted_iota(I32, (8, 1), 0)
    dist_p = qpos - lax.broadcasted_iota(I32, (1, past_len), 1)
    bucket_p = _rel_bucket(dist_p)
    dist_n = qpos - (past_len + lax.broadcasted_iota(I32, (1, PAGE), 1))
    ok_n = dist_n >= 0
    bucket_n = _rel_bucket(dist_n)
    knew = new_ref[:, 0:128].astype(BF16)
    vnew = new_ref[:, 128:256].astype(BF16)

    for pg in range(n_pages):
        page_copy(pg).wait()

    kall = kv_buf[:, 0:128].astype(BF16)
    vall = kv_buf[:, 128:256].astype(BF16)
    for g in range(NSA_GROUPS):
        qg = jnp.concatenate([_head_q128(q, g * NSA_P + p) for p in range(NSA_P)], axis=0).astype(BF16)
        s_p = _dot_nt(qg, kall) * (NSA_HD ** -0.5)
        s_n = _dot_nt(qg, knew) * (NSA_HD ** -0.5)
        mask_p = _dot(sel_ref[g].astype(BF16), e_ref[...]) > 0.5
        pps, pns, dens = [], [], []
        for p in range(NSA_P):
            h = g * NSA_P + p
            rows = slice(8 * p, 8 * p + 8)
            lp = jnp.where(mask_p, s_p[rows] + _bias_from_bucket(bucket_p, tab_ref, h), NEG_INF)
            ln = jnp.where(ok_n, s_n[rows] + _bias_from_bucket(bucket_n, tab_ref, h), NEG_INF)
            m = jnp.maximum(jnp.max(lp, axis=-1, keepdims=True), jnp.max(ln, axis=-1, keepdims=True))
            pp = jnp.where(mask_p, jnp.exp(lp - m), 0.0)
            pn = jnp.where(ok_n, jnp.exp(ln - m), 0.0)
            dens.append(jnp.sum(pp, axis=-1, keepdims=True) + jnp.sum(pn, axis=-1, keepdims=True))
            pps.append(pp.astype(BF16))
            pns.append(pn.astype(BF16))
        o = _dot(jnp.concatenate(pps, axis=0), vall) + _dot(jnp.concatenate(pns, axis=0), vnew)
        for p in range(NSA_P):
            o_ref[g * NSA_P + p] = o[8 * p:8 * p + 8] / jnp.maximum(dens[p], 1e-30)


def _slc_sample(page_table, tab, cache, q8, new_kv, sel, emat, past_len):
    db, n_pages = page_table.shape
    kern = functools.partial(_slc_sample_kernel, n_pages=n_pages, past_len=past_len)
    grid_spec = pltpu.PrefetchScalarGridSpec(
        num_scalar_prefetch=1,
        grid=(db,),
        in_specs=[pl.BlockSpec(memory_space=pltpu.SMEM),
                  pl.BlockSpec((None, 8, 512), lambda i, pt: (i, 0, 0)),
                  pl.BlockSpec((None, PAGE, 256), lambda i, pt: (i, 0, 1)),
                  pl.BlockSpec((None,) + sel.shape[1:], lambda i, pt: (i, 0, 0, 0)),
                  pl.BlockSpec(emat.shape, lambda i, pt: (0, 0)),
                  pl.BlockSpec(memory_space=pl.ANY)],
        out_specs=pl.BlockSpec((None, NSA_HEADS, 8, LANE), lambda i, pt: (i, 0, 0, 0)),
        scratch_shapes=[pltpu.VMEM((n_pages * PAGE, 256), F32), pltpu.SemaphoreType.DMA((1,))],
    )
    return pl.pallas_call(
        kern,
        grid_spec=grid_spec,
        out_shape=jax.ShapeDtypeStruct((db, NSA_HEADS, 8, LANE), F32),
        compiler_params=_cparams(("arbitrary",)),
        name="slc_sample",
    )(page_table, tab, q8, new_kv, sel, emat, cache)


def _win_sample_kernel(tab_ref, q_ref, w_ref, o_ref, *, past_len, wb, lq):
    q = q_ref[...]
    qpos = past_len + lax.broadcasted_iota(I32, (8, 1), 0)
    wseq = w_ref[...]
    n = wseq.shape[0]
    kw = wseq[:, 0:128].astype(BF16)
    vw = wseq[:, 128:256].astype(BF16)
    j = lax.broadcasted_iota(I32, (1, n), 1)
    kpos = past_len - wb + j
    dist = qpos - kpos
    valid = (dist >= 0) & (dist <= WINDOW) & (kpos >= 0) & (j < wb + lq)
    bucket = _rel_bucket(dist)
    for h in range(NSA_HEADS):
        qm = _head_q128(q, h).astype(BF16)
        s = _dot_nt(qm, kw) * (NSA_HD ** -0.5) + _bias_from_bucket(bucket, tab_ref, h)
        l = jnp.where(valid, s, NEG_INF)
        m = jnp.max(l, axis=-1, keepdims=True)
        pr = jnp.where(valid, jnp.exp(l - m), 0.0)
        pr = pr / jnp.maximum(jnp.sum(pr, axis=-1, keepdims=True), 1e-30)
        o_ref[h] = _dot(pr.astype(BF16), vw)


def _win_sample(tab, q8, wseq, past_len, wb, lq):
    db, n, _ = wseq.shape
    kern = functools.partial(_win_sample_kernel, past_len=past_len, wb=wb, lq=lq)
    return pl.pallas_call(
        kern,
        grid=(db,),
        in_specs=[pl.BlockSpec(memory_space=pltpu.SMEM),
                  pl.BlockSpec((None, 8, 512), lambda i: (i, 0, 0)),
                  pl.BlockSpec((None, n, 256), lambda i: (i, 0, 0))],
        out_specs=pl.BlockSpec((None, NSA_HEADS, 8, LANE), lambda i: (i, 0, 0, 0)),
        out_shape=jax.ShapeDtypeStruct((db, NSA_HEADS, 8, LANE), F32),
        compiler_params=_cparams(("parallel",)),
        name="win_sample",
    )(tab, q8, wseq)


def _outproj_kernel(x_ref, og_ref, oc_ref, os_ref, ow_ref, sm_ref, wg_ref, wn_ref, n2_ref,
                    hres_ref, xn_ref):
    gates = jax.nn.sigmoid(sm_ref[...])
    acc = x_ref[...] + _dot(og_ref[...].astype(BF16), wg_ref[...])
    for h in range(NSA_HEADS):
        c = 8 + 3 * h
        on = (gates[:, c:c + 1] * oc_ref[h] + gates[:, c + 1:c + 2] * os_ref[h]
              + gates[:, c + 2:c + 3] * ow_ref[h])
        acc = acc + _dot(on.astype(BF16), wn_ref[h])
    hres_ref[...] = acc
    ms = jnp.mean(acc * acc, axis=-1, keepdims=True)
    xn_ref[...] = acc * lax.rsqrt(ms + EPS) * n2_ref[...]


def _out_proj(x, og, ocmp, oslc, owin, sm, wg, wn, norm2, tm):
    b, l, _ = x.shape
    assert l % tm == 0
    seq = lambda w: pl.BlockSpec((None, tm, w), lambda i, j: (i, j, 0))
    heads = pl.BlockSpec((None, NSA_HEADS, tm, LANE), lambda i, j: (i, 0, j, 0))
    full = lambda a: pl.BlockSpec(a.shape, lambda i, j: (0,) * a.ndim)
    return pl.pallas_call(
        _outproj_kernel,
        grid=(b, l // tm),
        in_specs=[seq(D_MODEL), seq(512), heads, heads, heads, seq(LANE), full(wg), full(wn), full(norm2)],
        out_specs=[seq(D_MODEL), seq(D_MODEL)],
        out_shape=[jax.ShapeDtypeStruct((b, l, D_MODEL), F32)] * 2,
        compiler_params=_cparams(("parallel", "parallel")),
        name="out_proj",
    )(x, og, ocmp, oslc, owin, sm, wg, wn, norm2)


def _peer_topk_kernel(x_ref, wqt_ref, keys_ref, cflat_ref, eid_ref, gate_ref, qt_scr, sv_scr, si_scr, top_scr,
                      *, tm):
    qt_scr[...] = _dot_nt(wqt_ref[...], x_ref[...].astype(BF16))
    rows = _iota_f((PEER_NKEYS, tm), 0)
    cflat = jnp.broadcast_to(cflat_ref[...], (PEER_NCAND, tm))

    def head_body(h, carry):
        for c in range(2):
            off = pl.multiple_of(h * (2 * PEER_HALF) + c * PEER_HALF, PEER_HALF)
            qs = qt_scr[pl.ds(off, PEER_HALF), :].astype(BF16)
            s = _dot(keys_ref[h, c], qs)

            def round_body(r, s):
                m = jnp.max(s, axis=0, keepdims=True)
                idx = jnp.min(jnp.where(s == m, rows, float(PEER_NKEYS)), axis=0, keepdims=True)
                sv_scr[c, pl.ds(r, 1), :] = m
                si_scr[c, pl.ds(r, 1), :] = idx
                return jnp.where(rows == idx, REMOVED, s)

            lax.fori_loop(0, PEER_TOPK, round_body, s)
        s1 = sv_scr[0]
        s2 = sv_scr[1]
        i1 = si_scr[0] * float(PEER_NKEYS)
        i2 = si_scr[1]
        cand = [s1[0:1] + s2]
        eidc = [i1[0:1] + i2]
        for a in range(1, 8):
            cand.append(s1[a:a + 1] + s2[0:8])
            eidc.append(i1[a:a + 1] + i2[0:8])
        cand.append(s1[8:16] + s2[0:1])
        eidc.append(i1[8:16] + i2[0:1])
        cand = jnp.where(cflat >= 0.0, jnp.concatenate(cand, axis=0), REMOVED)
        eidc = jnp.concatenate(eidc, axis=0)

        def round2(r, cand):
            m = jnp.max(cand, axis=0, keepdims=True)
            f = jnp.min(jnp.where(cand == m, cflat, 1e9), axis=0, keepdims=True)
            hit = cflat == f
            top_scr[pl.ds(r, 1), :] = m
            eid_ref[h, pl.ds(r, 1), :] = jnp.sum(jnp.where(hit, eidc, 0.0), axis=0, keepdims=True).astype(I32)
            return jnp.where(hit, REMOVED, cand)

        lax.fori_loop(0, PEER_TOPK, round2, cand)
        top = top_scr[...]
        e = jnp.exp(top - jnp.max(top, axis=0, keepdims=True))
        gate_ref[h] = e / jnp.sum(e, axis=0, keepdims=True)
        return carry

    lax.fori_loop(0, PEER_HEADS, head_body, 0)


def _peer_cflat():
    rows = [(0, b) for b in range(16)]
    for a in range(1, 8):
        rows += [(a, b) for b in range(8)]
    rows += [(a, 0) for a in range(8, 16)]
    flat = [a * 16 + b if (a + 1) * (b + 1) <= PEER_TOPK else -1 for a, b in rows]
    assert len(flat) == PEER_NCAND
    return jnp.asarray(np.array(flat, np.float32).reshape(PEER_NCAND, 1))


def _peer_topk(xn, wqt, keys, tm):
    t = xn.shape[0]
    assert t % tm == 0
    cflat = _peer_cflat()
    kern = functools.partial(_peer_topk_kernel, tm=tm)
    full = lambda a: pl.BlockSpec(a.shape, lambda i: (0,) * a.ndim)
    out_spec = pl.BlockSpec((PEER_HEADS, PEER_TOPK, tm), lambda i: (0, 0, i))
    return pl.pallas_call(
        kern,
        grid=(t // tm,),
        in_specs=[pl.BlockSpec((tm, D_MODEL), lambda i: (i, 0)), full(wqt), full(keys), full(cflat)],
        out_specs=[out_spec, out_spec],
        out_shape=[jax.ShapeDtypeStruct((PEER_HEADS, PEER_TOPK, t), I32),
                   jax.ShapeDtypeStruct((PEER_HEADS, PEER_TOPK, t), F32)],
        scratch_shapes=[pltpu.VMEM((D_MODEL, tm), F32), pltpu.VMEM((2, PEER_TOPK, tm), F32),
                        pltpu.VMEM((2, PEER_TOPK, tm), F32), pltpu.VMEM((PEER_TOPK, tm), F32)],
        compiler_params=_cparams(("parallel",)),
        name="peer_topk",
    )(xn, wqt, keys, cflat)


PEER_GROUP = 2
PEER_SLOTS = 8
PEER_CHUNKS = D_MODEL // LANE


def _peer_expert_kernel(eid_ref, x_ref, gate_ref, hres_ref, uv_ref, y_ref, *scratch, tt):
    bufs = scratch[:PEER_SLOTS]
    sem = scratch[PEER_SLOTS]
    ahead = PEER_SLOTS - PEER_GROUP

    def row_copy(e, slot, k):
        return pltpu.make_async_copy(uv_ref.at[e], bufs[slot].at[pl.ds(k * PEER_CHUNKS, PEER_CHUNKS), :],
                                     sem.at[slot])

    def issue(t, slot, part=None):
        base = t * PEER_SEL
        i, n = (0, 1) if part is None else part
        for k in range(i * PEER_SEL // n, (i + 1) * PEER_SEL // n):
            row_copy(eid_ref[base + k], slot, k).start(priority=k % 2)

    def wait(slot):
        for k in range(PEER_SEL):
            row_copy(0, slot, k).wait()

    def chunk_words(slot, c):
        return bufs[slot][pl.ds(c, PEER_SEL, stride=PEER_CHUNKS), :]

    def evaluate(ts, slots, prefetch):
        for slot in slots:
            wait(slot)
        nparts = 2 * PEER_CHUNKS

        def start_part(i):
            if prefetch:
                for t, slot in zip(ts, slots):
                    issue(t + ahead, (slot + ahead) % PEER_SLOTS, (i, nparts))

        xbs = [x_ref[pl.ds(t, 1), :].astype(BF16).astype(F32) for t in ts]
        accs = [jnp.zeros((PEER_SEL, LANE), F32) for _ in ts]
        for c in range(PEER_CHUNKS):
            start_part(c)
            for j, slot in enumerate(slots):
                uf = pltpu.bitcast(chunk_words(slot, c) & jnp.int32(-65536), F32)
                accs[j] = accs[j] + uf * xbs[j][:, c * LANE:(c + 1) * LANE]
        w2s = []
        for j, t in enumerate(ts):
            act = jnp.sum(accs[j].T, axis=0, keepdims=True)
            w = gate_ref[pl.ds(t, 1), :] * (0.5 * act * (1.0 + lax.erf(act * SQRT_HALF)))
            wb = w.astype(BF16).astype(F32)
            w2s.append(jnp.broadcast_to(wb, (LANE, PEER_SEL)).T)
        outs = [[] for _ in ts]
        for c in range(PEER_CHUNKS):
            start_part(PEER_CHUNKS + c)
            for j, slot in enumerate(slots):
                vf = pltpu.bitcast(chunk_words(slot, c) << 16, F32)
                outs[j].append(jnp.sum(vf * w2s[j], axis=0, keepdims=True))
        for j, t in enumerate(ts):
            y_ref[pl.ds(t, 1), :] = hres_ref[pl.ds(t, 1), :] + jnp.concatenate(outs[j], axis=1)

    for t0 in range(ahead):
        issue(t0, t0)
    n_main = (tt - ahead) // PEER_SLOTS * PEER_SLOTS

    def body(i, carry):
        for r in range(0, PEER_SLOTS, PEER_GROUP):
            slots = list(range(r, r + PEER_GROUP))
            evaluate([i * PEER_SLOTS + s for s in slots], slots, True)
        return carry

    lax.fori_loop(0, n_main // PEER_SLOTS, body, 0)
    for t0 in range(n_main, tt, PEER_GROUP):
        ts = list(range(t0, t0 + PEER_GROUP))
        evaluate(ts, [t % PEER_SLOTS for t in ts], t0 + ahead < tt)


def _pack_expert_rows(u, v):
    bits = lambda a: lax.bitcast_convert_type(a.astype(BF16), jnp.uint16).astype(jnp.uint32)
    words = (bits(u) << 16) | bits(v)
    return lax.bitcast_convert_type(words, I32).reshape(-1, PEER_CHUNKS, LANE)


def _peer_experts(eid_flat, xn, gate, hres, uv, tt):
    t = xn.shape[0]
    assert t % tt == 0 and tt >= PEER_SLOTS
    kern = functools.partial(_peer_expert_kernel, tt=tt)
    row = lambda w: pl.BlockSpec((tt, w), lambda i: (i, 0))
    return pl.pallas_call(
        kern,
        grid=(t // tt,),
        in_specs=[pl.BlockSpec((tt * PEER_SEL,), lambda i: (i,), memory_space=pltpu.SMEM),
                  row(D_MODEL), row(PEER_SEL), row(D_MODEL),
                  pl.BlockSpec(memory_space=pl.ANY)],
        out_specs=row(D_MODEL),
        out_shape=jax.ShapeDtypeStruct((t, D_MODEL), F32),
        scratch_shapes=[pltpu.VMEM((PEER_SEL * PEER_CHUNKS, LANE), I32) for _ in range(PEER_SLOTS)]
        + [pltpu.SemaphoreType.DMA((PEER_SLOTS,))],
        compiler_params=_cparams(("arbitrary",)),
        name="peer_experts",
    )(eid_flat, xn, gate, hres, uv)


def _prep_params(norm1, w_in, gdn_conv_w, gdn_a_log, gdn_dt_bias, gdn_norm, nsa_q_norm, nsa_k_norm,
                 rel_bias, w_o, norm2, peer_wq, peer_subkeys, peer_u, peer_v):
    w = w_in[0]
    p = {}
    p["norm1"] = norm1[0][None]
    p["wm"] = jnp.concatenate([w[:, :OFF_B], w[:, OFF_NQ:OFF_NG]], axis=1).astype(BF16)
    p["ws"] = jnp.concatenate([w[:, OFF_B:OFF_NQ], w[:, OFF_NG:], jnp.zeros((D_MODEL, LANE - 32), F32)],
                              axis=1).astype(BF16)
    li = jnp.arange(LANE)
    p["seg"] = (li[:, None] // NSA_HD == li[None, :] // NSA_HD).astype(BF16)
    p["qg"] = jnp.tile(nsa_q_norm[0], 2)[None]
    p["kg"] = jnp.tile(nsa_k_norm[0], (1, 2))
    p["conv_w"] = gdn_conv_w[0]
    p["al_vec"] = jnp.zeros((1, LANE), F32).at[0, 4:8].set(gdn_a_log[0])
    p["dtb_vec"] = jnp.zeros((1, LANE), F32).at[0, 4:8].set(gdn_dt_bias[0])
    p["gnorm"] = gdn_norm[0][None]
    p["tab"] = rel_bias
    wo = w_o[0]
    p["wg"] = wo[:512].astype(BF16)
    wn = jnp.zeros((NSA_HEADS, LANE, D_MODEL), F32)
    for h in range(NSA_HEADS):
        g = h // NSA_P
        wn = wn.at[h, g * NSA_HD:(g + 1) * NSA_HD].set(wo[512 + h * NSA_HD:512 + (h + 1) * NSA_HD])
    p["wn"] = wn.astype(BF16)
    p["norm2"] = norm2[0][None]
    p["wqt"] = peer_wq[0].T.astype(BF16)
    p["keys"] = peer_subkeys[0].astype(BF16)
    p["uv"] = _pack_expert_rows(peer_u[0], peer_v[0])
    return p


def _perm_avg_matrix(n_blocks, n_rows):
    half = n_blocks // 2
    r = jnp.arange(n_blocks)
    blk = jnp.where(r < half, 2 * r, 2 * (r - half) + 1)
    return (jnp.arange(n_rows)[None, :] // CMP_BLOCK == blk[:, None]).astype(BF16)


def _token_mixer_tail(p, x, og, ocmp, oslc, owin, sm, tm_out, tm_topk, tt):
    b, l, _ = x.shape
    hres, xn2 = _out_proj(x, og, ocmp, oslc, owin, sm, p["wg"], p["wn"], p["norm2"], tm_out)
    t = b * l
    xn2 = xn2.reshape(t, D_MODEL)
    eid, gate = _peer_topk(xn2, p["wqt"], p["keys"], tm_topk)
    eid_flat = eid.reshape(PEER_SEL, t).T.reshape(t * PEER_SEL)
    gate_tok = gate.reshape(PEER_SEL, t).T
    y = _peer_experts(eid_flat, xn2, gate_tok, hres.reshape(t, D_MODEL), p["uv"], tt)
    return y.reshape(b, l, D_MODEL)


def kernel(x_prompt, x_sample, cache_nsa_kv, page_table, state_win_kv, state_conv, state_gdn, norm1, w_in, gdn_conv_w, gdn_a_log, gdn_dt_bias, gdn_norm, nsa_q_norm, nsa_k_norm, rel_bias, w_o, norm2, peer_wq, peer_subkeys, peer_u, peer_v):
    assert w_in.shape[0] == 1, "single layer"
    p = _prep_params(norm1, w_in, gdn_conv_w, gdn_a_log, gdn_dt_bias, gdn_norm, nsa_q_norm, nsa_k_norm,
                     rel_bias, w_o, norm2, peer_wq, peer_subkeys, peer_u, peer_v)
    b, l, _ = x_prompt.shape
    db, lq, _ = x_sample.shape
    n_pages = page_table.shape[1]
    past_len = n_pages * PAGE
    wb = state_win_kv.shape[2]
    assert cache_nsa_kv.shape[2] == PAGE and l >= WINDOW and l >= 3 and lq >= 3
    assert lq < CMP_BLOCK and lq <= 8 and wb == WINDOW and past_len >= wb

    tp = b * l
    tm = 256 if tp % 256 == 0 else LANE
    hc, z, sm, qn, kvn, winn = _in_proj(x_prompt.reshape(tp, D_MODEL), p["norm1"], p["wm"], p["ws"],
                                        p["seg"], p["qg"], p["kg"], tm)
    hc3, z3, sm3 = hc.reshape(b, l, GDN_CH), z.reshape(b, l, 512), sm.reshape(b, l, LANE)
    qn3, kvn3, winn3 = qn.reshape(b, l, 512), kvn.reshape(b, l, 512), winn.reshape(b, l, 256)
    og, gdn_p = _gdn(hc3, z3, sm3, p["conv_w"], p["al_vec"], p["dtb_vec"], p["gnorm"],
                     jnp.zeros((b, 3, GDN_CH), F32), jnp.zeros((b, GDN_HEADS, 128, 128), F32),
                     GDN_CHUNK, l)
    nc = l // CMP_BLOCK
    ocmp, sel = _cmp_prompt(p["tab"], qn3, kvn3, _perm_avg_matrix(nc, l), tm)
    oslc = _flash_prompt(p["tab"], qn3, kvn3, 2, 3, sel, LANE, False)
    owin = _flash_prompt(p["tab"], qn3, winn3, 0, 1, None, LANE, True)
    y_prompt = _token_mixer_tail(p, x_prompt, og, ocmp, oslc, owin, sm3, tm, tm, 64)

    ts = db * lq
    hc_s, z_s, sm_s, qn_s, kvn_s, winn_s = _in_proj(x_sample.reshape(ts, D_MODEL), p["norm1"], p["wm"], p["ws"],
                                                    p["seg"], p["qg"], p["kg"], min(ts, 256))
    pad_rows = lambda a, n: jnp.pad(a.reshape(db, lq, a.shape[-1]), ((0, 0), (0, n - lq), (0, 0)))
    og_s, gdn_s = _gdn(pad_rows(hc_s, GDN_CHUNK), pad_rows(z_s, GDN_CHUNK), pad_rows(sm_s, GDN_CHUNK),
                       p["conv_w"], p["al_vec"], p["dtb_vec"], p["gnorm"], state_conv[0], state_gdn[0],
                       GDN_CHUNK, lq)
    cache3 = cache_nsa_kv[0].reshape(cache_nsa_kv.shape[1], PAGE, 512)
    q8 = pad_rows(qn_s, 8)
    avg = _perm_avg_matrix(4 * PAGES_PER_STEP, PAGES_PER_STEP * PAGE)
    ocmp_s, sel_s = _cmp_sample(page_table, p["tab"], cache3, q8, avg, past_len)
    nsb_s = past_len // SEL_BLOCK
    emat_s = (jnp.arange(nsb_s)[:, None] == jnp.arange(past_len)[None, :] // SEL_BLOCK).astype(BF16)
    oslc_s = _slc_sample(page_table, p["tab"], cache3, q8, pad_rows(kvn_s, PAGE), sel_s, emat_s, past_len)
    wseq = jnp.concatenate([state_win_kv[0].reshape(db, wb, 256), pad_rows(winn_s, LANE)], axis=1)
    owin_s = _win_sample(p["tab"], q8, wseq, past_len, wb, lq)
    flat_heads = lambda o: o[:, :, :lq].transpose(1, 0, 2, 3).reshape(1, NSA_HEADS, ts, LANE)
    y_sample = _token_mixer_tail(p, x_sample.reshape(1, ts, D_MODEL), og_s[:, :lq].reshape(1, ts, 512),
                                 flat_heads(ocmp_s), flat_heads(oslc_s), flat_heads(owin_s),
                                 sm_s.reshape(1, ts, LANE), min(ts, 256), min(ts, 256), 64)

    kv_tail = (4, NSA_GROUPS, NSA_HD)
    return (y_prompt,
            y_sample.reshape(db, lq, D_MODEL),
            kvn.reshape((1, b, l) + kv_tail),
            winn3[:, l - WINDOW:].reshape(1, b, WINDOW, 2, NSA_GROUPS, NSA_HD),
            hc3[:, l - 3:][None],
            gdn_p[None],
            kvn_s.reshape((1, db, lq) + kv_tail),
            wseq[:, lq:lq + wb].reshape(1, db, wb, 2, NSA_GROUPS, NSA_HD),
            hc_s.reshape(db, lq, GDN_CH)[:, lq - 3:][None],
            gdn_s[None])
```

```python
import functools
import math

import numpy as np
import jax
import jax.numpy as jnp
from jax import lax
from jax.experimental import pallas as pl
from jax.experimental.pallas import tpu as pltpu

F32 = jnp.float32
BF16 = jnp.bfloat16
I32 = jnp.int32

D_MODEL = 1024
EPS = 1e-6
NEG_INF = -1e30
FORCE_SCORE = 1e4
GDN_HEADS = 4
GDN_DK = 128
GDN_CHUNK = 64
GDN_CH = 1536
NSA_HEADS = 8
NSA_GROUPS = 2
NSA_P = 4
NSA_HD = 64
CMP_BLOCK = 32
SEL_BLOCK = 64
SEL_TOPK = 16
WINDOW = 512
REL_BUCKETS = 32
PAGE = 128
PEER_HEADS = 8
PEER_NKEYS = 128
PEER_TOPK = 16
PEER_HALF = 64
PEER_SEL = PEER_HEADS * PEER_TOPK
PEER_NCAND = 80
OFF_B = 2048
OFF_NQ = 2056
OFF_NG = 3336
LANE = 128
VMEM_LIMIT = 56 * 1024 * 1024
REMOVED = -3.0e38
SQRT_HALF = 0.7071067811865476


def _cparams(sem):
    return pltpu.CompilerParams(dimension_semantics=sem, vmem_limit_bytes=VMEM_LIMIT)


def _dot(a, b):
    return jnp.dot(a, b, preferred_element_type=F32)


def _dot_nt(a, b):
    return lax.dot_general(a, b, (((1,), (1,)), ((), ())), preferred_element_type=F32)


def _dot_tn(a, b):
    return lax.dot_general(a, b, (((0,), (0,)), ((), ())), preferred_element_type=F32)


def _split2(x):
    hi = x.astype(BF16)
    lo = (x - hi.astype(F32)).astype(BF16)
    return hi, lo


def _split3(x):
    hi = x.astype(BF16)
    r = x - hi.astype(F32)
    mid = r.astype(BF16)
    lo = (r - mid.astype(F32)).astype(BF16)
    return hi, mid, lo


def _dot_exact_lhs(m01, x):
    hi, mid, lo = _split3(x)
    return _dot(m01, hi) + (_dot(m01, mid) + _dot(m01, lo))


def _mm3(a, b):
    ah, al = _split2(a)
    bh, bl = _split2(b)
    return _dot(ah, bh) + (_dot(ah, bl) + _dot(al, bh))


def _iota_f(shape, axis):
    return lax.broadcasted_iota(I32, shape, axis).astype(F32)


def _rel_bucket(dist):
    d = jnp.maximum(dist, 0)
    df = jnp.maximum(d, 1).astype(F32)
    large = 16 + (jnp.log(df / 16.0) / math.log(128.0) * 16.0).astype(I32)
    large = jnp.minimum(large, REL_BUCKETS - 1)
    return jnp.where(d < 16, d, large)


def _bias_from_bucket(bucket, tab_ref, head, lo=0, hi=REL_BUCKETS - 1):
    if isinstance(lo, int) and isinstance(hi, int):
        b = jnp.zeros(bucket.shape, F32)
        for k in range(lo, hi + 1):
            b = jnp.where(bucket == k, tab_ref[k, head], b)
        return b

    def body(k, b):
        return jnp.where(bucket == k, tab_ref[k, head], b)

    return lax.fori_loop(lo, hi + 1, body, jnp.zeros(bucket.shape, F32))


def _topk_mask(s, k, axis):
    n = s.shape[axis]
    ids = _iota_f(s.shape, axis)
    sel = jnp.zeros(s.shape, F32)
    for _ in range(k):
        m = jnp.max(s, axis=axis, keepdims=True)
        idx = jnp.min(jnp.where(s == m, ids, float(n)), axis=axis, keepdims=True)
        hit = ids == idx
        sel = jnp.where(hit, 1.0, sel)
        s = jnp.where(hit, REMOVED, s)
    return sel


def _seg_rmsnorm(v, gain, seg):
    sq = v * v
    hi, lo = _split2(sq)
    ssum = _dot(hi, seg) + _dot(lo, seg)
    return v * lax.rsqrt(ssum * (1.0 / NSA_HD) + EPS) * gain


def _inproj_kernel(x_ref, g1_ref, wm_ref, ws_ref, seg_ref, qg_ref, kg_ref,
                   hc_ref, z_ref, sm_ref, q_ref, kv_ref, win_ref):
    x = x_ref[...]
    ms = jnp.mean(x * x, axis=-1, keepdims=True)
    xn = (x * lax.rsqrt(ms + EPS) * g1_ref[...]).astype(BF16)
    h = _dot(xn, wm_ref[...])
    sm_ref[...] = _dot(xn, ws_ref[...])
    hc_ref[...] = h[:, :GDN_CH]
    z_ref[...] = h[:, GDN_CH:2048]
    seg = seg_ref[...]
    qg = qg_ref[...]
    kg = kg_ref[...]
    for i in range(4):
        q_ref[:, i * LANE:(i + 1) * LANE] = _seg_rmsnorm(h[:, 2048 + i * LANE:2048 + (i + 1) * LANE], qg, seg)
    kv_ref[:, 0:128] = _seg_rmsnorm(h[:, 2560:2688], kg[0:1], seg)
    kv_ref[:, 128:256] = h[:, 2688:2816]
    kv_ref[:, 256:384] = _seg_rmsnorm(h[:, 2816:2944], kg[1:2], seg)
    kv_ref[:, 384:512] = h[:, 2944:3072]
    win_ref[:, 0:128] = _seg_rmsnorm(h[:, 3072:3200], kg[2:3], seg)
    win_ref[:, 128:256] = h[:, 3200:3328]


def _in_proj(x, norm1, wm, ws, seg, qg, kg, tm):
    t = x.shape[0]
    assert t % tm == 0
    row = lambda w: pl.BlockSpec((tm, w), lambda i: (i, 0))
    full = lambda a: pl.BlockSpec(a.shape, lambda i: (0,) * a.ndim)
    widths = (GDN_CH, 512, LANE, 512, 512, 256)
    return pl.pallas_call(
        _inproj_kernel,
        grid=(t // tm,),
        in_specs=[row(D_MODEL), full(norm1), full(wm), full(ws), full(seg), full(qg), full(kg)],
        out_specs=[row(w) for w in widths],
        out_shape=[jax.ShapeDtypeStruct((t, w), F32) for w in widths],
        compiler_params=_cparams(("parallel",)),
        name="in_proj",
    )(x, norm1, wm, ws, seg, qg, kg)


def _inv_unit_lower(lmat, c):
    ri = lax.broadcasted_iota(I32, lmat.shape, 0)
    ci = lax.broadcasted_iota(I32, lmat.shape, 1)
    eye = jnp.where(ri == ci, 1.0, 0.0).astype(F32)
    n = -lmat
    p = eye + n
    m = _mm3(n, n)
    span = 2
    while True:
        p = p + _mm3(p, m)
        span *= 2
        if span >= c:
            break
        m = _mm3(m, m)
    return p


def _gdn_kernel(hc_ref, z_ref, sm_ref, cw_ref, al_ref, dtb_ref, gn_ref, conv0_ref, s0_ref,
                og_ref, sfin_ref, xbuf, s_scr, *, c, l_valid, n_chunks):
    ci = pl.program_id(1)

    @pl.when(ci == 0)
    def _():
        xbuf[0:8, :] = jnp.zeros((8, GDN_CH), F32)
        xbuf[5:8, :] = conv0_ref[...]
        s_scr[...] = s0_ref[...]

    xbuf[8:8 + c, :] = hc_ref[...]
    w = cw_ref[...]
    y = (xbuf[5:5 + c, :] * w[0:1] + xbuf[6:6 + c, :] * w[1:2]
         + xbuf[7:7 + c, :] * w[2:3] + xbuf[8:8 + c, :] * w[3:4])
    tail = xbuf[5 + c:8 + c, :]
    xbuf[5:8, :] = tail
    y = y * jax.nn.sigmoid(y)

    sm = sm_ref[...]
    rowid = ci * c + lax.broadcasted_iota(I32, (c, 1), 0)
    rvalid = rowid < l_valid
    beta_all = jnp.where(rvalid, jax.nn.sigmoid(sm), 0.0)
    sp_in = sm + dtb_ref[...]
    softplus = jnp.maximum(sp_in, 0.0) + jnp.log1p(jnp.exp(-jnp.abs(sp_in)))
    g_all = jnp.where(rvalid, -jnp.exp(al_ref[...]) * softplus, 0.0)

    ri = lax.broadcasted_iota(I32, (c, c), 0)
    cj = lax.broadcasted_iota(I32, (c, c), 1)
    tri = jnp.where(ri >= cj, 1.0, 0.0).astype(BF16)
    triu = jnp.where(ri <= cj, 1.0, 0.0).astype(BF16)
    g_hi, g_mid, g_lo = _split3(g_all)
    gcum = _dot(tri, g_hi) + (_dot(tri, g_mid) + _dot(tri, g_lo))
    gcum_t = _dot_tn(g_hi, triu) + (_dot_tn(g_mid, triu) + _dot_tn(g_lo, triu))
    gn = gn_ref[...]

    heads = range(GDN_HEADS)
    stack = lambda pieces: jnp.concatenate(pieces, axis=0)

    def l2n(x):
        return jnp.where(rvalid, x * lax.rsqrt(jnp.sum(x * x, axis=-1, keepdims=True) + EPS), 0.0)

    q_s = stack([l2n(y[:, h * 128:(h + 1) * 128]) * (GDN_DK ** -0.5) for h in heads])
    k_s = stack([l2n(y[:, 512 + h * 128:512 + (h + 1) * 128]) for h in heads])
    v_s = stack([jnp.where(rvalid, y[:, 1024 + h * 128:1024 + (h + 1) * 128], 0.0) for h in heads])
    beta_s = stack([beta_all[:, h:h + 1] for h in heads])
    gc_s = stack([gcum[:, 4 + h:5 + h] for h in heads])
    gl_s = stack([jnp.broadcast_to(gcum[c - 1:c, 4 + h:5 + h], (c, 1)) for h in heads])
    gct_s = jnp.concatenate([gcum_t[4 + h:5 + h, :] for h in heads], axis=1)
    r = GDN_HEADS * c
    rr = lax.broadcasted_iota(I32, (r, r), 0)
    rc = lax.broadcasted_iota(I32, (r, r), 1)
    same = (rr // c) == (rc // c)
    causal = same & (rr >= rc)
    strict = same & (rr > rc)
    decay = jnp.where(causal, jnp.exp(jnp.where(causal, gc_s - gct_s, 0.0)), 0.0)
    kb = k_s * beta_s
    ksb = k_s.astype(BF16)
    lmat = jnp.where(strict, _dot_nt(kb.astype(BF16), ksb) * decay, 0.0)
    tmat = _inv_unit_lower(lmat, c).astype(BF16)
    eg = jnp.exp(gc_s)
    u = _dot(tmat, (v_s * beta_s).astype(BF16))
    wmat = _dot(tmat, (kb * eg).astype(BF16)).astype(BF16)
    a_intra = (_dot_nt(q_s.astype(BF16), ksb) * decay).astype(BF16)
    q_dec = (q_s * eg).astype(BF16)
    k_dec = (k_s * jnp.exp(gl_s - gc_s)).astype(BF16)
    rows = lambda a, h: a[h * c:(h + 1) * c]
    sbs = [s_scr[h].astype(BF16) for h in heads]
    v_new = stack([rows(u, h) - _dot(rows(wmat, h), sbs[h]) for h in heads])
    vnb = v_new.astype(BF16)
    o_intra = _dot(a_intra, vnb)
    for h in heads:
        g_last = jnp.exp(gcum[c - 1:c, 4 + h:5 + h])
        s_scr[h] = s_scr[h] * g_last + _dot_tn(rows(k_dec, h), rows(vnb, h))
        o = _dot(rows(q_dec, h), sbs[h]) + rows(o_intra, h)
        o = o * lax.rsqrt(jnp.mean(o * o, axis=-1, keepdims=True) + EPS) * gn
        zh = z_ref[:, h * 128:(h + 1) * 128]
        og_ref[:, h * 128:(h + 1) * 128] = o * (zh * jax.nn.sigmoid(zh))

    @pl.when(ci == n_chunks - 1)
    def _():
        sfin_ref[...] = s_scr[...]


def _gdn(hc, z, sm, conv_w, al_vec, dtb_vec, gnorm, conv0, s0, c, l_valid):
    b, lp, _ = hc.shape
    assert lp % c == 0
    n_chunks = lp // c
    full = lambda a: pl.BlockSpec(a.shape, lambda i, j: (0,) * a.ndim)
    seq = lambda w: pl.BlockSpec((None, c, w), lambda i, j: (i, j, 0))
    kern = functools.partial(_gdn_kernel, c=c, l_valid=l_valid, n_chunks=n_chunks)
    return pl.pallas_call(
        kern,
        grid=(b, n_chunks),
        in_specs=[seq(GDN_CH), seq(512), seq(LANE), full(conv_w), full(al_vec), full(dtb_vec), full(gnorm),
                  pl.BlockSpec((None, 3, GDN_CH), lambda i, j: (i, 0, 0)),
                  pl.BlockSpec((None, GDN_HEADS, 128, 128), lambda i, j: (i, 0, 0, 0))],
        out_specs=[seq(512), pl.BlockSpec((None, GDN_HEADS, 128, 128), lambda i, j: (i, 0, 0, 0))],
        out_shape=[jax.ShapeDtypeStruct((b, lp, 512), F32),
                   jax.ShapeDtypeStruct((b, GDN_HEADS, 128, 128), F32)],
        scratch_shapes=[pltpu.VMEM((c + 8, GDN_CH), F32), pltpu.VMEM((GDN_HEADS, 128, 128), F32)],
        compiler_params=_cparams(("parallel", "arbitrary")),
        name="gdn",
    )(hc, z, sm, conv_w, al_vec, dtb_vec, gnorm, conv0, s0)


def _head_q128(q, h):
    g = h // NSA_P
    piece = q[:, (h // 2) * LANE:(h // 2 + 1) * LANE]
    lane = lax.broadcasted_iota(I32, piece.shape, 1)
    keep = (lane >= NSA_HD) if h % 2 == 1 else (lane < NSA_HD)
    qm = jnp.where(keep, piece, 0.0)
    if h % 2 != g:
        qm = pltpu.roll(qm, NSA_HD, 1)
    return qm


def _select_scores(imp, qpos, nsb):
    score = imp[:, :nsb] + imp[:, nsb:]
    j = lax.broadcasted_iota(I32, score.shape, 1)
    cur = qpos // SEL_BLOCK
    forced = (j == 0) | (j == cur) | (j == cur - 1)
    future = j * SEL_BLOCK > qpos
    return jnp.where(future, -1.0, jnp.where(forced, FORCE_SCORE, score))


def _cmp_attention(tab_ref, q, kcv, qpos, ocmp_ref):
    nc = kcv.shape[0]
    nsb = nc // 2
    kc = kcv[:, 0:128].astype(BF16)
    vc = kcv[:, 128:256].astype(BF16)
    lane = lax.broadcasted_iota(I32, (1, nc), 1)
    blk = jnp.where(lane < nsb, 2 * lane, 2 * (lane - nsb) + 1)
    dist = qpos - (blk * CMP_BLOCK + CMP_BLOCK - 1)
    valid = dist >= 0
    bucket = _rel_bucket(dist)
    scores = []
    for g in range(NSA_GROUPS):
        imp = jnp.zeros((q.shape[0], nc), F32)
        for p in range(NSA_P):
            h = g * NSA_P + p
            qm = _head_q128(q, h).astype(BF16)
            logits = _dot_nt(qm, kc) * (NSA_HD ** -0.5) + _bias_from_bucket(bucket, tab_ref, h)
            l = jnp.where(valid, logits, NEG_INF)
            m = jnp.max(l, axis=-1, keepdims=True)
            pr = jnp.where(valid, jnp.exp(l - m), 0.0)
            pr = pr / jnp.maximum(jnp.sum(pr, axis=-1, keepdims=True), 1e-30)
            ocmp_ref[h] = _dot(pr.astype(BF16), vc)
            imp = imp + pr
        scores.append(_select_scores(imp, qpos, nsb))
    return scores


def _flash_init(m_scr, l_scr, acc_scr):
    m_scr[...] = jnp.full(m_scr.shape, NEG_INF, F32)
    l_scr[...] = jnp.zeros(l_scr.shape, F32)
    acc_scr[...] = jnp.zeros(acc_scr.shape, F32)


def _cmp_prompt_kernel(tab_ref, q_ref, kv_ref, mavg_ref, ocmp_ref, sel_ref, kc_scr, *, tq, nsb):
    qi = pl.program_id(1)

    @pl.when(qi == 0)
    def _():
        kc_scr[...] = _dot_exact_lhs(mavg_ref[...], kv_ref[...]) * (1.0 / CMP_BLOCK)

    qpos = qi * tq + lax.broadcasted_iota(I32, (tq, 1), 0)
    scores = _cmp_attention(tab_ref, q_ref[...], kc_scr[...], qpos, ocmp_ref)
    st = jnp.concatenate(scores, axis=1).T
    k = min(SEL_TOPK, nsb)
    sel_ref[...] = jnp.concatenate([_topk_mask(st[:nsb], k, 0), _topk_mask(st[nsb:], k, 0)], axis=0)


def _cmp_prompt(tab, qn, kvn, mavg, tq):
    b, l, _ = qn.shape
    nc = l // CMP_BLOCK
    assert l % tq == 0 and l % SEL_BLOCK == 0
    kern = functools.partial(_cmp_prompt_kernel, tq=tq, nsb=nc // 2)
    return pl.pallas_call(
        kern,
        grid=(b, l // tq),
        in_specs=[pl.BlockSpec(memory_space=pltpu.SMEM),
                  pl.BlockSpec((None, tq, 512), lambda i, j: (i, j, 0)),
                  pl.BlockSpec((None, l, 256), lambda i, j: (i, 0, 0)),
                  pl.BlockSpec(mavg.shape, lambda i, j: (0, 0))],
        out_specs=[pl.BlockSpec((None, NSA_HEADS, tq, LANE), lambda i, j: (i, 0, j, 0)),
                   pl.BlockSpec((None, nc, tq), lambda i, j: (i, 0, j))],
        out_shape=[jax.ShapeDtypeStruct((b, NSA_HEADS, l, LANE), F32),
                   jax.ShapeDtypeStruct((b, nc, l), F32)],
        scratch_shapes=[pltpu.VMEM((nc, 256), F32)],
        compiler_params=_cparams(("parallel", "arbitrary")),
        name="cmp_prompt",
    )(tab, qn, kvn, mavg)


def _flash_prompt_kernel(tab_ref, q_ref, k_ref, v_ref, *rest, t, windowed, nsb, n_dist):
    if windowed:
        o_ref, bias_scr, m_scr, l_scr, acc_scr = rest
    else:
        sel_ref, o_ref, bias_scr, m_scr, l_scr, acc_scr = rest
    qi = pl.program_id(1)
    lane_i = lax.broadcasted_iota(I32, (1, t), 1)
    sub_j = lax.broadcasted_iota(I32, (t, 1), 0)

    @pl.when((pl.program_id(0) == 0) & (qi == 0))
    def _():
        def build(d, carry):
            bucket = _rel_bucket(d * t + lane_i - sub_j)
            for h in range(NSA_HEADS):
                bias_scr[h, d] = _bias_from_bucket(bucket, tab_ref, h)
            return carry

        lax.fori_loop(0, n_dist, build, 0)

    q = q_ref[...]
    qts = [_head_q128(q, h).T.astype(BF16) for h in range(NSA_HEADS)]
    qpos = qi * t + lane_i
    _flash_init(m_scr, l_scr, acc_scr)
    k_lo = jnp.maximum(qi - WINDOW // t, 0) if windowed else 0

    def body(ki, carry):
        k0 = pl.multiple_of(ki * t, t)
        kt = k_ref[pl.ds(k0, t), :].astype(BF16)
        vtt = v_ref[pl.ds(k0, t), :].T.astype(BF16)
        dist = qpos - (k0 + sub_j)
        ok = dist >= 0
        if windowed:
            ok = ok & (dist <= WINDOW)
        for g in range(NSA_GROUPS):
            if windowed:
                valid = ok
            else:
                blk = g * nsb + ki * (t // SEL_BLOCK)
                rows = [sel_ref[pl.ds(blk + r, 1), :] for r in range(t // SEL_BLOCK)]
                selm = rows[-1]
                for r in range(t // SEL_BLOCK - 2, -1, -1):
                    selm = jnp.where(sub_j < (r + 1) * SEL_BLOCK, rows[r], selm)
                valid = ok & (selm > 0.5)
            for p in range(NSA_P):
                h = g * NSA_P + p
                s = _dot(kt, qts[h]) * (NSA_HD ** -0.5) + bias_scr[h, qi - ki]
                l = jnp.where(valid, s, NEG_INF)
                m_old = m_scr[h]
                m_new = jnp.maximum(m_old, jnp.max(l, axis=0, keepdims=True))
                pr = jnp.where(valid, jnp.exp(l - m_new), 0.0)
                alpha = jnp.exp(m_old - m_new)
                l_scr[h] = alpha * l_scr[h] + jnp.sum(pr, axis=0, keepdims=True)
                acc_scr[h] = alpha * acc_scr[h] + _dot(vtt, pr.astype(BF16))
                m_scr[h] = m_new
        return carry

    lax.fori_loop(k_lo, qi + 1, body, 0)
    for h in range(NSA_HEADS):
        o_ref[h] = (acc_scr[h] / jnp.maximum(l_scr[h], 1e-30)).T


def _flash_prompt(tab, qn, kv_arr, k_blk, v_blk, sel, t, windowed):
    b, l, _ = qn.shape
    assert l % t == 0 and t == LANE
    n_dist = WINDOW // t + 1 if windowed else l // t
    kern = functools.partial(_flash_prompt_kernel, t=t, windowed=windowed, nsb=l // SEL_BLOCK, n_dist=n_dist)
    in_specs = [pl.BlockSpec(memory_space=pltpu.SMEM),
                pl.BlockSpec((None, t, 512), lambda i, j: (i, j, 0)),
                pl.BlockSpec((None, l, LANE), lambda i, j: (i, 0, k_blk)),
                pl.BlockSpec((None, l, LANE), lambda i, j: (i, 0, v_blk))]
    args = [tab, qn, kv_arr, kv_arr]
    if not windowed:
        in_specs += [pl.BlockSpec((None, sel.shape[1], t), lambda i, j: (i, 0, j))]
        args += [sel]
    return pl.pallas_call(
        kern,
        grid=(b, l // t),
        in_specs=in_specs,
        out_specs=pl.BlockSpec((None, NSA_HEADS, t, LANE), lambda i, j: (i, 0, j, 0)),
        out_shape=jax.ShapeDtypeStruct((b, NSA_HEADS, l, LANE), F32),
        scratch_shapes=[pltpu.VMEM((NSA_HEADS, n_dist, t, t), F32),
                        pltpu.VMEM((NSA_HEADS, 1, t), F32), pltpu.VMEM((NSA_HEADS, 1, t), F32),
                        pltpu.VMEM((NSA_HEADS, LANE, t), F32)],
        compiler_params=_cparams(("arbitrary", "arbitrary")),
        name="win_prompt" if windowed else "slc_prompt",
    )(*args)


def _cmp_sample_kernel(pt_ref, tab_ref, *rest, n_steps, pps, past_len):
    del pt_ref
    page_refs = rest[:pps]
    q_ref, avg_ref, ocmp_ref, sel_ref, kc_scr = rest[pps:]
    st = pl.program_id(1)
    nb = 2 * pps
    half = n_steps * nb
    pages = jnp.concatenate([r[...] for r in page_refs], axis=0)
    means = _dot_exact_lhs(avg_ref[...], pages) * (1.0 / CMP_BLOCK)
    off = pl.multiple_of(st * nb, 8)
    kc_scr[pl.ds(off, nb), :] = means[0:nb]
    kc_scr[pl.ds(half + off, nb), :] = means[nb:2 * nb]

    @pl.when(st == n_steps - 1)
    def _():
        qpos = past_len + lax.broadcasted_iota(I32, (8, 1), 0)
        scores = _cmp_attention(tab_ref, q_ref[...], kc_scr[...], qpos, ocmp_ref)
        k = min(SEL_TOPK, half + 1) - 1
        for g in range(NSA_GROUPS):
            sel_ref[g] = _topk_mask(scores[g], k, 1)


def _cmp_sample(page_table, tab, cache, q8, past_len):
    db, n_pages = page_table.shape
    pps = next(p for p in (16, 8, 4) if n_pages % p == 0)
    n_steps = n_pages // pps
    nsb = past_len // SEL_BLOCK
    avg = _perm_avg_matrix(4 * pps, pps * PAGE)
    kern = functools.partial(_cmp_sample_kernel, n_steps=n_steps, pps=pps, past_len=past_len)
    page_spec = lambda r: pl.BlockSpec((None, PAGE, 256), lambda i, j, pt: (pt[i, j * pps + r], 0, 0))
    grid_spec = pltpu.PrefetchScalarGridSpec(
        num_scalar_prefetch=1,
        grid=(db, n_steps),
        in_specs=[pl.BlockSpec(memory_space=pltpu.SMEM)]
        + [page_spec(r) for r in range(pps)]
        + [pl.BlockSpec((None, 8, 512), lambda i, j, pt: (i, 0, 0)),
           pl.BlockSpec(avg.shape, lambda i, j, pt: (0, 0))],
        out_specs=[pl.BlockSpec((None, NSA_HEADS, 8, LANE), lambda i, j, pt: (i, 0, 0, 0)),
                   pl.BlockSpec((None, NSA_GROUPS, 8, nsb), lambda i, j, pt: (i, 0, 0, 0))],
        scratch_shapes=[pltpu.VMEM((2 * nsb, 256), F32)],
    )
    return pl.pallas_call(
        kern,
        grid_spec=grid_spec,
        out_shape=[jax.ShapeDtypeStruct((db, NSA_HEADS, 8, LANE), F32),
                   jax.ShapeDtypeStruct((db, NSA_GROUPS, 8, nsb), F32)],
        compiler_params=_cparams(("parallel", "arbitrary")),
        name="cmp_sample",
    )(page_table, tab, *([cache] * pps), q8, avg)


def _slc_sample_kernel(pt_ref, tab_ref, q_ref, new_ref, sel_ref, e_ref, cache_ref, o_ref, kv_buf, sem,
                       *, n_pages, past_len):
    b = pl.program_id(0)

    def page_copy(pg):
        return pltpu.make_async_copy(cache_ref.at[pt_ref[b, pg], :, pl.ds(256, 256)],
                                     kv_buf.at[pl.ds(pg * PAGE, PAGE), :], sem.at[0])

    for pg in range(n_pages):
        page_copy(pg).start()

    q = q_ref[...]
    qpos = past_len + lax.broadcasted_iota(I32, (8, 1), 0)
    dist_p = qpos - lax.broadcasted_iota(I32, (1, past_len), 1)
    bucket_p = _rel_bucket(dist_p)
    dist_n = qpos - (past_len + lax.broadcasted_iota(I32, (1, PAGE), 1))
    ok_n = dist_n >= 0
    bucket_n = _rel_bucket(dist_n)
    knew = new_ref[:, 0:128].astype(BF16)
    vnew = new_ref[:, 128:256].astype(BF16)

    for pg in range(n_pages):
        page_copy(pg).wait()

    kall = kv_buf[:, 0:128].astype(BF16)
    vall = kv_buf[:, 128:256].astype(BF16)
    for g in range(NSA_GROUPS):
        qg = jnp.concatenate([_head_q128(q, g * NSA_P + p) for p in range(NSA_P)], axis=0).astype(BF16)
        s_p = _dot_nt(qg, kall) * (NSA_HD ** -0.5)
        s_n = _dot_nt(qg, knew) * (NSA_HD ** -0.5)
        mask_p = _dot(sel_ref[g].astype(BF16), e_ref[...]) > 0.5
        pps, pns, dens = [], [], []
        for p in range(NSA_P):
            h = g * NSA_P + p
            rows = slice(8 * p, 8 * p + 8)
            lp = jnp.where(mask_p, s_p[rows] + _bias_from_bucket(bucket_p, tab_ref, h), NEG_INF)
            ln = jnp.where(ok_n, s_n[rows] + _bias_from_bucket(bucket_n, tab_ref, h), NEG_INF)
            m = jnp.maximum(jnp.max(lp, axis=-1, keepdims=True), jnp.max(ln, axis=-1, keepdims=True))
            pp = jnp.where(mask_p, jnp.exp(lp - m), 0.0)
            pn = jnp.where(ok_n, jnp.exp(ln - m), 0.0)
            dens.append(jnp.sum(pp, axis=-1, keepdims=True) + jnp.sum(pn, axis=-1, keepdims=True))
            pps.append(pp.astype(BF16))
            pns.append(pn.astype(BF16))
        o = _dot(jnp.concatenate(pps, axis=0), vall) + _dot(jnp.concatenate(pns, axis=0), vnew)
        for p in range(NSA_P):
            o_ref[g * NSA_P + p] = o[8 * p:8 * p + 8] / jnp.maximum(dens[p], 1e-30)


def _slc_sample(page_table, tab, cache, q8, new_kv, sel, emat, past_len):
    db, n_pages = page_table.shape
    kern = functools.partial(_slc_sample_kernel, n_pages=n_pages, past_len=past_len)
    grid_spec = pltpu.PrefetchScalarGridSpec(
        num_scalar_prefetch=1,
        grid=(db,),
        in_specs=[pl.BlockSpec(memory_space=pltpu.SMEM),
                  pl.BlockSpec((None, 8, 512), lambda i, pt: (i, 0, 0)),
                  pl.BlockSpec((None, PAGE, 256), lambda i, pt: (i, 0, 1)),
                  pl.BlockSpec((None,) + sel.shape[1:], lambda i, pt: (i, 0, 0, 0)),
                  pl.BlockSpec(emat.shape, lambda i, pt: (0, 0)),
                  pl.BlockSpec(memory_space=pl.ANY)],
        out_specs=pl.BlockSpec((None, NSA_HEADS, 8, LANE), lambda i, pt: (i, 0, 0, 0)),
        scratch_shapes=[pltpu.VMEM((n_pages * PAGE, 256), F32), pltpu.SemaphoreType.DMA((1,))],
    )
    return pl.pallas_call(
        kern,
        grid_spec=grid_spec,
        out_shape=jax.ShapeDtypeStruct((db, NSA_HEADS, 8, LANE), F32),
        compiler_params=_cparams(("arbitrary",)),
        name="slc_sample",
    )(page_table, tab, q8, new_kv, sel, emat, cache)


def _win_sample_kernel(tab_ref, q_ref, w_ref, o_ref, *, past_len, wb, lq):
    q = q_ref[...]
    qpos = past_len + lax.broadcasted_iota(I32, (8, 1), 0)
    wseq = w_ref[...]
    n = wseq.shape[0]
    kw = wseq[:, 0:128].astype(BF16)
    vw = wseq[:, 128:256].astype(BF16)
    j = lax.broadcasted_iota(I32, (1, n), 1)
    kpos = past_len - wb + j
    dist = qpos - kpos
    valid = (dist >= 0) & (dist <= WINDOW) & (kpos >= 0) & (j < wb + lq)
    bucket = _rel_bucket(dist)
    for h in range(NSA_HEADS):
        qm = _head_q128(q, h).astype(BF16)
        s = _dot_nt(qm, kw) * (NSA_HD ** -0.5) + _bias_from_bucket(bucket, tab_ref, h)
        l = jnp.where(valid, s, NEG_INF)
        m = jnp.max(l, axis=-1, keepdims=True)
        pr = jnp.where(valid, jnp.exp(l - m), 0.0)
        pr = pr / jnp.maximum(jnp.sum(pr, axis=-1, keepdims=True), 1e-30)
        o_ref[h] = _dot(pr.astype(BF16), vw)


def _win_sample(tab, q8, wseq, past_len, wb, lq):
    db, n, _ = wseq.shape
    kern = functools.partial(_win_sample_kernel, past_len=past_len, wb=wb, lq=lq)
    return pl.pallas_call(
        kern,
        grid=(db,),
        in_specs=[pl.BlockSpec(memory_space=pltpu.SMEM),
                  pl.BlockSpec((None, 8, 512), lambda i: (i, 0, 0)),
                  pl.BlockSpec((None, n, 256), lambda i: (i, 0, 0))],
        out_specs=pl.BlockSpec((None, NSA_HEADS, 8, LANE), lambda i: (i, 0, 0, 0)),
        out_shape=jax.ShapeDtypeStruct((db, NSA_HEADS, 8, LANE), F32),
        compiler_params=_cparams(("parallel",)),
        name="win_sample",
    )(tab, q8, wseq)


def _outproj_kernel(x_ref, og_ref, oc_ref, os_ref, ow_ref, sm_ref, wg_ref, wn_ref, n2_ref,
                    hres_ref, xn_ref):
    gates = jax.nn.sigmoid(sm_ref[...])
    acc = x_ref[...] + _dot(og_ref[...].astype(BF16), wg_ref[...])
    for h in range(NSA_HEADS):
        c = 8 + 3 * h
        on = (gates[:, c:c + 1] * oc_ref[h] + gates[:, c + 1:c + 2] * os_ref[h]
              + gates[:, c + 2:c + 3] * ow_ref[h])
        acc = acc + _dot(on.astype(BF16), wn_ref[h])
    hres_ref[...] = acc
    ms = jnp.mean(acc * acc, axis=-1, keepdims=True)
    xn_ref[...] = acc * lax.rsqrt(ms + EPS) * n2_ref[...]


def _out_proj(x, og, ocmp, oslc, owin, sm, wg, wn, norm2, tm):
    b, l, _ = x.shape
    assert l % tm == 0
    seq = lambda w: pl.BlockSpec((None, tm, w), lambda i, j: (i, j, 0))
    heads = pl.BlockSpec((None, NSA_HEADS, tm, LANE), lambda i, j: (i, 0, j, 0))
    full = lambda a: pl.BlockSpec(a.shape, lambda i, j: (0,) * a.ndim)
    return pl.pallas_call(
        _outproj_kernel,
        grid=(b, l // tm),
        in_specs=[seq(D_MODEL), seq(512), heads, heads, heads, seq(LANE), full(wg), full(wn), full(norm2)],
        out_specs=[seq(D_MODEL), seq(D_MODEL)],
        out_shape=[jax.ShapeDtypeStruct((b, l, D_MODEL), F32)] * 2,
        compiler_params=_cparams(("parallel", "parallel")),
        name="out_proj",
    )(x, og, ocmp, oslc, owin, sm, wg, wn, norm2)


def _peer_topk_kernel(x_ref, wqt_ref, keys_ref, cflat_ref, eid_ref, gate_ref, qt_scr, sv_scr, si_scr, top_scr,
                      *, tm):
    qt_scr[...] = _dot_nt(wqt_ref[...], x_ref[...].astype(BF16))
    rows = _iota_f((PEER_NKEYS, tm), 0)
    cflat = jnp.broadcast_to(cflat_ref[...], (PEER_NCAND, tm))

    def head_body(h, carry):
        for c in range(2):
            off = pl.multiple_of(h * (2 * PEER_HALF) + c * PEER_HALF, PEER_HALF)
            qs = qt_scr[pl.ds(off, PEER_HALF), :].astype(BF16)
            s = _dot(keys_ref[h, c], qs)

            def round_body(r, s):
                m = jnp.max(s, axis=0, keepdims=True)
                idx = jnp.min(jnp.where(s == m, rows, float(PEER_NKEYS)), axis=0, keepdims=True)
                sv_scr[c, pl.ds(r, 1), :] = m
                si_scr[c, pl.ds(r, 1), :] = idx
                return jnp.where(rows == idx, REMOVED, s)

            lax.fori_loop(0, PEER_TOPK, round_body, s)
        s1 = sv_scr[0]
        s2 = sv_scr[1]
        i1 = si_scr[0] * float(PEER_NKEYS)
        i2 = si_scr[1]
        cand = [s1[0:1] + s2]
        eidc = [i1[0:1] + i2]
        for a in range(1, 8):
            cand.append(s1[a:a + 1] + s2[0:8])
            eidc.append(i1[a:a + 1] + i2[0:8])
        cand.append(s1[8:16] + s2[0:1])
        eidc.append(i1[8:16] + i2[0:1])
        cand = jnp.where(cflat >= 0.0, jnp.concatenate(cand, axis=0), REMOVED)
        eidc = jnp.concatenate(eidc, axis=0)

        def round2(r, cand):
            m = jnp.max(cand, axis=0, keepdims=True)
            f = jnp.min(jnp.where(cand == m, cflat, 1e9), axis=0, keepdims=True)
            hit = cflat == f
            top_scr[pl.ds(r, 1), :] = m
            eid_ref[h, pl.ds(r, 1), :] = jnp.sum(jnp.where(hit, eidc, 0.0), axis=0, keepdims=True).astype(I32)
            return jnp.where(hit, REMOVED, cand)

        lax.fori_loop(0, PEER_TOPK, round2, cand)
        top = top_scr[...]
        e = jnp.exp(top - jnp.max(top, axis=0, keepdims=True))
        gate_ref[h] = e / jnp.sum(e, axis=0, keepdims=True)
        return carry

    lax.fori_loop(0, PEER_HEADS, head_body, 0)


def _peer_cflat():
    rows = [(0, b) for b in range(16)]
    for a in range(1, 8):
        rows += [(a, b) for b in range(8)]
    rows += [(a, 0) for a in range(8, 16)]
    flat = [a * 16 + b if (a + 1) * (b + 1) <= PEER_TOPK else -1 for a, b in rows]
    assert len(flat) == PEER_NCAND
    return jnp.asarray(np.array(flat, np.float32).reshape(PEER_NCAND, 1))


def _peer_topk(xn, wqt, keys, tm):
    t = xn.shape[0]
    assert t % tm == 0
    cflat = _peer_cflat()
    kern = functools.partial(_peer_topk_kernel, tm=tm)
    full = lambda a: pl.BlockSpec(a.shape, lambda i: (0,) * a.ndim)
    out_spec = pl.BlockSpec((PEER_HEADS, PEER_TOPK, tm), lambda i: (0, 0, i))
    return pl.pallas_call(
        kern,
        grid=(t // tm,),
        in_specs=[pl.BlockSpec((tm, D_MODEL), lambda i: (i, 0)), full(wqt), full(keys), full(cflat)],
        out_specs=[out_spec, out_spec],
        out_shape=[jax.ShapeDtypeStruct((PEER_HEADS, PEER_TOPK, t), I32),
                   jax.ShapeDtypeStruct((PEER_HEADS, PEER_TOPK, t), F32)],
        scratch_shapes=[pltpu.VMEM((D_MODEL, tm), F32), pltpu.VMEM((2, PEER_TOPK, tm), F32),
                        pltpu.VMEM((2, PEER_TOPK, tm), F32), pltpu.VMEM((PEER_TOPK, tm), F32)],
        compiler_params=_cparams(("parallel",)),
        name="peer_topk",
    )(xn, wqt, keys, cflat)


PEER_GROUP = 2
PEER_SLOTS = 8
PEER_CHUNKS = D_MODEL // LANE


def _peer_expert_kernel(eid_ref, x_ref, gate_ref, hres_ref, uv_ref, y_ref, *scratch, tt):
    bufs = scratch[:PEER_SLOTS]
    sem = scratch[PEER_SLOTS]
    ahead = PEER_SLOTS - PEER_GROUP

    def row_copy(e, slot, k):
        return pltpu.make_async_copy(uv_ref.at[e], bufs[slot].at[pl.ds(k * PEER_CHUNKS, PEER_CHUNKS), :],
                                     sem.at[slot])

    def issue(t, slot, part=None):
        base = t * PEER_SEL
        i, n = (0, 1) if part is None else part
        for k in range(i * PEER_SEL // n, (i + 1) * PEER_SEL // n):
            row_copy(eid_ref[base + k], slot, k).start(priority=k % 2)

    def wait(slot):
        for k in range(PEER_SEL):
            row_copy(0, slot, k).wait()

    def chunk_words(slot, c):
        return bufs[slot][pl.ds(c, PEER_SEL, stride=PEER_CHUNKS), :]

    def evaluate(ts, slots, prefetch):
        for slot in slots:
            wait(slot)
        nparts = 2 * PEER_CHUNKS

        def start_part(i):
            if prefetch:
                for t, slot in zip(ts, slots):
                    issue(t + ahead, (slot + ahead) % PEER_SLOTS, (i, nparts))

        xbs = [x_ref[pl.ds(t, 1), :].astype(BF16).astype(F32) for t in ts]
        accs = [jnp.zeros((PEER_SEL, LANE), F32) for _ in ts]
        for c in range(PEER_CHUNKS):
            start_part(c)
            for j, slot in enumerate(slots):
                uf = pltpu.bitcast(chunk_words(slot, c) & jnp.int32(-65536), F32)
                accs[j] = accs[j] + uf * xbs[j][:, c * LANE:(c + 1) * LANE]
        w2s = []
        for j, t in enumerate(ts):
            act = jnp.sum(accs[j].T, axis=0, keepdims=True)
            w = gate_ref[pl.ds(t, 1), :] * (0.5 * act * (1.0 + lax.erf(act * SQRT_HALF)))
            wb = w.astype(BF16).astype(F32)
            w2s.append(jnp.broadcast_to(wb, (LANE, PEER_SEL)).T)
        outs = [[] for _ in ts]
        for c in range(PEER_CHUNKS):
            start_part(PEER_CHUNKS + c)
            for j, slot in enumerate(slots):
                vf = pltpu.bitcast(chunk_words(slot, c) << 16, F32)
                outs[j].append(jnp.sum(vf * w2s[j], axis=0, keepdims=True))
        for j, t in enumerate(ts):
            y_ref[pl.ds(t, 1), :] = hres_ref[pl.ds(t, 1), :] + jnp.concatenate(outs[j], axis=1)

    for t0 in range(ahead):
        issue(t0, t0)
    n_main = (tt - ahead) // PEER_SLOTS * PEER_SLOTS

    def body(i, carry):
        for r in range(0, PEER_SLOTS, PEER_GROUP):
            slots = list(range(r, r + PEER_GROUP))
            evaluate([i * PEER_SLOTS + s for s in slots], slots, True)
        return carry

    lax.fori_loop(0, n_main // PEER_SLOTS, body, 0)
    for t0 in range(n_main, tt, PEER_GROUP):
        ts = list(range(t0, t0 + PEER_GROUP))
        evaluate(ts, [t % PEER_SLOTS for t in ts], t0 + ahead < tt)


def _pack_expert_rows(u, v):
    bits = lambda a: lax.bitcast_convert_type(a.astype(BF16), jnp.uint16).astype(jnp.uint32)
    words = (bits(u) << 16) | bits(v)
    return lax.bitcast_convert_type(words, I32).reshape(-1, PEER_CHUNKS, LANE)


def _peer_experts(eid_flat, xn, gate, hres, uv, tt):
    t = xn.shape[0]
    assert t % tt == 0 and tt >= PEER_SLOTS
    kern = functools.partial(_peer_expert_kernel, tt=tt)
    row = lambda w: pl.BlockSpec((tt, w), lambda i: (i, 0))
    return pl.pallas_call(
        kern,
        grid=(t // tt,),
        in_specs=[pl.BlockSpec((tt * PEER_SEL,), lambda i: (i,), memory_space=pltpu.SMEM),
                  row(D_MODEL), row(PEER_SEL), row(D_MODEL),
                  pl.BlockSpec(memory_space=pl.ANY)],
        out_specs=row(D_MODEL),
        out_shape=jax.ShapeDtypeStruct((t, D_MODEL), F32),
        scratch_shapes=[pltpu.VMEM((PEER_SEL * PEER_CHUNKS, LANE), I32) for _ in range(PEER_SLOTS)]
        + [pltpu.SemaphoreType.DMA((PEER_SLOTS,))],
        compiler_params=_cparams(("arbitrary",)),
        name="peer_experts",
    )(eid_flat, xn, gate, hres, uv)


def _prep_params(norm1, w_in, gdn_conv_w, gdn_a_log, gdn_dt_bias, gdn_norm, nsa_q_norm, nsa_k_norm,
                 rel_bias, w_o, norm2, peer_wq, peer_subkeys, peer_u, peer_v):
    w = w_in[0]
    p = {}
    p["norm1"] = norm1[0][None]
    p["wm"] = jnp.concatenate([w[:, :OFF_B], w[:, OFF_NQ:OFF_NG]], axis=1).astype(BF16)
    p["ws"] = jnp.concatenate([w[:, OFF_B:OFF_NQ], w[:, OFF_NG:], jnp.zeros((D_MODEL, LANE - 32), F32)],
                              axis=1).astype(BF16)
    li = jnp.arange(LANE)
    p["seg"] = (li[:, None] // NSA_HD == li[None, :] // NSA_HD).astype(BF16)
    p["qg"] = jnp.tile(nsa_q_norm[0], 2)[None]
    p["kg"] = jnp.tile(nsa_k_norm[0], (1, 2))
    p["conv_w"] = gdn_conv_w[0]
    p["al_vec"] = jnp.zeros((1, LANE), F32).at[0, 4:8].set(gdn_a_log[0])
    p["dtb_vec"] = jnp.zeros((1, LANE), F32).at[0, 4:8].set(gdn_dt_bias[0])
    p["gnorm"] = gdn_norm[0][None]
    p["tab"] = rel_bias
    wo = w_o[0]
    p["wg"] = wo[:512].astype(BF16)
    wn = jnp.zeros((NSA_HEADS, LANE, D_MODEL), F32)
    for h in range(NSA_HEADS):
        g = h // NSA_P
        wn = wn.at[h, g * NSA_HD:(g + 1) * NSA_HD].set(wo[512 + h * NSA_HD:512 + (h + 1) * NSA_HD])
    p["wn"] = wn.astype(BF16)
    p["norm2"] = norm2[0][None]
    p["wqt"] = peer_wq[0].T.astype(BF16)
    p["keys"] = peer_subkeys[0].astype(BF16)
    p["uv"] = _pack_expert_rows(peer_u[0], peer_v[0])
    return p


def _perm_avg_matrix(n_blocks, n_rows):
    half = n_blocks // 2
    r = jnp.arange(n_blocks)
    blk = jnp.where(r < half, 2 * r, 2 * (r - half) + 1)
    return (jnp.arange(n_rows)[None, :] // CMP_BLOCK == blk[:, None]).astype(BF16)


def _token_mixer_tail(p, x, og, ocmp, oslc, owin, sm, tm_out, tm_topk, tt):
    b, l, _ = x.shape
    hres, xn2 = _out_proj(x, og, ocmp, oslc, owin, sm, p["wg"], p["wn"], p["norm2"], tm_out)
    t = b * l
    xn2 = xn2.reshape(t, D_MODEL)
    eid, gate = _peer_topk(xn2, p["wqt"], p["keys"], tm_topk)
    eid_flat = eid.reshape(PEER_SEL, t).T.reshape(t * PEER_SEL)
    gate_tok = gate.reshape(PEER_SEL, t).T
    y = _peer_experts(eid_flat, xn2, gate_tok, hres.reshape(t, D_MODEL), p["uv"], tt)
    return y.reshape(b, l, D_MODEL)


def kernel(x_prompt, x_sample, cache_nsa_kv, page_table, state_win_kv, state_conv, state_gdn, norm1, w_in, gdn_conv_w, gdn_a_log, gdn_dt_bias, gdn_norm, nsa_q_norm, nsa_k_norm, rel_bias, w_o, norm2, peer_wq, peer_subkeys, peer_u, peer_v):
    assert w_in.shape[0] == 1, "single layer"
    p = _prep_params(norm1, w_in, gdn_conv_w, gdn_a_log, gdn_dt_bias, gdn_norm, nsa_q_norm, nsa_k_norm,
                     rel_bias, w_o, norm2, peer_wq, peer_subkeys, peer_u, peer_v)
    b, l, _ = x_prompt.shape
    db, lq, _ = x_sample.shape
    n_pages = page_table.shape[1]
    past_len = n_pages * PAGE
    wb = state_win_kv.shape[2]
    assert cache_nsa_kv.shape[2] == PAGE and l >= WINDOW and l >= 3 and lq >= 3
    assert lq < CMP_BLOCK and lq <= 8 and wb == WINDOW and past_len >= wb

    tp = b * l
    tm = 256 if tp % 256 == 0 else LANE
    hc, z, sm, qn, kvn, winn = _in_proj(x_prompt.reshape(tp, D_MODEL), p["norm1"], p["wm"], p["ws"],
                                        p["seg"], p["qg"], p["kg"], tm)
    hc3, z3, sm3 = hc.reshape(b, l, GDN_CH), z.reshape(b, l, 512), sm.reshape(b, l, LANE)
    qn3, kvn3, winn3 = qn.reshape(b, l, 512), kvn.reshape(b, l, 512), winn.reshape(b, l, 256)
    og, gdn_p = _gdn(hc3, z3, sm3, p["conv_w"], p["al_vec"], p["dtb_vec"], p["gnorm"],
                     jnp.zeros((b, 3, GDN_CH), F32), jnp.zeros((b, GDN_HEADS, 128, 128), F32),
                     GDN_CHUNK, l)
    nc = l // CMP_BLOCK
    ocmp, sel = _cmp_prompt(p["tab"], qn3, kvn3, _perm_avg_matrix(nc, l), tm)
    oslc = _flash_prompt(p["tab"], qn3, kvn3, 2, 3, sel, LANE, False)
    owin = _flash_prompt(p["tab"], qn3, winn3, 0, 1, None, LANE, True)
    y_prompt = _token_mixer_tail(p, x_prompt, og, ocmp, oslc, owin, sm3, tm, tm, 256 if tp % 256 == 0 else 64)

    ts = db * lq
    hc_s, z_s, sm_s, qn_s, kvn_s, winn_s = _in_proj(x_sample.reshape(ts, D_MODEL), p["norm1"], p["wm"], p["ws"],
                                                    p["seg"], p["qg"], p["kg"], min(ts, 256))
    pad_rows = lambda a, n: jnp.pad(a.reshape(db, lq, a.shape[-1]), ((0, 0), (0, n - lq), (0, 0)))
    og_s, gdn_s = _gdn(pad_rows(hc_s, GDN_CHUNK), pad_rows(z_s, GDN_CHUNK), pad_rows(sm_s, GDN_CHUNK),
                       p["conv_w"], p["al_vec"], p["dtb_vec"], p["gnorm"], state_conv[0], state_gdn[0],
                       GDN_CHUNK, lq)
    cache3 = cache_nsa_kv[0].reshape(cache_nsa_kv.shape[1], PAGE, 512)
    q8 = pad_rows(qn_s, 8)
    ocmp_s, sel_s = _cmp_sample(page_table, p["tab"], cache3, q8, past_len)
    nsb_s = past_len // SEL_BLOCK
    emat_s = (jnp.arange(nsb_s)[:, None] == jnp.arange(past_len)[None, :] // SEL_BLOCK).astype(BF16)
    oslc_s = _slc_sample(page_table, p["tab"], cache3, q8, pad_rows(kvn_s, PAGE), sel_s, emat_s, past_len)
    wseq = jnp.concatenate([state_win_kv[0].reshape(db, wb, 256), pad_rows(winn_s, LANE)], axis=1)
    owin_s = _win_sample(p["tab"], q8, wseq, past_len, wb, lq)
    flat_heads = lambda o: o[:, :, :lq].transpose(1, 0, 2, 3).reshape(1, NSA_HEADS, ts, LANE)
    y_sample = _token_mixer_tail(p, x_sample.reshape(1, ts, D_MODEL), og_s[:, :lq].reshape(1, ts, 512),
                                 flat_heads(ocmp_s), flat_heads(oslc_s), flat_heads(owin_s),
                                 sm_s.reshape(1, ts, LANE), min(ts, 256), min(ts, 256), 64)

    kv_tail = (4, NSA_GROUPS, NSA_HD)
    return (y_prompt,
            y_sample.reshape(db, lq, D_MODEL),
            kvn.reshape((1, b, l) + kv_tail),
            winn3[:, l - WINDOW:].reshape(1, b, WINDOW, 2, NSA_GROUPS, NSA_HD),
            hc3[:, l - 3:][None],
            gdn_p[None],
            kvn_s.reshape((1, db, lq) + kv_tail),
            wseq[:, lq:lq + wb].reshape(1, db, wb, 2, NSA_GROUPS, NSA_HD),
            hc_s.reshape(db, lq, GDN_CH)[:, lq - 3:][None],
            gdn_s[None])
```

```python
import functools
import math

import numpy as np
import jax
import jax.numpy as jnp
from jax import lax
from jax.experimental import pallas as pl
from jax.experimental.pallas import tpu as pltpu

F32 = jnp.float32
BF16 = jnp.bfloat16
I32 = jnp.int32

D_MODEL = 1024
EPS = 1e-6
NEG_INF = -1e30
FORCE_SCORE = 1e4
GDN_HEADS = 4
GDN_DK = 128
GDN_CHUNK = 64
GDN_CH = 1536
NSA_HEADS = 8
NSA_GROUPS = 2
NSA_P = 4
NSA_HD = 64
CMP_BLOCK = 32
SEL_BLOCK = 64
SEL_TOPK = 16
WINDOW = 512
REL_BUCKETS = 32
PAGE = 128
PEER_HEADS = 8
PEER_NKEYS = 128
PEER_TOPK = 16
PEER_HALF = 64
PEER_SEL = PEER_HEADS * PEER_TOPK
PEER_NCAND = 80
OFF_B = 2048
OFF_NQ = 2056
OFF_NG = 3336
LANE = 128
VMEM_LIMIT = 56 * 1024 * 1024
REMOVED = -3.0e38
SQRT_HALF = 0.7071067811865476


def _cparams(sem):
    return pltpu.CompilerParams(dimension_semantics=sem, vmem_limit_bytes=VMEM_LIMIT)


def _dot(a, b):
    return jnp.dot(a, b, preferred_element_type=F32)


def _dot_nt(a, b):
    return lax.dot_general(a, b, (((1,), (1,)), ((), ())), preferred_element_type=F32)


def _dot_tn(a, b):
    return lax.dot_general(a, b, (((0,), (0,)), ((), ())), preferred_element_type=F32)


def _split2(x):
    hi = x.astype(BF16)
    lo = (x - hi.astype(F32)).astype(BF16)
    return hi, lo


def _split3(x):
    hi = x.astype(BF16)
    r = x - hi.astype(F32)
    mid = r.astype(BF16)
    lo = (r - mid.astype(F32)).astype(BF16)
    return hi, mid, lo


def _dot_exact_lhs(m01, x):
    hi, mid, lo = _split3(x)
    return _dot(m01, hi) + (_dot(m01, mid) + _dot(m01, lo))


def _mm3(a, b):
    ah, al = _split2(a)
    bh, bl = _split2(b)
    return _dot(ah, bh) + (_dot(ah, bl) + _dot(al, bh))


def _iota_f(shape, axis):
    return lax.broadcasted_iota(I32, shape, axis).astype(F32)


def _rel_bucket(dist):
    d = jnp.maximum(dist, 0)
    df = jnp.maximum(d, 1).astype(F32)
    large = 16 + (jnp.log(df / 16.0) / math.log(128.0) * 16.0).astype(I32)
    large = jnp.minimum(large, REL_BUCKETS - 1)
    return jnp.where(d < 16, d, large)


def _bias_from_bucket(bucket, tab_ref, head, lo=0, hi=REL_BUCKETS - 1):
    if isinstance(lo, int) and isinstance(hi, int):
        b = jnp.zeros(bucket.shape, F32)
        for k in range(lo, hi + 1):
            b = jnp.where(bucket == k, tab_ref[k, head], b)
        return b

    def body(k, b):
        return jnp.where(bucket == k, tab_ref[k, head], b)

    return lax.fori_loop(lo, hi + 1, body, jnp.zeros(bucket.shape, F32))


def _topk_mask(s, k, axis):
    n = s.shape[axis]
    ids = _iota_f(s.shape, axis)
    sel = jnp.zeros(s.shape, F32)
    for _ in range(k):
        m = jnp.max(s, axis=axis, keepdims=True)
        idx = jnp.min(jnp.where(s == m, ids, float(n)), axis=axis, keepdims=True)
        hit = ids == idx
        sel = jnp.where(hit, 1.0, sel)
        s = jnp.where(hit, REMOVED, s)
    return sel


def _seg_rmsnorm(v, gain, seg):
    sq = v * v
    hi, lo = _split2(sq)
    ssum = _dot(hi, seg) + _dot(lo, seg)
    return v * lax.rsqrt(ssum * (1.0 / NSA_HD) + EPS) * gain


def _inproj_kernel(x_ref, g1_ref, wm_ref, ws_ref, seg_ref, qg_ref, kg_ref,
                   hc_ref, z_ref, sm_ref, q_ref, kv_ref, win_ref):
    x = x_ref[...]
    ms = jnp.mean(x * x, axis=-1, keepdims=True)
    xn = (x * lax.rsqrt(ms + EPS) * g1_ref[...]).astype(BF16)
    h = _dot(xn, wm_ref[...])
    sm_ref[...] = _dot(xn, ws_ref[...])
    hc_ref[...] = h[:, :GDN_CH]
    z_ref[...] = h[:, GDN_CH:2048]
    seg = seg_ref[...]
    qg = qg_ref[...]
    kg = kg_ref[...]
    for i in range(4):
        q_ref[:, i * LANE:(i + 1) * LANE] = _seg_rmsnorm(h[:, 2048 + i * LANE:2048 + (i + 1) * LANE], qg, seg)
    kv_ref[:, 0:128] = _seg_rmsnorm(h[:, 2560:2688], kg[0:1], seg)
    kv_ref[:, 128:256] = h[:, 2688:2816]
    kv_ref[:, 256:384] = _seg_rmsnorm(h[:, 2816:2944], kg[1:2], seg)
    kv_ref[:, 384:512] = h[:, 2944:3072]
    win_ref[:, 0:128] = _seg_rmsnorm(h[:, 3072:3200], kg[2:3], seg)
    win_ref[:, 128:256] = h[:, 3200:3328]


def _in_proj(x, norm1, wm, ws, seg, qg, kg, tm):
    t = x.shape[0]
    assert t % tm == 0
    row = lambda w: pl.BlockSpec((tm, w), lambda i: (i, 0))
    full = lambda a: pl.BlockSpec(a.shape, lambda i: (0,) * a.ndim)
    widths = (GDN_CH, 512, LANE, 512, 512, 256)
    return pl.pallas_call(
        _inproj_kernel,
        grid=(t // tm,),
        in_specs=[row(D_MODEL), full(norm1), full(wm), full(ws), full(seg), full(qg), full(kg)],
        out_specs=[row(w) for w in widths],
        out_shape=[jax.ShapeDtypeStruct((t, w), F32) for w in widths],
        compiler_params=_cparams(("parallel",)),
        name="in_proj",
    )(x, norm1, wm, ws, seg, qg, kg)


def _inv_unit_lower(lmat, c):
    ri = lax.broadcasted_iota(I32, lmat.shape, 0)
    ci = lax.broadcasted_iota(I32, lmat.shape, 1)
    eye = jnp.where(ri == ci, 1.0, 0.0).astype(F32)
    n = -lmat
    p = eye + n
    m = _mm3(n, n)
    span = 2
    while True:
        mm = _mm3 if span == 2 else (lambda a, b: _dot(a.astype(BF16), b.astype(BF16)))
        p = p + mm(p, m)
        span *= 2
        if span >= c:
            break
        m = mm(m, m)
    return p


def _gdn_kernel(hc_ref, z_ref, sm_ref, cw_ref, al_ref, dtb_ref, gn_ref, conv0_ref, s0_ref,
                og_ref, sfin_ref, xbuf, s_scr, *, c, l_valid, n_chunks):
    ci = pl.program_id(1)

    @pl.when(ci == 0)
    def _():
        xbuf[0:8, :] = jnp.zeros((8, GDN_CH), F32)
        xbuf[5:8, :] = conv0_ref[...]
        s_scr[...] = s0_ref[...]

    xbuf[8:8 + c, :] = hc_ref[...]
    w = cw_ref[...]
    y = (xbuf[5:5 + c, :] * w[0:1] + xbuf[6:6 + c, :] * w[1:2]
         + xbuf[7:7 + c, :] * w[2:3] + xbuf[8:8 + c, :] * w[3:4])
    tail = xbuf[5 + c:8 + c, :]
    xbuf[5:8, :] = tail
    y = y * jax.nn.sigmoid(y)

    sm = sm_ref[...]
    rowid = ci * c + lax.broadcasted_iota(I32, (c, 1), 0)
    rvalid = rowid < l_valid
    beta_all = jnp.where(rvalid, jax.nn.sigmoid(sm), 0.0)
    sp_in = sm + dtb_ref[...]
    softplus = jnp.maximum(sp_in, 0.0) + jnp.log1p(jnp.exp(-jnp.abs(sp_in)))
    g_all = jnp.where(rvalid, -jnp.exp(al_ref[...]) * softplus, 0.0)

    ri = lax.broadcasted_iota(I32, (c, c), 0)
    cj = lax.broadcasted_iota(I32, (c, c), 1)
    tri = jnp.where(ri >= cj, 1.0, 0.0).astype(BF16)
    triu = jnp.where(ri <= cj, 1.0, 0.0).astype(BF16)
    g_hi, g_mid, g_lo = _split3(g_all)
    gcum = _dot(tri, g_hi) + (_dot(tri, g_mid) + _dot(tri, g_lo))
    gcum_t = _dot_tn(g_hi, triu) + (_dot_tn(g_mid, triu) + _dot_tn(g_lo, triu))
    gn = gn_ref[...]

    heads = range(GDN_HEADS)
    stack = lambda pieces: jnp.concatenate(pieces, axis=0)

    def l2n(x):
        return jnp.where(rvalid, x * lax.rsqrt(jnp.sum(x * x, axis=-1, keepdims=True) + EPS), 0.0)

    q_s = stack([l2n(y[:, h * 128:(h + 1) * 128]) * (GDN_DK ** -0.5) for h in heads])
    k_s = stack([l2n(y[:, 512 + h * 128:512 + (h + 1) * 128]) for h in heads])
    v_s = stack([jnp.where(rvalid, y[:, 1024 + h * 128:1024 + (h + 1) * 128], 0.0) for h in heads])
    beta_s = stack([beta_all[:, h:h + 1] for h in heads])
    gc_s = stack([gcum[:, 4 + h:5 + h] for h in heads])
    gl_s = stack([jnp.broadcast_to(gcum[c - 1:c, 4 + h:5 + h], (c, 1)) for h in heads])
    gct_s = jnp.concatenate([gcum_t[4 + h:5 + h, :] for h in heads], axis=1)
    r = GDN_HEADS * c
    rr = lax.broadcasted_iota(I32, (r, r), 0)
    rc = lax.broadcasted_iota(I32, (r, r), 1)
    same = (rr // c) == (rc // c)
    causal = same & (rr >= rc)
    strict = same & (rr > rc)
    decay = jnp.where(causal, jnp.exp(jnp.where(causal, gc_s - gct_s, 0.0)), 0.0)
    kb = k_s * beta_s
    ksb = k_s.astype(BF16)
    lmat = jnp.where(strict, _dot_nt(kb.astype(BF16), ksb) * decay, 0.0)
    tmat = _inv_unit_lower(lmat, c).astype(BF16)
    eg = jnp.exp(gc_s)
    u = _dot(tmat, (v_s * beta_s).astype(BF16))
    wmat = _dot(tmat, (kb * eg).astype(BF16)).astype(BF16)
    a_intra = (_dot_nt(q_s.astype(BF16), ksb) * decay).astype(BF16)
    q_dec = (q_s * eg).astype(BF16)
    k_dec = (k_s * jnp.exp(gl_s - gc_s)).astype(BF16)
    rows = lambda a, h: a[h * c:(h + 1) * c]
    sbs = [s_scr[h].astype(BF16) for h in heads]
    v_new = stack([rows(u, h) - _dot(rows(wmat, h), sbs[h]) for h in heads])
    vnb = v_new.astype(BF16)
    o_intra = _dot(a_intra, vnb)
    for h in heads:
        g_last = jnp.exp(gcum[c - 1:c, 4 + h:5 + h])
        s_scr[h] = s_scr[h] * g_last + _dot_tn(rows(k_dec, h), rows(vnb, h))
        o = _dot(rows(q_dec, h), sbs[h]) + rows(o_intra, h)
        o = o * lax.rsqrt(jnp.mean(o * o, axis=-1, keepdims=True) + EPS) * gn
        zh = z_ref[:, h * 128:(h + 1) * 128]
        og_ref[:, h * 128:(h + 1) * 128] = o * (zh * jax.nn.sigmoid(zh))

    @pl.when(ci == n_chunks - 1)
    def _():
        sfin_ref[...] = s_scr[...]


def _gdn(hc, z, sm, conv_w, al_vec, dtb_vec, gnorm, conv0, s0, c, l_valid):
    b, lp, _ = hc.shape
    assert lp % c == 0
    n_chunks = lp // c
    full = lambda a: pl.BlockSpec(a.shape, lambda i, j: (0,) * a.ndim)
    seq = lambda w: pl.BlockSpec((None, c, w), lambda i, j: (i, j, 0))
    kern = functools.partial(_gdn_kernel, c=c, l_valid=l_valid, n_chunks=n_chunks)
    return pl.pallas_call(
        kern,
        grid=(b, n_chunks),
        in_specs=[seq(GDN_CH), seq(512), seq(LANE), full(conv_w), full(al_vec), full(dtb_vec), full(gnorm),
                  pl.BlockSpec((None, 3, GDN_CH), lambda i, j: (i, 0, 0)),
                  pl.BlockSpec((None, GDN_HEADS, 128, 128), lambda i, j: (i, 0, 0, 0))],
        out_specs=[seq(512), pl.BlockSpec((None, GDN_HEADS, 128, 128), lambda i, j: (i, 0, 0, 0))],
        out_shape=[jax.ShapeDtypeStruct((b, lp, 512), F32),
                   jax.ShapeDtypeStruct((b, GDN_HEADS, 128, 128), F32)],
        scratch_shapes=[pltpu.VMEM((c + 8, GDN_CH), F32), pltpu.VMEM((GDN_HEADS, 128, 128), F32)],
        compiler_params=_cparams(("parallel", "arbitrary")),
        name="gdn",
    )(hc, z, sm, conv_w, al_vec, dtb_vec, gnorm, conv0, s0)


def _head_q128(q, h):
    g = h // NSA_P
    piece = q[:, (h // 2) * LANE:(h // 2 + 1) * LANE]
    lane = lax.broadcasted_iota(I32, piece.shape, 1)
    keep = (lane >= NSA_HD) if h % 2 == 1 else (lane < NSA_HD)
    qm = jnp.where(keep, piece, 0.0)
    if h % 2 != g:
        qm = pltpu.roll(qm, NSA_HD, 1)
    return qm


def _select_scores(imp, qpos, nsb):
    score = imp[:, :nsb] + imp[:, nsb:]
    j = lax.broadcasted_iota(I32, score.shape, 1)
    cur = qpos // SEL_BLOCK
    forced = (j == 0) | (j == cur) | (j == cur - 1)
    future = j * SEL_BLOCK > qpos
    return jnp.where(future, -1.0, jnp.where(forced, FORCE_SCORE, score))


def _cmp_attention(tab_ref, q, kcv, qpos, ocmp_ref):
    nc = kcv.shape[0]
    nsb = nc // 2
    kc = kcv[:, 0:128].astype(BF16)
    vc = kcv[:, 128:256].astype(BF16)
    lane = lax.broadcasted_iota(I32, (1, nc), 1)
    blk = jnp.where(lane < nsb, 2 * lane, 2 * (lane - nsb) + 1)
    dist = qpos - (blk * CMP_BLOCK + CMP_BLOCK - 1)
    valid = dist >= 0
    bucket = _rel_bucket(dist)
    scores = []
    for g in range(NSA_GROUPS):
        imp = jnp.zeros((q.shape[0], nc), F32)
        for p in range(NSA_P):
            h = g * NSA_P + p
            qm = _head_q128(q, h).astype(BF16)
            logits = _dot_nt(qm, kc) * (NSA_HD ** -0.5) + _bias_from_bucket(bucket, tab_ref, h)
            l = jnp.where(valid, logits, NEG_INF)
            m = jnp.max(l, axis=-1, keepdims=True)
            pr = jnp.where(valid, jnp.exp(l - m), 0.0)
            pr = pr / jnp.maximum(jnp.sum(pr, axis=-1, keepdims=True), 1e-30)
            ocmp_ref[h] = _dot(pr.astype(BF16), vc)
            imp = imp + pr
        scores.append(_select_scores(imp, qpos, nsb))
    return scores


def _flash_init(m_scr, l_scr, acc_scr):
    m_scr[...] = jnp.full(m_scr.shape, NEG_INF, F32)
    l_scr[...] = jnp.zeros(l_scr.shape, F32)
    acc_scr[...] = jnp.zeros(acc_scr.shape, F32)


def _cmp_prompt_kernel(tab_ref, q_ref, kv_ref, mavg_ref, ocmp_ref, sel_ref, kc_scr, *, tq, nsb):
    qi = pl.program_id(1)

    @pl.when(qi == 0)
    def _():
        kc_scr[...] = _dot_exact_lhs(mavg_ref[...], kv_ref[...]) * (1.0 / CMP_BLOCK)

    qpos = qi * tq + lax.broadcasted_iota(I32, (tq, 1), 0)
    scores = _cmp_attention(tab_ref, q_ref[...], kc_scr[...], qpos, ocmp_ref)
    st = jnp.concatenate(scores, axis=1).T
    k = min(SEL_TOPK, nsb)
    sel_ref[...] = jnp.concatenate([_topk_mask(st[:nsb], k, 0), _topk_mask(st[nsb:], k, 0)], axis=0)


def _cmp_prompt(tab, qn, kvn, mavg, tq):
    b, l, _ = qn.shape
    nc = l // CMP_BLOCK
    assert l % tq == 0 and l % SEL_BLOCK == 0
    kern = functools.partial(_cmp_prompt_kernel, tq=tq, nsb=nc // 2)
    return pl.pallas_call(
        kern,
        grid=(b, l // tq),
        in_specs=[pl.BlockSpec(memory_space=pltpu.SMEM),
                  pl.BlockSpec((None, tq, 512), lambda i, j: (i, j, 0)),
                  pl.BlockSpec((None, l, 256), lambda i, j: (i, 0, 0)),
                  pl.BlockSpec(mavg.shape, lambda i, j: (0, 0))],
        out_specs=[pl.BlockSpec((None, NSA_HEADS, tq, LANE), lambda i, j: (i, 0, j, 0)),
                   pl.BlockSpec((None, nc, tq), lambda i, j: (i, 0, j))],
        out_shape=[jax.ShapeDtypeStruct((b, NSA_HEADS, l, LANE), F32),
                   jax.ShapeDtypeStruct((b, nc, l), F32)],
        scratch_shapes=[pltpu.VMEM((nc, 256), F32)],
        compiler_params=_cparams(("parallel", "arbitrary")),
        name="cmp_prompt",
    )(tab, qn, kvn, mavg)


def _flash_prompt_kernel(tab_ref, q_ref, k_ref, v_ref, *rest, t, windowed, nsb, n_dist):
    if windowed:
        o_ref, bias_scr, m_scr, l_scr, acc_scr = rest
    else:
        sel_ref, o_ref, bias_scr, m_scr, l_scr, acc_scr = rest
    qi = pl.program_id(1)
    lane_i = lax.broadcasted_iota(I32, (1, t), 1)
    sub_j = lax.broadcasted_iota(I32, (t, 1), 0)

    @pl.when((pl.program_id(0) == 0) & (qi == 0))
    def _():
        def build(d, carry):
            bucket = _rel_bucket(d * t + lane_i - sub_j)
            for h in range(NSA_HEADS):
                bias_scr[h, d] = _bias_from_bucket(bucket, tab_ref, h)
            return carry

        lax.fori_loop(0, n_dist, build, 0)

    q = q_ref[...]
    qts = [_head_q128(q, h).T.astype(BF16) for h in range(NSA_HEADS)]
    qpos = qi * t + lane_i
    _flash_init(m_scr, l_scr, acc_scr)
    tk = 2 * t
    sub_k = lax.broadcasted_iota(I32, (tk, 1), 0)
    k_lo = jnp.maximum(qi - WINDOW // t, 0) // 2 if windowed else 0

    def body(ki, carry):
        k0 = pl.multiple_of(ki * tk, tk)
        kt = k_ref[pl.ds(k0, tk), :].astype(BF16)
        vtt = v_ref[pl.ds(k0, tk), :].T.astype(BF16)
        dist = qpos - (k0 + sub_k)
        ok = dist >= 0
        if windowed:
            ok = ok & (dist <= WINDOW)
        d = qi - 2 * ki
        d0 = jnp.minimum(d, n_dist - 1)
        d1 = jnp.clip(d - 1, 0, n_dist - 1)
        for g in range(NSA_GROUPS):
            if windowed:
                valid = ok
            else:
                blk = g * nsb + ki * (tk // SEL_BLOCK)
                rows = [sel_ref[pl.ds(blk + r, 1), :] for r in range(tk // SEL_BLOCK)]
                selm = rows[-1]
                for r in range(tk // SEL_BLOCK - 2, -1, -1):
                    selm = jnp.where(sub_k < (r + 1) * SEL_BLOCK, rows[r], selm)
                valid = ok & (selm > 0.5)
            for p in range(NSA_P):
                h = g * NSA_P + p
                bias = jnp.concatenate([bias_scr[h, d0], bias_scr[h, d1]], axis=0)
                s = _dot(kt, qts[h]) * (NSA_HD ** -0.5) + bias
                l = jnp.where(valid, s, NEG_INF)
                m_old = m_scr[h]
                m_new = jnp.maximum(m_old, jnp.max(l, axis=0, keepdims=True))
                pr = jnp.where(valid, jnp.exp(l - m_new), 0.0)
                alpha = jnp.exp(m_old - m_new)
                l_scr[h] = alpha * l_scr[h] + jnp.sum(pr, axis=0, keepdims=True)
                acc_scr[h] = alpha * acc_scr[h] + _dot(vtt, pr.astype(BF16))
                m_scr[h] = m_new
        return carry

    lax.fori_loop(k_lo, qi // 2 + 1, body, 0)
    for h in range(NSA_HEADS):
        o_ref[h] = (acc_scr[h] / jnp.maximum(l_scr[h], 1e-30)).T


def _flash_prompt(tab, qn, kv_arr, k_blk, v_blk, sel, t, windowed):
    b, l, _ = qn.shape
    assert l % (2 * t) == 0 and t == LANE
    n_dist = WINDOW // t + 1 if windowed else l // t
    kern = functools.partial(_flash_prompt_kernel, t=t, windowed=windowed, nsb=l // SEL_BLOCK, n_dist=n_dist)
    in_specs = [pl.BlockSpec(memory_space=pltpu.SMEM),
                pl.BlockSpec((None, t, 512), lambda i, j: (i, j, 0)),
                pl.BlockSpec((None, l, LANE), lambda i, j: (i, 0, k_blk)),
                pl.BlockSpec((None, l, LANE), lambda i, j: (i, 0, v_blk))]
    args = [tab, qn, kv_arr, kv_arr]
    if not windowed:
        in_specs += [pl.BlockSpec((None, sel.shape[1], t), lambda i, j: (i, 0, j))]
        args += [sel]
    return pl.pallas_call(
        kern,
        grid=(b, l // t),
        in_specs=in_specs,
        out_specs=pl.BlockSpec((None, NSA_HEADS, t, LANE), lambda i, j: (i, 0, j, 0)),
        out_shape=jax.ShapeDtypeStruct((b, NSA_HEADS, l, LANE), F32),
        scratch_shapes=[pltpu.VMEM((NSA_HEADS, n_dist, t, t), F32),
                        pltpu.VMEM((NSA_HEADS, 1, t), F32), pltpu.VMEM((NSA_HEADS, 1, t), F32),
                        pltpu.VMEM((NSA_HEADS, LANE, t), F32)],
        compiler_params=_cparams(("arbitrary", "arbitrary")),
        name="win_prompt" if windowed else "slc_prompt",
    )(*args)


def _cmp_sample_kernel(pt_ref, tab_ref, *rest, n_steps, pps, past_len):
    del pt_ref
    page_refs = rest[:pps]
    q_ref, avg_ref, ocmp_ref, sel_ref, kc_scr = rest[pps:]
    st = pl.program_id(1)
    nb = 2 * pps
    half = n_steps * nb
    pages = jnp.concatenate([r[...] for r in page_refs], axis=0)
    means = _dot_exact_lhs(avg_ref[...], pages) * (1.0 / CMP_BLOCK)
    off = pl.multiple_of(st * nb, 8)
    kc_scr[pl.ds(off, nb), :] = means[0:nb]
    kc_scr[pl.ds(half + off, nb), :] = means[nb:2 * nb]

    @pl.when(st == n_steps - 1)
    def _():
        qpos = past_len + lax.broadcasted_iota(I32, (8, 1), 0)
        scores = _cmp_attention(tab_ref, q_ref[...], kc_scr[...], qpos, ocmp_ref)
        k = min(SEL_TOPK, half + 1) - 1
        for g in range(NSA_GROUPS):
            sel_ref[g] = _topk_mask(scores[g], k, 1)


def _cmp_sample(page_table, tab, cache, q8, past_len):
    db, n_pages = page_table.shape
    pps = next(p for p in (16, 8, 4) if n_pages % p == 0)
    n_steps = n_pages // pps
    nsb = past_len // SEL_BLOCK
    avg = _perm_avg_matrix(4 * pps, pps * PAGE)
    kern = functools.partial(_cmp_sample_kernel, n_steps=n_steps, pps=pps, past_len=past_len)
    page_spec = lambda r: pl.BlockSpec((None, PAGE, 256), lambda i, j, pt: (pt[i, j * pps + r], 0, 0))
    grid_spec = pltpu.PrefetchScalarGridSpec(
        num_scalar_prefetch=1,
        grid=(db, n_steps),
        in_specs=[pl.BlockSpec(memory_space=pltpu.SMEM)]
        + [page_spec(r) for r in range(pps)]
        + [pl.BlockSpec((None, 8, 512), lambda i, j, pt: (i, 0, 0)),
           pl.BlockSpec(avg.shape, lambda i, j, pt: (0, 0))],
        out_specs=[pl.BlockSpec((None, NSA_HEADS, 8, LANE), lambda i, j, pt: (i, 0, 0, 0)),
                   pl.BlockSpec((None, NSA_GROUPS, 8, nsb), lambda i, j, pt: (i, 0, 0, 0))],
        scratch_shapes=[pltpu.VMEM((2 * nsb, 256), F32)],
    )
    return pl.pallas_call(
        kern,
        grid_spec=grid_spec,
        out_shape=[jax.ShapeDtypeStruct((db, NSA_HEADS, 8, LANE), F32),
                   jax.ShapeDtypeStruct((db, NSA_GROUPS, 8, nsb), F32)],
        compiler_params=_cparams(("parallel", "arbitrary")),
        name="cmp_sample",
    )(page_table, tab, *([cache] * pps), q8, avg)


def _slc_sample_kernel(pt_ref, tab_ref, q_ref, new_ref, sel_ref, e_ref, cache_ref, o_ref, kv_buf, sem,
                       *, n_pages, past_len):
    b = pl.program_id(0)

    def page_copy(pg):
        return pltpu.make_async_copy(cache_ref.at[pt_ref[b, pg], :, pl.ds(256, 256)],
                                     kv_buf.at[pl.ds(pg * PAGE, PAGE), :], sem.at[0])

    for pg in range(n_pages):
        page_copy(pg).start()

    q = q_ref[...]
    qpos = past_len + lax.broadcasted_iota(I32, (8, 1), 0)
    dist_p = qpos - lax.broadcasted_iota(I32, (1, past_len), 1)
    bucket_p = _rel_bucket(dist_p)
    dist_n = qpos - (past_len + lax.broadcasted_iota(I32, (1, PAGE), 1))
    ok_n = dist_n >= 0
    bucket_n = _rel_bucket(dist_n)
    knew = new_ref[:, 0:128].astype(BF16)
    vnew = new_ref[:, 128:256].astype(BF16)

    for pg in range(n_pages):
        page_copy(pg).wait()

    kall = kv_buf[:, 0:128].astype(BF16)
    vall = kv_buf[:, 128:256].astype(BF16)
    for g in range(NSA_GROUPS):
        qg = jnp.concatenate([_head_q128(q, g * NSA_P + p) for p in range(NSA_P)], axis=0).astype(BF16)
        s_p = _dot_nt(qg, kall) * (NSA_HD ** -0.5)
        s_n = _dot_nt(qg, knew) * (NSA_HD ** -0.5)
        mask_p = _dot(sel_ref[g].astype(BF16), e_ref[...]) > 0.5
        pps, pns, dens = [], [], []
        for p in range(NSA_P):
            h = g * NSA_P + p
            rows = slice(8 * p, 8 * p + 8)
            lp = jnp.where(mask_p, s_p[rows] + _bias_from_bucket(bucket_p, tab_ref, h), NEG_INF)
            ln = jnp.where(ok_n, s_n[rows] + _bias_from_bucket(bucket_n, tab_ref, h), NEG_INF)
            m = jnp.maximum(jnp.max(lp, axis=-1, keepdims=True), jnp.max(ln, axis=-1, keepdims=True))
            pp = jnp.where(mask_p, jnp.exp(lp - m), 0.0)
            pn = jnp.where(ok_n, jnp.exp(ln - m), 0.0)
            dens.append(jnp.sum(pp, axis=-1, keepdims=True) + jnp.sum(pn, axis=-1, keepdims=True))
            pps.append(pp.astype(BF16))
            pns.append(pn.astype(BF16))
        o = _dot(jnp.concatenate(pps, axis=0), vall) + _dot(jnp.concatenate(pns, axis=0), vnew)
        for p in range(NSA_P):
            o_ref[g * NSA_P + p] = o[8 * p:8 * p + 8] / jnp.maximum(dens[p], 1e-30)


def _slc_sample(page_table, tab, cache, q8, new_kv, sel, emat, past_len):
    db, n_pages = page_table.shape
    kern = functools.partial(_slc_sample_kernel, n_pages=n_pages, past_len=past_len)
    grid_spec = pltpu.PrefetchScalarGridSpec(
        num_scalar_prefetch=1,
        grid=(db,),
        in_specs=[pl.BlockSpec(memory_space=pltpu.SMEM),
                  pl.BlockSpec((None, 8, 512), lambda i, pt: (i, 0, 0)),
                  pl.BlockSpec((None, PAGE, 256), lambda i, pt: (i, 0, 1)),
                  pl.BlockSpec((None,) + sel.shape[1:], lambda i, pt: (i, 0, 0, 0)),
                  pl.BlockSpec(emat.shape, lambda i, pt: (0, 0)),
                  pl.BlockSpec(memory_space=pl.ANY)],
        out_specs=pl.BlockSpec((None, NSA_HEADS, 8, LANE), lambda i, pt: (i, 0, 0, 0)),
        scratch_shapes=[pltpu.VMEM((n_pages * PAGE, 256), F32), pltpu.SemaphoreType.DMA((1,))],
    )
    return pl.pallas_call(
        kern,
        grid_spec=grid_spec,
        out_shape=jax.ShapeDtypeStruct((db, NSA_HEADS, 8, LANE), F32),
        compiler_params=_cparams(("arbitrary",)),
        name="slc_sample",
    )(page_table, tab, q8, new_kv, sel, emat, cache)


def _win_sample_kernel(tab_ref, q_ref, w_ref, o_ref, *, past_len, wb, lq):
    q = q_ref[...]
    qpos = past_len + lax.broadcasted_iota(I32, (8, 1), 0)
    wseq = w_ref[...]
    n = wseq.shape[0]
    kw = wseq[:, 0:128].astype(BF16)
    vw = wseq[:, 128:256].astype(BF16)
    j = lax.broadcasted_iota(I32, (1, n), 1)
    kpos = past_len - wb + j
    dist = qpos - kpos
    valid = (dist >= 0) & (dist <= WINDOW) & (kpos >= 0) & (j < wb + lq)
    bucket = _rel_bucket(dist)
    for h in range(NSA_HEADS):
        qm = _head_q128(q, h).astype(BF16)
        s = _dot_nt(qm, kw) * (NSA_HD ** -0.5) + _bias_from_bucket(bucket, tab_ref, h)
        l = jnp.where(valid, s, NEG_INF)
        m = jnp.max(l, axis=-1, keepdims=True)
        pr = jnp.where(valid, jnp.exp(l - m), 0.0)
        pr = pr / jnp.maximum(jnp.sum(pr, axis=-1, keepdims=True), 1e-30)
        o_ref[h] = _dot(pr.astype(BF16), vw)


def _win_sample(tab, q8, wseq, past_len, wb, lq):
    db, n, _ = wseq.shape
    kern = functools.partial(_win_sample_kernel, past_len=past_len, wb=wb, lq=lq)
    return pl.pallas_call(
        kern,
        grid=(db,),
        in_specs=[pl.BlockSpec(memory_space=pltpu.SMEM),
                  pl.BlockSpec((None, 8, 512), lambda i: (i, 0, 0)),
                  pl.BlockSpec((None, n, 256), lambda i: (i, 0, 0))],
        out_specs=pl.BlockSpec((None, NSA_HEADS, 8, LANE), lambda i: (i, 0, 0, 0)),
        out_shape=jax.ShapeDtypeStruct((db, NSA_HEADS, 8, LANE), F32),
        compiler_params=_cparams(("parallel",)),
        name="win_sample",
    )(tab, q8, wseq)


def _outproj_kernel(x_ref, og_ref, oc_ref, os_ref, ow_ref, sm_ref, wg_ref, wn_ref, n2_ref,
                    hres_ref, xn_ref):
    gates = jax.nn.sigmoid(sm_ref[...])
    acc = x_ref[...] + _dot(og_ref[...].astype(BF16), wg_ref[...])
    for h in range(NSA_HEADS):
        c = 8 + 3 * h
        on = (gates[:, c:c + 1] * oc_ref[h] + gates[:, c + 1:c + 2] * os_ref[h]
              + gates[:, c + 2:c + 3] * ow_ref[h])
        acc = acc + _dot(on.astype(BF16), wn_ref[h])
    hres_ref[...] = acc
    ms = jnp.mean(acc * acc, axis=-1, keepdims=True)
    xn_ref[...] = acc * lax.rsqrt(ms + EPS) * n2_ref[...]


def _out_proj(x, og, ocmp, oslc, owin, sm, wg, wn, norm2, tm):
    b, l, _ = x.shape
    assert l % tm == 0
    seq = lambda w: pl.BlockSpec((None, tm, w), lambda i, j: (i, j, 0))
    heads = pl.BlockSpec((None, NSA_HEADS, tm, LANE), lambda i, j: (i, 0, j, 0))
    full = lambda a: pl.BlockSpec(a.shape, lambda i, j: (0,) * a.ndim)
    return pl.pallas_call(
        _outproj_kernel,
        grid=(b, l // tm),
        in_specs=[seq(D_MODEL), seq(512), heads, heads, heads, seq(LANE), full(wg), full(wn), full(norm2)],
        out_specs=[seq(D_MODEL), seq(D_MODEL)],
        out_shape=[jax.ShapeDtypeStruct((b, l, D_MODEL), F32)] * 2,
        compiler_params=_cparams(("parallel", "parallel")),
        name="out_proj",
    )(x, og, ocmp, oslc, owin, sm, wg, wn, norm2)


def _peer_topk_kernel(x_ref, wqt_ref, keys_ref, cflat_ref, eid_ref, gate_ref, qt_scr, sv_scr, si_scr, top_scr,
                      *, tm):
    qt_scr[...] = _dot_nt(wqt_ref[...], x_ref[...].astype(BF16))
    rows = _iota_f((PEER_NKEYS, tm), 0)
    cflat = jnp.broadcast_to(cflat_ref[...], (PEER_NCAND, tm))

    def head_body(h, carry):
        for c in range(2):
            off = pl.multiple_of(h * (2 * PEER_HALF) + c * PEER_HALF, PEER_HALF)
            qs = qt_scr[pl.ds(off, PEER_HALF), :].astype(BF16)
            s = _dot(keys_ref[h, c], qs)

            def round_body(r, s):
                m = jnp.max(s, axis=0, keepdims=True)
                idx = jnp.min(jnp.where(s == m, rows, float(PEER_NKEYS)), axis=0, keepdims=True)
                sv_scr[c, pl.ds(r, 1), :] = m
                si_scr[c, pl.ds(r, 1), :] = idx
                return jnp.where(rows == idx, REMOVED, s)

            lax.fori_loop(0, PEER_TOPK, round_body, s)
        s1 = sv_scr[0]
        s2 = sv_scr[1]
        i1 = si_scr[0] * float(PEER_NKEYS)
        i2 = si_scr[1]
        cand = [s1[0:1] + s2]
        eidc = [i1[0:1] + i2]
        for a in range(1, 8):
            cand.append(s1[a:a + 1] + s2[0:8])
            eidc.append(i1[a:a + 1] + i2[0:8])
        cand.append(s1[8:16] + s2[0:1])
        eidc.append(i1[8:16] + i2[0:1])
        cand = jnp.where(cflat >= 0.0, jnp.concatenate(cand, axis=0), REMOVED)
        eidc = jnp.concatenate(eidc, axis=0)

        def round2(r, cand):
            m = jnp.max(cand, axis=0, keepdims=True)
            f = jnp.min(jnp.where(cand == m, cflat, 1e9), axis=0, keepdims=True)
            hit = cflat == f
            top_scr[pl.ds(r, 1), :] = m
            eid_ref[h, pl.ds(r, 1), :] = jnp.sum(jnp.where(hit, eidc, 0.0), axis=0, keepdims=True).astype(I32)
            return jnp.where(hit, REMOVED, cand)

        lax.fori_loop(0, PEER_TOPK, round2, cand)
        top = top_scr[...]
        e = jnp.exp(top - jnp.max(top, axis=0, keepdims=True))
        gate_ref[h] = e / jnp.sum(e, axis=0, keepdims=True)
        return carry

    lax.fori_loop(0, PEER_HEADS, head_body, 0)


def _peer_cflat():
    rows = [(0, b) for b in range(16)]
    for a in range(1, 8):
        rows += [(a, b) for b in range(8)]
    rows += [(a, 0) for a in range(8, 16)]
    flat = [a * 16 + b if (a + 1) * (b + 1) <= PEER_TOPK else -1 for a, b in rows]
    assert len(flat) == PEER_NCAND
    return jnp.asarray(np.array(flat, np.float32).reshape(PEER_NCAND, 1))


def _peer_topk(xn, wqt, keys, tm):
    t = xn.shape[0]
    assert t % tm == 0
    cflat = _peer_cflat()
    kern = functools.partial(_peer_topk_kernel, tm=tm)
    full = lambda a: pl.BlockSpec(a.shape, lambda i: (0,) * a.ndim)
    out_spec = pl.BlockSpec((PEER_HEADS, PEER_TOPK, tm), lambda i: (0, 0, i))
    return pl.pallas_call(
        kern,
        grid=(t // tm,),
        in_specs=[pl.BlockSpec((tm, D_MODEL), lambda i: (i, 0)), full(wqt), full(keys), full(cflat)],
        out_specs=[out_spec, out_spec],
        out_shape=[jax.ShapeDtypeStruct((PEER_HEADS, PEER_TOPK, t), I32),
                   jax.ShapeDtypeStruct((PEER_HEADS, PEER_TOPK, t), F32)],
        scratch_shapes=[pltpu.VMEM((D_MODEL, tm), F32), pltpu.VMEM((2, PEER_TOPK, tm), F32),
                        pltpu.VMEM((2, PEER_TOPK, tm), F32), pltpu.VMEM((PEER_TOPK, tm), F32)],
        compiler_params=_cparams(("parallel",)),
        name="peer_topk",
    )(xn, wqt, keys, cflat)


PEER_GROUP = 2
PEER_SLOTS = 8
PEER_CHUNKS = D_MODEL // LANE


def _peer_expert_kernel(eid_ref, x_ref, gate_ref, hres_ref, uv_ref, y_ref, *scratch, tt):
    bufs = scratch[:PEER_SLOTS]
    sem = scratch[PEER_SLOTS]
    ahead = PEER_SLOTS - PEER_GROUP

    def row_copy(e, slot, k):
        return pltpu.make_async_copy(uv_ref.at[e], bufs[slot].at[pl.ds(k * PEER_CHUNKS, PEER_CHUNKS), :],
                                     sem.at[slot])

    def issue(t, slot, part=None):
        base = t * PEER_SEL
        i, n = (0, 1) if part is None else part
        for k in range(i * PEER_SEL // n, (i + 1) * PEER_SEL // n):
            row_copy(eid_ref[base + k], slot, k).start(priority=k % 2)

    def wait(slot):
        for k in range(PEER_SEL):
            row_copy(0, slot, k).wait()

    def chunk_words(slot, c):
        return bufs[slot][pl.ds(c, PEER_SEL, stride=PEER_CHUNKS), :]

    def evaluate(ts, slots, prefetch):
        for slot in slots:
            wait(slot)
        nparts = 2 * PEER_CHUNKS

        def start_part(i):
            if prefetch:
                for t, slot in zip(ts, slots):
                    issue(t + ahead, (slot + ahead) % PEER_SLOTS, (i, nparts))

        xbs = [x_ref[pl.ds(t, 1), :].astype(BF16).astype(F32) for t in ts]
        accs = [jnp.zeros((PEER_SEL, LANE), F32) for _ in ts]
        for c in range(PEER_CHUNKS):
            start_part(c)
            for j, slot in enumerate(slots):
                uf = pltpu.bitcast(chunk_words(slot, c) & jnp.int32(-65536), F32)
                accs[j] = accs[j] + uf * xbs[j][:, c * LANE:(c + 1) * LANE]
        w2s = []
        for j, t in enumerate(ts):
            act = jnp.sum(accs[j].T, axis=0, keepdims=True)
            w = gate_ref[pl.ds(t, 1), :] * (0.5 * act * (1.0 + lax.erf(act * SQRT_HALF)))
            wb = w.astype(BF16).astype(F32)
            w2s.append(jnp.broadcast_to(wb, (LANE, PEER_SEL)).T)
        outs = [[] for _ in ts]
        for c in range(PEER_CHUNKS):
            start_part(PEER_CHUNKS + c)
            for j, slot in enumerate(slots):
                vf = pltpu.bitcast(chunk_words(slot, c) << 16, F32)
                outs[j].append(jnp.sum(vf * w2s[j], axis=0, keepdims=True))
        for j, t in enumerate(ts):
            y_ref[pl.ds(t, 1), :] = hres_ref[pl.ds(t, 1), :] + jnp.concatenate(outs[j], axis=1)

    for t0 in range(ahead):
        issue(t0, t0)
    n_main = (tt - ahead) // PEER_SLOTS * PEER_SLOTS

    def body(i, carry):
        for r in range(0, PEER_SLOTS, PEER_GROUP):
            slots = list(range(r, r + PEER_GROUP))
            evaluate([i * PEER_SLOTS + s for s in slots], slots, True)
        return carry

    lax.fori_loop(0, n_main // PEER_SLOTS, body, 0)
    for t0 in range(n_main, tt, PEER_GROUP):
        ts = list(range(t0, t0 + PEER_GROUP))
        evaluate(ts, [t % PEER_SLOTS for t in ts], t0 + ahead < tt)


def _pack_expert_rows(u, v):
    bits = lambda a: lax.bitcast_convert_type(a.astype(BF16), jnp.uint16).astype(jnp.uint32)
    words = (bits(u) << 16) | bits(v)
    return lax.bitcast_convert_type(words, I32).reshape(-1, PEER_CHUNKS, LANE)


def _peer_experts(eid_flat, xn, gate, hres, uv, tt):
    t = xn.shape[0]
    assert t % tt == 0 and tt >= PEER_SLOTS
    kern = functools.partial(_peer_expert_kernel, tt=tt)
    row = lambda w: pl.BlockSpec((tt, w), lambda i: (i, 0))
    return pl.pallas_call(
        kern,
        grid=(t // tt,),
        in_specs=[pl.BlockSpec((tt * PEER_SEL,), lambda i: (i,), memory_space=pltpu.SMEM),
                  row(D_MODEL), row(PEER_SEL), row(D_MODEL),
                  pl.BlockSpec(memory_space=pl.ANY)],
        out_specs=row(D_MODEL),
        out_shape=jax.ShapeDtypeStruct((t, D_MODEL), F32),
        scratch_shapes=[pltpu.VMEM((PEER_SEL * PEER_CHUNKS, LANE), I32) for _ in range(PEER_SLOTS)]
        + [pltpu.SemaphoreType.DMA((PEER_SLOTS,))],
        compiler_params=_cparams(("arbitrary",)),
        name="peer_experts",
    )(eid_flat, xn, gate, hres, uv)


def _prep_params(norm1, w_in, gdn_conv_w, gdn_a_log, gdn_dt_bias, gdn_norm, nsa_q_norm, nsa_k_norm,
                 rel_bias, w_o, norm2, peer_wq, peer_subkeys, peer_u, peer_v):
    w = w_in[0]
    p = {}
    p["norm1"] = norm1[0][None]
    p["wm"] = jnp.concatenate([w[:, :OFF_B], w[:, OFF_NQ:OFF_NG]], axis=1).astype(BF16)
    p["ws"] = jnp.concatenate([w[:, OFF_B:OFF_NQ], w[:, OFF_NG:], jnp.zeros((D_MODEL, LANE - 32), F32)],
                              axis=1).astype(BF16)
    li = jnp.arange(LANE)
    p["seg"] = (li[:, None] // NSA_HD == li[None, :] // NSA_HD).astype(BF16)
    p["qg"] = jnp.tile(nsa_q_norm[0], 2)[None]
    p["kg"] = jnp.tile(nsa_k_norm[0], (1, 2))
    p["conv_w"] = gdn_conv_w[0]
    p["al_vec"] = jnp.zeros((1, LANE), F32).at[0, 4:8].set(gdn_a_log[0])
    p["dtb_vec"] = jnp.zeros((1, LANE), F32).at[0, 4:8].set(gdn_dt_bias[0])
    p["gnorm"] = gdn_norm[0][None]
    p["tab"] = rel_bias
    wo = w_o[0]
    p["wg"] = wo[:512].astype(BF16)
    wn = jnp.zeros((NSA_HEADS, LANE, D_MODEL), F32)
    for h in range(NSA_HEADS):
        g = h // NSA_P
        wn = wn.at[h, g * NSA_HD:(g + 1) * NSA_HD].set(wo[512 + h * NSA_HD:512 + (h + 1) * NSA_HD])
    p["wn"] = wn.astype(BF16)
    p["norm2"] = norm2[0][None]
    p["wqt"] = peer_wq[0].T.astype(BF16)
    p["keys"] = peer_subkeys[0].astype(BF16)
    p["uv"] = _pack_expert_rows(peer_u[0], peer_v[0])
    return p


def _perm_avg_matrix(n_blocks, n_rows):
    half = n_blocks // 2
    r = jnp.arange(n_blocks)
    blk = jnp.where(r < half, 2 * r, 2 * (r - half) + 1)
    return (jnp.arange(n_rows)[None, :] // CMP_BLOCK == blk[:, None]).astype(BF16)


def _token_mixer_tail(p, x, og, ocmp, oslc, owin, sm, tm_out, tm_topk, tt):
    b, l, _ = x.shape
    hres, xn2 = _out_proj(x, og, ocmp, oslc, owin, sm, p["wg"], p["wn"], p["norm2"], tm_out)
    t = b * l
    xn2 = xn2.reshape(t, D_MODEL)
    eid, gate = _peer_topk(xn2, p["wqt"], p["keys"], tm_topk)
    eid_flat = eid.reshape(PEER_SEL, t).T.reshape(t * PEER_SEL)
    gate_tok = gate.reshape(PEER_SEL, t).T
    y = _peer_experts(eid_flat, xn2, gate_tok, hres.reshape(t, D_MODEL), p["uv"], tt)
    return y.reshape(b, l, D_MODEL)


def kernel(x_prompt, x_sample, cache_nsa_kv, page_table, state_win_kv, state_conv, state_gdn, norm1, w_in, gdn_conv_w, gdn_a_log, gdn_dt_bias, gdn_norm, nsa_q_norm, nsa_k_norm, rel_bias, w_o, norm2, peer_wq, peer_subkeys, peer_u, peer_v):
    assert w_in.shape[0] == 1, "single layer"
    p = _prep_params(norm1, w_in, gdn_conv_w, gdn_a_log, gdn_dt_bias, gdn_norm, nsa_q_norm, nsa_k_norm,
                     rel_bias, w_o, norm2, peer_wq, peer_subkeys, peer_u, peer_v)
    b, l, _ = x_prompt.shape
    db, lq, _ = x_sample.shape
    n_pages = page_table.shape[1]
    past_len = n_pages * PAGE
    wb = state_win_kv.shape[2]
    assert cache_nsa_kv.shape[2] == PAGE and l >= WINDOW and l >= 3 and lq >= 3
    assert lq < CMP_BLOCK and lq <= 8 and wb == WINDOW and past_len >= wb

    tp = b * l
    tm = 256 if tp % 256 == 0 else LANE
    hc, z, sm, qn, kvn, winn = _in_proj(x_prompt.reshape(tp, D_MODEL), p["norm1"], p["wm"], p["ws"],
                                        p["seg"], p["qg"], p["kg"], tm)
    hc3, z3, sm3 = hc.reshape(b, l, GDN_CH), z.reshape(b, l, 512), sm.reshape(b, l, LANE)
    qn3, kvn3, winn3 = qn.reshape(b, l, 512), kvn.reshape(b, l, 512), winn.reshape(b, l, 256)
    og, gdn_p = _gdn(hc3, z3, sm3, p["conv_w"], p["al_vec"], p["dtb_vec"], p["gnorm"],
                     jnp.zeros((b, 3, GDN_CH), F32), jnp.zeros((b, GDN_HEADS, 128, 128), F32),
                     GDN_CHUNK, l)
    nc = l // CMP_BLOCK
    ocmp, sel = _cmp_prompt(p["tab"], qn3, kvn3, _perm_avg_matrix(nc, l), tm)
    oslc = _flash_prompt(p["tab"], qn3, kvn3, 2, 3, sel, LANE, False)
    owin = _flash_prompt(p["tab"], qn3, winn3, 0, 1, None, LANE, True)
    y_prompt = _token_mixer_tail(p, x_prompt, og, ocmp, oslc, owin, sm3, tm, tm, 256 if tp % 256 == 0 else 64)

    ts = db * lq
    hc_s, z_s, sm_s, qn_s, kvn_s, winn_s = _in_proj(x_sample.reshape(ts, D_MODEL), p["norm1"], p["wm"], p["ws"],
                                                    p["seg"], p["qg"], p["kg"], min(ts, 256))
    pad_rows = lambda a, n: jnp.pad(a.reshape(db, lq, a.shape[-1]), ((0, 0), (0, n - lq), (0, 0)))
    og_s, gdn_s = _gdn(pad_rows(hc_s, GDN_CHUNK), pad_rows(z_s, GDN_CHUNK), pad_rows(sm_s, GDN_CHUNK),
                       p["conv_w"], p["al_vec"], p["dtb_vec"], p["gnorm"], state_conv[0], state_gdn[0],
                       GDN_CHUNK, lq)
    cache3 = cache_nsa_kv[0].reshape(cache_nsa_kv.shape[1], PAGE, 512)
    q8 = pad_rows(qn_s, 8)
    ocmp_s, sel_s = _cmp_sample(page_table, p["tab"], cache3, q8, past_len)
    nsb_s = past_len // SEL_BLOCK
    emat_s = (jnp.arange(nsb_s)[:, None] == jnp.arange(past_len)[None, :] // SEL_BLOCK).astype(BF16)
    oslc_s = _slc_sample(page_table, p["tab"], cache3, q8, pad_rows(kvn_s, PAGE), sel_s, emat_s, past_len)
    wseq = jnp.concatenate([state_win_kv[0].reshape(db, wb, 256), pad_rows(winn_s, LANE)], axis=1)
    owin_s = _win_sample(p["tab"], q8, wseq, past_len, wb, lq)
    flat_heads = lambda o: o[:, :, :lq].transpose(1, 0, 2, 3).reshape(1, NSA_HEADS, ts, LANE)
    y_sample = _token_mixer_tail(p, x_sample.reshape(1, ts, D_MODEL), og_s[:, :lq].reshape(1, ts, 512),
                                 flat_heads(ocmp_s), flat_heads(oslc_s), flat_heads(owin_s),
                                 sm_s.reshape(1, ts, LANE), min(ts, 256), min(ts, 256), 64)

    kv_tail = (4, NSA_GROUPS, NSA_HD)
    return (y_prompt,
            y_sample.reshape(db, lq, D_MODEL),
            kvn.reshape((1, b, l) + kv_tail),
            winn3[:, l - WINDOW:].reshape(1, b, WINDOW, 2, NSA_GROUPS, NSA_HD),
            hc3[:, l - 3:][None],
            gdn_p[None],
            kvn_s.reshape((1, db, lq) + kv_tail),
            wseq[:, lq:lq + wb].reshape(1, db, wb, 2, NSA_GROUPS, NSA_HD),
            hc_s.reshape(db, lq, GDN_CH)[:, lq - 3:][None],
            gdn_s[None])
```

```python
import functools
import math

import numpy as np
import jax
import jax.numpy as jnp
from jax import lax
from jax.experimental import pallas as pl
from jax.experimental.pallas import tpu as pltpu

F32 = jnp.float32
BF16 = jnp.bfloat16
I32 = jnp.int32

D_MODEL = 1024
EPS = 1e-6
NEG_INF = -1e30
FORCE_SCORE = 1e4
GDN_HEADS = 4
GDN_DK = 128
GDN_CHUNK = 64
GDN_CH = 1536
NSA_HEADS = 8
NSA_GROUPS = 2
NSA_P = 4
NSA_HD = 64
CMP_BLOCK = 32
SEL_BLOCK = 64
SEL_TOPK = 16
WINDOW = 512
REL_BUCKETS = 32
PAGE = 128
PEER_HEADS = 8
PEER_NKEYS = 128
PEER_TOPK = 16
PEER_HALF = 64
PEER_SEL = PEER_HEADS * PEER_TOPK
PEER_NCAND = 80
OFF_B = 2048
OFF_NQ = 2056
OFF_NG = 3336
LANE = 128
VMEM_LIMIT = 56 * 1024 * 1024
REMOVED = -3.0e38
SQRT_HALF = 0.7071067811865476


def _cparams(sem):
    return pltpu.CompilerParams(dimension_semantics=sem, vmem_limit_bytes=VMEM_LIMIT)


def _dot(a, b):
    return jnp.dot(a, b, preferred_element_type=F32)


def _dot_nt(a, b):
    return lax.dot_general(a, b, (((1,), (1,)), ((), ())), preferred_element_type=F32)


def _dot_tn(a, b):
    return lax.dot_general(a, b, (((0,), (0,)), ((), ())), preferred_element_type=F32)


def _split2(x):
    hi = x.astype(BF16)
    lo = (x - hi.astype(F32)).astype(BF16)
    return hi, lo


def _split3(x):
    hi = x.astype(BF16)
    r = x - hi.astype(F32)
    mid = r.astype(BF16)
    lo = (r - mid.astype(F32)).astype(BF16)
    return hi, mid, lo


def _dot_exact_lhs(m01, x):
    hi, mid, lo = _split3(x)
    return _dot(m01, hi) + (_dot(m01, mid) + _dot(m01, lo))


def _mm3(a, b):
    ah, al = _split2(a)
    bh, bl = _split2(b)
    return _dot(ah, bh) + (_dot(ah, bl) + _dot(al, bh))


def _iota_f(shape, axis):
    return lax.broadcasted_iota(I32, shape, axis).astype(F32)


def _rel_bucket(dist):
    d = jnp.maximum(dist, 0)
    df = jnp.maximum(d, 1).astype(F32)
    large = 16 + (jnp.log(df / 16.0) / math.log(128.0) * 16.0).astype(I32)
    large = jnp.minimum(large, REL_BUCKETS - 1)
    return jnp.where(d < 16, d, large)


def _bias_from_bucket(bucket, tab_ref, head, lo=0, hi=REL_BUCKETS - 1):
    if isinstance(lo, int) and isinstance(hi, int):
        b = jnp.zeros(bucket.shape, F32)
        for k in range(lo, hi + 1):
            b = jnp.where(bucket == k, tab_ref[k, head], b)
        return b

    def body(k, b):
        return jnp.where(bucket == k, tab_ref[k, head], b)

    return lax.fori_loop(lo, hi + 1, body, jnp.zeros(bucket.shape, F32))


def _topk_mask(s, k, axis):
    n = s.shape[axis]
    ids = _iota_f(s.shape, axis)
    sel = jnp.zeros(s.shape, F32)
    for _ in range(k):
        m = jnp.max(s, axis=axis, keepdims=True)
        idx = jnp.min(jnp.where(s == m, ids, float(n)), axis=axis, keepdims=True)
        hit = ids == idx
        sel = jnp.where(hit, 1.0, sel)
        s = jnp.where(hit, REMOVED, s)
    return sel


def _seg_rmsnorm(v, gain, seg):
    sq = v * v
    hi, lo = _split2(sq)
    ssum = _dot(hi, seg) + _dot(lo, seg)
    return v * lax.rsqrt(ssum * (1.0 / NSA_HD) + EPS) * gain


def _inproj_kernel(x_ref, g1_ref, wm_ref, ws_ref, seg_ref, qg_ref, kg_ref,
                   hc_ref, z_ref, sm_ref, q_ref, kv_ref, win_ref):
    x = x_ref[...]
    ms = jnp.mean(x * x, axis=-1, keepdims=True)
    xn = (x * lax.rsqrt(ms + EPS) * g1_ref[...]).astype(BF16)
    h = _dot(xn, wm_ref[...])
    sm_ref[...] = _dot(xn, ws_ref[...])
    hc_ref[...] = h[:, :GDN_CH]
    z_ref[...] = h[:, GDN_CH:2048]
    seg = seg_ref[...]
    qg = qg_ref[...]
    kg = kg_ref[...]
    for i in range(4):
        q_ref[:, i * LANE:(i + 1) * LANE] = _seg_rmsnorm(h[:, 2048 + i * LANE:2048 + (i + 1) * LANE], qg, seg)
    kv_ref[:, 0:128] = _seg_rmsnorm(h[:, 2560:2688], kg[0:1], seg)
    kv_ref[:, 128:256] = h[:, 2688:2816]
    kv_ref[:, 256:384] = _seg_rmsnorm(h[:, 2816:2944], kg[1:2], seg)
    kv_ref[:, 384:512] = h[:, 2944:3072]
    win_ref[:, 0:128] = _seg_rmsnorm(h[:, 3072:3200], kg[2:3], seg)
    win_ref[:, 128:256] = h[:, 3200:3328]


def _in_proj(x, norm1, wm, ws, seg, qg, kg, tm):
    t = x.shape[0]
    assert t % tm == 0
    row = lambda w: pl.BlockSpec((tm, w), lambda i: (i, 0))
    full = lambda a: pl.BlockSpec(a.shape, lambda i: (0,) * a.ndim)
    widths = (GDN_CH, 512, LANE, 512, 512, 256)
    return pl.pallas_call(
        _inproj_kernel,
        grid=(t // tm,),
        in_specs=[row(D_MODEL), full(norm1), full(wm), full(ws), full(seg), full(qg), full(kg)],
        out_specs=[row(w) for w in widths],
        out_shape=[jax.ShapeDtypeStruct((t, w), F32) for w in widths],
        compiler_params=_cparams(("parallel",)),
        name="in_proj",
    )(x, norm1, wm, ws, seg, qg, kg)


def _inv_unit_lower(lmat, c):
    ri = lax.broadcasted_iota(I32, lmat.shape, 0)
    ci = lax.broadcasted_iota(I32, lmat.shape, 1)
    eye = jnp.where(ri == ci, 1.0, 0.0).astype(F32)
    n = -lmat
    p = eye + n
    m = _mm3(n, n)
    span = 2
    while True:
        mm = _mm3 if span == 2 else (lambda a, b: _dot(a.astype(BF16), b.astype(BF16)))
        p = p + mm(p, m)
        span *= 2
        if span >= c:
            break
        m = mm(m, m)
    return p


def _gdn_kernel(hc_ref, z_ref, sm_ref, cw_ref, al_ref, dtb_ref, gn_ref, conv0_ref, s0_ref,
                og_ref, sfin_ref, xbuf, s_scr, *, c, l_valid, n_chunks):
    ci = pl.program_id(1)

    @pl.when(ci == 0)
    def _():
        xbuf[0:8, :] = jnp.zeros((8, GDN_CH), F32)
        xbuf[5:8, :] = conv0_ref[...]
        s_scr[...] = s0_ref[...]

    xbuf[8:8 + c, :] = hc_ref[...]
    w = cw_ref[...]
    y = (xbuf[5:5 + c, :] * w[0:1] + xbuf[6:6 + c, :] * w[1:2]
         + xbuf[7:7 + c, :] * w[2:3] + xbuf[8:8 + c, :] * w[3:4])
    tail = xbuf[5 + c:8 + c, :]
    xbuf[5:8, :] = tail
    y = y * jax.nn.sigmoid(y)

    sm = sm_ref[...]
    rowid = ci * c + lax.broadcasted_iota(I32, (c, 1), 0)
    rvalid = rowid < l_valid
    beta_all = jnp.where(rvalid, jax.nn.sigmoid(sm), 0.0)
    sp_in = sm + dtb_ref[...]
    softplus = jnp.maximum(sp_in, 0.0) + jnp.log1p(jnp.exp(-jnp.abs(sp_in)))
    g_all = jnp.where(rvalid, -jnp.exp(al_ref[...]) * softplus, 0.0)

    ri = lax.broadcasted_iota(I32, (c, c), 0)
    cj = lax.broadcasted_iota(I32, (c, c), 1)
    tri = jnp.where(ri >= cj, 1.0, 0.0).astype(BF16)
    triu = jnp.where(ri <= cj, 1.0, 0.0).astype(BF16)
    g_hi, g_mid, g_lo = _split3(g_all)
    gcum = _dot(tri, g_hi) + (_dot(tri, g_mid) + _dot(tri, g_lo))
    gcum_t = _dot_tn(g_hi, triu) + (_dot_tn(g_mid, triu) + _dot_tn(g_lo, triu))
    gn = gn_ref[...]

    heads = range(GDN_HEADS)
    stack = lambda pieces: jnp.concatenate(pieces, axis=0)

    def l2n(x):
        return jnp.where(rvalid, x * lax.rsqrt(jnp.sum(x * x, axis=-1, keepdims=True) + EPS), 0.0)

    q_s = stack([l2n(y[:, h * 128:(h + 1) * 128]) * (GDN_DK ** -0.5) for h in heads])
    k_s = stack([l2n(y[:, 512 + h * 128:512 + (h + 1) * 128]) for h in heads])
    v_s = stack([jnp.where(rvalid, y[:, 1024 + h * 128:1024 + (h + 1) * 128], 0.0) for h in heads])
    beta_s = stack([beta_all[:, h:h + 1] for h in heads])
    gc_s = stack([gcum[:, 4 + h:5 + h] for h in heads])
    gl_s = stack([jnp.broadcast_to(gcum[c - 1:c, 4 + h:5 + h], (c, 1)) for h in heads])
    gct_s = jnp.concatenate([gcum_t[4 + h:5 + h, :] for h in heads], axis=1)
    r = GDN_HEADS * c
    rr = lax.broadcasted_iota(I32, (r, r), 0)
    rc = lax.broadcasted_iota(I32, (r, r), 1)
    same = (rr // c) == (rc // c)
    causal = same & (rr >= rc)
    strict = same & (rr > rc)
    decay = jnp.where(causal, jnp.exp(jnp.where(causal, gc_s - gct_s, 0.0)), 0.0)
    kb = k_s * beta_s
    ksb = k_s.astype(BF16)
    lmat = jnp.where(strict, _dot_nt(kb.astype(BF16), ksb) * decay, 0.0)
    tmat = _inv_unit_lower(lmat, c).astype(BF16)
    eg = jnp.exp(gc_s)
    u = _dot(tmat, (v_s * beta_s).astype(BF16))
    wmat = _dot(tmat, (kb * eg).astype(BF16)).astype(BF16)
    a_intra = (_dot_nt(q_s.astype(BF16), ksb) * decay).astype(BF16)
    q_dec = (q_s * eg).astype(BF16)
    k_dec = (k_s * jnp.exp(gl_s - gc_s)).astype(BF16)
    rows = lambda a, h: a[h * c:(h + 1) * c]
    sbs = [s_scr[h].astype(BF16) for h in heads]
    v_new = stack([rows(u, h) - _dot(rows(wmat, h), sbs[h]) for h in heads])
    vnb = v_new.astype(BF16)
    o_intra = _dot(a_intra, vnb)
    for h in heads:
        g_last = jnp.exp(gcum[c - 1:c, 4 + h:5 + h])
        s_scr[h] = s_scr[h] * g_last + _dot_tn(rows(k_dec, h), rows(vnb, h))
        o = _dot(rows(q_dec, h), sbs[h]) + rows(o_intra, h)
        o = o * lax.rsqrt(jnp.mean(o * o, axis=-1, keepdims=True) + EPS) * gn
        zh = z_ref[:, h * 128:(h + 1) * 128]
        og_ref[:, h * 128:(h + 1) * 128] = o * (zh * jax.nn.sigmoid(zh))

    @pl.when(ci == n_chunks - 1)
    def _():
        sfin_ref[...] = s_scr[...]


def _gdn(hc, z, sm, conv_w, al_vec, dtb_vec, gnorm, conv0, s0, c, l_valid):
    b, lp, _ = hc.shape
    assert lp % c == 0
    n_chunks = lp // c
    full = lambda a: pl.BlockSpec(a.shape, lambda i, j: (0,) * a.ndim)
    seq = lambda w: pl.BlockSpec((None, c, w), lambda i, j: (i, j, 0))
    kern = functools.partial(_gdn_kernel, c=c, l_valid=l_valid, n_chunks=n_chunks)
    return pl.pallas_call(
        kern,
        grid=(b, n_chunks),
        in_specs=[seq(GDN_CH), seq(512), seq(LANE), full(conv_w), full(al_vec), full(dtb_vec), full(gnorm),
                  pl.BlockSpec((None, 3, GDN_CH), lambda i, j: (i, 0, 0)),
                  pl.BlockSpec((None, GDN_HEADS, 128, 128), lambda i, j: (i, 0, 0, 0))],
        out_specs=[seq(512), pl.BlockSpec((None, GDN_HEADS, 128, 128), lambda i, j: (i, 0, 0, 0))],
        out_shape=[jax.ShapeDtypeStruct((b, lp, 512), F32),
                   jax.ShapeDtypeStruct((b, GDN_HEADS, 128, 128), F32)],
        scratch_shapes=[pltpu.VMEM((c + 8, GDN_CH), F32), pltpu.VMEM((GDN_HEADS, 128, 128), F32)],
        compiler_params=_cparams(("parallel", "arbitrary")),
        name="gdn",
    )(hc, z, sm, conv_w, al_vec, dtb_vec, gnorm, conv0, s0)


def _head_q128(q, h):
    g = h // NSA_P
    piece = q[:, (h // 2) * LANE:(h // 2 + 1) * LANE]
    lane = lax.broadcasted_iota(I32, piece.shape, 1)
    keep = (lane >= NSA_HD) if h % 2 == 1 else (lane < NSA_HD)
    qm = jnp.where(keep, piece, 0.0)
    if h % 2 != g:
        qm = pltpu.roll(qm, NSA_HD, 1)
    return qm


def _select_scores(imp, qpos, nsb):
    score = imp[:, :nsb] + imp[:, nsb:]
    j = lax.broadcasted_iota(I32, score.shape, 1)
    cur = qpos // SEL_BLOCK
    forced = (j == 0) | (j == cur) | (j == cur - 1)
    future = j * SEL_BLOCK > qpos
    return jnp.where(future, -1.0, jnp.where(forced, FORCE_SCORE, score))


def _cmp_attention(tab_ref, q, kcv, qpos, ocmp_ref):
    nc = kcv.shape[0]
    nsb = nc // 2
    kc = kcv[:, 0:128].astype(BF16)
    vc = kcv[:, 128:256].astype(BF16)
    lane = lax.broadcasted_iota(I32, (1, nc), 1)
    blk = jnp.where(lane < nsb, 2 * lane, 2 * (lane - nsb) + 1)
    dist = qpos - (blk * CMP_BLOCK + CMP_BLOCK - 1)
    valid = dist >= 0
    bucket = _rel_bucket(dist)
    scores = []
    for g in range(NSA_GROUPS):
        imp = jnp.zeros((q.shape[0], nc), F32)
        for p in range(NSA_P):
            h = g * NSA_P + p
            qm = _head_q128(q, h).astype(BF16)
            logits = _dot_nt(qm, kc) * (NSA_HD ** -0.5) + _bias_from_bucket(bucket, tab_ref, h)
            l = jnp.where(valid, logits, NEG_INF)
            m = jnp.max(l, axis=-1, keepdims=True)
            pr = jnp.where(valid, jnp.exp(l - m), 0.0)
            pr = pr / jnp.maximum(jnp.sum(pr, axis=-1, keepdims=True), 1e-30)
            ocmp_ref[h] = _dot(pr.astype(BF16), vc)
            imp = imp + pr
        scores.append(_select_scores(imp, qpos, nsb))
    return scores


def _flash_init(m_scr, l_scr, acc_scr):
    m_scr[...] = jnp.full(m_scr.shape, NEG_INF, F32)
    l_scr[...] = jnp.zeros(l_scr.shape, F32)
    acc_scr[...] = jnp.zeros(acc_scr.shape, F32)


def _cmp_prompt_kernel(tab_ref, q_ref, kv_ref, mavg_ref, ocmp_ref, sel_ref, kc_scr, *, tq, nsb):
    qi = pl.program_id(1)

    @pl.when(qi == 0)
    def _():
        kc_scr[...] = _dot_exact_lhs(mavg_ref[...], kv_ref[...]) * (1.0 / CMP_BLOCK)

    qpos = qi * tq + lax.broadcasted_iota(I32, (tq, 1), 0)
    scores = _cmp_attention(tab_ref, q_ref[...], kc_scr[...], qpos, ocmp_ref)
    st = jnp.concatenate(scores, axis=1).T
    k = min(SEL_TOPK, nsb)
    sel_ref[...] = jnp.concatenate([_topk_mask(st[:nsb], k, 0), _topk_mask(st[nsb:], k, 0)], axis=0)


def _cmp_prompt(tab, qn, kvn, mavg, tq):
    b, l, _ = qn.shape
    nc = l // CMP_BLOCK
    assert l % tq == 0 and l % SEL_BLOCK == 0
    kern = functools.partial(_cmp_prompt_kernel, tq=tq, nsb=nc // 2)
    return pl.pallas_call(
        kern,
        grid=(b, l // tq),
        in_specs=[pl.BlockSpec(memory_space=pltpu.SMEM),
                  pl.BlockSpec((None, tq, 512), lambda i, j: (i, j, 0)),
                  pl.BlockSpec((None, l, 256), lambda i, j: (i, 0, 0)),
                  pl.BlockSpec(mavg.shape, lambda i, j: (0, 0))],
        out_specs=[pl.BlockSpec((None, NSA_HEADS, tq, LANE), lambda i, j: (i, 0, j, 0)),
                   pl.BlockSpec((None, nc, tq), lambda i, j: (i, 0, j))],
        out_shape=[jax.ShapeDtypeStruct((b, NSA_HEADS, l, LANE), F32),
                   jax.ShapeDtypeStruct((b, nc, l), F32)],
        scratch_shapes=[pltpu.VMEM((nc, 256), F32)],
        compiler_params=_cparams(("parallel", "arbitrary")),
        name="cmp_prompt",
    )(tab, qn, kvn, mavg)


def _flash_prompt_kernel(tab_ref, q_ref, k_ref, v_ref, *rest, t, windowed, nsb, n_dist):
    if windowed:
        o_ref, bias_scr, m_scr, l_scr, acc_scr = rest
    else:
        sel_ref, o_ref, bias_scr, m_scr, l_scr, acc_scr = rest
    qi = pl.program_id(1)
    lane_i = lax.broadcasted_iota(I32, (1, t), 1)
    sub_j = lax.broadcasted_iota(I32, (t, 1), 0)

    @pl.when((pl.program_id(0) == 0) & (qi == 0))
    def _():
        def build(d, carry):
            bucket = _rel_bucket(d * t + lane_i - sub_j)
            for h in range(NSA_HEADS):
                bias_scr[h, d] = _bias_from_bucket(bucket, tab_ref, h)
            return carry

        lax.fori_loop(0, n_dist, build, 0)

    q = q_ref[...]
    qts = [_head_q128(q, h).T.astype(BF16) for h in range(NSA_HEADS)]
    qpos = qi * t + lane_i
    _flash_init(m_scr, l_scr, acc_scr)
    tk = 2 * t
    sub_k = lax.broadcasted_iota(I32, (tk, 1), 0)
    k_lo = jnp.maximum(qi - WINDOW // t, 0) // 2 if windowed else 0

    def body(ki, carry):
        k0 = pl.multiple_of(ki * tk, tk)
        kt = k_ref[pl.ds(k0, tk), :].astype(BF16)
        vtt = v_ref[pl.ds(k0, tk), :].T.astype(BF16)
        dist = qpos - (k0 + sub_k)
        ok = dist >= 0
        if windowed:
            ok = ok & (dist <= WINDOW)
        d = qi - 2 * ki
        d0 = jnp.minimum(d, n_dist - 1)
        d1 = jnp.clip(d - 1, 0, n_dist - 1)
        for g in range(NSA_GROUPS):
            if windowed:
                valid = ok
            else:
                blk = g * nsb + ki * (tk // SEL_BLOCK)
                rows = [sel_ref[pl.ds(blk + r, 1), :] for r in range(tk // SEL_BLOCK)]
                selm = rows[-1]
                for r in range(tk // SEL_BLOCK - 2, -1, -1):
                    selm = jnp.where(sub_k < (r + 1) * SEL_BLOCK, rows[r], selm)
                valid = ok & (selm > 0.5)
            for p in range(NSA_P):
                h = g * NSA_P + p
                bias = jnp.concatenate([bias_scr[h, d0], bias_scr[h, d1]], axis=0)
                s = _dot(kt, qts[h]) * (NSA_HD ** -0.5) + bias
                l = jnp.where(valid, s, NEG_INF)
                m_old = m_scr[h]
                m_new = jnp.maximum(m_old, jnp.max(l, axis=0, keepdims=True))
                pr = jnp.where(valid, jnp.exp(l - m_new), 0.0)
                alpha = jnp.exp(m_old - m_new)
                l_scr[h] = alpha * l_scr[h] + jnp.sum(pr, axis=0, keepdims=True)
                acc_scr[h] = alpha * acc_scr[h] + _dot(vtt, pr.astype(BF16))
                m_scr[h] = m_new
        return carry

    lax.fori_loop(k_lo, qi // 2 + 1, body, 0)
    for h in range(NSA_HEADS):
        o_ref[h] = (acc_scr[h] / jnp.maximum(l_scr[h], 1e-30)).T


def _flash_prompt(tab, qn, kv_arr, k_blk, v_blk, sel, t, windowed):
    b, l, _ = qn.shape
    assert l % (2 * t) == 0 and t == LANE
    n_dist = WINDOW // t + 1 if windowed else l // t
    kern = functools.partial(_flash_prompt_kernel, t=t, windowed=windowed, nsb=l // SEL_BLOCK, n_dist=n_dist)
    in_specs = [pl.BlockSpec(memory_space=pltpu.SMEM),
                pl.BlockSpec((None, t, 512), lambda i, j: (i, j, 0)),
                pl.BlockSpec((None, l, LANE), lambda i, j: (i, 0, k_blk)),
                pl.BlockSpec((None, l, LANE), lambda i, j: (i, 0, v_blk))]
    args = [tab, qn, kv_arr, kv_arr]
    if not windowed:
        in_specs += [pl.BlockSpec((None, sel.shape[1], t), lambda i, j: (i, 0, j))]
        args += [sel]
    return pl.pallas_call(
        kern,
        grid=(b, l // t),
        in_specs=in_specs,
        out_specs=pl.BlockSpec((None, NSA_HEADS, t, LANE), lambda i, j: (i, 0, j, 0)),
        out_shape=jax.ShapeDtypeStruct((b, NSA_HEADS, l, LANE), F32),
        scratch_shapes=[pltpu.VMEM((NSA_HEADS, n_dist, t, t), F32),
                        pltpu.VMEM((NSA_HEADS, 1, t), F32), pltpu.VMEM((NSA_HEADS, 1, t), F32),
                        pltpu.VMEM((NSA_HEADS, LANE, t), F32)],
        compiler_params=_cparams(("arbitrary", "arbitrary")),
        name="win_prompt" if windowed else "slc_prompt",
    )(*args)


def _nsa_sample_kernel(pt_ref, tab_ref, q_ref, new_ref, e_ref, cache_ref, ocmp_ref, o_ref, kv_buf, sem,
                       *, n_pages, past_len, db):
    b = pl.program_id(0)
    slot = b % 2
    nsb = past_len // SEL_BLOCK

    def page_copy(seq, buf, pg):
        return pltpu.make_async_copy(cache_ref.at[pt_ref[seq, pg]],
                                     kv_buf.at[buf, pl.ds(pg * PAGE, PAGE), :], sem.at[buf])

    @pl.when(b == 0)
    def _():
        for pg in range(n_pages):
            page_copy(0, 0, pg).start()

    @pl.when(b + 1 < db)
    def _():
        for pg in range(n_pages):
            page_copy(b + 1, 1 - slot, pg).start()

    q = q_ref[...]
    qpos = past_len + lax.broadcasted_iota(I32, (8, 1), 0)
    dist_p = qpos - lax.broadcasted_iota(I32, (1, past_len), 1)
    bucket_p = _rel_bucket(dist_p)
    dist_n = qpos - (past_len + lax.broadcasted_iota(I32, (1, PAGE), 1))
    ok_n = dist_n >= 0
    bucket_n = _rel_bucket(dist_n)
    knew = new_ref[:, 0:128].astype(BF16)
    vnew = new_ref[:, 128:256].astype(BF16)

    for pg in range(n_pages):
        page_copy(b, slot, pg).wait()

    x3 = kv_buf[slot, :, 0:256].reshape(nsb, SEL_BLOCK, 256)
    kcv = jnp.concatenate([jnp.sum(x3[:, :CMP_BLOCK], axis=1), jnp.sum(x3[:, CMP_BLOCK:], axis=1)],
                          axis=0) * (1.0 / CMP_BLOCK)
    scores = _cmp_attention(tab_ref, q, kcv, qpos, ocmp_ref)
    k_sel = min(SEL_TOPK, nsb + 1) - 1

    kall = kv_buf[slot, :, 256:384].astype(BF16)
    vall = kv_buf[slot, :, 384:512].astype(BF16)
    for g in range(NSA_GROUPS):
        sel_g = _topk_mask(scores[g], k_sel, 1).astype(BF16)
        qg = jnp.concatenate([_head_q128(q, g * NSA_P + p) for p in range(NSA_P)], axis=0).astype(BF16)
        s_p = _dot_nt(qg, kall) * (NSA_HD ** -0.5)
        s_n = _dot_nt(qg, knew) * (NSA_HD ** -0.5)
        mask_p = _dot(sel_g, e_ref[...]) > 0.5
        pps, pns, dens = [], [], []
        for p in range(NSA_P):
            h = g * NSA_P + p
            rows = slice(8 * p, 8 * p + 8)
            lp = jnp.where(mask_p, s_p[rows] + _bias_from_bucket(bucket_p, tab_ref, h), NEG_INF)
            ln = jnp.where(ok_n, s_n[rows] + _bias_from_bucket(bucket_n, tab_ref, h), NEG_INF)
            m = jnp.maximum(jnp.max(lp, axis=-1, keepdims=True), jnp.max(ln, axis=-1, keepdims=True))
            pp = jnp.where(mask_p, jnp.exp(lp - m), 0.0)
            pn = jnp.where(ok_n, jnp.exp(ln - m), 0.0)
            dens.append(jnp.sum(pp, axis=-1, keepdims=True) + jnp.sum(pn, axis=-1, keepdims=True))
            pps.append(pp.astype(BF16))
            pns.append(pn.astype(BF16))
        o = _dot(jnp.concatenate(pps, axis=0), vall) + _dot(jnp.concatenate(pns, axis=0), vnew)
        for p in range(NSA_P):
            o_ref[g * NSA_P + p] = o[8 * p:8 * p + 8] / jnp.maximum(dens[p], 1e-30)


def _nsa_sample(page_table, tab, cache, q8, new_kv, emat, past_len):
    db, n_pages = page_table.shape
    kern = functools.partial(_nsa_sample_kernel, n_pages=n_pages, past_len=past_len, db=db)
    heads = pl.BlockSpec((None, NSA_HEADS, 8, LANE), lambda i, pt: (i, 0, 0, 0))
    grid_spec = pltpu.PrefetchScalarGridSpec(
        num_scalar_prefetch=1,
        grid=(db,),
        in_specs=[pl.BlockSpec(memory_space=pltpu.SMEM),
                  pl.BlockSpec((None, 8, 512), lambda i, pt: (i, 0, 0)),
                  pl.BlockSpec((None, PAGE, 256), lambda i, pt: (i, 0, 1)),
                  pl.BlockSpec(emat.shape, lambda i, pt: (0, 0)),
                  pl.BlockSpec(memory_space=pl.ANY)],
        out_specs=[heads, heads],
        scratch_shapes=[pltpu.VMEM((2, n_pages * PAGE, 512), F32), pltpu.SemaphoreType.DMA((2,))],
    )
    return pl.pallas_call(
        kern,
        grid_spec=grid_spec,
        out_shape=[jax.ShapeDtypeStruct((db, NSA_HEADS, 8, LANE), F32)] * 2,
        compiler_params=_cparams(("arbitrary",)),
        name="nsa_sample",
    )(page_table, tab, q8, new_kv, emat, cache)


def _win_sample_kernel(tab_ref, q_ref, w_ref, o_ref, *, past_len, wb, lq):
    q = q_ref[...]
    qpos = past_len + lax.broadcasted_iota(I32, (8, 1), 0)
    wseq = w_ref[...]
    n = wseq.shape[0]
    kw = wseq[:, 0:128].astype(BF16)
    vw = wseq[:, 128:256].astype(BF16)
    j = lax.broadcasted_iota(I32, (1, n), 1)
    kpos = past_len - wb + j
    dist = qpos - kpos
    valid = (dist >= 0) & (dist <= WINDOW) & (kpos >= 0) & (j < wb + lq)
    bucket = _rel_bucket(dist)
    for h in range(NSA_HEADS):
        qm = _head_q128(q, h).astype(BF16)
        s = _dot_nt(qm, kw) * (NSA_HD ** -0.5) + _bias_from_bucket(bucket, tab_ref, h)
        l = jnp.where(valid, s, NEG_INF)
        m = jnp.max(l, axis=-1, keepdims=True)
        pr = jnp.where(valid, jnp.exp(l - m), 0.0)
        pr = pr / jnp.maximum(jnp.sum(pr, axis=-1, keepdims=True), 1e-30)
        o_ref[h] = _dot(pr.astype(BF16), vw)


def _win_sample(tab, q8, wseq, past_len, wb, lq):
    db, n, _ = wseq.shape
    kern = functools.partial(_win_sample_kernel, past_len=past_len, wb=wb, lq=lq)
    return pl.pallas_call(
        kern,
        grid=(db,),
        in_specs=[pl.BlockSpec(memory_space=pltpu.SMEM),
                  pl.BlockSpec((None, 8, 512), lambda i: (i, 0, 0)),
                  pl.BlockSpec((None, n, 256), lambda i: (i, 0, 0))],
        out_specs=pl.BlockSpec((None, NSA_HEADS, 8, LANE), lambda i: (i, 0, 0, 0)),
        out_shape=jax.ShapeDtypeStruct((db, NSA_HEADS, 8, LANE), F32),
        compiler_params=_cparams(("parallel",)),
        name="win_sample",
    )(tab, q8, wseq)


def _outproj_kernel(x_ref, og_ref, oc_ref, os_ref, ow_ref, sm_ref, wg_ref, wn_ref, n2_ref,
                    hres_ref, xn_ref):
    gates = jax.nn.sigmoid(sm_ref[...])
    acc = x_ref[...] + _dot(og_ref[...].astype(BF16), wg_ref[...])
    for h in range(NSA_HEADS):
        c = 8 + 3 * h
        on = (gates[:, c:c + 1] * oc_ref[h] + gates[:, c + 1:c + 2] * os_ref[h]
              + gates[:, c + 2:c + 3] * ow_ref[h])
        acc = acc + _dot(on.astype(BF16), wn_ref[h])
    hres_ref[...] = acc
    ms = jnp.mean(acc * acc, axis=-1, keepdims=True)
    xn_ref[...] = acc * lax.rsqrt(ms + EPS) * n2_ref[...]


def _out_proj(x, og, ocmp, oslc, owin, sm, wg, wn, norm2, tm):
    b, l, _ = x.shape
    assert l % tm == 0
    seq = lambda w: pl.BlockSpec((None, tm, w), lambda i, j: (i, j, 0))
    heads = pl.BlockSpec((None, NSA_HEADS, tm, LANE), lambda i, j: (i, 0, j, 0))
    full = lambda a: pl.BlockSpec(a.shape, lambda i, j: (0,) * a.ndim)
    return pl.pallas_call(
        _outproj_kernel,
        grid=(b, l // tm),
        in_specs=[seq(D_MODEL), seq(512), heads, heads, heads, seq(LANE), full(wg), full(wn), full(norm2)],
        out_specs=[seq(D_MODEL), seq(D_MODEL)],
        out_shape=[jax.ShapeDtypeStruct((b, l, D_MODEL), F32)] * 2,
        compiler_params=_cparams(("parallel", "parallel")),
        name="out_proj",
    )(x, og, ocmp, oslc, owin, sm, wg, wn, norm2)


def _peer_topk_kernel(x_ref, wqt_ref, keys_ref, cflat_ref, eid_ref, gate_ref, qt_scr, sv_scr, si_scr, top_scr,
                      *, tm):
    qt_scr[...] = _dot_nt(wqt_ref[...], x_ref[...].astype(BF16))
    rows = _iota_f((PEER_NKEYS, tm), 0)
    cflat = jnp.broadcast_to(cflat_ref[...], (PEER_NCAND, tm))

    def head_body(h, carry):
        for c in range(2):
            off = pl.multiple_of(h * (2 * PEER_HALF) + c * PEER_HALF, PEER_HALF)
            qs = qt_scr[pl.ds(off, PEER_HALF), :].astype(BF16)
            s = _dot(keys_ref[h, c], qs)

            def round_body(r, s):
                m = jnp.max(s, axis=0, keepdims=True)
                idx = jnp.min(jnp.where(s == m, rows, float(PEER_NKEYS)), axis=0, keepdims=True)
                sv_scr[c, pl.ds(r, 1), :] = m
                si_scr[c, pl.ds(r, 1), :] = idx
                return jnp.where(rows == idx, REMOVED, s)

            lax.fori_loop(0, PEER_TOPK, round_body, s)
        s1 = sv_scr[0]
        s2 = sv_scr[1]
        i1 = si_scr[0] * float(PEER_NKEYS)
        i2 = si_scr[1]
        cand = [s1[0:1] + s2]
        eidc = [i1[0:1] + i2]
        for a in range(1, 8):
            cand.append(s1[a:a + 1] + s2[0:8])
            eidc.append(i1[a:a + 1] + i2[0:8])
        cand.append(s1[8:16] + s2[0:1])
        eidc.append(i1[8:16] + i2[0:1])
        cand = jnp.where(cflat >= 0.0, jnp.concatenate(cand, axis=0), REMOVED)
        eidc = jnp.concatenate(eidc, axis=0)

        def round2(r, cand):
            m = jnp.max(cand, axis=0, keepdims=True)
            f = jnp.min(jnp.where(cand == m, cflat, 1e9), axis=0, keepdims=True)
            hit = cflat == f
            top_scr[pl.ds(r, 1), :] = m
            eid_ref[h, pl.ds(r, 1), :] = jnp.sum(jnp.where(hit, eidc, 0.0), axis=0, keepdims=True).astype(I32)
            return jnp.where(hit, REMOVED, cand)

        lax.fori_loop(0, PEER_TOPK, round2, cand)
        top = top_scr[...]
        e = jnp.exp(top - jnp.max(top, axis=0, keepdims=True))
        gate_ref[h] = e / jnp.sum(e, axis=0, keepdims=True)
        return carry

    lax.fori_loop(0, PEER_HEADS, head_body, 0)


def _peer_cflat():
    rows = [(0, b) for b in range(16)]
    for a in range(1, 8):
        rows += [(a, b) for b in range(8)]
    rows += [(a, 0) for a in range(8, 16)]
    flat = [a * 16 + b if (a + 1) * (b + 1) <= PEER_TOPK else -1 for a, b in rows]
    assert len(flat) == PEER_NCAND
    return jnp.asarray(np.array(flat, np.float32).reshape(PEER_NCAND, 1))


def _peer_topk(xn, wqt, keys, tm):
    t = xn.shape[0]
    assert t % tm == 0
    cflat = _peer_cflat()
    kern = functools.partial(_peer_topk_kernel, tm=tm)
    full = lambda a: pl.BlockSpec(a.shape, lambda i: (0,) * a.ndim)
    out_spec = pl.BlockSpec((PEER_HEADS, PEER_TOPK, tm), lambda i: (0, 0, i))
    return pl.pallas_call(
        kern,
        grid=(t // tm,),
        in_specs=[pl.BlockSpec((tm, D_MODEL), lambda i: (i, 0)), full(wqt), full(keys), full(cflat)],
        out_specs=[out_spec, out_spec],
        out_shape=[jax.ShapeDtypeStruct((PEER_HEADS, PEER_TOPK, t), I32),
                   jax.ShapeDtypeStruct((PEER_HEADS, PEER_TOPK, t), F32)],
        scratch_shapes=[pltpu.VMEM((D_MODEL, tm), F32), pltpu.VMEM((2, PEER_TOPK, tm), F32),
                        pltpu.VMEM((2, PEER_TOPK, tm), F32), pltpu.VMEM((PEER_TOPK, tm), F32)],
        compiler_params=_cparams(("parallel",)),
        name="peer_topk",
    )(xn, wqt, keys, cflat)


PEER_GROUP = 2
PEER_SLOTS = 8
PEER_CHUNKS = D_MODEL // LANE


def _peer_expert_kernel(eid_ref, x_ref, gate_ref, hres_ref, uv_ref, y_ref, *scratch, tt):
    bufs = scratch[:PEER_SLOTS]
    sem = scratch[PEER_SLOTS]
    ahead = PEER_SLOTS - PEER_GROUP

    def row_copy(e, slot, k):
        return pltpu.make_async_copy(uv_ref.at[e], bufs[slot].at[pl.ds(k * PEER_CHUNKS, PEER_CHUNKS), :],
                                     sem.at[slot])

    def issue(t, slot, part=None):
        base = t * PEER_SEL
        i, n = (0, 1) if part is None else part
        for k in range(i * PEER_SEL // n, (i + 1) * PEER_SEL // n):
            row_copy(eid_ref[base + k], slot, k).start(priority=k % 2)

    def wait(slot):
        for k in range(PEER_SEL):
            row_copy(0, slot, k).wait()

    def chunk_words(slot, c):
        return bufs[slot][pl.ds(c, PEER_SEL, stride=PEER_CHUNKS), :]

    def evaluate(ts, slots, prefetch):
        for slot in slots:
            wait(slot)
        nparts = 2 * PEER_CHUNKS

        def start_part(i):
            if prefetch:
                for t, slot in zip(ts, slots):
                    issue(t + ahead, (slot + ahead) % PEER_SLOTS, (i, nparts))

        xbs = [x_ref[pl.ds(t, 1), :].astype(BF16).astype(F32) for t in ts]
        accs = [jnp.zeros((PEER_SEL, LANE), F32) for _ in ts]
        for c in range(PEER_CHUNKS):
            start_part(c)
            for j, slot in enumerate(slots):
                uf = pltpu.bitcast(chunk_words(slot, c) & jnp.int32(-65536), F32)
                accs[j] = accs[j] + uf * xbs[j][:, c * LANE:(c + 1) * LANE]
        w2s = []
        for j, t in enumerate(ts):
            act = jnp.sum(accs[j].T, axis=0, keepdims=True)
            w = gate_ref[pl.ds(t, 1), :] * (0.5 * act * (1.0 + lax.erf(act * SQRT_HALF)))
            wb = w.astype(BF16).astype(F32)
            w2s.append(jnp.broadcast_to(wb, (LANE, PEER_SEL)).T)
        outs = [[] for _ in ts]
        for c in range(PEER_CHUNKS):
            start_part(PEER_CHUNKS + c)
            for j, slot in enumerate(slots):
                vf = pltpu.bitcast(chunk_words(slot, c) << 16, F32)
                outs[j].append(jnp.sum(vf * w2s[j], axis=0, keepdims=True))
        for j, t in enumerate(ts):
            y_ref[pl.ds(t, 1), :] = hres_ref[pl.ds(t, 1), :] + jnp.concatenate(outs[j], axis=1)

    for t0 in range(ahead):
        issue(t0, t0)
    n_main = (tt - ahead) // PEER_SLOTS * PEER_SLOTS

    def body(i, carry):
        for r in range(0, PEER_SLOTS, PEER_GROUP):
            slots = list(range(r, r + PEER_GROUP))
            evaluate([i * PEER_SLOTS + s for s in slots], slots, True)
        return carry

    lax.fori_loop(0, n_main // PEER_SLOTS, body, 0)
    for t0 in range(n_main, tt, PEER_GROUP):
        ts = list(range(t0, t0 + PEER_GROUP))
        evaluate(ts, [t % PEER_SLOTS for t in ts], t0 + ahead < tt)


def _pack_expert_rows(u, v):
    bits = lambda a: lax.bitcast_convert_type(a.astype(BF16), jnp.uint16).astype(jnp.uint32)
    words = (bits(u) << 16) | bits(v)
    return lax.bitcast_convert_type(words, I32).reshape(-1, PEER_CHUNKS, LANE)


def _peer_experts(eid_flat, xn, gate, hres, uv, tt):
    t = xn.shape[0]
    assert t % tt == 0 and tt >= PEER_SLOTS
    kern = functools.partial(_peer_expert_kernel, tt=tt)
    row = lambda w: pl.BlockSpec((tt, w), lambda i: (i, 0))
    return pl.pallas_call(
        kern,
        grid=(t // tt,),
        in_specs=[pl.BlockSpec((tt * PEER_SEL,), lambda i: (i,), memory_space=pltpu.SMEM),
                  row(D_MODEL), row(PEER_SEL), row(D_MODEL),
                  pl.BlockSpec(memory_space=pl.ANY)],
        out_specs=row(D_MODEL),
        out_shape=jax.ShapeDtypeStruct((t, D_MODEL), F32),
        scratch_shapes=[pltpu.VMEM((PEER_SEL * PEER_CHUNKS, LANE), I32) for _ in range(PEER_SLOTS)]
        + [pltpu.SemaphoreType.DMA((PEER_SLOTS,))],
        compiler_params=_cparams(("arbitrary",)),
        name="peer_experts",
    )(eid_flat, xn, gate, hres, uv)


def _prep_params(norm1, w_in, gdn_conv_w, gdn_a_log, gdn_dt_bias, gdn_norm, nsa_q_norm, nsa_k_norm,
                 rel_bias, w_o, norm2, peer_wq, peer_subkeys, peer_u, peer_v):
    w = w_in[0]
    p = {}
    p["norm1"] = norm1[0][None]
    p["wm"] = jnp.concatenate([w[:, :OFF_B], w[:, OFF_NQ:OFF_NG]], axis=1).astype(BF16)
    p["ws"] = jnp.concatenate([w[:, OFF_B:OFF_NQ], w[:, OFF_NG:], jnp.zeros((D_MODEL, LANE - 32), F32)],
                              axis=1).astype(BF16)
    li = jnp.arange(LANE)
    p["seg"] = (li[:, None] // NSA_HD == li[None, :] // NSA_HD).astype(BF16)
    p["qg"] = jnp.tile(nsa_q_norm[0], 2)[None]
    p["kg"] = jnp.tile(nsa_k_norm[0], (1, 2))
    p["conv_w"] = gdn_conv_w[0]
    p["al_vec"] = jnp.zeros((1, LANE), F32).at[0, 4:8].set(gdn_a_log[0])
    p["dtb_vec"] = jnp.zeros((1, LANE), F32).at[0, 4:8].set(gdn_dt_bias[0])
    p["gnorm"] = gdn_norm[0][None]
    p["tab"] = rel_bias
    wo = w_o[0]
    p["wg"] = wo[:512].astype(BF16)
    wn = jnp.zeros((NSA_HEADS, LANE, D_MODEL), F32)
    for h in range(NSA_HEADS):
        g = h // NSA_P
        wn = wn.at[h, g * NSA_HD:(g + 1) * NSA_HD].set(wo[512 + h * NSA_HD:512 + (h + 1) * NSA_HD])
    p["wn"] = wn.astype(BF16)
    p["norm2"] = norm2[0][None]
    p["wqt"] = peer_wq[0].T.astype(BF16)
    p["keys"] = peer_subkeys[0].astype(BF16)
    p["uv"] = _pack_expert_rows(peer_u[0], peer_v[0])
    return p


def _perm_avg_matrix(n_blocks, n_rows):
    half = n_blocks // 2
    r = jnp.arange(n_blocks)
    blk = jnp.where(r < half, 2 * r, 2 * (r - half) + 1)
    return (jnp.arange(n_rows)[None, :] // CMP_BLOCK == blk[:, None]).astype(BF16)


def _token_mixer_tail(p, x, og, ocmp, oslc, owin, sm, tm_out, tm_topk, tt):
    b, l, _ = x.shape
    hres, xn2 = _out_proj(x, og, ocmp, oslc, owin, sm, p["wg"], p["wn"], p["norm2"], tm_out)
    t = b * l
    xn2 = xn2.reshape(t, D_MODEL)
    eid, gate = _peer_topk(xn2, p["wqt"], p["keys"], tm_topk)
    eid_flat = eid.reshape(PEER_SEL, t).T.reshape(t * PEER_SEL)
    gate_tok = gate.reshape(PEER_SEL, t).T
    y = _peer_experts(eid_flat, xn2, gate_tok, hres.reshape(t, D_MODEL), p["uv"], tt)
    return y.reshape(b, l, D_MODEL)


def kernel(x_prompt, x_sample, cache_nsa_kv, page_table, state_win_kv, state_conv, state_gdn, norm1, w_in, gdn_conv_w, gdn_a_log, gdn_dt_bias, gdn_norm, nsa_q_norm, nsa_k_norm, rel_bias, w_o, norm2, peer_wq, peer_subkeys, peer_u, peer_v):
    assert w_in.shape[0] == 1, "single layer"
    p = _prep_params(norm1, w_in, gdn_conv_w, gdn_a_log, gdn_dt_bias, gdn_norm, nsa_q_norm, nsa_k_norm,
                     rel_bias, w_o, norm2, peer_wq, peer_subkeys, peer_u, peer_v)
    b, l, _ = x_prompt.shape
    db, lq, _ = x_sample.shape
    n_pages = page_table.shape[1]
    past_len = n_pages * PAGE
    wb = state_win_kv.shape[2]
    assert cache_nsa_kv.shape[2] == PAGE and l >= WINDOW and l >= 3 and lq >= 3
    assert lq < CMP_BLOCK and lq <= 8 and wb == WINDOW and past_len >= wb

    tp = b * l
    tm = 256 if tp % 256 == 0 else LANE
    hc, z, sm, qn, kvn, winn = _in_proj(x_prompt.reshape(tp, D_MODEL), p["norm1"], p["wm"], p["ws"],
                                        p["seg"], p["qg"], p["kg"], tm)
    hc3, z3, sm3 = hc.reshape(b, l, GDN_CH), z.reshape(b, l, 512), sm.reshape(b, l, LANE)
    qn3, kvn3, winn3 = qn.reshape(b, l, 512), kvn.reshape(b, l, 512), winn.reshape(b, l, 256)
    og, gdn_p = _gdn(hc3, z3, sm3, p["conv_w"], p["al_vec"], p["dtb_vec"], p["gnorm"],
                     jnp.zeros((b, 3, GDN_CH), F32), jnp.zeros((b, GDN_HEADS, 128, 128), F32),
                     GDN_CHUNK, l)
    nc = l // CMP_BLOCK
    ocmp, sel = _cmp_prompt(p["tab"], qn3, kvn3, _perm_avg_matrix(nc, l), tm)
    oslc = _flash_prompt(p["tab"], qn3, kvn3, 2, 3, sel, LANE, False)
    owin = _flash_prompt(p["tab"], qn3, winn3, 0, 1, None, LANE, True)
    y_prompt = _token_mixer_tail(p, x_prompt, og, ocmp, oslc, owin, sm3, tm, tm, 256 if tp % 256 == 0 else 64)

    ts = db * lq
    hc_s, z_s, sm_s, qn_s, kvn_s, winn_s = _in_proj(x_sample.reshape(ts, D_MODEL), p["norm1"], p["wm"], p["ws"],
                                                    p["seg"], p["qg"], p["kg"], min(ts, 256))
    pad_rows = lambda a, n: jnp.pad(a.reshape(db, lq, a.shape[-1]), ((0, 0), (0, n - lq), (0, 0)))
    og_s, gdn_s = _gdn(pad_rows(hc_s, GDN_CHUNK), pad_rows(z_s, GDN_CHUNK), pad_rows(sm_s, GDN_CHUNK),
                       p["conv_w"], p["al_vec"], p["dtb_vec"], p["gnorm"], state_conv[0], state_gdn[0],
                       GDN_CHUNK, lq)
    cache3 = cache_nsa_kv[0].reshape(cache_nsa_kv.shape[1], PAGE, 512)
    q8 = pad_rows(qn_s, 8)
    nsb_s = past_len // SEL_BLOCK
    emat_s = (jnp.arange(nsb_s)[:, None] == jnp.arange(past_len)[None, :] // SEL_BLOCK).astype(BF16)
    ocmp_s, oslc_s = _nsa_sample(page_table, p["tab"], cache3, q8, pad_rows(kvn_s, PAGE), emat_s, past_len)
    wseq = jnp.concatenate([state_win_kv[0].reshape(db, wb, 256), pad_rows(winn_s, LANE)], axis=1)
    owin_s = _win_sample(p["tab"], q8, wseq, past_len, wb, lq)
    flat_heads = lambda o: o[:, :, :lq].transpose(1, 0, 2, 3).reshape(1, NSA_HEADS, ts, LANE)
    y_sample = _token_mixer_tail(p, x_sample.reshape(1, ts, D_MODEL), og_s[:, :lq].reshape(1, ts, 512),
                                 flat_heads(ocmp_s), flat_heads(oslc_s), flat_heads(owin_s),
                                 sm_s.reshape(1, ts, LANE), min(ts, 256), min(ts, 256), 64)

    kv_tail = (4, NSA_GROUPS, NSA_HD)
    return (y_prompt,
            y_sample.reshape(db, lq, D_MODEL),
            kvn.reshape((1, b, l) + kv_tail),
            winn3[:, l - WINDOW:].reshape(1, b, WINDOW, 2, NSA_GROUPS, NSA_HD),
            hc3[:, l - 3:][None],
            gdn_p[None],
            kvn_s.reshape((1, db, lq) + kv_tail),
            wseq[:, lq:lq + wb].reshape(1, db, wb, 2, NSA_GROUPS, NSA_HD),
            hc_s.reshape(db, lq, GDN_CH)[:, lq - 3:][None],
            gdn_s[None])
```

```python
import functools
import math

import numpy as np
import jax
import jax.numpy as jnp
from jax import lax
from jax.experimental import pallas as pl
from jax.experimental.pallas import tpu as pltpu

F32 = jnp.float32
BF16 = jnp.bfloat16
I32 = jnp.int32

D_MODEL = 1024
EPS = 1e-6
NEG_INF = -1e30
FORCE_SCORE = 1e4
GDN_HEADS = 4
GDN_DK = 128
GDN_CHUNK = 64
GDN_CH = 1536
NSA_HEADS = 8
NSA_GROUPS = 2
NSA_P = 4
NSA_HD = 64
CMP_BLOCK = 32
SEL_BLOCK = 64
SEL_TOPK = 16
WINDOW = 512
REL_BUCKETS = 32
PAGE = 128
PEER_HEADS = 8
PEER_NKEYS = 128
PEER_TOPK = 16
PEER_HALF = 64
PEER_SEL = PEER_HEADS * PEER_TOPK
PEER_NCAND = 80
OFF_B = 2048
OFF_NQ = 2056
OFF_NG = 3336
LANE = 128
VMEM_LIMIT = 56 * 1024 * 1024
REMOVED = -3.0e38
SQRT_HALF = 0.7071067811865476


def _cparams(sem):
    return pltpu.CompilerParams(dimension_semantics=sem, vmem_limit_bytes=VMEM_LIMIT)


def _dot(a, b):
    return jnp.dot(a, b, preferred_element_type=F32)


def _dot_nt(a, b):
    return lax.dot_general(a, b, (((1,), (1,)), ((), ())), preferred_element_type=F32)


def _dot_tn(a, b):
    return lax.dot_general(a, b, (((0,), (0,)), ((), ())), preferred_element_type=F32)


def _split2(x):
    hi = x.astype(BF16)
    lo = (x - hi.astype(F32)).astype(BF16)
    return hi, lo


def _split3(x):
    hi = x.astype(BF16)
    r = x - hi.astype(F32)
    mid = r.astype(BF16)
    lo = (r - mid.astype(F32)).astype(BF16)
    return hi, mid, lo


def _dot_exact_lhs(m01, x):
    hi, mid, lo = _split3(x)
    return _dot(m01, hi) + (_dot(m01, mid) + _dot(m01, lo))


def _mm3(a, b):
    ah, al = _split2(a)
    bh, bl = _split2(b)
    return _dot(ah, bh) + (_dot(ah, bl) + _dot(al, bh))


def _iota_f(shape, axis):
    return lax.broadcasted_iota(I32, shape, axis).astype(F32)


def _rel_bucket(dist):
    d = jnp.maximum(dist, 0)
    df = jnp.maximum(d, 1).astype(F32)
    large = 16 + (jnp.log(df / 16.0) / math.log(128.0) * 16.0).astype(I32)
    large = jnp.minimum(large, REL_BUCKETS - 1)
    return jnp.where(d < 16, d, large)


def _bias_from_bucket(bucket, tab_ref, head, lo=0, hi=REL_BUCKETS - 1):
    if isinstance(lo, int) and isinstance(hi, int):
        b = jnp.zeros(bucket.shape, F32)
        for k in range(lo, hi + 1):
            b = jnp.where(bucket == k, tab_ref[k, head], b)
        return b

    def body(k, b):
        return jnp.where(bucket == k, tab_ref[k, head], b)

    return lax.fori_loop(lo, hi + 1, body, jnp.zeros(bucket.shape, F32))


def _topk_mask(s, k, axis):
    n = s.shape[axis]
    ids = _iota_f(s.shape, axis)
    sel = jnp.zeros(s.shape, F32)
    for _ in range(k):
        m = jnp.max(s, axis=axis, keepdims=True)
        idx = jnp.min(jnp.where(s == m, ids, float(n)), axis=axis, keepdims=True)
        hit = ids == idx
        sel = jnp.where(hit, 1.0, sel)
        s = jnp.where(hit, REMOVED, s)
    return sel


def _seg_rmsnorm(v, gain, seg):
    sq = v * v
    hi, lo = _split2(sq)
    ssum = _dot(hi, seg) + _dot(lo, seg)
    return v * lax.rsqrt(ssum * (1.0 / NSA_HD) + EPS) * gain


def _inproj_kernel(x_ref, g1_ref, wm_ref, ws_ref, seg_ref, qg_ref, kg_ref,
                   hc_ref, z_ref, sm_ref, q_ref, kv_ref, win_ref):
    x = x_ref[...]
    ms = jnp.mean(x * x, axis=-1, keepdims=True)
    xn = (x * lax.rsqrt(ms + EPS) * g1_ref[...]).astype(BF16)
    h = _dot(xn, wm_ref[...])
    sm_ref[...] = _dot(xn, ws_ref[...])
    hc_ref[...] = h[:, :GDN_CH]
    z_ref[...] = h[:, GDN_CH:2048]
    seg = seg_ref[...]
    qg = qg_ref[...]
    kg = kg_ref[...]
    for i in range(4):
        q_ref[:, i * LANE:(i + 1) * LANE] = _seg_rmsnorm(h[:, 2048 + i * LANE:2048 + (i + 1) * LANE], qg, seg)
    kv_ref[:, 0:128] = _seg_rmsnorm(h[:, 2560:2688], kg[0:1], seg)
    kv_ref[:, 128:256] = h[:, 2688:2816]
    kv_ref[:, 256:384] = _seg_rmsnorm(h[:, 2816:2944], kg[1:2], seg)
    kv_ref[:, 384:512] = h[:, 2944:3072]
    win_ref[:, 0:128] = _seg_rmsnorm(h[:, 3072:3200], kg[2:3], seg)
    win_ref[:, 128:256] = h[:, 3200:3328]


def _in_proj(x, norm1, wm, ws, seg, qg, kg, tm):
    t = x.shape[0]
    assert t % tm == 0
    row = lambda w: pl.BlockSpec((tm, w), lambda i: (i, 0))
    full = lambda a: pl.BlockSpec(a.shape, lambda i: (0,) * a.ndim)
    widths = (GDN_CH, 512, LANE, 512, 512, 256)
    return pl.pallas_call(
        _inproj_kernel,
        grid=(t // tm,),
        in_specs=[row(D_MODEL), full(norm1), full(wm), full(ws), full(seg), full(qg), full(kg)],
        out_specs=[row(w) for w in widths],
        out_shape=[jax.ShapeDtypeStruct((t, w), F32) for w in widths],
        compiler_params=_cparams(("parallel",)),
        name="in_proj",
    )(x, norm1, wm, ws, seg, qg, kg)


def _inv_unit_lower(lmat, c):
    ri = lax.broadcasted_iota(I32, lmat.shape, 0)
    ci = lax.broadcasted_iota(I32, lmat.shape, 1)
    eye = jnp.where(ri == ci, 1.0, 0.0).astype(F32)
    n = -lmat
    p = eye + n
    m = _mm3(n, n)
    span = 2
    while True:
        mm = _mm3 if span == 2 else (lambda a, b: _dot(a.astype(BF16), b.astype(BF16)))
        p = p + mm(p, m)
        span *= 2
        if span >= c:
            break
        m = mm(m, m)
    return p


def _gdn_kernel(hc_ref, z_ref, sm_ref, cw_ref, al_ref, dtb_ref, gn_ref, conv0_ref, s0_ref,
                og_ref, sfin_ref, xbuf, s_scr, *, c, l_valid, n_chunks, nseq):
    ci = pl.program_id(1)

    @pl.when(ci == 0)
    def _():
        for sq in range(nseq):
            xbuf[sq, 0:8, :] = jnp.zeros((8, GDN_CH), F32)
            xbuf[sq, 5:8, :] = conv0_ref[sq]
        s_scr[...] = s0_ref[...]

    for sq in range(nseq):
        _gdn_chunk(hc_ref.at[sq], z_ref.at[sq], sm_ref.at[sq], cw_ref, al_ref, dtb_ref, gn_ref,
                   og_ref.at[sq], xbuf.at[sq], s_scr.at[sq], ci, c, l_valid)

    @pl.when(ci == n_chunks - 1)
    def _():
        sfin_ref[...] = s_scr[...]


def _gdn_chunk(hc_ref, z_ref, sm_ref, cw_ref, al_ref, dtb_ref, gn_ref, og_ref, xbuf, s_scr, ci, c, l_valid):
    xbuf[8:8 + c, :] = hc_ref[...]
    w = cw_ref[...]
    y = (xbuf[5:5 + c, :] * w[0:1] + xbuf[6:6 + c, :] * w[1:2]
         + xbuf[7:7 + c, :] * w[2:3] + xbuf[8:8 + c, :] * w[3:4])
    tail = xbuf[5 + c:8 + c, :]
    xbuf[5:8, :] = tail
    y = y * jax.nn.sigmoid(y)

    sm = sm_ref[...]
    rowid = ci * c + lax.broadcasted_iota(I32, (c, 1), 0)
    rvalid = rowid < l_valid
    beta_all = jnp.where(rvalid, jax.nn.sigmoid(sm), 0.0)
    sp_in = sm + dtb_ref[...]
    softplus = jnp.maximum(sp_in, 0.0) + jnp.log1p(jnp.exp(-jnp.abs(sp_in)))
    g_all = jnp.where(rvalid, -jnp.exp(al_ref[...]) * softplus, 0.0)

    ri = lax.broadcasted_iota(I32, (c, c), 0)
    cj = lax.broadcasted_iota(I32, (c, c), 1)
    tri = jnp.where(ri >= cj, 1.0, 0.0).astype(BF16)
    triu = jnp.where(ri <= cj, 1.0, 0.0).astype(BF16)
    g_hi, g_mid, g_lo = _split3(g_all)
    gcum = _dot(tri, g_hi) + (_dot(tri, g_mid) + _dot(tri, g_lo))
    gcum_t = _dot_tn(g_hi, triu) + (_dot_tn(g_mid, triu) + _dot_tn(g_lo, triu))
    gn = gn_ref[...]

    heads = range(GDN_HEADS)
    stack = lambda pieces: jnp.concatenate(pieces, axis=0)

    def l2n(x):
        return jnp.where(rvalid, x * lax.rsqrt(jnp.sum(x * x, axis=-1, keepdims=True) + EPS), 0.0)

    q_s = stack([l2n(y[:, h * 128:(h + 1) * 128]) * (GDN_DK ** -0.5) for h in heads])
    k_s = stack([l2n(y[:, 512 + h * 128:512 + (h + 1) * 128]) for h in heads])
    v_s = stack([jnp.where(rvalid, y[:, 1024 + h * 128:1024 + (h + 1) * 128], 0.0) for h in heads])
    beta_s = stack([beta_all[:, h:h + 1] for h in heads])
    gc_s = stack([gcum[:, 4 + h:5 + h] for h in heads])
    gl_s = stack([jnp.broadcast_to(gcum[c - 1:c, 4 + h:5 + h], (c, 1)) for h in heads])
    gct_s = jnp.concatenate([gcum_t[4 + h:5 + h, :] for h in heads], axis=1)
    r = GDN_HEADS * c
    rr = lax.broadcasted_iota(I32, (r, r), 0)
    rc = lax.broadcasted_iota(I32, (r, r), 1)
    same = (rr // c) == (rc // c)
    causal = same & (rr >= rc)
    strict = same & (rr > rc)
    decay = jnp.where(causal, jnp.exp(jnp.where(causal, gc_s - gct_s, 0.0)), 0.0)
    kb = k_s * beta_s
    ksb = k_s.astype(BF16)
    lmat = jnp.where(strict, _dot_nt(kb.astype(BF16), ksb) * decay, 0.0)
    tmat = _inv_unit_lower(lmat, c).astype(BF16)
    eg = jnp.exp(gc_s)
    u = _dot(tmat, (v_s * beta_s).astype(BF16))
    wmat = _dot(tmat, (kb * eg).astype(BF16)).astype(BF16)
    a_intra = (_dot_nt(q_s.astype(BF16), ksb) * decay).astype(BF16)
    q_dec = (q_s * eg).astype(BF16)
    k_dec = (k_s * jnp.exp(gl_s - gc_s)).astype(BF16)
    rows = lambda a, h: a[h * c:(h + 1) * c]
    sbs = [s_scr[h].astype(BF16) for h in heads]
    v_new = stack([rows(u, h) - _dot(rows(wmat, h), sbs[h]) for h in heads])
    vnb = v_new.astype(BF16)
    o_intra = _dot(a_intra, vnb)
    for h in heads:
        g_last = jnp.exp(gcum[c - 1:c, 4 + h:5 + h])
        s_scr[h] = s_scr[h] * g_last + _dot_tn(rows(k_dec, h), rows(vnb, h))
        o = _dot(rows(q_dec, h), sbs[h]) + rows(o_intra, h)
        o = o * lax.rsqrt(jnp.mean(o * o, axis=-1, keepdims=True) + EPS) * gn
        zh = z_ref[:, h * 128:(h + 1) * 128]
        og_ref[:, h * 128:(h + 1) * 128] = o * (zh * jax.nn.sigmoid(zh))


GDN_SEQS = 2


def _gdn(hc, z, sm, conv_w, al_vec, dtb_vec, gnorm, conv0, s0, c, l_valid):
    b, lp, _ = hc.shape
    nseq = GDN_SEQS if b % GDN_SEQS == 0 else 1
    assert lp % c == 0
    n_chunks = lp // c
    full = lambda a: pl.BlockSpec(a.shape, lambda i, j: (0,) * a.ndim)
    seq = lambda w: pl.BlockSpec((nseq, c, w), lambda i, j: (i, j, 0))
    state = pl.BlockSpec((nseq, GDN_HEADS, 128, 128), lambda i, j: (i, 0, 0, 0))
    kern = functools.partial(_gdn_kernel, c=c, l_valid=l_valid, n_chunks=n_chunks, nseq=nseq)
    return pl.pallas_call(
        kern,
        grid=(b // nseq, n_chunks),
        in_specs=[seq(GDN_CH), seq(512), seq(LANE), full(conv_w), full(al_vec), full(dtb_vec), full(gnorm),
                  pl.BlockSpec((nseq, 3, GDN_CH), lambda i, j: (i, 0, 0)), state],
        out_specs=[seq(512), state],
        out_shape=[jax.ShapeDtypeStruct((b, lp, 512), F32),
                   jax.ShapeDtypeStruct((b, GDN_HEADS, 128, 128), F32)],
        scratch_shapes=[pltpu.VMEM((nseq, c + 8, GDN_CH), F32), pltpu.VMEM((nseq, GDN_HEADS, 128, 128), F32)],
        compiler_params=_cparams(("parallel", "arbitrary")),
        name="gdn",
    )(hc, z, sm, conv_w, al_vec, dtb_vec, gnorm, conv0, s0)


def _head_q128(q, h):
    g = h // NSA_P
    piece = q[:, (h // 2) * LANE:(h // 2 + 1) * LANE]
    lane = lax.broadcasted_iota(I32, piece.shape, 1)
    keep = (lane >= NSA_HD) if h % 2 == 1 else (lane < NSA_HD)
    qm = jnp.where(keep, piece, 0.0)
    if h % 2 != g:
        qm = pltpu.roll(qm, NSA_HD, 1)
    return qm


def _select_scores(imp, qpos, nsb):
    score = imp[:, :nsb] + imp[:, nsb:]
    j = lax.broadcasted_iota(I32, score.shape, 1)
    cur = qpos // SEL_BLOCK
    forced = (j == 0) | (j == cur) | (j == cur - 1)
    future = j * SEL_BLOCK > qpos
    return jnp.where(future, -1.0, jnp.where(forced, FORCE_SCORE, score))


def _cmp_attention(tab_ref, q, kcv, qpos, ocmp_ref):
    nc = kcv.shape[0]
    nsb = nc // 2
    kc = kcv[:, 0:128].astype(BF16)
    vc = kcv[:, 128:256].astype(BF16)
    lane = lax.broadcasted_iota(I32, (1, nc), 1)
    blk = jnp.where(lane < nsb, 2 * lane, 2 * (lane - nsb) + 1)
    dist = qpos - (blk * CMP_BLOCK + CMP_BLOCK - 1)
    valid = dist >= 0
    bucket = _rel_bucket(dist)
    scores = []
    for g in range(NSA_GROUPS):
        imp = jnp.zeros((q.shape[0], nc), F32)
        for p in range(NSA_P):
            h = g * NSA_P + p
            qm = _head_q128(q, h).astype(BF16)
            logits = _dot_nt(qm, kc) * (NSA_HD ** -0.5) + _bias_from_bucket(bucket, tab_ref, h)
            l = jnp.where(valid, logits, NEG_INF)
            m = jnp.max(l, axis=-1, keepdims=True)
            pr = jnp.where(valid, jnp.exp(l - m), 0.0)
            pr = pr / jnp.maximum(jnp.sum(pr, axis=-1, keepdims=True), 1e-30)
            ocmp_ref[h] = _dot(pr.astype(BF16), vc)
            imp = imp + pr
        scores.append(_select_scores(imp, qpos, nsb))
    return scores


def _flash_init(m_scr, l_scr, acc_scr):
    m_scr[...] = jnp.full(m_scr.shape, NEG_INF, F32)
    l_scr[...] = jnp.zeros(l_scr.shape, F32)
    acc_scr[...] = jnp.zeros(acc_scr.shape, F32)


def _cmp_prompt_kernel(tab_ref, q_ref, kv_ref, mavg_ref, ocmp_ref, sel_ref, kc_scr, *, tq, nsb):
    qi = pl.program_id(1)

    @pl.when(qi == 0)
    def _():
        kc_scr[...] = _dot_exact_lhs(mavg_ref[...], kv_ref[...]) * (1.0 / CMP_BLOCK)

    qpos = qi * tq + lax.broadcasted_iota(I32, (tq, 1), 0)
    scores = _cmp_attention(tab_ref, q_ref[...], kc_scr[...], qpos, ocmp_ref)
    st = jnp.concatenate(scores, axis=1).T
    k = min(SEL_TOPK, nsb)
    sel_ref[...] = jnp.concatenate([_topk_mask(st[:nsb], k, 0), _topk_mask(st[nsb:], k, 0)], axis=0)


def _cmp_prompt(tab, qn, kvn, mavg, tq):
    b, l, _ = qn.shape
    nc = l // CMP_BLOCK
    assert l % tq == 0 and l % SEL_BLOCK == 0
    kern = functools.partial(_cmp_prompt_kernel, tq=tq, nsb=nc // 2)
    return pl.pallas_call(
        kern,
        grid=(b, l // tq),
        in_specs=[pl.BlockSpec(memory_space=pltpu.SMEM),
                  pl.BlockSpec((None, tq, 512), lambda i, j: (i, j, 0)),
                  pl.BlockSpec((None, l, 256), lambda i, j: (i, 0, 0)),
                  pl.BlockSpec(mavg.shape, lambda i, j: (0, 0))],
        out_specs=[pl.BlockSpec((None, NSA_HEADS, tq, LANE), lambda i, j: (i, 0, j, 0)),
                   pl.BlockSpec((None, nc, tq), lambda i, j: (i, 0, j))],
        out_shape=[jax.ShapeDtypeStruct((b, NSA_HEADS, l, LANE), F32),
                   jax.ShapeDtypeStruct((b, nc, l), F32)],
        scratch_shapes=[pltpu.VMEM((nc, 256), F32)],
        compiler_params=_cparams(("parallel", "arbitrary")),
        name="cmp_prompt",
    )(tab, qn, kvn, mavg)


def _flash_prompt_kernel(tab_ref, q_ref, k_ref, v_ref, *rest, t, windowed, nsb, n_dist):
    if windowed:
        o_ref, bias_scr, m_scr, l_scr, acc_scr = rest
    else:
        sel_ref, o_ref, bias_scr, m_scr, l_scr, acc_scr = rest
    qi = pl.program_id(1)
    lane_i = lax.broadcasted_iota(I32, (1, t), 1)
    sub_j = lax.broadcasted_iota(I32, (t, 1), 0)

    @pl.when((pl.program_id(0) == 0) & (qi == 0))
    def _():
        def build(d, carry):
            bucket = _rel_bucket(d * t + lane_i - sub_j)
            for h in range(NSA_HEADS):
                bias_scr[h, d] = _bias_from_bucket(bucket, tab_ref, h)
            return carry

        lax.fori_loop(0, n_dist, build, 0)

    q = q_ref[...]
    qts = [_head_q128(q, h).T.astype(BF16) for h in range(NSA_HEADS)]
    qpos = qi * t + lane_i
    _flash_init(m_scr, l_scr, acc_scr)
    tk = 2 * t
    sub_k = lax.broadcasted_iota(I32, (tk, 1), 0)
    k_lo = jnp.maximum(qi - WINDOW // t, 0) // 2 if windowed else 0

    def body(ki, carry):
        k0 = pl.multiple_of(ki * tk, tk)
        kt = k_ref[pl.ds(k0, tk), :].astype(BF16)
        vtt = v_ref[pl.ds(k0, tk), :].T.astype(BF16)
        dist = qpos - (k0 + sub_k)
        ok = dist >= 0
        if windowed:
            ok = ok & (dist <= WINDOW)
        d = qi - 2 * ki
        d0 = jnp.minimum(d, n_dist - 1)
        d1 = jnp.clip(d - 1, 0, n_dist - 1)
        for g in range(NSA_GROUPS):
            if windowed:
                valid = ok
            else:
                blk = g * nsb + ki * (tk // SEL_BLOCK)
                rows = [sel_ref[pl.ds(blk + r, 1), :] for r in range(tk // SEL_BLOCK)]
                selm = rows[-1]
                for r in range(tk // SEL_BLOCK - 2, -1, -1):
                    selm = jnp.where(sub_k < (r + 1) * SEL_BLOCK, rows[r], selm)
                valid = ok & (selm > 0.5)
            for p in range(NSA_P):
                h = g * NSA_P + p
                bias = jnp.concatenate([bias_scr[h, d0], bias_scr[h, d1]], axis=0)
                s = _dot(kt, qts[h]) * (NSA_HD ** -0.5) + bias
                l = jnp.where(valid, s, NEG_INF)
                m_old = m_scr[h]
                m_new = jnp.maximum(m_old, jnp.max(l, axis=0, keepdims=True))
                pr = jnp.where(valid, jnp.exp(l - m_new), 0.0)
                alpha = jnp.exp(m_old - m_new)
                l_scr[h] = alpha * l_scr[h] + jnp.sum(pr, axis=0, keepdims=True)
                acc_scr[h] = alpha * acc_scr[h] + _dot(vtt, pr.astype(BF16))
                m_scr[h] = m_new
        return carry

    lax.fori_loop(k_lo, qi // 2 + 1, body, 0)
    for h in range(NSA_HEADS):
        o_ref[h] = (acc_scr[h] / jnp.maximum(l_scr[h], 1e-30)).T


def _flash_prompt(tab, qn, kv_arr, k_blk, v_blk, sel, t, windowed):
    b, l, _ = qn.shape
    assert l % (2 * t) == 0 and t == LANE
    n_dist = WINDOW // t + 1 if windowed else l // t
    kern = functools.partial(_flash_prompt_kernel, t=t, windowed=windowed, nsb=l // SEL_BLOCK, n_dist=n_dist)
    in_specs = [pl.BlockSpec(memory_space=pltpu.SMEM),
                pl.BlockSpec((None, t, 512), lambda i, j: (i, j, 0)),
                pl.BlockSpec((None, l, LANE), lambda i, j: (i, 0, k_blk)),
                pl.BlockSpec((None, l, LANE), lambda i, j: (i, 0, v_blk))]
    args = [tab, qn, kv_arr, kv_arr]
    if not windowed:
        in_specs += [pl.BlockSpec((None, sel.shape[1], t), lambda i, j: (i, 0, j))]
        args += [sel]
    return pl.pallas_call(
        kern,
        grid=(b, l // t),
        in_specs=in_specs,
        out_specs=pl.BlockSpec((None, NSA_HEADS, t, LANE), lambda i, j: (i, 0, j, 0)),
        out_shape=jax.ShapeDtypeStruct((b, NSA_HEADS, l, LANE), F32),
        scratch_shapes=[pltpu.VMEM((NSA_HEADS, n_dist, t, t), F32),
                        pltpu.VMEM((NSA_HEADS, 1, t), F32), pltpu.VMEM((NSA_HEADS, 1, t), F32),
                        pltpu.VMEM((NSA_HEADS, LANE, t), F32)],
        compiler_params=_cparams(("arbitrary", "arbitrary")),
        name="win_prompt" if windowed else "slc_prompt",
    )(*args)


def _nsa_sample_kernel(pt_ref, tab_ref, q_ref, new_ref, e_ref, cache_ref, ocmp_ref, o_ref, kv_buf, sem,
                       *, n_pages, past_len, db):
    b = pl.program_id(0)
    slot = b % 2
    nsb = past_len // SEL_BLOCK

    def page_copy(seq, buf, pg):
        return pltpu.make_async_copy(cache_ref.at[pt_ref[seq, pg]],
                                     kv_buf.at[buf, pl.ds(pg * PAGE, PAGE), :], sem.at[buf])

    @pl.when(b == 0)
    def _():
        for pg in range(n_pages):
            page_copy(0, 0, pg).start()

    @pl.when(b + 1 < db)
    def _():
        for pg in range(n_pages):
            page_copy(b + 1, 1 - slot, pg).start()

    q = q_ref[...]
    qpos = past_len + lax.broadcasted_iota(I32, (8, 1), 0)
    dist_p = qpos - lax.broadcasted_iota(I32, (1, past_len), 1)
    bucket_p = _rel_bucket(dist_p)
    dist_n = qpos - (past_len + lax.broadcasted_iota(I32, (1, PAGE), 1))
    ok_n = dist_n >= 0
    bucket_n = _rel_bucket(dist_n)
    knew = new_ref[:, 0:128].astype(BF16)
    vnew = new_ref[:, 128:256].astype(BF16)

    for pg in range(n_pages):
        page_copy(b, slot, pg).wait()

    x3 = kv_buf[slot, :, 0:256].reshape(nsb, SEL_BLOCK, 256)
    kcv = jnp.concatenate([jnp.sum(x3[:, :CMP_BLOCK], axis=1), jnp.sum(x3[:, CMP_BLOCK:], axis=1)],
                          axis=0) * (1.0 / CMP_BLOCK)
    scores = _cmp_attention(tab_ref, q, kcv, qpos, ocmp_ref)
    k_sel = min(SEL_TOPK, nsb + 1) - 1

    kall = kv_buf[slot, :, 256:384].astype(BF16)
    vall = kv_buf[slot, :, 384:512].astype(BF16)
    for g in range(NSA_GROUPS):
        sel_g = _topk_mask(scores[g], k_sel, 1).astype(BF16)
        qg = jnp.concatenate([_head_q128(q, g * NSA_P + p) for p in range(NSA_P)], axis=0).astype(BF16)
        s_p = _dot_nt(qg, kall) * (NSA_HD ** -0.5)
        s_n = _dot_nt(qg, knew) * (NSA_HD ** -0.5)
        mask_p = _dot(sel_g, e_ref[...]) > 0.5
        pps, pns, dens = [], [], []
        for p in range(NSA_P):
            h = g * NSA_P + p
            rows = slice(8 * p, 8 * p + 8)
            lp = jnp.where(mask_p, s_p[rows] + _bias_from_bucket(bucket_p, tab_ref, h), NEG_INF)
            ln = jnp.where(ok_n, s_n[rows] + _bias_from_bucket(bucket_n, tab_ref, h), NEG_INF)
            m = jnp.maximum(jnp.max(lp, axis=-1, keepdims=True), jnp.max(ln, axis=-1, keepdims=True))
            pp = jnp.where(mask_p, jnp.exp(lp - m), 0.0)
            pn = jnp.where(ok_n, jnp.exp(ln - m), 0.0)
            dens.append(jnp.sum(pp, axis=-1, keepdims=True) + jnp.sum(pn, axis=-1, keepdims=True))
            pps.append(pp.astype(BF16))
            pns.append(pn.astype(BF16))
        o = _dot(jnp.concatenate(pps, axis=0), vall) + _dot(jnp.concatenate(pns, axis=0), vnew)
        for p in range(NSA_P):
            o_ref[g * NSA_P + p] = o[8 * p:8 * p + 8] / jnp.maximum(dens[p], 1e-30)


def _nsa_sample(page_table, tab, cache, q8, new_kv, emat, past_len):
    db, n_pages = page_table.shape
    kern = functools.partial(_nsa_sample_kernel, n_pages=n_pages, past_len=past_len, db=db)
    heads = pl.BlockSpec((None, NSA_HEADS, 8, LANE), lambda i, pt: (i, 0, 0, 0))
    grid_spec = pltpu.PrefetchScalarGridSpec(
        num_scalar_prefetch=1,
        grid=(db,),
        in_specs=[pl.BlockSpec(memory_space=pltpu.SMEM),
                  pl.BlockSpec((None, 8, 512), lambda i, pt: (i, 0, 0)),
                  pl.BlockSpec((None, PAGE, 256), lambda i, pt: (i, 0, 1)),
                  pl.BlockSpec(emat.shape, lambda i, pt: (0, 0)),
                  pl.BlockSpec(memory_space=pl.ANY)],
        out_specs=[heads, heads],
        scratch_shapes=[pltpu.VMEM((2, n_pages * PAGE, 512), F32), pltpu.SemaphoreType.DMA((2,))],
    )
    return pl.pallas_call(
        kern,
        grid_spec=grid_spec,
        out_shape=[jax.ShapeDtypeStruct((db, NSA_HEADS, 8, LANE), F32)] * 2,
        compiler_params=_cparams(("arbitrary",)),
        name="nsa_sample",
    )(page_table, tab, q8, new_kv, emat, cache)


def _win_sample_kernel(tab_ref, q_ref, w_ref, o_ref, *, past_len, wb, lq):
    q = q_ref[...]
    qpos = past_len + lax.broadcasted_iota(I32, (8, 1), 0)
    wseq = w_ref[...]
    n = wseq.shape[0]
    kw = wseq[:, 0:128].astype(BF16)
    vw = wseq[:, 128:256].astype(BF16)
    j = lax.broadcasted_iota(I32, (1, n), 1)
    kpos = past_len - wb + j
    dist = qpos - kpos
    valid = (dist >= 0) & (dist <= WINDOW) & (kpos >= 0) & (j < wb + lq)
    bucket = _rel_bucket(dist)
    for h in range(NSA_HEADS):
        qm = _head_q128(q, h).astype(BF16)
        s = _dot_nt(qm, kw) * (NSA_HD ** -0.5) + _bias_from_bucket(bucket, tab_ref, h)
        l = jnp.where(valid, s, NEG_INF)
        m = jnp.max(l, axis=-1, keepdims=True)
        pr = jnp.where(valid, jnp.exp(l - m), 0.0)
        pr = pr / jnp.maximum(jnp.sum(pr, axis=-1, keepdims=True), 1e-30)
        o_ref[h] = _dot(pr.astype(BF16), vw)


def _win_sample(tab, q8, wseq, past_len, wb, lq):
    db, n, _ = wseq.shape
    kern = functools.partial(_win_sample_kernel, past_len=past_len, wb=wb, lq=lq)
    return pl.pallas_call(
        kern,
        grid=(db,),
        in_specs=[pl.BlockSpec(memory_space=pltpu.SMEM),
                  pl.BlockSpec((None, 8, 512), lambda i: (i, 0, 0)),
                  pl.BlockSpec((None, n, 256), lambda i: (i, 0, 0))],
        out_specs=pl.BlockSpec((None, NSA_HEADS, 8, LANE), lambda i: (i, 0, 0, 0)),
        out_shape=jax.ShapeDtypeStruct((db, NSA_HEADS, 8, LANE), F32),
        compiler_params=_cparams(("parallel",)),
        name="win_sample",
    )(tab, q8, wseq)


def _outproj_kernel(x_ref, og_ref, oc_ref, os_ref, ow_ref, sm_ref, wg_ref, wn_ref, n2_ref,
                    hres_ref, xn_ref):
    gates = jax.nn.sigmoid(sm_ref[...])
    acc = x_ref[...] + _dot(og_ref[...].astype(BF16), wg_ref[...])
    for h in range(NSA_HEADS):
        c = 8 + 3 * h
        on = (gates[:, c:c + 1] * oc_ref[h] + gates[:, c + 1:c + 2] * os_ref[h]
              + gates[:, c + 2:c + 3] * ow_ref[h])
        acc = acc + _dot(on.astype(BF16), wn_ref[h])
    hres_ref[...] = acc
    ms = jnp.mean(acc * acc, axis=-1, keepdims=True)
    xn_ref[...] = acc * lax.rsqrt(ms + EPS) * n2_ref[...]


def _out_proj(x, og, ocmp, oslc, owin, sm, wg, wn, norm2, tm):
    b, l, _ = x.shape
    assert l % tm == 0
    seq = lambda w: pl.BlockSpec((None, tm, w), lambda i, j: (i, j, 0))
    heads = pl.BlockSpec((None, NSA_HEADS, tm, LANE), lambda i, j: (i, 0, j, 0))
    full = lambda a: pl.BlockSpec(a.shape, lambda i, j: (0,) * a.ndim)
    return pl.pallas_call(
        _outproj_kernel,
        grid=(b, l // tm),
        in_specs=[seq(D_MODEL), seq(512), heads, heads, heads, seq(LANE), full(wg), full(wn), full(norm2)],
        out_specs=[seq(D_MODEL), seq(D_MODEL)],
        out_shape=[jax.ShapeDtypeStruct((b, l, D_MODEL), F32)] * 2,
        compiler_params=_cparams(("parallel", "parallel")),
        name="out_proj",
    )(x, og, ocmp, oslc, owin, sm, wg, wn, norm2)


def _peer_topk_kernel(x_ref, wqt_ref, keys_ref, cflat_ref, eid_ref, gate_ref, qt_scr, sv_scr, si_scr, top_scr,
                      *, tm):
    qt_scr[...] = _dot_nt(wqt_ref[...], x_ref[...].astype(BF16))
    rows = _iota_f((PEER_NKEYS, tm), 0)
    cflat = jnp.broadcast_to(cflat_ref[...], (PEER_NCAND, tm))

    def head_body(h, carry):
        for c in range(2):
            off = pl.multiple_of(h * (2 * PEER_HALF) + c * PEER_HALF, PEER_HALF)
            qs = qt_scr[pl.ds(off, PEER_HALF), :].astype(BF16)
            s = _dot(keys_ref[h, c], qs)

            def round_body(r, s):
                m = jnp.max(s, axis=0, keepdims=True)
                idx = jnp.min(jnp.where(s == m, rows, float(PEER_NKEYS)), axis=0, keepdims=True)
                sv_scr[c, pl.ds(r, 1), :] = m
                si_scr[c, pl.ds(r, 1), :] = idx
                return jnp.where(rows == idx, REMOVED, s)

            lax.fori_loop(0, PEER_TOPK, round_body, s)
        s1 = sv_scr[0]
        s2 = sv_scr[1]
        i1 = si_scr[0] * float(PEER_NKEYS)
        i2 = si_scr[1]
        cand = [s1[0:1] + s2]
        eidc = [i1[0:1] + i2]
        for a in range(1, 8):
            cand.append(s1[a:a + 1] + s2[0:8])
            eidc.append(i1[a:a + 1] + i2[0:8])
        cand.append(s1[8:16] + s2[0:1])
        eidc.append(i1[8:16] + i2[0:1])
        cand = jnp.where(cflat >= 0.0, jnp.concatenate(cand, axis=0), REMOVED)
        eidc = jnp.concatenate(eidc, axis=0)

        def round2(r, cand):
            m = jnp.max(cand, axis=0, keepdims=True)
            f = jnp.min(jnp.where(cand == m, cflat, 1e9), axis=0, keepdims=True)
            hit = cflat == f
            top_scr[pl.ds(r, 1), :] = m
            eid_ref[h, pl.ds(r, 1), :] = jnp.sum(jnp.where(hit, eidc, 0.0), axis=0, keepdims=True).astype(I32)
            return jnp.where(hit, REMOVED, cand)

        lax.fori_loop(0, PEER_TOPK, round2, cand)
        top = top_scr[...]
        e = jnp.exp(top - jnp.max(top, axis=0, keepdims=True))
        gate_ref[h] = e / jnp.sum(e, axis=0, keepdims=True)
        return carry

    lax.fori_loop(0, PEER_HEADS, head_body, 0)


def _peer_cflat():
    rows = [(0, b) for b in range(16)]
    for a in range(1, 8):
        rows += [(a, b) for b in range(8)]
    rows += [(a, 0) for a in range(8, 16)]
    flat = [a * 16 + b if (a + 1) * (b + 1) <= PEER_TOPK else -1 for a, b in rows]
    assert len(flat) == PEER_NCAND
    return jnp.asarray(np.array(flat, np.float32).reshape(PEER_NCAND, 1))


def _peer_topk(xn, wqt, keys, tm):
    t = xn.shape[0]
    assert t % tm == 0
    cflat = _peer_cflat()
    kern = functools.partial(_peer_topk_kernel, tm=tm)
    full = lambda a: pl.BlockSpec(a.shape, lambda i: (0,) * a.ndim)
    out_spec = pl.BlockSpec((PEER_HEADS, PEER_TOPK, tm), lambda i: (0, 0, i))
    return pl.pallas_call(
        kern,
        grid=(t // tm,),
        in_specs=[pl.BlockSpec((tm, D_MODEL), lambda i: (i, 0)), full(wqt), full(keys), full(cflat)],
        out_specs=[out_spec, out_spec],
        out_shape=[jax.ShapeDtypeStruct((PEER_HEADS, PEER_TOPK, t), I32),
                   jax.ShapeDtypeStruct((PEER_HEADS, PEER_TOPK, t), F32)],
        scratch_shapes=[pltpu.VMEM((D_MODEL, tm), F32), pltpu.VMEM((2, PEER_TOPK, tm), F32),
                        pltpu.VMEM((2, PEER_TOPK, tm), F32), pltpu.VMEM((PEER_TOPK, tm), F32)],
        compiler_params=_cparams(("parallel",)),
        name="peer_topk",
    )(xn, wqt, keys, cflat)


PEER_GROUP = 2
PEER_SLOTS = 8
PEER_CHUNKS = D_MODEL // LANE


def _peer_expert_kernel(eid_ref, x_ref, gate_ref, hres_ref, uv_ref, y_ref, *scratch, tt):
    bufs = scratch[:PEER_SLOTS]
    sem = scratch[PEER_SLOTS]
    ahead = PEER_SLOTS - PEER_GROUP

    def row_copy(e, slot, k):
        return pltpu.make_async_copy(uv_ref.at[e], bufs[slot].at[pl.ds(k * PEER_CHUNKS, PEER_CHUNKS), :],
                                     sem.at[slot])

    def issue(t, slot, part=None):
        base = t * PEER_SEL
        i, n = (0, 1) if part is None else part
        for k in range(i * PEER_SEL // n, (i + 1) * PEER_SEL // n):
            row_copy(eid_ref[base + k], slot, k).start(priority=k % 2)

    def wait(slot):
        for k in range(PEER_SEL):
            row_copy(0, slot, k).wait()

    def chunk_words(slot, c):
        return bufs[slot][pl.ds(c, PEER_SEL, stride=PEER_CHUNKS), :]

    def evaluate(ts, slots, prefetch):
        for slot in slots:
            wait(slot)
        nparts = 2 * PEER_CHUNKS

        def start_part(i):
            if prefetch:
                for t, slot in zip(ts, slots):
                    issue(t + ahead, (slot + ahead) % PEER_SLOTS, (i, nparts))

        xbs = [x_ref[pl.ds(t, 1), :].astype(BF16).astype(F32) for t in ts]
        accs = [jnp.zeros((PEER_SEL, LANE), F32) for _ in ts]
        for c in range(PEER_CHUNKS):
            start_part(c)
            for j, slot in enumerate(slots):
                uf = pltpu.bitcast(chunk_words(slot, c) & jnp.int32(-65536), F32)
                accs[j] = accs[j] + uf * xbs[j][:, c * LANE:(c + 1) * LANE]
        w2s = []
        for j, t in enumerate(ts):
            act = jnp.sum(accs[j].T, axis=0, keepdims=True)
            w = gate_ref[pl.ds(t, 1), :] * (0.5 * act * (1.0 + lax.erf(act * SQRT_HALF)))
            wb = w.astype(BF16).astype(F32)
            w2s.append(jnp.broadcast_to(wb, (LANE, PEER_SEL)).T)
        outs = [[] for _ in ts]
        for c in range(PEER_CHUNKS):
            start_part(PEER_CHUNKS + c)
            for j, slot in enumerate(slots):
                vf = pltpu.bitcast(chunk_words(slot, c) << 16, F32)
                outs[j].append(jnp.sum(vf * w2s[j], axis=0, keepdims=True))
        for j, t in enumerate(ts):
            y_ref[pl.ds(t, 1), :] = hres_ref[pl.ds(t, 1), :] + jnp.concatenate(outs[j], axis=1)

    for t0 in range(ahead):
        issue(t0, t0)
    n_main = (tt - ahead) // PEER_SLOTS * PEER_SLOTS

    def body(i, carry):
        for r in range(0, PEER_SLOTS, PEER_GROUP):
            slots = list(range(r, r + PEER_GROUP))
            evaluate([i * PEER_SLOTS + s for s in slots], slots, True)
        return carry

    lax.fori_loop(0, n_main // PEER_SLOTS, body, 0)
    for t0 in range(n_main, tt, PEER_GROUP):
        ts = list(range(t0, t0 + PEER_GROUP))
        evaluate(ts, [t % PEER_SLOTS for t in ts], t0 + ahead < tt)


def _pack_expert_rows(u, v):
    bits = lambda a: lax.bitcast_convert_type(a.astype(BF16), jnp.uint16).astype(jnp.uint32)
    words = (bits(u) << 16) | bits(v)
    return lax.bitcast_convert_type(words, I32).reshape(-1, PEER_CHUNKS, LANE)


def _peer_experts(eid_flat, xn, gate, hres, uv, tt):
    t = xn.shape[0]
    assert t % tt == 0 and tt >= PEER_SLOTS
    kern = functools.partial(_peer_expert_kernel, tt=tt)
    row = lambda w: pl.BlockSpec((tt, w), lambda i: (i, 0))
    return pl.pallas_call(
        kern,
        grid=(t // tt,),
        in_specs=[pl.BlockSpec((tt * PEER_SEL,), lambda i: (i,), memory_space=pltpu.SMEM),
                  row(D_MODEL), row(PEER_SEL), row(D_MODEL),
                  pl.BlockSpec(memory_space=pl.ANY)],
        out_specs=row(D_MODEL),
        out_shape=jax.ShapeDtypeStruct((t, D_MODEL), F32),
        scratch_shapes=[pltpu.VMEM((PEER_SEL * PEER_CHUNKS, LANE), I32) for _ in range(PEER_SLOTS)]
        + [pltpu.SemaphoreType.DMA((PEER_SLOTS,))],
        compiler_params=_cparams(("arbitrary",)),
        name="peer_experts",
    )(eid_flat, xn, gate, hres, uv)


def _prep_params(norm1, w_in, gdn_conv_w, gdn_a_log, gdn_dt_bias, gdn_norm, nsa_q_norm, nsa_k_norm,
                 rel_bias, w_o, norm2, peer_wq, peer_subkeys, peer_u, peer_v):
    w = w_in[0]
    p = {}
    p["norm1"] = norm1[0][None]
    p["wm"] = jnp.concatenate([w[:, :OFF_B], w[:, OFF_NQ:OFF_NG]], axis=1).astype(BF16)
    p["ws"] = jnp.concatenate([w[:, OFF_B:OFF_NQ], w[:, OFF_NG:], jnp.zeros((D_MODEL, LANE - 32), F32)],
                              axis=1).astype(BF16)
    li = jnp.arange(LANE)
    p["seg"] = (li[:, None] // NSA_HD == li[None, :] // NSA_HD).astype(BF16)
    p["qg"] = jnp.tile(nsa_q_norm[0], 2)[None]
    p["kg"] = jnp.tile(nsa_k_norm[0], (1, 2))
    p["conv_w"] = gdn_conv_w[0]
    p["al_vec"] = jnp.zeros((1, LANE), F32).at[0, 4:8].set(gdn_a_log[0])
    p["dtb_vec"] = jnp.zeros((1, LANE), F32).at[0, 4:8].set(gdn_dt_bias[0])
    p["gnorm"] = gdn_norm[0][None]
    p["tab"] = rel_bias
    wo = w_o[0]
    p["wg"] = wo[:512].astype(BF16)
    wn = jnp.zeros((NSA_HEADS, LANE, D_MODEL), F32)
    for h in range(NSA_HEADS):
        g = h // NSA_P
        wn = wn.at[h, g * NSA_HD:(g + 1) * NSA_HD].set(wo[512 + h * NSA_HD:512 + (h + 1) * NSA_HD])
    p["wn"] = wn.astype(BF16)
    p["norm2"] = norm2[0][None]
    p["wqt"] = peer_wq[0].T.astype(BF16)
    p["keys"] = peer_subkeys[0].astype(BF16)
    p["uv"] = _pack_expert_rows(peer_u[0], peer_v[0])
    return p


def _perm_avg_matrix(n_blocks, n_rows):
    half = n_blocks // 2
    r = jnp.arange(n_blocks)
    blk = jnp.where(r < half, 2 * r, 2 * (r - half) + 1)
    return (jnp.arange(n_rows)[None, :] // CMP_BLOCK == blk[:, None]).astype(BF16)


def _token_mixer_tail(p, x, og, ocmp, oslc, owin, sm, tm_out, tm_topk, tt):
    b, l, _ = x.shape
    hres, xn2 = _out_proj(x, og, ocmp, oslc, owin, sm, p["wg"], p["wn"], p["norm2"], tm_out)
    t = b * l
    xn2 = xn2.reshape(t, D_MODEL)
    eid, gate = _peer_topk(xn2, p["wqt"], p["keys"], tm_topk)
    eid_flat = eid.reshape(PEER_SEL, t).T.reshape(t * PEER_SEL)
    gate_tok = gate.reshape(PEER_SEL, t).T
    y = _peer_experts(eid_flat, xn2, gate_tok, hres.reshape(t, D_MODEL), p["uv"], tt)
    return y.reshape(b, l, D_MODEL)


def kernel(x_prompt, x_sample, cache_nsa_kv, page_table, state_win_kv, state_conv, state_gdn, norm1, w_in, gdn_conv_w, gdn_a_log, gdn_dt_bias, gdn_norm, nsa_q_norm, nsa_k_norm, rel_bias, w_o, norm2, peer_wq, peer_subkeys, peer_u, peer_v):
    assert w_in.shape[0] == 1, "single layer"
    p = _prep_params(norm1, w_in, gdn_conv_w, gdn_a_log, gdn_dt_bias, gdn_norm, nsa_q_norm, nsa_k_norm,
                     rel_bias, w_o, norm2, peer_wq, peer_subkeys, peer_u, peer_v)
    b, l, _ = x_prompt.shape
    db, lq, _ = x_sample.shape
    n_pages = page_table.shape[1]
    past_len = n_pages * PAGE
    wb = state_win_kv.shape[2]
    assert cache_nsa_kv.shape[2] == PAGE and l >= WINDOW and l >= 3 and lq >= 3
    assert lq < CMP_BLOCK and lq <= 8 and wb == WINDOW and past_len >= wb

    tp = b * l
    tm = 256 if tp % 256 == 0 else LANE
    hc, z, sm, qn, kvn, winn = _in_proj(x_prompt.reshape(tp, D_MODEL), p["norm1"], p["wm"], p["ws"],
                                        p["seg"], p["qg"], p["kg"], tm)
    hc3, z3, sm3 = hc.reshape(b, l, GDN_CH), z.reshape(b, l, 512), sm.reshape(b, l, LANE)
    qn3, kvn3, winn3 = qn.reshape(b, l, 512), kvn.reshape(b, l, 512), winn.reshape(b, l, 256)
    og, gdn_p = _gdn(hc3, z3, sm3, p["conv_w"], p["al_vec"], p["dtb_vec"], p["gnorm"],
                     jnp.zeros((b, 3, GDN_CH), F32), jnp.zeros((b, GDN_HEADS, 128, 128), F32),
                     GDN_CHUNK, l)
    nc = l // CMP_BLOCK
    ocmp, sel = _cmp_prompt(p["tab"], qn3, kvn3, _perm_avg_matrix(nc, l), tm)
    oslc = _flash_prompt(p["tab"], qn3, kvn3, 2, 3, sel, LANE, False)
    owin = _flash_prompt(p["tab"], qn3, winn3, 0, 1, None, LANE, True)
    y_prompt = _token_mixer_tail(p, x_prompt, og, ocmp, oslc, owin, sm3, tm, tm, 256 if tp % 256 == 0 else 64)

    ts = db * lq
    hc_s, z_s, sm_s, qn_s, kvn_s, winn_s = _in_proj(x_sample.reshape(ts, D_MODEL), p["norm1"], p["wm"], p["ws"],
                                                    p["seg"], p["qg"], p["kg"], min(ts, 256))
    pad_rows = lambda a, n: jnp.pad(a.reshape(db, lq, a.shape[-1]), ((0, 0), (0, n - lq), (0, 0)))
    og_s, gdn_s = _gdn(pad_rows(hc_s, GDN_CHUNK), pad_rows(z_s, GDN_CHUNK), pad_rows(sm_s, GDN_CHUNK),
                       p["conv_w"], p["al_vec"], p["dtb_vec"], p["gnorm"], state_conv[0], state_gdn[0],
                       GDN_CHUNK, lq)
    cache3 = cache_nsa_kv[0].reshape(cache_nsa_kv.shape[1], PAGE, 512)
    q8 = pad_rows(qn_s, 8)
    nsb_s = past_len // SEL_BLOCK
    emat_s = (jnp.arange(nsb_s)[:, None] == jnp.arange(past_len)[None, :] // SEL_BLOCK).astype(BF16)
    ocmp_s, oslc_s = _nsa_sample(page_table, p["tab"], cache3, q8, pad_rows(kvn_s, PAGE), emat_s, past_len)
    wseq = jnp.concatenate([state_win_kv[0].reshape(db, wb, 256), pad_rows(winn_s, LANE)], axis=1)
    owin_s = _win_sample(p["tab"], q8, wseq, past_len, wb, lq)
    flat_heads = lambda o: o[:, :, :lq].transpose(1, 0, 2, 3).reshape(1, NSA_HEADS, ts, LANE)
    y_sample = _token_mixer_tail(p, x_sample.reshape(1, ts, D_MODEL), og_s[:, :lq].reshape(1, ts, 512),
                                 flat_heads(ocmp_s), flat_heads(oslc_s), flat_heads(owin_s),
                                 sm_s.reshape(1, ts, LANE), min(ts, 256), min(ts, 256), 64)

    kv_tail = (4, NSA_GROUPS, NSA_HD)
    return (y_prompt,
            y_sample.reshape(db, lq, D_MODEL),
            kvn.reshape((1, b, l) + kv_tail),
            winn3[:, l - WINDOW:].reshape(1, b, WINDOW, 2, NSA_GROUPS, NSA_HD),
            hc3[:, l - 3:][None],
            gdn_p[None],
            kvn_s.reshape((1, db, lq) + kv_tail),
            wseq[:, lq:lq + wb].reshape(1, db, wb, 2, NSA_GROUPS, NSA_HD),
            hc_s.reshape(db, lq, GDN_CH)[:, lq - 3:][None],
            gdn_s[None])
```

```python
import functools
import math

import numpy as np
import jax
import jax.numpy as jnp
from jax import lax
from jax.experimental import pallas as pl
from jax.experimental.pallas import tpu as pltpu

F32 = jnp.float32
BF16 = jnp.bfloat16
I32 = jnp.int32

D_MODEL = 1024
EPS = 1e-6
NEG_INF = -1e30
FORCE_SCORE = 1e4
GDN_HEADS = 4
GDN_DK = 128
GDN_CHUNK = 64
GDN_CH = 1536
NSA_HEADS = 8
NSA_GROUPS = 2
NSA_P = 4
NSA_HD = 64
CMP_BLOCK = 32
SEL_BLOCK = 64
SEL_TOPK = 16
WINDOW = 512
REL_BUCKETS = 32
PAGE = 128
PEER_HEADS = 8
PEER_NKEYS = 128
PEER_TOPK = 16
PEER_HALF = 64
PEER_SEL = PEER_HEADS * PEER_TOPK
PEER_NCAND = 80
OFF_B = 2048
OFF_NQ = 2056
OFF_NG = 3336
LANE = 128
VMEM_LIMIT = 56 * 1024 * 1024
REMOVED = -3.0e38
SQRT_HALF = 0.7071067811865476


def _cparams(sem):
    return pltpu.CompilerParams(dimension_semantics=sem, vmem_limit_bytes=VMEM_LIMIT)


def _dot(a, b):
    return jnp.dot(a, b, preferred_element_type=F32)


def _dot_nt(a, b):
    return lax.dot_general(a, b, (((1,), (1,)), ((), ())), preferred_element_type=F32)


def _dot_tn(a, b):
    return lax.dot_general(a, b, (((0,), (0,)), ((), ())), preferred_element_type=F32)


def _split2(x):
    hi = x.astype(BF16)
    lo = (x - hi.astype(F32)).astype(BF16)
    return hi, lo


def _split3(x):
    hi = x.astype(BF16)
    r = x - hi.astype(F32)
    mid = r.astype(BF16)
    lo = (r - mid.astype(F32)).astype(BF16)
    return hi, mid, lo


def _dot_exact_lhs(m01, x):
    hi, mid, lo = _split3(x)
    return _dot(m01, hi) + (_dot(m01, mid) + _dot(m01, lo))


def _mm3(a, b):
    ah, al = _split2(a)
    bh, bl = _split2(b)
    return _dot(ah, bh) + (_dot(ah, bl) + _dot(al, bh))


def _iota_f(shape, axis):
    return lax.broadcasted_iota(I32, shape, axis).astype(F32)


def _rel_bucket(dist):
    d = jnp.maximum(dist, 0)
    df = jnp.maximum(d, 1).astype(F32)
    large = 16 + (jnp.log(df / 16.0) / math.log(128.0) * 16.0).astype(I32)
    large = jnp.minimum(large, REL_BUCKETS - 1)
    return jnp.where(d < 16, d, large)


def _bias_from_bucket(bucket, tab_ref, head, lo=0, hi=REL_BUCKETS - 1):
    if isinstance(lo, int) and isinstance(hi, int):
        b = jnp.zeros(bucket.shape, F32)
        for k in range(lo, hi + 1):
            b = jnp.where(bucket == k, tab_ref[k, head], b)
        return b

    def body(k, b):
        return jnp.where(bucket == k, tab_ref[k, head], b)

    return lax.fori_loop(lo, hi + 1, body, jnp.zeros(bucket.shape, F32))


def _topk_mask(s, k, axis):
    n = s.shape[axis]
    ids = _iota_f(s.shape, axis)
    sel = jnp.zeros(s.shape, F32)
    for _ in range(k):
        m = jnp.max(s, axis=axis, keepdims=True)
        idx = jnp.min(jnp.where(s == m, ids, float(n)), axis=axis, keepdims=True)
        hit = ids == idx
        sel = jnp.where(hit, 1.0, sel)
        s = jnp.where(hit, REMOVED, s)
    return sel


def _seg_rmsnorm(v, gain, seg):
    sq = v * v
    hi, lo = _split2(sq)
    ssum = _dot(hi, seg) + _dot(lo, seg)
    return v * lax.rsqrt(ssum * (1.0 / NSA_HD) + EPS) * gain


def _inproj_kernel(x_ref, g1_ref, wm_ref, ws_ref, seg_ref, qg_ref, kg_ref,
                   hc_ref, z_ref, sm_ref, q_ref, kv_ref, win_ref):
    x = x_ref[...]
    ms = jnp.mean(x * x, axis=-1, keepdims=True)
    xn = (x * lax.rsqrt(ms + EPS) * g1_ref[...]).astype(BF16)
    h = _dot(xn, wm_ref[...])
    sm_ref[...] = _dot(xn, ws_ref[...])
    hc_ref[...] = h[:, :GDN_CH]
    z_ref[...] = h[:, GDN_CH:2048]
    seg = seg_ref[...]
    qg = qg_ref[...]
    kg = kg_ref[...]
    for i in range(4):
        q_ref[:, i * LANE:(i + 1) * LANE] = _seg_rmsnorm(h[:, 2048 + i * LANE:2048 + (i + 1) * LANE], qg, seg)
    kv_ref[:, 0:128] = _seg_rmsnorm(h[:, 2560:2688], kg[0:1], seg)
    kv_ref[:, 128:256] = h[:, 2688:2816]
    kv_ref[:, 256:384] = _seg_rmsnorm(h[:, 2816:2944], kg[1:2], seg)
    kv_ref[:, 384:512] = h[:, 2944:3072]
    win_ref[:, 0:128] = _seg_rmsnorm(h[:, 3072:3200], kg[2:3], seg)
    win_ref[:, 128:256] = h[:, 3200:3328]


def _in_proj(x, norm1, wm, ws, seg, qg, kg, tm):
    t = x.shape[0]
    assert t % tm == 0
    row = lambda w: pl.BlockSpec((tm, w), lambda i: (i, 0))
    full = lambda a: pl.BlockSpec(a.shape, lambda i: (0,) * a.ndim)
    widths = (GDN_CH, 512, LANE, 512, 512, 256)
    return pl.pallas_call(
        _inproj_kernel,
        grid=(t // tm,),
        in_specs=[row(D_MODEL), full(norm1), full(wm), full(ws), full(seg), full(qg), full(kg)],
        out_specs=[row(w) for w in widths],
        out_shape=[jax.ShapeDtypeStruct((t, w), F32) for w in widths],
        compiler_params=_cparams(("parallel",)),
        name="in_proj",
    )(x, norm1, wm, ws, seg, qg, kg)


def _inv_unit_lower(lmat, c):
    ri = lax.broadcasted_iota(I32, lmat.shape, 0)
    ci = lax.broadcasted_iota(I32, lmat.shape, 1)
    eye = jnp.where(ri == ci, 1.0, 0.0).astype(F32)
    n = -lmat
    p = eye + n
    m = _mm3(n, n)
    span = 2
    while True:
        mm = _mm3 if span == 2 else (lambda a, b: _dot(a.astype(BF16), b.astype(BF16)))
        p = p + mm(p, m)
        span *= 2
        if span >= c:
            break
        m = mm(m, m)
    return p


def _gdn_kernel(hc_ref, z_ref, sm_ref, cw_ref, al_ref, dtb_ref, gn_ref, conv0_ref, s0_ref,
                og_ref, sfin_ref, xbuf, s_scr, *, c, l_valid, n_chunks, nseq):
    ci = pl.program_id(1)

    @pl.when(ci == 0)
    def _():
        for sq in range(nseq):
            xbuf[sq, 0:8, :] = jnp.zeros((8, GDN_CH), F32)
            xbuf[sq, 5:8, :] = conv0_ref[sq]
        s_scr[...] = s0_ref[...]

    for sq in range(nseq):
        _gdn_chunk(hc_ref.at[sq], z_ref.at[sq], sm_ref.at[sq], cw_ref, al_ref, dtb_ref, gn_ref,
                   og_ref.at[sq], xbuf.at[sq], s_scr.at[sq], ci, c, l_valid)

    @pl.when(ci == n_chunks - 1)
    def _():
        sfin_ref[...] = s_scr[...]


def _gdn_chunk(hc_ref, z_ref, sm_ref, cw_ref, al_ref, dtb_ref, gn_ref, og_ref, xbuf, s_scr, ci, c, l_valid):
    xbuf[8:8 + c, :] = hc_ref[...]
    w = cw_ref[...]
    y = (xbuf[5:5 + c, :] * w[0:1] + xbuf[6:6 + c, :] * w[1:2]
         + xbuf[7:7 + c, :] * w[2:3] + xbuf[8:8 + c, :] * w[3:4])
    tail = xbuf[5 + c:8 + c, :]
    xbuf[5:8, :] = tail
    y = y * jax.nn.sigmoid(y)

    sm = sm_ref[...]
    rowid = ci * c + lax.broadcasted_iota(I32, (c, 1), 0)
    rvalid = rowid < l_valid
    beta_all = jnp.where(rvalid, jax.nn.sigmoid(sm), 0.0)
    sp_in = sm + dtb_ref[...]
    softplus = jnp.maximum(sp_in, 0.0) + jnp.log1p(jnp.exp(-jnp.abs(sp_in)))
    g_all = jnp.where(rvalid, -jnp.exp(al_ref[...]) * softplus, 0.0)

    ri = lax.broadcasted_iota(I32, (c, c), 0)
    cj = lax.broadcasted_iota(I32, (c, c), 1)
    tri = jnp.where(ri >= cj, 1.0, 0.0).astype(BF16)
    triu = jnp.where(ri <= cj, 1.0, 0.0).astype(BF16)
    g_hi, g_mid, g_lo = _split3(g_all)
    gcum = _dot(tri, g_hi) + (_dot(tri, g_mid) + _dot(tri, g_lo))
    gcum_t = _dot_tn(g_hi, triu) + (_dot_tn(g_mid, triu) + _dot_tn(g_lo, triu))
    gn = gn_ref[...]

    heads = range(GDN_HEADS)
    stack = lambda pieces: jnp.concatenate(pieces, axis=0)

    def l2n(x):
        return jnp.where(rvalid, x * lax.rsqrt(jnp.sum(x * x, axis=-1, keepdims=True) + EPS), 0.0)

    q_s = stack([l2n(y[:, h * 128:(h + 1) * 128]) * (GDN_DK ** -0.5) for h in heads])
    k_s = stack([l2n(y[:, 512 + h * 128:512 + (h + 1) * 128]) for h in heads])
    v_s = stack([jnp.where(rvalid, y[:, 1024 + h * 128:1024 + (h + 1) * 128], 0.0) for h in heads])
    beta_s = stack([beta_all[:, h:h + 1] for h in heads])
    gc_s = stack([gcum[:, 4 + h:5 + h] for h in heads])
    gl_s = stack([jnp.broadcast_to(gcum[c - 1:c, 4 + h:5 + h], (c, 1)) for h in heads])
    gct_s = jnp.concatenate([gcum_t[4 + h:5 + h, :] for h in heads], axis=1)
    r = GDN_HEADS * c
    rr = lax.broadcasted_iota(I32, (r, r), 0)
    rc = lax.broadcasted_iota(I32, (r, r), 1)
    same = (rr // c) == (rc // c)
    causal = same & (rr >= rc)
    strict = same & (rr > rc)
    decay = jnp.where(causal, jnp.exp(jnp.where(causal, gc_s - gct_s, 0.0)), 0.0)
    kb = k_s * beta_s
    ksb = k_s.astype(BF16)
    lmat = jnp.where(strict, _dot_nt(kb.astype(BF16), ksb) * decay, 0.0)
    tmat = _inv_unit_lower(lmat, c).astype(BF16)
    eg = jnp.exp(gc_s)
    u = _dot(tmat, (v_s * beta_s).astype(BF16))
    wmat = _dot(tmat, (kb * eg).astype(BF16)).astype(BF16)
    a_intra = (_dot_nt(q_s.astype(BF16), ksb) * decay).astype(BF16)
    q_dec = (q_s * eg).astype(BF16)
    k_dec = (k_s * jnp.exp(gl_s - gc_s)).astype(BF16)
    rows = lambda a, h: a[h * c:(h + 1) * c]
    sbs = [s_scr[h].astype(BF16) for h in heads]
    v_new = stack([rows(u, h) - _dot(rows(wmat, h), sbs[h]) for h in heads])
    vnb = v_new.astype(BF16)
    o_intra = _dot(a_intra, vnb)
    for h in heads:
        g_last = jnp.exp(gcum[c - 1:c, 4 + h:5 + h])
        s_scr[h] = s_scr[h] * g_last + _dot_tn(rows(k_dec, h), rows(vnb, h))
        o = _dot(rows(q_dec, h), sbs[h]) + rows(o_intra, h)
        o = o * lax.rsqrt(jnp.mean(o * o, axis=-1, keepdims=True) + EPS) * gn
        zh = z_ref[:, h * 128:(h + 1) * 128]
        og_ref[:, h * 128:(h + 1) * 128] = o * (zh * jax.nn.sigmoid(zh))


GDN_SEQS = 2


def _gdn(hc, z, sm, conv_w, al_vec, dtb_vec, gnorm, conv0, s0, c, l_valid):
    b, lp, _ = hc.shape
    nseq = GDN_SEQS if b % GDN_SEQS == 0 else 1
    assert lp % c == 0
    n_chunks = lp // c
    full = lambda a: pl.BlockSpec(a.shape, lambda i, j: (0,) * a.ndim)
    seq = lambda w: pl.BlockSpec((nseq, c, w), lambda i, j: (i, j, 0))
    state = pl.BlockSpec((nseq, GDN_HEADS, 128, 128), lambda i, j: (i, 0, 0, 0))
    kern = functools.partial(_gdn_kernel, c=c, l_valid=l_valid, n_chunks=n_chunks, nseq=nseq)
    return pl.pallas_call(
        kern,
        grid=(b // nseq, n_chunks),
        in_specs=[seq(GDN_CH), seq(512), seq(LANE), full(conv_w), full(al_vec), full(dtb_vec), full(gnorm),
                  pl.BlockSpec((nseq, 3, GDN_CH), lambda i, j: (i, 0, 0)), state],
        out_specs=[seq(512), state],
        out_shape=[jax.ShapeDtypeStruct((b, lp, 512), F32),
                   jax.ShapeDtypeStruct((b, GDN_HEADS, 128, 128), F32)],
        scratch_shapes=[pltpu.VMEM((nseq, c + 8, GDN_CH), F32), pltpu.VMEM((nseq, GDN_HEADS, 128, 128), F32)],
        compiler_params=_cparams(("parallel", "arbitrary")),
        name="gdn",
    )(hc, z, sm, conv_w, al_vec, dtb_vec, gnorm, conv0, s0)


def _head_q128(q, h):
    g = h // NSA_P
    piece = q[:, (h // 2) * LANE:(h // 2 + 1) * LANE]
    lane = lax.broadcasted_iota(I32, piece.shape, 1)
    keep = (lane >= NSA_HD) if h % 2 == 1 else (lane < NSA_HD)
    qm = jnp.where(keep, piece, 0.0)
    if h % 2 != g:
        qm = pltpu.roll(qm, NSA_HD, 1)
    return qm


def _select_scores(imp, qpos, nsb):
    score = imp[:, :nsb] + imp[:, nsb:]
    j = lax.broadcasted_iota(I32, score.shape, 1)
    cur = qpos // SEL_BLOCK
    forced = (j == 0) | (j == cur) | (j == cur - 1)
    future = j * SEL_BLOCK > qpos
    return jnp.where(future, -1.0, jnp.where(forced, FORCE_SCORE, score))


def _cmp_attention(tab_ref, q, kcv, qpos, ocmp_ref):
    nc = kcv.shape[0]
    nsb = nc // 2
    kc = kcv[:, 0:128].astype(BF16)
    vc = kcv[:, 128:256].astype(BF16)
    lane = lax.broadcasted_iota(I32, (1, nc), 1)
    blk = jnp.where(lane < nsb, 2 * lane, 2 * (lane - nsb) + 1)
    dist = qpos - (blk * CMP_BLOCK + CMP_BLOCK - 1)
    valid = dist >= 0
    bucket = _rel_bucket(dist)
    scores = []
    for g in range(NSA_GROUPS):
        imp = jnp.zeros((q.shape[0], nc), F32)
        for p in range(NSA_P):
            h = g * NSA_P + p
            qm = _head_q128(q, h).astype(BF16)
            logits = _dot_nt(qm, kc) * (NSA_HD ** -0.5) + _bias_from_bucket(bucket, tab_ref, h)
            l = jnp.where(valid, logits, NEG_INF)
            m = jnp.max(l, axis=-1, keepdims=True)
            pr = jnp.where(valid, jnp.exp(l - m), 0.0)
            pr = pr / jnp.maximum(jnp.sum(pr, axis=-1, keepdims=True), 1e-30)
            ocmp_ref[h] = _dot(pr.astype(BF16), vc)
            imp = imp + pr
        scores.append(_select_scores(imp, qpos, nsb))
    return scores


def _flash_init(m_scr, l_scr, acc_scr):
    m_scr[...] = jnp.full(m_scr.shape, NEG_INF, F32)
    l_scr[...] = jnp.zeros(l_scr.shape, F32)
    acc_scr[...] = jnp.zeros(acc_scr.shape, F32)


def _cmp_prompt_kernel(tab_ref, q_ref, kv_ref, mavg_ref, ocmp_ref, sel_ref, kc_scr, *, tq, nsb):
    qi = pl.program_id(1)

    @pl.when(qi == 0)
    def _():
        kc_scr[...] = _dot_exact_lhs(mavg_ref[...], kv_ref[...]) * (1.0 / CMP_BLOCK)

    qpos = qi * tq + lax.broadcasted_iota(I32, (tq, 1), 0)
    scores = _cmp_attention(tab_ref, q_ref[...], kc_scr[...], qpos, ocmp_ref)
    st = jnp.concatenate(scores, axis=1).T
    k = min(SEL_TOPK, nsb)
    sel_ref[...] = jnp.concatenate([_topk_mask(st[:nsb], k, 0), _topk_mask(st[nsb:], k, 0)], axis=0)


def _cmp_prompt(tab, qn, kvn, mavg, tq):
    b, l, _ = qn.shape
    nc = l // CMP_BLOCK
    assert l % tq == 0 and l % SEL_BLOCK == 0
    kern = functools.partial(_cmp_prompt_kernel, tq=tq, nsb=nc // 2)
    return pl.pallas_call(
        kern,
        grid=(b, l // tq),
        in_specs=[pl.BlockSpec(memory_space=pltpu.SMEM),
                  pl.BlockSpec((None, tq, 512), lambda i, j: (i, j, 0)),
                  pl.BlockSpec((None, l, 256), lambda i, j: (i, 0, 0)),
                  pl.BlockSpec(mavg.shape, lambda i, j: (0, 0))],
        out_specs=[pl.BlockSpec((None, NSA_HEADS, tq, LANE), lambda i, j: (i, 0, j, 0)),
                   pl.BlockSpec((None, nc, tq), lambda i, j: (i, 0, j))],
        out_shape=[jax.ShapeDtypeStruct((b, NSA_HEADS, l, LANE), F32),
                   jax.ShapeDtypeStruct((b, nc, l), F32)],
        scratch_shapes=[pltpu.VMEM((nc, 256), F32)],
        compiler_params=_cparams(("parallel", "arbitrary")),
        name="cmp_prompt",
    )(tab, qn, kvn, mavg)


def _flash_prompt_kernel(tab_ref, q_ref, k_ref, v_ref, *rest, t, windowed, nsb, n_dist):
    if windowed:
        o_ref, bias_scr, m_scr, l_scr, acc_scr = rest
    else:
        sel_ref, o_ref, bias_scr, m_scr, l_scr, acc_scr = rest
    qi = pl.program_id(1)
    lane_i = lax.broadcasted_iota(I32, (1, t), 1)
    sub_j = lax.broadcasted_iota(I32, (t, 1), 0)

    @pl.when((pl.program_id(0) == 0) & (qi == 0))
    def _():
        def build(d, carry):
            bucket = _rel_bucket(d * t + lane_i - sub_j)
            for h in range(NSA_HEADS):
                bias_scr[h, d] = _bias_from_bucket(bucket, tab_ref, h)
            return carry

        lax.fori_loop(0, n_dist, build, 0)

    q = q_ref[...]
    qts = [_head_q128(q, h).T.astype(BF16) for h in range(NSA_HEADS)]
    qpos = qi * t + lane_i
    _flash_init(m_scr, l_scr, acc_scr)
    tk = 2 * t
    sub_k = lax.broadcasted_iota(I32, (tk, 1), 0)
    k_lo = jnp.maximum(qi - WINDOW // t, 0) // 2 if windowed else 0

    def body(ki, carry):
        k0 = pl.multiple_of(ki * tk, tk)
        kt = k_ref[pl.ds(k0, tk), :].astype(BF16)
        vtt = v_ref[pl.ds(k0, tk), :].T.astype(BF16)
        dist = qpos - (k0 + sub_k)
        ok = dist >= 0
        if windowed:
            ok = ok & (dist <= WINDOW)
        d = qi - 2 * ki
        d0 = jnp.minimum(d, n_dist - 1)
        d1 = jnp.clip(d - 1, 0, n_dist - 1)
        for g in range(NSA_GROUPS):
            if windowed:
                valid = ok
            else:
                blk = g * nsb + ki * (tk // SEL_BLOCK)
                rows = [sel_ref[pl.ds(blk + r, 1), :] for r in range(tk // SEL_BLOCK)]
                selm = rows[-1]
                for r in range(tk // SEL_BLOCK - 2, -1, -1):
                    selm = jnp.where(sub_k < (r + 1) * SEL_BLOCK, rows[r], selm)
                valid = ok & (selm > 0.5)
            for p in range(NSA_P):
                h = g * NSA_P + p
                bias = jnp.concatenate([bias_scr[h, d0], bias_scr[h, d1]], axis=0)
                s = _dot(kt, qts[h]) * (NSA_HD ** -0.5) + bias
                l = jnp.where(valid, s, NEG_INF)
                m_old = m_scr[h]
                m_new = jnp.maximum(m_old, jnp.max(l, axis=0, keepdims=True))
                pr = jnp.where(valid, jnp.exp(l - m_new), 0.0)
                alpha = jnp.exp(m_old - m_new)
                l_scr[h] = alpha * l_scr[h] + jnp.sum(pr, axis=0, keepdims=True)
                acc_scr[h] = alpha * acc_scr[h] + _dot(vtt, pr.astype(BF16))
                m_scr[h] = m_new
        return carry

    lax.fori_loop(k_lo, qi // 2 + 1, body, 0)
    for h in range(NSA_HEADS):
        o_ref[h] = (acc_scr[h] / jnp.maximum(l_scr[h], 1e-30)).T


def _flash_prompt(tab, qn, kv_arr, k_blk, v_blk, sel, t, windowed):
    b, l, _ = qn.shape
    assert l % (2 * t) == 0 and t == LANE
    n_dist = WINDOW // t + 1 if windowed else l // t
    kern = functools.partial(_flash_prompt_kernel, t=t, windowed=windowed, nsb=l // SEL_BLOCK, n_dist=n_dist)
    in_specs = [pl.BlockSpec(memory_space=pltpu.SMEM),
                pl.BlockSpec((None, t, 512), lambda i, j: (i, j, 0)),
                pl.BlockSpec((None, l, LANE), lambda i, j: (i, 0, k_blk)),
                pl.BlockSpec((None, l, LANE), lambda i, j: (i, 0, v_blk))]
    args = [tab, qn, kv_arr, kv_arr]
    if not windowed:
        in_specs += [pl.BlockSpec((None, sel.shape[1], t), lambda i, j: (i, 0, j))]
        args += [sel]
    return pl.pallas_call(
        kern,
        grid=(b, l // t),
        in_specs=in_specs,
        out_specs=pl.BlockSpec((None, NSA_HEADS, t, LANE), lambda i, j: (i, 0, j, 0)),
        out_shape=jax.ShapeDtypeStruct((b, NSA_HEADS, l, LANE), F32),
        scratch_shapes=[pltpu.VMEM((NSA_HEADS, n_dist, t, t), F32),
                        pltpu.VMEM((NSA_HEADS, 1, t), F32), pltpu.VMEM((NSA_HEADS, 1, t), F32),
                        pltpu.VMEM((NSA_HEADS, LANE, t), F32)],
        compiler_params=_cparams(("arbitrary", "arbitrary")),
        name="win_prompt" if windowed else "slc_prompt",
    )(*args)


def _nsa_sample_kernel(pt_ref, tab_ref, q_ref, new_ref, e_ref, cache_ref, ocmp_ref, o_ref, kv_buf, sem,
                       *, n_pages, past_len, db):
    b = pl.program_id(0)
    slot = b % 2
    nsb = past_len // SEL_BLOCK

    def page_copy(seq, buf, pg):
        return pltpu.make_async_copy(cache_ref.at[pt_ref[seq, pg]],
                                     kv_buf.at[buf, pl.ds(pg * PAGE, PAGE), :], sem.at[buf])

    @pl.when(b == 0)
    def _():
        for pg in range(n_pages):
            page_copy(0, 0, pg).start()

    @pl.when(b + 1 < db)
    def _():
        for pg in range(n_pages):
            page_copy(b + 1, 1 - slot, pg).start()

    q = q_ref[...]
    qpos = past_len + lax.broadcasted_iota(I32, (8, 1), 0)
    dist_p = qpos - lax.broadcasted_iota(I32, (1, past_len), 1)
    bucket_p = _rel_bucket(dist_p)
    dist_n = qpos - (past_len + lax.broadcasted_iota(I32, (1, PAGE), 1))
    ok_n = dist_n >= 0
    bucket_n = _rel_bucket(dist_n)
    knew = new_ref[:, 0:128].astype(BF16)
    vnew = new_ref[:, 128:256].astype(BF16)

    for pg in range(n_pages):
        page_copy(b, slot, pg).wait()

    x3 = kv_buf[slot, :, 0:256].reshape(nsb, SEL_BLOCK, 256)
    kcv = jnp.concatenate([jnp.sum(x3[:, :CMP_BLOCK], axis=1), jnp.sum(x3[:, CMP_BLOCK:], axis=1)],
                          axis=0) * (1.0 / CMP_BLOCK)
    scores = _cmp_attention(tab_ref, q, kcv, qpos, ocmp_ref)
    k_sel = min(SEL_TOPK, nsb + 1) - 1

    kall = kv_buf[slot, :, 256:384].astype(BF16)
    vall = kv_buf[slot, :, 384:512].astype(BF16)
    for g in range(NSA_GROUPS):
        sel_g = _topk_mask(scores[g], k_sel, 1).astype(BF16)
        qg = jnp.concatenate([_head_q128(q, g * NSA_P + p) for p in range(NSA_P)], axis=0).astype(BF16)
        s_p = _dot_nt(qg, kall) * (NSA_HD ** -0.5)
        s_n = _dot_nt(qg, knew) * (NSA_HD ** -0.5)
        mask_p = _dot(sel_g, e_ref[...]) > 0.5
        pps, pns, dens = [], [], []
        for p in range(NSA_P):
            h = g * NSA_P + p
            rows = slice(8 * p, 8 * p + 8)
            lp = jnp.where(mask_p, s_p[rows] + _bias_from_bucket(bucket_p, tab_ref, h), NEG_INF)
            ln = jnp.where(ok_n, s_n[rows] + _bias_from_bucket(bucket_n, tab_ref, h), NEG_INF)
            m = jnp.maximum(jnp.max(lp, axis=-1, keepdims=True), jnp.max(ln, axis=-1, keepdims=True))
            pp = jnp.where(mask_p, jnp.exp(lp - m), 0.0)
            pn = jnp.where(ok_n, jnp.exp(ln - m), 0.0)
            dens.append(jnp.sum(pp, axis=-1, keepdims=True) + jnp.sum(pn, axis=-1, keepdims=True))
            pps.append(pp.astype(BF16))
            pns.append(pn.astype(BF16))
        o = _dot(jnp.concatenate(pps, axis=0), vall) + _dot(jnp.concatenate(pns, axis=0), vnew)
        for p in range(NSA_P):
            o_ref[g * NSA_P + p] = o[8 * p:8 * p + 8] / jnp.maximum(dens[p], 1e-30)


def _nsa_sample(page_table, tab, cache, q8, new_kv, emat, past_len):
    db, n_pages = page_table.shape
    kern = functools.partial(_nsa_sample_kernel, n_pages=n_pages, past_len=past_len, db=db)
    heads = pl.BlockSpec((None, NSA_HEADS, 8, LANE), lambda i, pt: (i, 0, 0, 0))
    grid_spec = pltpu.PrefetchScalarGridSpec(
        num_scalar_prefetch=1,
        grid=(db,),
        in_specs=[pl.BlockSpec(memory_space=pltpu.SMEM),
                  pl.BlockSpec((None, 8, 512), lambda i, pt: (i, 0, 0)),
                  pl.BlockSpec((None, PAGE, 256), lambda i, pt: (i, 0, 1)),
                  pl.BlockSpec(emat.shape, lambda i, pt: (0, 0)),
                  pl.BlockSpec(memory_space=pl.ANY)],
        out_specs=[heads, heads],
        scratch_shapes=[pltpu.VMEM((2, n_pages * PAGE, 512), F32), pltpu.SemaphoreType.DMA((2,))],
    )
    return pl.pallas_call(
        kern,
        grid_spec=grid_spec,
        out_shape=[jax.ShapeDtypeStruct((db, NSA_HEADS, 8, LANE), F32)] * 2,
        compiler_params=_cparams(("arbitrary",)),
        name="nsa_sample",
    )(page_table, tab, q8, new_kv, emat, cache)


def _win_sample_kernel(tab_ref, q_ref, w_ref, o_ref, *, past_len, wb, lq):
    q = q_ref[...]
    qpos = past_len + lax.broadcasted_iota(I32, (8, 1), 0)
    wseq = w_ref[...]
    n = wseq.shape[0]
    kw = wseq[:, 0:128].astype(BF16)
    vw = wseq[:, 128:256].astype(BF16)
    j = lax.broadcasted_iota(I32, (1, n), 1)
    kpos = past_len - wb + j
    dist = qpos - kpos
    valid = (dist >= 0) & (dist <= WINDOW) & (kpos >= 0) & (j < wb + lq)
    bucket = _rel_bucket(dist)
    for h in range(NSA_HEADS):
        qm = _head_q128(q, h).astype(BF16)
        s = _dot_nt(qm, kw) * (NSA_HD ** -0.5) + _bias_from_bucket(bucket, tab_ref, h)
        l = jnp.where(valid, s, NEG_INF)
        m = jnp.max(l, axis=-1, keepdims=True)
        pr = jnp.where(valid, jnp.exp(l - m), 0.0)
        pr = pr / jnp.maximum(jnp.sum(pr, axis=-1, keepdims=True), 1e-30)
        o_ref[h] = _dot(pr.astype(BF16), vw)


def _win_sample(tab, q8, wseq, past_len, wb, lq):
    db, n, _ = wseq.shape
    kern = functools.partial(_win_sample_kernel, past_len=past_len, wb=wb, lq=lq)
    return pl.pallas_call(
        kern,
        grid=(db,),
        in_specs=[pl.BlockSpec(memory_space=pltpu.SMEM),
                  pl.BlockSpec((None, 8, 512), lambda i: (i, 0, 0)),
                  pl.BlockSpec((None, n, 256), lambda i: (i, 0, 0))],
        out_specs=pl.BlockSpec((None, NSA_HEADS, 8, LANE), lambda i: (i, 0, 0, 0)),
        out_shape=jax.ShapeDtypeStruct((db, NSA_HEADS, 8, LANE), F32),
        compiler_params=_cparams(("parallel",)),
        name="win_sample",
    )(tab, q8, wseq)


def _outproj_kernel(x_ref, og_ref, oc_ref, os_ref, ow_ref, sm_ref, wg_ref, wn_ref, n2_ref,
                    hres_ref, xn_ref):
    gates = jax.nn.sigmoid(sm_ref[...])
    acc = x_ref[...] + _dot(og_ref[...].astype(BF16), wg_ref[...])
    for h in range(NSA_HEADS):
        c = 8 + 3 * h
        on = (gates[:, c:c + 1] * oc_ref[h] + gates[:, c + 1:c + 2] * os_ref[h]
              + gates[:, c + 2:c + 3] * ow_ref[h])
        acc = acc + _dot(on.astype(BF16), wn_ref[h])
    hres_ref[...] = acc
    ms = jnp.mean(acc * acc, axis=-1, keepdims=True)
    xn_ref[...] = acc * lax.rsqrt(ms + EPS) * n2_ref[...]


def _out_proj(x, og, ocmp, oslc, owin, sm, wg, wn, norm2, tm):
    b, l, _ = x.shape
    assert l % tm == 0
    seq = lambda w: pl.BlockSpec((None, tm, w), lambda i, j: (i, j, 0))
    heads = pl.BlockSpec((None, NSA_HEADS, tm, LANE), lambda i, j: (i, 0, j, 0))
    full = lambda a: pl.BlockSpec(a.shape, lambda i, j: (0,) * a.ndim)
    return pl.pallas_call(
        _outproj_kernel,
        grid=(b, l // tm),
        in_specs=[seq(D_MODEL), seq(512), heads, heads, heads, seq(LANE), full(wg), full(wn), full(norm2)],
        out_specs=[seq(D_MODEL), seq(D_MODEL)],
        out_shape=[jax.ShapeDtypeStruct((b, l, D_MODEL), F32)] * 2,
        compiler_params=_cparams(("parallel", "parallel")),
        name="out_proj",
    )(x, og, ocmp, oslc, owin, sm, wg, wn, norm2)


def _peer_topk_kernel(x_ref, wqt_ref, keys_ref, cflat_ref, eid_ref, gate_ref, qt_scr, sv_scr, si_scr, top_scr,
                      *, tm):
    qt_scr[...] = _dot_nt(wqt_ref[...], x_ref[...].astype(BF16))
    rows = _iota_f((PEER_NKEYS, tm), 0)
    cflat = jnp.broadcast_to(cflat_ref[...], (PEER_NCAND, tm))

    def candidates(hh):
        s1 = sv_scr[hh, 0]
        s2 = sv_scr[hh, 1]
        i1 = si_scr[hh, 0] * float(PEER_NKEYS)
        i2 = si_scr[hh, 1]
        cand = [s1[0:1] + s2]
        eidc = [i1[0:1] + i2]
        for a in range(1, 8):
            cand.append(s1[a:a + 1] + s2[0:8])
            eidc.append(i1[a:a + 1] + i2[0:8])
        cand.append(s1[8:16] + s2[0:1])
        eidc.append(i1[8:16] + i2[0:1])
        return jnp.where(cflat >= 0.0, jnp.concatenate(cand, axis=0), REMOVED), jnp.concatenate(eidc, axis=0)

    def head_pair_body(hp, carry):
        for hh in range(2):
            h = 2 * hp + hh
            for c in range(2):
                off = pl.multiple_of(h * (2 * PEER_HALF) + c * PEER_HALF, PEER_HALF)
                qs = qt_scr[pl.ds(off, PEER_HALF), :].astype(BF16)
                s = _dot(keys_ref[h, c], qs)

                def round_body(r, s):
                    m = jnp.max(s, axis=0, keepdims=True)
                    idx = jnp.min(jnp.where(s == m, rows, float(PEER_NKEYS)), axis=0, keepdims=True)
                    sv_scr[hh, c, pl.ds(r, 1), :] = m
                    si_scr[hh, c, pl.ds(r, 1), :] = idx
                    return jnp.where(rows == idx, REMOVED, s)

                lax.fori_loop(0, PEER_TOPK, round_body, s)
        cands, eidcs = zip(*[candidates(hh) for hh in range(2)])

        def round2(r, cands):
            out = []
            for hh, cand in enumerate(cands):
                m = jnp.max(cand, axis=0, keepdims=True)
                f = jnp.min(jnp.where(cand == m, cflat, 1e9), axis=0, keepdims=True)
                hit = cflat == f
                top_scr[hh, pl.ds(r, 1), :] = m
                eid_ref[2 * hp + hh, pl.ds(r, 1), :] = jnp.sum(
                    jnp.where(hit, eidcs[hh], 0.0), axis=0, keepdims=True).astype(I32)
                out.append(jnp.where(hit, REMOVED, cand))
            return tuple(out)

        lax.fori_loop(0, PEER_TOPK, round2, tuple(cands))
        for hh in range(2):
            top = top_scr[hh]
            e = jnp.exp(top - jnp.max(top, axis=0, keepdims=True))
            gate_ref[2 * hp + hh] = e / jnp.sum(e, axis=0, keepdims=True)
        return carry

    lax.fori_loop(0, PEER_HEADS // 2, head_pair_body, 0)


def _peer_cflat():
    rows = [(0, b) for b in range(16)]
    for a in range(1, 8):
        rows += [(a, b) for b in range(8)]
    rows += [(a, 0) for a in range(8, 16)]
    flat = [a * 16 + b if (a + 1) * (b + 1) <= PEER_TOPK else -1 for a, b in rows]
    assert len(flat) == PEER_NCAND
    return jnp.asarray(np.array(flat, np.float32).reshape(PEER_NCAND, 1))


def _peer_topk(xn, wqt, keys, tm):
    t = xn.shape[0]
    assert t % tm == 0
    cflat = _peer_cflat()
    kern = functools.partial(_peer_topk_kernel, tm=tm)
    full = lambda a: pl.BlockSpec(a.shape, lambda i: (0,) * a.ndim)
    out_spec = pl.BlockSpec((PEER_HEADS, PEER_TOPK, tm), lambda i: (0, 0, i))
    return pl.pallas_call(
        kern,
        grid=(t // tm,),
        in_specs=[pl.BlockSpec((tm, D_MODEL), lambda i: (i, 0)), full(wqt), full(keys), full(cflat)],
        out_specs=[out_spec, out_spec],
        out_shape=[jax.ShapeDtypeStruct((PEER_HEADS, PEER_TOPK, t), I32),
                   jax.ShapeDtypeStruct((PEER_HEADS, PEER_TOPK, t), F32)],
        scratch_shapes=[pltpu.VMEM((D_MODEL, tm), F32), pltpu.VMEM((2, 2, PEER_TOPK, tm), F32),
                        pltpu.VMEM((2, 2, PEER_TOPK, tm), F32), pltpu.VMEM((2, PEER_TOPK, tm), F32)],
        compiler_params=_cparams(("parallel",)),
        name="peer_topk",
    )(xn, wqt, keys, cflat)


PEER_GROUP = 2
PEER_SLOTS = 8
PEER_CHUNKS = D_MODEL // LANE


def _peer_expert_kernel(eid_ref, x_ref, gate_ref, hres_ref, uv_ref, y_ref, *scratch, tt):
    bufs = scratch[:PEER_SLOTS]
    sem = scratch[PEER_SLOTS]
    ahead = PEER_SLOTS - PEER_GROUP

    def row_copy(e, slot, k):
        return pltpu.make_async_copy(uv_ref.at[e], bufs[slot].at[pl.ds(k * PEER_CHUNKS, PEER_CHUNKS), :],
                                     sem.at[slot])

    def issue(t, slot, part=None):
        base = t * PEER_SEL
        i, n = (0, 1) if part is None else part
        for k in range(i * PEER_SEL // n, (i + 1) * PEER_SEL // n):
            row_copy(eid_ref[base + k], slot, k).start(priority=k % 2)

    def wait(slot):
        for k in range(PEER_SEL):
            row_copy(0, slot, k).wait()

    def chunk_words(slot, c):
        return bufs[slot][pl.ds(c, PEER_SEL, stride=PEER_CHUNKS), :]

    def evaluate(ts, slots, prefetch):
        for slot in slots:
            wait(slot)
        nparts = 2 * PEER_CHUNKS

        def start_part(i):
            if prefetch:
                for t, slot in zip(ts, slots):
                    issue(t + ahead, (slot + ahead) % PEER_SLOTS, (i, nparts))

        xbs = [x_ref[pl.ds(t, 1), :].astype(BF16).astype(F32) for t in ts]
        accs = [jnp.zeros((PEER_SEL, LANE), F32) for _ in ts]
        for c in range(PEER_CHUNKS):
            start_part(c)
            for j, slot in enumerate(slots):
                uf = pltpu.bitcast(chunk_words(slot, c) & jnp.int32(-65536), F32)
                accs[j] = accs[j] + uf * xbs[j][:, c * LANE:(c + 1) * LANE]
        w2s = []
        for j, t in enumerate(ts):
            act = jnp.sum(accs[j].T, axis=0, keepdims=True)
            w = gate_ref[pl.ds(t, 1), :] * (0.5 * act * (1.0 + lax.erf(act * SQRT_HALF)))
            wb = w.astype(BF16).astype(F32)
            w2s.append(jnp.broadcast_to(wb, (LANE, PEER_SEL)).T)
        outs = [[] for _ in ts]
        for c in range(PEER_CHUNKS):
            start_part(PEER_CHUNKS + c)
            for j, slot in enumerate(slots):
                vf = pltpu.bitcast(chunk_words(slot, c) << 16, F32)
                outs[j].append(jnp.sum(vf * w2s[j], axis=0, keepdims=True))
        for j, t in enumerate(ts):
            y_ref[pl.ds(t, 1), :] = hres_ref[pl.ds(t, 1), :] + jnp.concatenate(outs[j], axis=1)

    for t0 in range(ahead):
        issue(t0, t0)
    n_main = (tt - ahead) // PEER_SLOTS * PEER_SLOTS

    def body(i, carry):
        for r in range(0, PEER_SLOTS, PEER_GROUP):
            slots = list(range(r, r + PEER_GROUP))
            evaluate([i * PEER_SLOTS + s for s in slots], slots, True)
        return carry

    lax.fori_loop(0, n_main // PEER_SLOTS, body, 0)
    for t0 in range(n_main, tt, PEER_GROUP):
        ts = list(range(t0, t0 + PEER_GROUP))
        evaluate(ts, [t % PEER_SLOTS for t in ts], t0 + ahead < tt)


def _pack_expert_rows(u, v):
    bits = lambda a: lax.bitcast_convert_type(a.astype(BF16), jnp.uint16).astype(jnp.uint32)
    words = (bits(u) << 16) | bits(v)
    return lax.bitcast_convert_type(words, I32).reshape(-1, PEER_CHUNKS, LANE)


def _peer_experts(eid_flat, xn, gate, hres, uv, tt):
    t = xn.shape[0]
    assert t % tt == 0 and tt >= PEER_SLOTS
    kern = functools.partial(_peer_expert_kernel, tt=tt)
    row = lambda w: pl.BlockSpec((tt, w), lambda i: (i, 0))
    return pl.pallas_call(
        kern,
        grid=(t // tt,),
        in_specs=[pl.BlockSpec((tt * PEER_SEL,), lambda i: (i,), memory_space=pltpu.SMEM),
                  row(D_MODEL), row(PEER_SEL), row(D_MODEL),
                  pl.BlockSpec(memory_space=pl.ANY)],
        out_specs=row(D_MODEL),
        out_shape=jax.ShapeDtypeStruct((t, D_MODEL), F32),
        scratch_shapes=[pltpu.VMEM((PEER_SEL * PEER_CHUNKS, LANE), I32) for _ in range(PEER_SLOTS)]
        + [pltpu.SemaphoreType.DMA((PEER_SLOTS,))],
        compiler_params=_cparams(("arbitrary",)),
        name="peer_experts",
    )(eid_flat, xn, gate, hres, uv)


def _prep_params(norm1, w_in, gdn_conv_w, gdn_a_log, gdn_dt_bias, gdn_norm, nsa_q_norm, nsa_k_norm,
                 rel_bias, w_o, norm2, peer_wq, peer_subkeys, peer_u, peer_v):
    w = w_in[0]
    p = {}
    p["norm1"] = norm1[0][None]
    p["wm"] = jnp.concatenate([w[:, :OFF_B], w[:, OFF_NQ:OFF_NG]], axis=1).astype(BF16)
    p["ws"] = jnp.concatenate([w[:, OFF_B:OFF_NQ], w[:, OFF_NG:], jnp.zeros((D_MODEL, LANE - 32), F32)],
                              axis=1).astype(BF16)
    li = jnp.arange(LANE)
    p["seg"] = (li[:, None] // NSA_HD == li[None, :] // NSA_HD).astype(BF16)
    p["qg"] = jnp.tile(nsa_q_norm[0], 2)[None]
    p["kg"] = jnp.tile(nsa_k_norm[0], (1, 2))
    p["conv_w"] = gdn_conv_w[0]
    p["al_vec"] = jnp.zeros((1, LANE), F32).at[0, 4:8].set(gdn_a_log[0])
    p["dtb_vec"] = jnp.zeros((1, LANE), F32).at[0, 4:8].set(gdn_dt_bias[0])
    p["gnorm"] = gdn_norm[0][None]
    p["tab"] = rel_bias
    wo = w_o[0]
    p["wg"] = wo[:512].astype(BF16)
    wn = jnp.zeros((NSA_HEADS, LANE, D_MODEL), F32)
    for h in range(NSA_HEADS):
        g = h // NSA_P
        wn = wn.at[h, g * NSA_HD:(g + 1) * NSA_HD].set(wo[512 + h * NSA_HD:512 + (h + 1) * NSA_HD])
    p["wn"] = wn.astype(BF16)
    p["norm2"] = norm2[0][None]
    p["wqt"] = peer_wq[0].T.astype(BF16)
    p["keys"] = peer_subkeys[0].astype(BF16)
    p["uv"] = _pack_expert_rows(peer_u[0], peer_v[0])
    return p


def _perm_avg_matrix(n_blocks, n_rows):
    half = n_blocks // 2
    r = jnp.arange(n_blocks)
    blk = jnp.where(r < half, 2 * r, 2 * (r - half) + 1)
    return (jnp.arange(n_rows)[None, :] // CMP_BLOCK == blk[:, None]).astype(BF16)


def _token_mixer_tail(p, x, og, ocmp, oslc, owin, sm, tm_out, tm_topk, tt):
    b, l, _ = x.shape
    hres, xn2 = _out_proj(x, og, ocmp, oslc, owin, sm, p["wg"], p["wn"], p["norm2"], tm_out)
    t = b * l
    xn2 = xn2.reshape(t, D_MODEL)
    eid, gate = _peer_topk(xn2, p["wqt"], p["keys"], tm_topk)
    eid_flat = eid.reshape(PEER_SEL, t).T.reshape(t * PEER_SEL)
    gate_tok = gate.reshape(PEER_SEL, t).T
    y = _peer_experts(eid_flat, xn2, gate_tok, hres.reshape(t, D_MODEL), p["uv"], tt)
    return y.reshape(b, l, D_MODEL)


def kernel(x_prompt, x_sample, cache_nsa_kv, page_table, state_win_kv, state_conv, state_gdn, norm1, w_in, gdn_conv_w, gdn_a_log, gdn_dt_bias, gdn_norm, nsa_q_norm, nsa_k_norm, rel_bias, w_o, norm2, peer_wq, peer_subkeys, peer_u, peer_v):
    assert w_in.shape[0] == 1, "single layer"
    p = _prep_params(norm1, w_in, gdn_conv_w, gdn_a_log, gdn_dt_bias, gdn_norm, nsa_q_norm, nsa_k_norm,
                     rel_bias, w_o, norm2, peer_wq, peer_subkeys, peer_u, peer_v)
    b, l, _ = x_prompt.shape
    db, lq, _ = x_sample.shape
    n_pages = page_table.shape[1]
    past_len = n_pages * PAGE
    wb = state_win_kv.shape[2]
    assert cache_nsa_kv.shape[2] == PAGE and l >= WINDOW and l >= 3 and lq >= 3
    assert lq < CMP_BLOCK and lq <= 8 and wb == WINDOW and past_len >= wb

    tp = b * l
    tm = 256 if tp % 256 == 0 else LANE
    hc, z, sm, qn, kvn, winn = _in_proj(x_prompt.reshape(tp, D_MODEL), p["norm1"], p["wm"], p["ws"],
                                        p["seg"], p["qg"], p["kg"], tm)
    hc3, z3, sm3 = hc.reshape(b, l, GDN_CH), z.reshape(b, l, 512), sm.reshape(b, l, LANE)
    qn3, kvn3, winn3 = qn.reshape(b, l, 512), kvn.reshape(b, l, 512), winn.reshape(b, l, 256)
    og, gdn_p = _gdn(hc3, z3, sm3, p["conv_w"], p["al_vec"], p["dtb_vec"], p["gnorm"],
                     jnp.zeros((b, 3, GDN_CH), F32), jnp.zeros((b, GDN_HEADS, 128, 128), F32),
                     GDN_CHUNK, l)
    nc = l // CMP_BLOCK
    ocmp, sel = _cmp_prompt(p["tab"], qn3, kvn3, _perm_avg_matrix(nc, l), tm)
    oslc = _flash_prompt(p["tab"], qn3, kvn3, 2, 3, sel, LANE, False)
    owin = _flash_prompt(p["tab"], qn3, winn3, 0, 1, None, LANE, True)
    y_prompt = _token_mixer_tail(p, x_prompt, og, ocmp, oslc, owin, sm3, tm, tm, 512 if tp % 512 == 0 else 64)

    ts = db * lq
    hc_s, z_s, sm_s, qn_s, kvn_s, winn_s = _in_proj(x_sample.reshape(ts, D_MODEL), p["norm1"], p["wm"], p["ws"],
                                                    p["seg"], p["qg"], p["kg"], min(ts, 256))
    pad_rows = lambda a, n: jnp.pad(a.reshape(db, lq, a.shape[-1]), ((0, 0), (0, n - lq), (0, 0)))
    og_s, gdn_s = _gdn(pad_rows(hc_s, GDN_CHUNK), pad_rows(z_s, GDN_CHUNK), pad_rows(sm_s, GDN_CHUNK),
                       p["conv_w"], p["al_vec"], p["dtb_vec"], p["gnorm"], state_conv[0], state_gdn[0],
                       GDN_CHUNK, lq)
    cache3 = cache_nsa_kv[0].reshape(cache_nsa_kv.shape[1], PAGE, 512)
    q8 = pad_rows(qn_s, 8)
    nsb_s = past_len // SEL_BLOCK
    emat_s = (jnp.arange(nsb_s)[:, None] == jnp.arange(past_len)[None, :] // SEL_BLOCK).astype(BF16)
    ocmp_s, oslc_s = _nsa_sample(page_table, p["tab"], cache3, q8, pad_rows(kvn_s, PAGE), emat_s, past_len)
    wseq = jnp.concatenate([state_win_kv[0].reshape(db, wb, 256), pad_rows(winn_s, LANE)], axis=1)
    owin_s = _win_sample(p["tab"], q8, wseq, past_len, wb, lq)
    flat_heads = lambda o: o[:, :, :lq].transpose(1, 0, 2, 3).reshape(1, NSA_HEADS, ts, LANE)
    y_sample = _token_mixer_tail(p, x_sample.reshape(1, ts, D_MODEL), og_s[:, :lq].reshape(1, ts, 512),
                                 flat_heads(ocmp_s), flat_heads(oslc_s), flat_heads(owin_s),
                                 sm_s.reshape(1, ts, LANE), min(ts, 256), min(ts, 256), 64)

    kv_tail = (4, NSA_GROUPS, NSA_HD)
    return (y_prompt,
            y_sample.reshape(db, lq, D_MODEL),
            kvn.reshape((1, b, l) + kv_tail),
            winn3[:, l - WINDOW:].reshape(1, b, WINDOW, 2, NSA_GROUPS, NSA_HD),
            hc3[:, l - 3:][None],
            gdn_p[None],
            kvn_s.reshape((1, db, lq) + kv_tail),
            wseq[:, lq:lq + wb].reshape(1, db, wb, 2, NSA_GROUPS, NSA_HD),
            hc_s.reshape(db, lq, GDN_CH)[:, lq - 3:][None],
            gdn_s[None])
```

```python
import functools
import math

import numpy as np
import jax
import jax.numpy as jnp
from jax import lax
from jax.experimental import pallas as pl
from jax.experimental.pallas import tpu as pltpu

F32 = jnp.float32
BF16 = jnp.bfloat16
I32 = jnp.int32

D_MODEL = 1024
EPS = 1e-6
NEG_INF = -1e30
FORCE_SCORE = 1e4
GDN_HEADS = 4
GDN_DK = 128
GDN_CHUNK = 64
GDN_CH = 1536
NSA_HEADS = 8
NSA_GROUPS = 2
NSA_P = 4
NSA_HD = 64
CMP_BLOCK = 32
SEL_BLOCK = 64
SEL_TOPK = 16
WINDOW = 512
REL_BUCKETS = 32
PAGE = 128
PEER_HEADS = 8
PEER_NKEYS = 128
PEER_TOPK = 16
PEER_HALF = 64
PEER_SEL = PEER_HEADS * PEER_TOPK
PEER_NCAND = 80
OFF_B = 2048
OFF_NQ = 2056
OFF_NG = 3336
LANE = 128
VMEM_LIMIT = 56 * 1024 * 1024
REMOVED = -3.0e38
SQRT_HALF = 0.7071067811865476


def _cparams(sem):
    return pltpu.CompilerParams(dimension_semantics=sem, vmem_limit_bytes=VMEM_LIMIT)


def _dot(a, b):
    return jnp.dot(a, b, preferred_element_type=F32)


def _dot_nt(a, b):
    return lax.dot_general(a, b, (((1,), (1,)), ((), ())), preferred_element_type=F32)


def _dot_tn(a, b):
    return lax.dot_general(a, b, (((0,), (0,)), ((), ())), preferred_element_type=F32)


def _split2(x):
    hi = x.astype(BF16)
    lo = (x - hi.astype(F32)).astype(BF16)
    return hi, lo


def _split3(x):
    hi = x.astype(BF16)
    r = x - hi.astype(F32)
    mid = r.astype(BF16)
    lo = (r - mid.astype(F32)).astype(BF16)
    return hi, mid, lo


def _dot_exact_lhs(m01, x):
    hi, mid, lo = _split3(x)
    return _dot(m01, hi) + (_dot(m01, mid) + _dot(m01, lo))


def _mm3(a, b):
    ah, al = _split2(a)
    bh, bl = _split2(b)
    return _dot(ah, bh) + (_dot(ah, bl) + _dot(al, bh))


def _iota_f(shape, axis):
    return lax.broadcasted_iota(I32, shape, axis).astype(F32)


def _rel_bucket(dist):
    d = jnp.maximum(dist, 0)
    df = jnp.maximum(d, 1).astype(F32)
    large = 16 + (jnp.log(df / 16.0) / math.log(128.0) * 16.0).astype(I32)
    large = jnp.minimum(large, REL_BUCKETS - 1)
    return jnp.where(d < 16, d, large)


def _bias_from_bucket(bucket, tab_ref, head, lo=0, hi=REL_BUCKETS - 1):
    if isinstance(lo, int) and isinstance(hi, int):
        b = jnp.zeros(bucket.shape, F32)
        for k in range(lo, hi + 1):
            b = jnp.where(bucket == k, tab_ref[k, head], b)
        return b

    def body(k, b):
        return jnp.where(bucket == k, tab_ref[k, head], b)

    return lax.fori_loop(lo, hi + 1, body, jnp.zeros(bucket.shape, F32))


def _bias_lookup(bucket, tab_ref, tabt_ref, head):
    rows, width = bucket.shape
    if width % LANE != 0:
        return _bias_from_bucket(bucket, tab_ref, head)
    row = jnp.broadcast_to(tabt_ref[head:head + 1, :], (rows, LANE))
    pieces = [jnp.take_along_axis(row, bucket[:, c:c + LANE], axis=1) for c in range(0, width, LANE)]
    return pieces[0] if len(pieces) == 1 else jnp.concatenate(pieces, axis=1)


def _topk_mask(s, k, axis):
    n = s.shape[axis]
    ids = _iota_f(s.shape, axis)
    sel = jnp.zeros(s.shape, F32)
    for _ in range(k):
        m = jnp.max(s, axis=axis, keepdims=True)
        idx = jnp.min(jnp.where(s == m, ids, float(n)), axis=axis, keepdims=True)
        hit = ids == idx
        sel = jnp.where(hit, 1.0, sel)
        s = jnp.where(hit, REMOVED, s)
    return sel


def _seg_rmsnorm(v, gain, seg):
    sq = v * v
    hi, lo = _split2(sq)
    ssum = _dot(hi, seg) + _dot(lo, seg)
    return v * lax.rsqrt(ssum * (1.0 / NSA_HD) + EPS) * gain


def _inproj_kernel(x_ref, g1_ref, wm_ref, ws_ref, seg_ref, qg_ref, kg_ref,
                   hc_ref, z_ref, sm_ref, q_ref, kv_ref, win_ref):
    x = x_ref[...]
    ms = jnp.mean(x * x, axis=-1, keepdims=True)
    xn = (x * lax.rsqrt(ms + EPS) * g1_ref[...]).astype(BF16)
    h = _dot(xn, wm_ref[...])
    sm_ref[...] = _dot(xn, ws_ref[...])
    hc_ref[...] = h[:, :GDN_CH]
    z_ref[...] = h[:, GDN_CH:2048]
    seg = seg_ref[...]
    qg = qg_ref[...]
    kg = kg_ref[...]
    for i in range(4):
        q_ref[:, i * LANE:(i + 1) * LANE] = _seg_rmsnorm(h[:, 2048 + i * LANE:2048 + (i + 1) * LANE], qg, seg)
    kv_ref[:, 0:128] = _seg_rmsnorm(h[:, 2560:2688], kg[0:1], seg)
    kv_ref[:, 128:256] = h[:, 2688:2816]
    kv_ref[:, 256:384] = _seg_rmsnorm(h[:, 2816:2944], kg[1:2], seg)
    kv_ref[:, 384:512] = h[:, 2944:3072]
    win_ref[:, 0:128] = _seg_rmsnorm(h[:, 3072:3200], kg[2:3], seg)
    win_ref[:, 128:256] = h[:, 3200:3328]


def _in_proj(x, norm1, wm, ws, seg, qg, kg, tm):
    t = x.shape[0]
    assert t % tm == 0
    row = lambda w: pl.BlockSpec((tm, w), lambda i: (i, 0))
    full = lambda a: pl.BlockSpec(a.shape, lambda i: (0,) * a.ndim)
    widths = (GDN_CH, 512, LANE, 512, 512, 256)
    return pl.pallas_call(
        _inproj_kernel,
        grid=(t // tm,),
        in_specs=[row(D_MODEL), full(norm1), full(wm), full(ws), full(seg), full(qg), full(kg)],
        out_specs=[row(w) for w in widths],
        out_shape=[jax.ShapeDtypeStruct((t, w), F32) for w in widths],
        compiler_params=_cparams(("parallel",)),
        name="in_proj",
    )(x, norm1, wm, ws, seg, qg, kg)


def _inv_unit_lower(lmat, c):
    ri = lax.broadcasted_iota(I32, lmat.shape, 0)
    ci = lax.broadcasted_iota(I32, lmat.shape, 1)
    eye = jnp.where(ri == ci, 1.0, 0.0).astype(F32)
    n = -lmat
    p = eye + n
    m = _mm3(n, n)
    span = 2
    while True:
        mm = _mm3 if span == 2 else (lambda a, b: _dot(a.astype(BF16), b.astype(BF16)))
        p = p + mm(p, m)
        span *= 2
        if span >= c:
            break
        m = mm(m, m)
    return p


def _gdn_kernel(hc_ref, z_ref, sm_ref, cw_ref, al_ref, dtb_ref, gn_ref, conv0_ref, s0_ref,
                og_ref, sfin_ref, xbuf, s_scr, *, c, l_valid, n_chunks, nseq):
    ci = pl.program_id(1)

    @pl.when(ci == 0)
    def _():
        for sq in range(nseq):
            xbuf[sq, 0:8, :] = jnp.zeros((8, GDN_CH), F32)
            xbuf[sq, 5:8, :] = conv0_ref[sq]
        s_scr[...] = s0_ref[...]

    for sq in range(nseq):
        _gdn_chunk(hc_ref.at[sq], z_ref.at[sq], sm_ref.at[sq], cw_ref, al_ref, dtb_ref, gn_ref,
                   og_ref.at[sq], xbuf.at[sq], s_scr.at[sq], ci, c, l_valid)

    @pl.when(ci == n_chunks - 1)
    def _():
        sfin_ref[...] = s_scr[...]


def _gdn_chunk(hc_ref, z_ref, sm_ref, cw_ref, al_ref, dtb_ref, gn_ref, og_ref, xbuf, s_scr, ci, c, l_valid):
    xbuf[8:8 + c, :] = hc_ref[...]
    w = cw_ref[...]
    y = (xbuf[5:5 + c, :] * w[0:1] + xbuf[6:6 + c, :] * w[1:2]
         + xbuf[7:7 + c, :] * w[2:3] + xbuf[8:8 + c, :] * w[3:4])
    tail = xbuf[5 + c:8 + c, :]
    xbuf[5:8, :] = tail
    y = y * jax.nn.sigmoid(y)

    sm = sm_ref[...]
    rowid = ci * c + lax.broadcasted_iota(I32, (c, 1), 0)
    rvalid = rowid < l_valid
    beta_all = jnp.where(rvalid, jax.nn.sigmoid(sm), 0.0)
    sp_in = sm + dtb_ref[...]
    softplus = jnp.maximum(sp_in, 0.0) + jnp.log1p(jnp.exp(-jnp.abs(sp_in)))
    g_all = jnp.where(rvalid, -jnp.exp(al_ref[...]) * softplus, 0.0)

    ri = lax.broadcasted_iota(I32, (c, c), 0)
    cj = lax.broadcasted_iota(I32, (c, c), 1)
    tri = jnp.where(ri >= cj, 1.0, 0.0).astype(BF16)
    triu = jnp.where(ri <= cj, 1.0, 0.0).astype(BF16)
    g_hi, g_mid, g_lo = _split3(g_all)
    gcum = _dot(tri, g_hi) + (_dot(tri, g_mid) + _dot(tri, g_lo))
    gcum_t = _dot_tn(g_hi, triu) + (_dot_tn(g_mid, triu) + _dot_tn(g_lo, triu))
    gn = gn_ref[...]

    heads = range(GDN_HEADS)
    stack = lambda pieces: jnp.concatenate(pieces, axis=0)

    def l2n(x):
        return jnp.where(rvalid, x * lax.rsqrt(jnp.sum(x * x, axis=-1, keepdims=True) + EPS), 0.0)

    q_s = stack([l2n(y[:, h * 128:(h + 1) * 128]) * (GDN_DK ** -0.5) for h in heads])
    k_s = stack([l2n(y[:, 512 + h * 128:512 + (h + 1) * 128]) for h in heads])
    v_s = stack([jnp.where(rvalid, y[:, 1024 + h * 128:1024 + (h + 1) * 128], 0.0) for h in heads])
    beta_s = stack([beta_all[:, h:h + 1] for h in heads])
    gc_s = stack([gcum[:, 4 + h:5 + h] for h in heads])
    gl_s = stack([jnp.broadcast_to(gcum[c - 1:c, 4 + h:5 + h], (c, 1)) for h in heads])
    gct_s = jnp.concatenate([gcum_t[4 + h:5 + h, :] for h in heads], axis=1)
    r = GDN_HEADS * c
    rr = lax.broadcasted_iota(I32, (r, r), 0)
    rc = lax.broadcasted_iota(I32, (r, r), 1)
    same = (rr // c) == (rc // c)
    causal = same & (rr >= rc)
    strict = same & (rr > rc)
    decay = jnp.where(causal, jnp.exp(jnp.where(causal, gc_s - gct_s, 0.0)), 0.0)
    kb = k_s * beta_s
    ksb = k_s.astype(BF16)
    lmat = jnp.where(strict, _dot_nt(kb.astype(BF16), ksb) * decay, 0.0)
    tmat = _inv_unit_lower(lmat, c).astype(BF16)
    eg = jnp.exp(gc_s)
    u = _dot(tmat, (v_s * beta_s).astype(BF16))
    wmat = _dot(tmat, (kb * eg).astype(BF16)).astype(BF16)
    a_intra = (_dot_nt(q_s.astype(BF16), ksb) * decay).astype(BF16)
    q_dec = (q_s * eg).astype(BF16)
    k_dec = (k_s * jnp.exp(gl_s - gc_s)).astype(BF16)
    rows = lambda a, h: a[h * c:(h + 1) * c]
    sbs = [s_scr[h].astype(BF16) for h in heads]
    v_new = stack([rows(u, h) - _dot(rows(wmat, h), sbs[h]) for h in heads])
    vnb = v_new.astype(BF16)
    o_intra = _dot(a_intra, vnb)
    for h in heads:
        g_last = jnp.exp(gcum[c - 1:c, 4 + h:5 + h])
        s_scr[h] = s_scr[h] * g_last + _dot_tn(rows(k_dec, h), rows(vnb, h))
        o = _dot(rows(q_dec, h), sbs[h]) + rows(o_intra, h)
        o = o * lax.rsqrt(jnp.mean(o * o, axis=-1, keepdims=True) + EPS) * gn
        zh = z_ref[:, h * 128:(h + 1) * 128]
        og_ref[:, h * 128:(h + 1) * 128] = o * (zh * jax.nn.sigmoid(zh))


GDN_SEQS = 2


def _gdn(hc, z, sm, conv_w, al_vec, dtb_vec, gnorm, conv0, s0, c, l_valid):
    b, lp, _ = hc.shape
    nseq = GDN_SEQS if b % GDN_SEQS == 0 else 1
    assert lp % c == 0
    n_chunks = lp // c
    full = lambda a: pl.BlockSpec(a.shape, lambda i, j: (0,) * a.ndim)
    seq = lambda w: pl.BlockSpec((nseq, c, w), lambda i, j: (i, j, 0))
    state = pl.BlockSpec((nseq, GDN_HEADS, 128, 128), lambda i, j: (i, 0, 0, 0))
    kern = functools.partial(_gdn_kernel, c=c, l_valid=l_valid, n_chunks=n_chunks, nseq=nseq)
    return pl.pallas_call(
        kern,
        grid=(b // nseq, n_chunks),
        in_specs=[seq(GDN_CH), seq(512), seq(LANE), full(conv_w), full(al_vec), full(dtb_vec), full(gnorm),
                  pl.BlockSpec((nseq, 3, GDN_CH), lambda i, j: (i, 0, 0)), state],
        out_specs=[seq(512), state],
        out_shape=[jax.ShapeDtypeStruct((b, lp, 512), F32),
                   jax.ShapeDtypeStruct((b, GDN_HEADS, 128, 128), F32)],
        scratch_shapes=[pltpu.VMEM((nseq, c + 8, GDN_CH), F32), pltpu.VMEM((nseq, GDN_HEADS, 128, 128), F32)],
        compiler_params=_cparams(("parallel", "arbitrary")),
        name="gdn",
    )(hc, z, sm, conv_w, al_vec, dtb_vec, gnorm, conv0, s0)


def _head_q128(q, h):
    g = h // NSA_P
    piece = q[:, (h // 2) * LANE:(h // 2 + 1) * LANE]
    lane = lax.broadcasted_iota(I32, piece.shape, 1)
    keep = (lane >= NSA_HD) if h % 2 == 1 else (lane < NSA_HD)
    qm = jnp.where(keep, piece, 0.0)
    if h % 2 != g:
        qm = pltpu.roll(qm, NSA_HD, 1)
    return qm


def _select_scores(imp, qpos, nsb):
    score = imp[:, :nsb] + imp[:, nsb:]
    j = lax.broadcasted_iota(I32, score.shape, 1)
    cur = qpos // SEL_BLOCK
    forced = (j == 0) | (j == cur) | (j == cur - 1)
    future = j * SEL_BLOCK > qpos
    return jnp.where(future, -1.0, jnp.where(forced, FORCE_SCORE, score))


def _cmp_attention(tab_ref, tabt_ref, q, kcv, qpos, ocmp_ref):
    nc = kcv.shape[0]
    nsb = nc // 2
    kc = kcv[:, 0:128].astype(BF16)
    vc = kcv[:, 128:256].astype(BF16)
    lane = lax.broadcasted_iota(I32, (1, nc), 1)
    blk = jnp.where(lane < nsb, 2 * lane, 2 * (lane - nsb) + 1)
    dist = qpos - (blk * CMP_BLOCK + CMP_BLOCK - 1)
    valid = dist >= 0
    bucket = _rel_bucket(dist)
    scores = []
    for g in range(NSA_GROUPS):
        imp = jnp.zeros((q.shape[0], nc), F32)
        for p in range(NSA_P):
            h = g * NSA_P + p
            qm = _head_q128(q, h).astype(BF16)
            logits = _dot_nt(qm, kc) * (NSA_HD ** -0.5) + _bias_lookup(bucket, tab_ref, tabt_ref, h)
            l = jnp.where(valid, logits, NEG_INF)
            m = jnp.max(l, axis=-1, keepdims=True)
            pr = jnp.where(valid, jnp.exp(l - m), 0.0)
            pr = pr / jnp.maximum(jnp.sum(pr, axis=-1, keepdims=True), 1e-30)
            ocmp_ref[h] = _dot(pr.astype(BF16), vc)
            imp = imp + pr
        scores.append(_select_scores(imp, qpos, nsb))
    return scores


def _flash_init(m_scr, l_scr, acc_scr):
    m_scr[...] = jnp.full(m_scr.shape, NEG_INF, F32)
    l_scr[...] = jnp.zeros(l_scr.shape, F32)
    acc_scr[...] = jnp.zeros(acc_scr.shape, F32)


def _cmp_prompt_kernel(tab_ref, tabt_ref, q_ref, kv_ref, mavg_ref, ocmp_ref, sel_ref, kc_scr, *, tq, nsb):
    qi = pl.program_id(1)

    @pl.when(qi == 0)
    def _():
        kc_scr[...] = _dot_exact_lhs(mavg_ref[...], kv_ref[...]) * (1.0 / CMP_BLOCK)

    qpos = qi * tq + lax.broadcasted_iota(I32, (tq, 1), 0)
    scores = _cmp_attention(tab_ref, tabt_ref, q_ref[...], kc_scr[...], qpos, ocmp_ref)
    st = jnp.concatenate(scores, axis=1).T
    k = min(SEL_TOPK, nsb)
    sel_ref[...] = jnp.concatenate([_topk_mask(st[:nsb], k, 0), _topk_mask(st[nsb:], k, 0)], axis=0)


def _cmp_prompt(tab, tabt, qn, kvn, mavg, tq):
    b, l, _ = qn.shape
    nc = l // CMP_BLOCK
    assert l % tq == 0 and l % SEL_BLOCK == 0
    kern = functools.partial(_cmp_prompt_kernel, tq=tq, nsb=nc // 2)
    return pl.pallas_call(
        kern,
        grid=(b, l // tq),
        in_specs=[pl.BlockSpec(memory_space=pltpu.SMEM),
                  pl.BlockSpec(tabt.shape, lambda i, j: (0, 0)),
                  pl.BlockSpec((None, tq, 512), lambda i, j: (i, j, 0)),
                  pl.BlockSpec((None, l, 256), lambda i, j: (i, 0, 0)),
                  pl.BlockSpec(mavg.shape, lambda i, j: (0, 0))],
        out_specs=[pl.BlockSpec((None, NSA_HEADS, tq, LANE), lambda i, j: (i, 0, j, 0)),
                   pl.BlockSpec((None, nc, tq), lambda i, j: (i, 0, j))],
        out_shape=[jax.ShapeDtypeStruct((b, NSA_HEADS, l, LANE), F32),
                   jax.ShapeDtypeStruct((b, nc, l), F32)],
        scratch_shapes=[pltpu.VMEM((nc, 256), F32)],
        compiler_params=_cparams(("parallel", "arbitrary")),
        name="cmp_prompt",
    )(tab, tabt, qn, kvn, mavg)


def _flash_prompt_kernel(tab_ref, q_ref, k_ref, v_ref, *rest, t, windowed, nsb, n_dist):
    if windowed:
        o_ref, bias_scr, m_scr, l_scr, acc_scr = rest
    else:
        sel_ref, o_ref, bias_scr, m_scr, l_scr, acc_scr = rest
    qi = pl.program_id(1)
    lane_i = lax.broadcasted_iota(I32, (1, t), 1)
    sub_j = lax.broadcasted_iota(I32, (t, 1), 0)

    @pl.when((pl.program_id(0) == 0) & (qi == 0))
    def _():
        def build(d, carry):
            bucket = _rel_bucket(d * t + lane_i - sub_j)
            for h in range(NSA_HEADS):
                bias_scr[h, d] = _bias_from_bucket(bucket, tab_ref, h)
            return carry

        lax.fori_loop(0, n_dist, build, 0)

    q = q_ref[...]
    qts = [_head_q128(q, h).T.astype(BF16) for h in range(NSA_HEADS)]
    qpos = qi * t + lane_i
    _flash_init(m_scr, l_scr, acc_scr)
    tk = 2 * t
    sub_k = lax.broadcasted_iota(I32, (tk, 1), 0)
    k_lo = jnp.maximum(qi - WINDOW // t, 0) // 2 if windowed else 0

    def body(ki, carry):
        k0 = pl.multiple_of(ki * tk, tk)
        kt = k_ref[pl.ds(k0, tk), :].astype(BF16)
        vtt = v_ref[pl.ds(k0, tk), :].T.astype(BF16)
        dist = qpos - (k0 + sub_k)
        ok = dist >= 0
        if windowed:
            ok = ok & (dist <= WINDOW)
        d = qi - 2 * ki
        d0 = jnp.minimum(d, n_dist - 1)
        d1 = jnp.clip(d - 1, 0, n_dist - 1)
        for g in range(NSA_GROUPS):
            if windowed:
                valid = ok
            else:
                blk = g * nsb + ki * (tk // SEL_BLOCK)
                rows = [sel_ref[pl.ds(blk + r, 1), :] for r in range(tk // SEL_BLOCK)]
                selm = rows[-1]
                for r in range(tk // SEL_BLOCK - 2, -1, -1):
                    selm = jnp.where(sub_k < (r + 1) * SEL_BLOCK, rows[r], selm)
                valid = ok & (selm > 0.5)
            for p in range(NSA_P):
                h = g * NSA_P + p
                bias = jnp.concatenate([bias_scr[h, d0], bias_scr[h, d1]], axis=0)
                s = _dot(kt, qts[h]) * (NSA_HD ** -0.5) + bias
                l = jnp.where(valid, s, NEG_INF)
                m_old = m_scr[h]
                m_new = jnp.maximum(m_old, jnp.max(l, axis=0, keepdims=True))
                pr = jnp.where(valid, jnp.exp(l - m_new), 0.0)
                alpha = jnp.exp(m_old - m_new)
                l_scr[h] = alpha * l_scr[h] + jnp.sum(pr, axis=0, keepdims=True)
                acc_scr[h] = alpha * acc_scr[h] + _dot(vtt, pr.astype(BF16))
                m_scr[h] = m_new
        return carry

    lax.fori_loop(k_lo, qi // 2 + 1, body, 0)
    for h in range(NSA_HEADS):
        o_ref[h] = (acc_scr[h] / jnp.maximum(l_scr[h], 1e-30)).T


def _flash_prompt(tab, qn, kv_arr, k_blk, v_blk, sel, t, windowed):
    b, l, _ = qn.shape
    assert l % (2 * t) == 0 and t == LANE
    n_dist = WINDOW // t + 1 if windowed else l // t
    kern = functools.partial(_flash_prompt_kernel, t=t, windowed=windowed, nsb=l // SEL_BLOCK, n_dist=n_dist)
    in_specs = [pl.BlockSpec(memory_space=pltpu.SMEM),
                pl.BlockSpec((None, t, 512), lambda i, j: (i, j, 0)),
                pl.BlockSpec((None, l, LANE), lambda i, j: (i, 0, k_blk)),
                pl.BlockSpec((None, l, LANE), lambda i, j: (i, 0, v_blk))]
    args = [tab, qn, kv_arr, kv_arr]
    if not windowed:
        in_specs += [pl.BlockSpec((None, sel.shape[1], t), lambda i, j: (i, 0, j))]
        args += [sel]
    return pl.pallas_call(
        kern,
        grid=(b, l // t),
        in_specs=in_specs,
        out_specs=pl.BlockSpec((None, NSA_HEADS, t, LANE), lambda i, j: (i, 0, j, 0)),
        out_shape=jax.ShapeDtypeStruct((b, NSA_HEADS, l, LANE), F32),
        scratch_shapes=[pltpu.VMEM((NSA_HEADS, n_dist, t, t), F32),
                        pltpu.VMEM((NSA_HEADS, 1, t), F32), pltpu.VMEM((NSA_HEADS, 1, t), F32),
                        pltpu.VMEM((NSA_HEADS, LANE, t), F32)],
        compiler_params=_cparams(("arbitrary", "arbitrary")),
        name="win_prompt" if windowed else "slc_prompt",
    )(*args)


def _nsa_sample_kernel(pt_ref, tab_ref, tabt_ref, q_ref, new_ref, e_ref, cache_ref, ocmp_ref, o_ref, kv_buf, sem,
                       *, n_pages, past_len, db):
    b = pl.program_id(0)
    slot = b % 2
    nsb = past_len // SEL_BLOCK

    def page_copy(seq, buf, pg):
        return pltpu.make_async_copy(cache_ref.at[pt_ref[seq, pg]],
                                     kv_buf.at[buf, pl.ds(pg * PAGE, PAGE), :], sem.at[buf])

    @pl.when(b == 0)
    def _():
        for pg in range(n_pages):
            page_copy(0, 0, pg).start()

    @pl.when(b + 1 < db)
    def _():
        for pg in range(n_pages):
            page_copy(b + 1, 1 - slot, pg).start()

    q = q_ref[...]
    qpos = past_len + lax.broadcasted_iota(I32, (8, 1), 0)
    dist_p = qpos - lax.broadcasted_iota(I32, (1, past_len), 1)
    bucket_p = _rel_bucket(dist_p)
    dist_n = qpos - (past_len + lax.broadcasted_iota(I32, (1, PAGE), 1))
    ok_n = dist_n >= 0
    bucket_n = _rel_bucket(dist_n)
    knew = new_ref[:, 0:128].astype(BF16)
    vnew = new_ref[:, 128:256].astype(BF16)

    for pg in range(n_pages):
        page_copy(b, slot, pg).wait()

    x3 = kv_buf[slot, :, 0:256].reshape(nsb, SEL_BLOCK, 256)
    kcv = jnp.concatenate([jnp.sum(x3[:, :CMP_BLOCK], axis=1), jnp.sum(x3[:, CMP_BLOCK:], axis=1)],
                          axis=0) * (1.0 / CMP_BLOCK)
    scores = _cmp_attention(tab_ref, tabt_ref, q, kcv, qpos, ocmp_ref)
    k_sel = min(SEL_TOPK, nsb + 1) - 1

    kall = kv_buf[slot, :, 256:384].astype(BF16)
    vall = kv_buf[slot, :, 384:512].astype(BF16)
    for g in range(NSA_GROUPS):
        sel_g = _topk_mask(scores[g], k_sel, 1).astype(BF16)
        qg = jnp.concatenate([_head_q128(q, g * NSA_P + p) for p in range(NSA_P)], axis=0).astype(BF16)
        s_p = _dot_nt(qg, kall) * (NSA_HD ** -0.5)
        s_n = _dot_nt(qg, knew) * (NSA_HD ** -0.5)
        mask_p = _dot(sel_g, e_ref[...]) > 0.5
        pps, pns, dens = [], [], []
        for p in range(NSA_P):
            h = g * NSA_P + p
            rows = slice(8 * p, 8 * p + 8)
            lp = jnp.where(mask_p, s_p[rows] + _bias_lookup(bucket_p, tab_ref, tabt_ref, h), NEG_INF)
            ln = jnp.where(ok_n, s_n[rows] + _bias_lookup(bucket_n, tab_ref, tabt_ref, h), NEG_INF)
            m = jnp.maximum(jnp.max(lp, axis=-1, keepdims=True), jnp.max(ln, axis=-1, keepdims=True))
            pp = jnp.where(mask_p, jnp.exp(lp - m), 0.0)
            pn = jnp.where(ok_n, jnp.exp(ln - m), 0.0)
            dens.append(jnp.sum(pp, axis=-1, keepdims=True) + jnp.sum(pn, axis=-1, keepdims=True))
            pps.append(pp.astype(BF16))
            pns.append(pn.astype(BF16))
        o = _dot(jnp.concatenate(pps, axis=0), vall) + _dot(jnp.concatenate(pns, axis=0), vnew)
        for p in range(NSA_P):
            o_ref[g * NSA_P + p] = o[8 * p:8 * p + 8] / jnp.maximum(dens[p], 1e-30)


def _nsa_sample(page_table, tab, tabt, cache, q8, new_kv, emat, past_len):
    db, n_pages = page_table.shape
    kern = functools.partial(_nsa_sample_kernel, n_pages=n_pages, past_len=past_len, db=db)
    heads = pl.BlockSpec((None, NSA_HEADS, 8, LANE), lambda i, pt: (i, 0, 0, 0))
    grid_spec = pltpu.PrefetchScalarGridSpec(
        num_scalar_prefetch=1,
        grid=(db,),
        in_specs=[pl.BlockSpec(memory_space=pltpu.SMEM),
                  pl.BlockSpec(tabt.shape, lambda i, pt: (0, 0)),
                  pl.BlockSpec((None, 8, 512), lambda i, pt: (i, 0, 0)),
                  pl.BlockSpec((None, PAGE, 256), lambda i, pt: (i, 0, 1)),
                  pl.BlockSpec(emat.shape, lambda i, pt: (0, 0)),
                  pl.BlockSpec(memory_space=pl.ANY)],
        out_specs=[heads, heads],
        scratch_shapes=[pltpu.VMEM((2, n_pages * PAGE, 512), F32), pltpu.SemaphoreType.DMA((2,))],
    )
    return pl.pallas_call(
        kern,
        grid_spec=grid_spec,
        out_shape=[jax.ShapeDtypeStruct((db, NSA_HEADS, 8, LANE), F32)] * 2,
        compiler_params=_cparams(("arbitrary",)),
        name="nsa_sample",
    )(page_table, tab, tabt, q8, new_kv, emat, cache)


def _win_sample_kernel(tab_ref, tabt_ref, q_ref, w_ref, o_ref, *, past_len, wb, lq):
    q = q_ref[...]
    qpos = past_len + lax.broadcasted_iota(I32, (8, 1), 0)
    wseq = w_ref[...]
    n = wseq.shape[0]
    kw = wseq[:, 0:128].astype(BF16)
    vw = wseq[:, 128:256].astype(BF16)
    j = lax.broadcasted_iota(I32, (1, n), 1)
    kpos = past_len - wb + j
    dist = qpos - kpos
    valid = (dist >= 0) & (dist <= WINDOW) & (kpos >= 0) & (j < wb + lq)
    bucket = _rel_bucket(dist)
    for h in range(NSA_HEADS):
        qm = _head_q128(q, h).astype(BF16)
        s = _dot_nt(qm, kw) * (NSA_HD ** -0.5) + _bias_lookup(bucket, tab_ref, tabt_ref, h)
        l = jnp.where(valid, s, NEG_INF)
        m = jnp.max(l, axis=-1, keepdims=True)
        pr = jnp.where(valid, jnp.exp(l - m), 0.0)
        pr = pr / jnp.maximum(jnp.sum(pr, axis=-1, keepdims=True), 1e-30)
        o_ref[h] = _dot(pr.astype(BF16), vw)


def _win_sample(tab, tabt, q8, wseq, past_len, wb, lq):
    db, n, _ = wseq.shape
    kern = functools.partial(_win_sample_kernel, past_len=past_len, wb=wb, lq=lq)
    return pl.pallas_call(
        kern,
        grid=(db,),
        in_specs=[pl.BlockSpec(memory_space=pltpu.SMEM),
                  pl.BlockSpec(tabt.shape, lambda i: (0, 0)),
                  pl.BlockSpec((None, 8, 512), lambda i: (i, 0, 0)),
                  pl.BlockSpec((None, n, 256), lambda i: (i, 0, 0))],
        out_specs=pl.BlockSpec((None, NSA_HEADS, 8, LANE), lambda i: (i, 0, 0, 0)),
        out_shape=jax.ShapeDtypeStruct((db, NSA_HEADS, 8, LANE), F32),
        compiler_params=_cparams(("parallel",)),
        name="win_sample",
    )(tab, tabt, q8, wseq)


def _outproj_kernel(x_ref, og_ref, oc_ref, os_ref, ow_ref, sm_ref, wg_ref, wn_ref, n2_ref,
                    hres_ref, xn_ref):
    gates = jax.nn.sigmoid(sm_ref[...])
    acc = x_ref[...] + _dot(og_ref[...].astype(BF16), wg_ref[...])
    for h in range(NSA_HEADS):
        c = 8 + 3 * h
        on = (gates[:, c:c + 1] * oc_ref[h] + gates[:, c + 1:c + 2] * os_ref[h]
              + gates[:, c + 2:c + 3] * ow_ref[h])
        acc = acc + _dot(on.astype(BF16), wn_ref[h])
    hres_ref[...] = acc
    ms = jnp.mean(acc * acc, axis=-1, keepdims=True)
    xn_ref[...] = acc * lax.rsqrt(ms + EPS) * n2_ref[...]


def _out_proj(x, og, ocmp, oslc, owin, sm, wg, wn, norm2, tm):
    b, l, _ = x.shape
    assert l % tm == 0
    seq = lambda w: pl.BlockSpec((None, tm, w), lambda i, j: (i, j, 0))
    heads = pl.BlockSpec((None, NSA_HEADS, tm, LANE), lambda i, j: (i, 0, j, 0))
    full = lambda a: pl.BlockSpec(a.shape, lambda i, j: (0,) * a.ndim)
    return pl.pallas_call(
        _outproj_kernel,
        grid=(b, l // tm),
        in_specs=[seq(D_MODEL), seq(512), heads, heads, heads, seq(LANE), full(wg), full(wn), full(norm2)],
        out_specs=[seq(D_MODEL), seq(D_MODEL)],
        out_shape=[jax.ShapeDtypeStruct((b, l, D_MODEL), F32)] * 2,
        compiler_params=_cparams(("parallel", "parallel")),
        name="out_proj",
    )(x, og, ocmp, oslc, owin, sm, wg, wn, norm2)


def _peer_topk_kernel(x_ref, wqt_ref, keys_ref, cflat_ref, eid_ref, gate_ref, qt_scr, sv_scr, si_scr, top_scr,
                      *, tm):
    qt_scr[...] = _dot_nt(wqt_ref[...], x_ref[...].astype(BF16))
    rows = _iota_f((PEER_NKEYS, tm), 0)
    cflat = jnp.broadcast_to(cflat_ref[...], (PEER_NCAND, tm))

    def candidates(hh):
        s1 = sv_scr[hh, 0]
        s2 = sv_scr[hh, 1]
        i1 = si_scr[hh, 0] * float(PEER_NKEYS)
        i2 = si_scr[hh, 1]
        cand = [s1[0:1] + s2]
        eidc = [i1[0:1] + i2]
        for a in range(1, 8):
            cand.append(s1[a:a + 1] + s2[0:8])
            eidc.append(i1[a:a + 1] + i2[0:8])
        cand.append(s1[8:16] + s2[0:1])
        eidc.append(i1[8:16] + i2[0:1])
        return jnp.where(cflat >= 0.0, jnp.concatenate(cand, axis=0), REMOVED), jnp.concatenate(eidc, axis=0)

    def head_pair_body(hp, carry):
        for hh in range(2):
            h = 2 * hp + hh
            for c in range(2):
                off = pl.multiple_of(h * (2 * PEER_HALF) + c * PEER_HALF, PEER_HALF)
                qs = qt_scr[pl.ds(off, PEER_HALF), :].astype(BF16)
                s = _dot(keys_ref[h, c], qs)

                def round_body(r, s):
                    m = jnp.max(s, axis=0, keepdims=True)
                    idx = jnp.min(jnp.where(s == m, rows, float(PEER_NKEYS)), axis=0, keepdims=True)
                    sv_scr[hh, c, pl.ds(r, 1), :] = m
                    si_scr[hh, c, pl.ds(r, 1), :] = idx
                    return jnp.where(rows == idx, REMOVED, s)

                lax.fori_loop(0, PEER_TOPK, round_body, s)
        cands, eidcs = zip(*[candidates(hh) for hh in range(2)])

        def round2(r, cands):
            out = []
            for hh, cand in enumerate(cands):
                m = jnp.max(cand, axis=0, keepdims=True)
                f = jnp.min(jnp.where(cand == m, cflat, 1e9), axis=0, keepdims=True)
                hit = cflat == f
                top_scr[hh, pl.ds(r, 1), :] = m
                eid_ref[2 * hp + hh, pl.ds(r, 1), :] = jnp.sum(
                    jnp.where(hit, eidcs[hh], 0.0), axis=0, keepdims=True).astype(I32)
                out.append(jnp.where(hit, REMOVED, cand))
            return tuple(out)

        lax.fori_loop(0, PEER_TOPK, round2, tuple(cands))
        for hh in range(2):
            top = top_scr[hh]
            e = jnp.exp(top - jnp.max(top, axis=0, keepdims=True))
            gate_ref[2 * hp + hh] = e / jnp.sum(e, axis=0, keepdims=True)
        return carry

    lax.fori_loop(0, PEER_HEADS // 2, head_pair_body, 0)


def _peer_cflat():
    rows = [(0, b) for b in range(16)]
    for a in range(1, 8):
        rows += [(a, b) for b in range(8)]
    rows += [(a, 0) for a in range(8, 16)]
    flat = [a * 16 + b if (a + 1) * (b + 1) <= PEER_TOPK else -1 for a, b in rows]
    assert len(flat) == PEER_NCAND
    return jnp.asarray(np.array(flat, np.float32).reshape(PEER_NCAND, 1))


def _peer_topk(xn, wqt, keys, tm):
    t = xn.shape[0]
    assert t % tm == 0
    cflat = _peer_cflat()
    kern = functools.partial(_peer_topk_kernel, tm=tm)
    full = lambda a: pl.BlockSpec(a.shape, lambda i: (0,) * a.ndim)
    out_spec = pl.BlockSpec((PEER_HEADS, PEER_TOPK, tm), lambda i: (0, 0, i))
    return pl.pallas_call(
        kern,
        grid=(t // tm,),
        in_specs=[pl.BlockSpec((tm, D_MODEL), lambda i: (i, 0)), full(wqt), full(keys), full(cflat)],
        out_specs=[out_spec, out_spec],
        out_shape=[jax.ShapeDtypeStruct((PEER_HEADS, PEER_TOPK, t), I32),
                   jax.ShapeDtypeStruct((PEER_HEADS, PEER_TOPK, t), F32)],
        scratch_shapes=[pltpu.VMEM((D_MODEL, tm), F32), pltpu.VMEM((2, 2, PEER_TOPK, tm), F32),
                        pltpu.VMEM((2, 2, PEER_TOPK, tm), F32), pltpu.VMEM((2, PEER_TOPK, tm), F32)],
        compiler_params=_cparams(("parallel",)),
        name="peer_topk",
    )(xn, wqt, keys, cflat)


PEER_GROUP = 2
PEER_SLOTS = 8
PEER_CHUNKS = D_MODEL // LANE


def _peer_expert_kernel(eid_ref, x_ref, gate_ref, hres_ref, uv_ref, y_ref, *scratch, tt):
    bufs = scratch[:PEER_SLOTS]
    sem = scratch[PEER_SLOTS]
    ahead = PEER_SLOTS - PEER_GROUP

    def row_copy(e, slot, k):
        return pltpu.make_async_copy(uv_ref.at[e], bufs[slot].at[pl.ds(k * PEER_CHUNKS, PEER_CHUNKS), :],
                                     sem.at[slot])

    def issue(t, slot, part=None):
        base = t * PEER_SEL
        i, n = (0, 1) if part is None else part
        for k in range(i * PEER_SEL // n, (i + 1) * PEER_SEL // n):
            row_copy(eid_ref[base + k], slot, k).start(priority=k % 2)

    def wait(slot):
        for k in range(PEER_SEL):
            row_copy(0, slot, k).wait()

    def chunk_words(slot, c):
        return bufs[slot][pl.ds(c, PEER_SEL, stride=PEER_CHUNKS), :]

    def evaluate(ts, slots, prefetch):
        for slot in slots:
            wait(slot)
        nparts = 2 * PEER_CHUNKS

        def start_part(i):
            if prefetch:
                for t, slot in zip(ts, slots):
                    issue(t + ahead, (slot + ahead) % PEER_SLOTS, (i, nparts))

        xbs = [x_ref[pl.ds(t, 1), :].astype(BF16).astype(F32) for t in ts]
        accs = [jnp.zeros((PEER_SEL, LANE), F32) for _ in ts]
        for c in range(PEER_CHUNKS):
            start_part(c)
            for j, slot in enumerate(slots):
                uf = pltpu.bitcast(chunk_words(slot, c) & jnp.int32(-65536), F32)
                accs[j] = accs[j] + uf * xbs[j][:, c * LANE:(c + 1) * LANE]
        w2s = []
        for j, t in enumerate(ts):
            act = jnp.sum(accs[j].T, axis=0, keepdims=True)
            w = gate_ref[pl.ds(t, 1), :] * (0.5 * act * (1.0 + lax.erf(act * SQRT_HALF)))
            wb = w.astype(BF16).astype(F32)
            w2s.append(jnp.broadcast_to(wb, (LANE, PEER_SEL)).T)
        outs = [[] for _ in ts]
        for c in range(PEER_CHUNKS):
            start_part(PEER_CHUNKS + c)
            for j, slot in enumerate(slots):
                vf = pltpu.bitcast(chunk_words(slot, c) << 16, F32)
                outs[j].append(jnp.sum(vf * w2s[j], axis=0, keepdims=True))
        for j, t in enumerate(ts):
            y_ref[pl.ds(t, 1), :] = hres_ref[pl.ds(t, 1), :] + jnp.concatenate(outs[j], axis=1)

    for t0 in range(ahead):
        issue(t0, t0)
    n_main = (tt - ahead) // PEER_SLOTS * PEER_SLOTS

    def body(i, carry):
        for r in range(0, PEER_SLOTS, PEER_GROUP):
            slots = list(range(r, r + PEER_GROUP))
            evaluate([i * PEER_SLOTS + s for s in slots], slots, True)
        return carry

    lax.fori_loop(0, n_main // PEER_SLOTS, body, 0)
    for t0 in range(n_main, tt, PEER_GROUP):
        ts = list(range(t0, t0 + PEER_GROUP))
        evaluate(ts, [t % PEER_SLOTS for t in ts], t0 + ahead < tt)


def _pack_expert_rows(u, v):
    bits = lambda a: lax.bitcast_convert_type(a.astype(BF16), jnp.uint16).astype(jnp.uint32)
    words = (bits(u) << 16) | bits(v)
    return lax.bitcast_convert_type(words, I32).reshape(-1, PEER_CHUNKS, LANE)


def _peer_experts(eid_flat, xn, gate, hres, uv, tt):
    t = xn.shape[0]
    assert t % tt == 0 and tt >= PEER_SLOTS
    kern = functools.partial(_peer_expert_kernel, tt=tt)
    row = lambda w: pl.BlockSpec((tt, w), lambda i: (i, 0))
    return pl.pallas_call(
        kern,
        grid=(t // tt,),
        in_specs=[pl.BlockSpec((tt * PEER_SEL,), lambda i: (i,), memory_space=pltpu.SMEM),
                  row(D_MODEL), row(PEER_SEL), row(D_MODEL),
                  pl.BlockSpec(memory_space=pl.ANY)],
        out_specs=row(D_MODEL),
        out_shape=jax.ShapeDtypeStruct((t, D_MODEL), F32),
        scratch_shapes=[pltpu.VMEM((PEER_SEL * PEER_CHUNKS, LANE), I32) for _ in range(PEER_SLOTS)]
        + [pltpu.SemaphoreType.DMA((PEER_SLOTS,))],
        compiler_params=_cparams(("arbitrary",)),
        name="peer_experts",
    )(eid_flat, xn, gate, hres, uv)


def _prep_params(norm1, w_in, gdn_conv_w, gdn_a_log, gdn_dt_bias, gdn_norm, nsa_q_norm, nsa_k_norm,
                 rel_bias, w_o, norm2, peer_wq, peer_subkeys, peer_u, peer_v):
    w = w_in[0]
    p = {}
    p["norm1"] = norm1[0][None]
    p["wm"] = jnp.concatenate([w[:, :OFF_B], w[:, OFF_NQ:OFF_NG]], axis=1).astype(BF16)
    p["ws"] = jnp.concatenate([w[:, OFF_B:OFF_NQ], w[:, OFF_NG:], jnp.zeros((D_MODEL, LANE - 32), F32)],
                              axis=1).astype(BF16)
    li = jnp.arange(LANE)
    p["seg"] = (li[:, None] // NSA_HD == li[None, :] // NSA_HD).astype(BF16)
    p["qg"] = jnp.tile(nsa_q_norm[0], 2)[None]
    p["kg"] = jnp.tile(nsa_k_norm[0], (1, 2))
    p["conv_w"] = gdn_conv_w[0]
    p["al_vec"] = jnp.zeros((1, LANE), F32).at[0, 4:8].set(gdn_a_log[0])
    p["dtb_vec"] = jnp.zeros((1, LANE), F32).at[0, 4:8].set(gdn_dt_bias[0])
    p["gnorm"] = gdn_norm[0][None]
    p["tab"] = rel_bias
    p["tabt"] = jnp.pad(rel_bias.T, ((0, 0), (0, LANE - REL_BUCKETS)))
    wo = w_o[0]
    p["wg"] = wo[:512].astype(BF16)
    wn = jnp.zeros((NSA_HEADS, LANE, D_MODEL), F32)
    for h in range(NSA_HEADS):
        g = h // NSA_P
        wn = wn.at[h, g * NSA_HD:(g + 1) * NSA_HD].set(wo[512 + h * NSA_HD:512 + (h + 1) * NSA_HD])
    p["wn"] = wn.astype(BF16)
    p["norm2"] = norm2[0][None]
    p["wqt"] = peer_wq[0].T.astype(BF16)
    p["keys"] = peer_subkeys[0].astype(BF16)
    p["uv"] = _pack_expert_rows(peer_u[0], peer_v[0])
    return p


def _perm_avg_matrix(n_blocks, n_rows):
    half = n_blocks // 2
    r = jnp.arange(n_blocks)
    blk = jnp.where(r < half, 2 * r, 2 * (r - half) + 1)
    return (jnp.arange(n_rows)[None, :] // CMP_BLOCK == blk[:, None]).astype(BF16)


def _token_mixer_tail(p, x, og, ocmp, oslc, owin, sm, tm_out, tm_topk, tt):
    b, l, _ = x.shape
    hres, xn2 = _out_proj(x, og, ocmp, oslc, owin, sm, p["wg"], p["wn"], p["norm2"], tm_out)
    t = b * l
    xn2 = xn2.reshape(t, D_MODEL)
    eid, gate = _peer_topk(xn2, p["wqt"], p["keys"], tm_topk)
    eid_flat = eid.reshape(PEER_SEL, t).T.reshape(t * PEER_SEL)
    gate_tok = gate.reshape(PEER_SEL, t).T
    y = _peer_experts(eid_flat, xn2, gate_tok, hres.reshape(t, D_MODEL), p["uv"], tt)
    return y.reshape(b, l, D_MODEL)


def kernel(x_prompt, x_sample, cache_nsa_kv, page_table, state_win_kv, state_conv, state_gdn, norm1, w_in, gdn_conv_w, gdn_a_log, gdn_dt_bias, gdn_norm, nsa_q_norm, nsa_k_norm, rel_bias, w_o, norm2, peer_wq, peer_subkeys, peer_u, peer_v):
    assert w_in.shape[0] == 1, "single layer"
    p = _prep_params(norm1, w_in, gdn_conv_w, gdn_a_log, gdn_dt_bias, gdn_norm, nsa_q_norm, nsa_k_norm,
                     rel_bias, w_o, norm2, peer_wq, peer_subkeys, peer_u, peer_v)
    b, l, _ = x_prompt.shape
    db, lq, _ = x_sample.shape
    n_pages = page_table.shape[1]
    past_len = n_pages * PAGE
    wb = state_win_kv.shape[2]
    assert cache_nsa_kv.shape[2] == PAGE and l >= WINDOW and l >= 3 and lq >= 3
    assert lq < CMP_BLOCK and lq <= 8 and wb == WINDOW and past_len >= wb

    tp = b * l
    tm = 256 if tp % 256 == 0 else LANE
    hc, z, sm, qn, kvn, winn = _in_proj(x_prompt.reshape(tp, D_MODEL), p["norm1"], p["wm"], p["ws"],
                                        p["seg"], p["qg"], p["kg"], tm)
    hc3, z3, sm3 = hc.reshape(b, l, GDN_CH), z.reshape(b, l, 512), sm.reshape(b, l, LANE)
    qn3, kvn3, winn3 = qn.reshape(b, l, 512), kvn.reshape(b, l, 512), winn.reshape(b, l, 256)
    og, gdn_p = _gdn(hc3, z3, sm3, p["conv_w"], p["al_vec"], p["dtb_vec"], p["gnorm"],
                     jnp.zeros((b, 3, GDN_CH), F32), jnp.zeros((b, GDN_HEADS, 128, 128), F32),
                     GDN_CHUNK, l)
    nc = l // CMP_BLOCK
    ocmp, sel = _cmp_prompt(p["tab"], p["tabt"], qn3, kvn3, _perm_avg_matrix(nc, l), tm)
    oslc = _flash_prompt(p["tab"], qn3, kvn3, 2, 3, sel, LANE, False)
    owin = _flash_prompt(p["tab"], qn3, winn3, 0, 1, None, LANE, True)
    y_prompt = _token_mixer_tail(p, x_prompt, og, ocmp, oslc, owin, sm3, tm, tm, 512 if tp % 512 == 0 else 64)

    ts = db * lq
    hc_s, z_s, sm_s, qn_s, kvn_s, winn_s = _in_proj(x_sample.reshape(ts, D_MODEL), p["norm1"], p["wm"], p["ws"],
                                                    p["seg"], p["qg"], p["kg"], min(ts, 256))
    pad_rows = lambda a, n: jnp.pad(a.reshape(db, lq, a.shape[-1]), ((0, 0), (0, n - lq), (0, 0)))
    og_s, gdn_s = _gdn(pad_rows(hc_s, GDN_CHUNK), pad_rows(z_s, GDN_CHUNK), pad_rows(sm_s, GDN_CHUNK),
                       p["conv_w"], p["al_vec"], p["dtb_vec"], p["gnorm"], state_conv[0], state_gdn[0],
                       GDN_CHUNK, lq)
    cache3 = cache_nsa_kv[0].reshape(cache_nsa_kv.shape[1], PAGE, 512)
    q8 = pad_rows(qn_s, 8)
    nsb_s = past_len // SEL_BLOCK
    emat_s = (jnp.arange(nsb_s)[:, None] == jnp.arange(past_len)[None, :] // SEL_BLOCK).astype(BF16)
    ocmp_s, oslc_s = _nsa_sample(page_table, p["tab"], p["tabt"], cache3, q8, pad_rows(kvn_s, PAGE), emat_s,
                                 past_len)
    wseq = jnp.concatenate([state_win_kv[0].reshape(db, wb, 256), pad_rows(winn_s, LANE)], axis=1)
    owin_s = _win_sample(p["tab"], p["tabt"], q8, wseq, past_len, wb, lq)
    flat_heads = lambda o: o[:, :, :lq].transpose(1, 0, 2, 3).reshape(1, NSA_HEADS, ts, LANE)
    y_sample = _token_mixer_tail(p, x_sample.reshape(1, ts, D_MODEL), og_s[:, :lq].reshape(1, ts, 512),
                                 flat_heads(ocmp_s), flat_heads(oslc_s), flat_heads(owin_s),
                                 sm_s.reshape(1, ts, LANE), min(ts, 256), min(ts, 256), 64)

    kv_tail = (4, NSA_GROUPS, NSA_HD)
    return (y_prompt,
            y_sample.reshape(db, lq, D_MODEL),
            kvn.reshape((1, b, l) + kv_tail),
            winn3[:, l - WINDOW:].reshape(1, b, WINDOW, 2, NSA_GROUPS, NSA_HD),
            hc3[:, l - 3:][None],
            gdn_p[None],
            kvn_s.reshape((1, db, lq) + kv_tail),
            wseq[:, lq:lq + wb].reshape(1, db, wb, 2, NSA_GROUPS, NSA_HD),
            hc_s.reshape(db, lq, GDN_CH)[:, lq - 3:][None],
            gdn_s[None])
```

```python
import functools
import math

import numpy as np
import jax
import jax.numpy as jnp
from jax import lax
from jax.experimental import pallas as pl
from jax.experimental.pallas import tpu as pltpu

F32 = jnp.float32
BF16 = jnp.bfloat16
I32 = jnp.int32

D_MODEL = 1024
EPS = 1e-6
NEG_INF = -1e30
FORCE_SCORE = 1e4
GDN_HEADS = 4
GDN_DK = 128
GDN_CHUNK = 64
GDN_CH = 1536
NSA_HEADS = 8
NSA_GROUPS = 2
NSA_P = 4
NSA_HD = 64
CMP_BLOCK = 32
SEL_BLOCK = 64
SEL_TOPK = 16
WINDOW = 512
REL_BUCKETS = 32
PAGE = 128
PEER_HEADS = 8
PEER_NKEYS = 128
PEER_TOPK = 16
PEER_HALF = 64
PEER_SEL = PEER_HEADS * PEER_TOPK
PEER_NCAND = 80
OFF_B = 2048
OFF_NQ = 2056
OFF_NG = 3336
LANE = 128
VMEM_LIMIT = 56 * 1024 * 1024
REMOVED = -3.0e38
SQRT_HALF = 0.7071067811865476


def _cparams(sem):
    return pltpu.CompilerParams(dimension_semantics=sem, vmem_limit_bytes=VMEM_LIMIT)


def _dot(a, b):
    return jnp.dot(a, b, preferred_element_type=F32)


def _dot_nt(a, b):
    return lax.dot_general(a, b, (((1,), (1,)), ((), ())), preferred_element_type=F32)


def _dot_tn(a, b):
    return lax.dot_general(a, b, (((0,), (0,)), ((), ())), preferred_element_type=F32)


def _split2(x):
    hi = x.astype(BF16)
    lo = (x - hi.astype(F32)).astype(BF16)
    return hi, lo


def _split3(x):
    hi = x.astype(BF16)
    r = x - hi.astype(F32)
    mid = r.astype(BF16)
    lo = (r - mid.astype(F32)).astype(BF16)
    return hi, mid, lo


def _dot_exact_lhs(m01, x):
    hi, mid, lo = _split3(x)
    return _dot(m01, hi) + (_dot(m01, mid) + _dot(m01, lo))


def _mm3(a, b):
    ah, al = _split2(a)
    bh, bl = _split2(b)
    return _dot(ah, bh) + (_dot(ah, bl) + _dot(al, bh))


def _iota_f(shape, axis):
    return lax.broadcasted_iota(I32, shape, axis).astype(F32)


def _rel_bucket(dist):
    d = jnp.maximum(dist, 0)
    df = jnp.maximum(d, 1).astype(F32)
    large = 16 + (jnp.log(df / 16.0) / math.log(128.0) * 16.0).astype(I32)
    large = jnp.minimum(large, REL_BUCKETS - 1)
    return jnp.where(d < 16, d, large)


def _bias_from_bucket(bucket, tab_ref, head, lo=0, hi=REL_BUCKETS - 1):
    if isinstance(lo, int) and isinstance(hi, int):
        b = jnp.zeros(bucket.shape, F32)
        for k in range(lo, hi + 1):
            b = jnp.where(bucket == k, tab_ref[k, head], b)
        return b

    def body(k, b):
        return jnp.where(bucket == k, tab_ref[k, head], b)

    return lax.fori_loop(lo, hi + 1, body, jnp.zeros(bucket.shape, F32))


def _bias_lookup(bucket, tab_ref, tabt_ref, head):
    rows, width = bucket.shape
    if width % LANE != 0:
        return _bias_from_bucket(bucket, tab_ref, head)
    row = jnp.broadcast_to(tabt_ref[head:head + 1, :], (rows, LANE))
    pieces = [jnp.take_along_axis(row, bucket[:, c:c + LANE], axis=1) for c in range(0, width, LANE)]
    return pieces[0] if len(pieces) == 1 else jnp.concatenate(pieces, axis=1)


def _topk_mask(s, k, axis):
    n = s.shape[axis]
    ids = _iota_f(s.shape, axis)
    sel = jnp.zeros(s.shape, F32)
    for _ in range(k):
        m = jnp.max(s, axis=axis, keepdims=True)
        idx = jnp.min(jnp.where(s == m, ids, float(n)), axis=axis, keepdims=True)
        hit = ids == idx
        sel = jnp.where(hit, 1.0, sel)
        s = jnp.where(hit, REMOVED, s)
    return sel


def _seg_rmsnorm(v, gain, seg):
    sq = v * v
    hi, lo = _split2(sq)
    ssum = _dot(hi, seg) + _dot(lo, seg)
    return v * lax.rsqrt(ssum * (1.0 / NSA_HD) + EPS) * gain


def _inproj_kernel(x_ref, g1_ref, wm_ref, ws_ref, seg_ref, qg_ref, kg_ref,
                   hc_ref, z_ref, sm_ref, q_ref, kv_ref, win_ref):
    x = x_ref[...]
    ms = jnp.mean(x * x, axis=-1, keepdims=True)
    xn = (x * lax.rsqrt(ms + EPS) * g1_ref[...]).astype(BF16)
    h = _dot(xn, wm_ref[...])
    sm_ref[...] = _dot(xn, ws_ref[...])
    hc_ref[...] = h[:, :GDN_CH]
    z_ref[...] = h[:, GDN_CH:2048]
    seg = seg_ref[...]
    qg = qg_ref[...]
    kg = kg_ref[...]
    for i in range(4):
        q_ref[:, i * LANE:(i + 1) * LANE] = _seg_rmsnorm(h[:, 2048 + i * LANE:2048 + (i + 1) * LANE], qg, seg)
    kv_ref[:, 0:128] = _seg_rmsnorm(h[:, 2560:2688], kg[0:1], seg)
    kv_ref[:, 128:256] = h[:, 2688:2816]
    kv_ref[:, 256:384] = _seg_rmsnorm(h[:, 2816:2944], kg[1:2], seg)
    kv_ref[:, 384:512] = h[:, 2944:3072]
    win_ref[:, 0:128] = _seg_rmsnorm(h[:, 3072:3200], kg[2:3], seg)
    win_ref[:, 128:256] = h[:, 3200:3328]


def _in_proj(x, norm1, wm, ws, seg, qg, kg, tm):
    t = x.shape[0]
    assert t % tm == 0
    row = lambda w: pl.BlockSpec((tm, w), lambda i: (i, 0))
    full = lambda a: pl.BlockSpec(a.shape, lambda i: (0,) * a.ndim)
    widths = (GDN_CH, 512, LANE, 512, 512, 256)
    return pl.pallas_call(
        _inproj_kernel,
        grid=(t // tm,),
        in_specs=[row(D_MODEL), full(norm1), full(wm), full(ws), full(seg), full(qg), full(kg)],
        out_specs=[row(w) for w in widths],
        out_shape=[jax.ShapeDtypeStruct((t, w), F32) for w in widths],
        compiler_params=_cparams(("parallel",)),
        name="in_proj",
    )(x, norm1, wm, ws, seg, qg, kg)


def _inv_unit_lower(lmat, c):
    ri = lax.broadcasted_iota(I32, lmat.shape, 0)
    ci = lax.broadcasted_iota(I32, lmat.shape, 1)
    eye = jnp.where(ri == ci, 1.0, 0.0).astype(F32)
    n = -lmat
    p = eye + n
    m = _mm3(n, n)
    span = 2
    while True:
        mm = _mm3 if span == 2 else (lambda a, b: _dot(a.astype(BF16), b.astype(BF16)))
        p = p + mm(p, m)
        span *= 2
        if span >= c:
            break
        m = mm(m, m)
    return p


def _gdn_kernel(hc_ref, z_ref, sm_ref, cw_ref, al_ref, dtb_ref, gn_ref, conv0_ref, s0_ref,
                og_ref, sfin_ref, xbuf, s_scr, *, c, l_valid, n_chunks, nseq):
    ci = pl.program_id(1)

    @pl.when(ci == 0)
    def _():
        for sq in range(nseq):
            xbuf[sq, 0:8, :] = jnp.zeros((8, GDN_CH), F32)
            xbuf[sq, 5:8, :] = conv0_ref[sq]
        s_scr[...] = s0_ref[...]

    for sq in range(nseq):
        _gdn_chunk(hc_ref.at[sq], z_ref.at[sq], sm_ref.at[sq], cw_ref, al_ref, dtb_ref, gn_ref,
                   og_ref.at[sq], xbuf.at[sq], s_scr.at[sq], ci, c, l_valid)

    @pl.when(ci == n_chunks - 1)
    def _():
        sfin_ref[...] = s_scr[...]


def _gdn_chunk(hc_ref, z_ref, sm_ref, cw_ref, al_ref, dtb_ref, gn_ref, og_ref, xbuf, s_scr, ci, c, l_valid):
    xbuf[8:8 + c, :] = hc_ref[...]
    w = cw_ref[...]
    y = (xbuf[5:5 + c, :] * w[0:1] + xbuf[6:6 + c, :] * w[1:2]
         + xbuf[7:7 + c, :] * w[2:3] + xbuf[8:8 + c, :] * w[3:4])
    tail = xbuf[5 + c:8 + c, :]
    xbuf[5:8, :] = tail
    y = y * jax.nn.sigmoid(y)

    sm = sm_ref[...]
    rowid = ci * c + lax.broadcasted_iota(I32, (c, 1), 0)
    rvalid = rowid < l_valid
    beta_all = jnp.where(rvalid, jax.nn.sigmoid(sm), 0.0)
    sp_in = sm + dtb_ref[...]
    softplus = jnp.maximum(sp_in, 0.0) + jnp.log1p(jnp.exp(-jnp.abs(sp_in)))
    g_all = jnp.where(rvalid, -jnp.exp(al_ref[...]) * softplus, 0.0)

    ri = lax.broadcasted_iota(I32, (c, c), 0)
    cj = lax.broadcasted_iota(I32, (c, c), 1)
    tri = jnp.where(ri >= cj, 1.0, 0.0).astype(BF16)
    triu = jnp.where(ri <= cj, 1.0, 0.0).astype(BF16)
    g_hi, g_mid, g_lo = _split3(g_all)
    gcum = _dot(tri, g_hi) + (_dot(tri, g_mid) + _dot(tri, g_lo))
    gcum_t = _dot_tn(g_hi, triu) + (_dot_tn(g_mid, triu) + _dot_tn(g_lo, triu))
    gn = gn_ref[...]

    heads = range(GDN_HEADS)
    stack = lambda pieces: jnp.concatenate(pieces, axis=0)

    def l2n(x):
        return jnp.where(rvalid, x * lax.rsqrt(jnp.sum(x * x, axis=-1, keepdims=True) + EPS), 0.0)

    q_s = stack([l2n(y[:, h * 128:(h + 1) * 128]) * (GDN_DK ** -0.5) for h in heads])
    k_s = stack([l2n(y[:, 512 + h * 128:512 + (h + 1) * 128]) for h in heads])
    v_s = stack([jnp.where(rvalid, y[:, 1024 + h * 128:1024 + (h + 1) * 128], 0.0) for h in heads])
    beta_s = stack([beta_all[:, h:h + 1] for h in heads])
    gc_s = stack([gcum[:, 4 + h:5 + h] for h in heads])
    gl_s = stack([jnp.broadcast_to(gcum[c - 1:c, 4 + h:5 + h], (c, 1)) for h in heads])
    gct_s = jnp.concatenate([gcum_t[4 + h:5 + h, :] for h in heads], axis=1)
    r = GDN_HEADS * c
    rr = lax.broadcasted_iota(I32, (r, r), 0)
    rc = lax.broadcasted_iota(I32, (r, r), 1)
    same = (rr // c) == (rc // c)
    causal = same & (rr >= rc)
    strict = same & (rr > rc)
    decay = jnp.where(causal, jnp.exp(jnp.where(causal, gc_s - gct_s, 0.0)), 0.0)
    kb = k_s * beta_s
    ksb = k_s.astype(BF16)
    lmat = jnp.where(strict, _dot_nt(kb.astype(BF16), ksb) * decay, 0.0)
    tmat = _inv_unit_lower(lmat, c).astype(BF16)
    eg = jnp.exp(gc_s)
    u = _dot(tmat, (v_s * beta_s).astype(BF16))
    wmat = _dot(tmat, (kb * eg).astype(BF16)).astype(BF16)
    a_intra = (_dot_nt(q_s.astype(BF16), ksb) * decay).astype(BF16)
    q_dec = (q_s * eg).astype(BF16)
    k_dec = (k_s * jnp.exp(gl_s - gc_s)).astype(BF16)
    rows = lambda a, h: a[h * c:(h + 1) * c]
    sbs = [s_scr[h].astype(BF16) for h in heads]
    v_new = stack([rows(u, h) - _dot(rows(wmat, h), sbs[h]) for h in heads])
    vnb = v_new.astype(BF16)
    o_intra = _dot(a_intra, vnb)
    for h in heads:
        g_last = jnp.exp(gcum[c - 1:c, 4 + h:5 + h])
        s_scr[h] = s_scr[h] * g_last + _dot_tn(rows(k_dec, h), rows(vnb, h))
        o = _dot(rows(q_dec, h), sbs[h]) + rows(o_intra, h)
        o = o * lax.rsqrt(jnp.mean(o * o, axis=-1, keepdims=True) + EPS) * gn
        zh = z_ref[:, h * 128:(h + 1) * 128]
        og_ref[:, h * 128:(h + 1) * 128] = o * (zh * jax.nn.sigmoid(zh))


GDN_SEQS = 2


def _gdn(hc, z, sm, conv_w, al_vec, dtb_vec, gnorm, conv0, s0, c, l_valid):
    b, lp, _ = hc.shape
    nseq = GDN_SEQS if b % GDN_SEQS == 0 else 1
    assert lp % c == 0
    n_chunks = lp // c
    full = lambda a: pl.BlockSpec(a.shape, lambda i, j: (0,) * a.ndim)
    seq = lambda w: pl.BlockSpec((nseq, c, w), lambda i, j: (i, j, 0))
    state = pl.BlockSpec((nseq, GDN_HEADS, 128, 128), lambda i, j: (i, 0, 0, 0))
    kern = functools.partial(_gdn_kernel, c=c, l_valid=l_valid, n_chunks=n_chunks, nseq=nseq)
    return pl.pallas_call(
        kern,
        grid=(b // nseq, n_chunks),
        in_specs=[seq(GDN_CH), seq(512), seq(LANE), full(conv_w), full(al_vec), full(dtb_vec), full(gnorm),
                  pl.BlockSpec((nseq, 3, GDN_CH), lambda i, j: (i, 0, 0)), state],
        out_specs=[seq(512), state],
        out_shape=[jax.ShapeDtypeStruct((b, lp, 512), F32),
                   jax.ShapeDtypeStruct((b, GDN_HEADS, 128, 128), F32)],
        scratch_shapes=[pltpu.VMEM((nseq, c + 8, GDN_CH), F32), pltpu.VMEM((nseq, GDN_HEADS, 128, 128), F32)],
        compiler_params=_cparams(("parallel", "arbitrary")),
        name="gdn",
    )(hc, z, sm, conv_w, al_vec, dtb_vec, gnorm, conv0, s0)


def _head_q128(q, h):
    g = h // NSA_P
    piece = q[:, (h // 2) * LANE:(h // 2 + 1) * LANE]
    lane = lax.broadcasted_iota(I32, piece.shape, 1)
    keep = (lane >= NSA_HD) if h % 2 == 1 else (lane < NSA_HD)
    qm = jnp.where(keep, piece, 0.0)
    if h % 2 != g:
        qm = pltpu.roll(qm, NSA_HD, 1)
    return qm


def _select_scores(imp, qpos, nsb):
    score = imp[:, :nsb] + imp[:, nsb:]
    j = lax.broadcasted_iota(I32, score.shape, 1)
    cur = qpos // SEL_BLOCK
    forced = (j == 0) | (j == cur) | (j == cur - 1)
    future = j * SEL_BLOCK > qpos
    return jnp.where(future, -1.0, jnp.where(forced, FORCE_SCORE, score))


def _cmp_attention(tab_ref, tabt_ref, q, kcv, qpos, ocmp_ref):
    nc = kcv.shape[0]
    nsb = nc // 2
    kc = kcv[:, 0:128].astype(BF16)
    vc = kcv[:, 128:256].astype(BF16)
    lane = lax.broadcasted_iota(I32, (1, nc), 1)
    blk = jnp.where(lane < nsb, 2 * lane, 2 * (lane - nsb) + 1)
    dist = qpos - (blk * CMP_BLOCK + CMP_BLOCK - 1)
    valid = dist >= 0
    bucket = _rel_bucket(dist)
    scores = []
    for g in range(NSA_GROUPS):
        imp = jnp.zeros((q.shape[0], nc), F32)
        for p in range(NSA_P):
            h = g * NSA_P + p
            qm = _head_q128(q, h).astype(BF16)
            logits = _dot_nt(qm, kc) * (NSA_HD ** -0.5) + _bias_lookup(bucket, tab_ref, tabt_ref, h)
            l = jnp.where(valid, logits, NEG_INF)
            m = jnp.max(l, axis=-1, keepdims=True)
            pr = jnp.where(valid, jnp.exp(l - m), 0.0)
            pr = pr / jnp.maximum(jnp.sum(pr, axis=-1, keepdims=True), 1e-30)
            ocmp_ref[h] = _dot(pr.astype(BF16), vc)
            imp = imp + pr
        scores.append(_select_scores(imp, qpos, nsb))
    return scores


def _flash_init(m_scr, l_scr, acc_scr):
    m_scr[...] = jnp.full(m_scr.shape, NEG_INF, F32)
    l_scr[...] = jnp.zeros(l_scr.shape, F32)
    acc_scr[...] = jnp.zeros(acc_scr.shape, F32)


def _cmp_prompt_kernel(tab_ref, tabt_ref, q_ref, kv_ref, mavg_ref, ocmp_ref, sel_ref, kc_scr, *, tq, nsb):
    qi = pl.program_id(1)

    @pl.when(qi == 0)
    def _():
        kc_scr[...] = _dot_exact_lhs(mavg_ref[...], kv_ref[...]) * (1.0 / CMP_BLOCK)

    qpos = qi * tq + lax.broadcasted_iota(I32, (tq, 1), 0)
    scores = _cmp_attention(tab_ref, tabt_ref, q_ref[...], kc_scr[...], qpos, ocmp_ref)
    st = jnp.concatenate(scores, axis=1).T
    k = min(SEL_TOPK, nsb)
    sel_ref[...] = jnp.concatenate([_topk_mask(st[:nsb], k, 0), _topk_mask(st[nsb:], k, 0)], axis=0)


def _cmp_prompt(tab, tabt, qn, kvn, mavg, tq):
    b, l, _ = qn.shape
    nc = l // CMP_BLOCK
    assert l % tq == 0 and l % SEL_BLOCK == 0
    kern = functools.partial(_cmp_prompt_kernel, tq=tq, nsb=nc // 2)
    return pl.pallas_call(
        kern,
        grid=(b, l // tq),
        in_specs=[pl.BlockSpec(memory_space=pltpu.SMEM),
                  pl.BlockSpec(tabt.shape, lambda i, j: (0, 0)),
                  pl.BlockSpec((None, tq, 512), lambda i, j: (i, j, 0)),
                  pl.BlockSpec((None, l, 256), lambda i, j: (i, 0, 0)),
                  pl.BlockSpec(mavg.shape, lambda i, j: (0, 0))],
        out_specs=[pl.BlockSpec((None, NSA_HEADS, tq, LANE), lambda i, j: (i, 0, j, 0)),
                   pl.BlockSpec((None, nc, tq), lambda i, j: (i, 0, j))],
        out_shape=[jax.ShapeDtypeStruct((b, NSA_HEADS, l, LANE), F32),
                   jax.ShapeDtypeStruct((b, nc, l), F32)],
        scratch_shapes=[pltpu.VMEM((nc, 256), F32)],
        compiler_params=_cparams(("parallel", "arbitrary")),
        name="cmp_prompt",
    )(tab, tabt, qn, kvn, mavg)


def _flash_prompt_kernel(tab_ref, q_ref, k_ref, v_ref, *rest, t, windowed, nsb, n_dist):
    if windowed:
        o_ref, bias_scr, m_scr, l_scr, acc_scr = rest
    else:
        sel_ref, o_ref, bias_scr, m_scr, l_scr, acc_scr = rest
    qi = pl.program_id(1)
    lane_i = lax.broadcasted_iota(I32, (1, t), 1)
    sub_j = lax.broadcasted_iota(I32, (t, 1), 0)

    @pl.when((pl.program_id(0) == 0) & (qi == 0))
    def _():
        def build(d, carry):
            bucket = _rel_bucket(d * t + lane_i - sub_j)
            for h in range(NSA_HEADS):
                bias_scr[h, d] = _bias_from_bucket(bucket, tab_ref, h)
            return carry

        lax.fori_loop(0, n_dist, build, 0)

    q = q_ref[...]
    qts = [_head_q128(q, h).T.astype(BF16) for h in range(NSA_HEADS)]
    qpos = qi * t + lane_i
    _flash_init(m_scr, l_scr, acc_scr)
    tk = 2 * t
    sub_k = lax.broadcasted_iota(I32, (tk, 1), 0)
    k_lo = jnp.maximum(qi - WINDOW // t, 0) // 2 if windowed else 0

    def body(ki, carry):
        k0 = pl.multiple_of(ki * tk, tk)
        kt = k_ref[pl.ds(k0, tk), :].astype(BF16)
        vtt = v_ref[pl.ds(k0, tk), :].T.astype(BF16)
        dist = qpos - (k0 + sub_k)
        ok = dist >= 0
        if windowed:
            ok = ok & (dist <= WINDOW)
        d = qi - 2 * ki
        d0 = jnp.minimum(d, n_dist - 1)
        d1 = jnp.clip(d - 1, 0, n_dist - 1)
        for g in range(NSA_GROUPS):
            if windowed:
                valid = ok
            else:
                blk = g * nsb + ki * (tk // SEL_BLOCK)
                rows = [sel_ref[pl.ds(blk + r, 1), :] for r in range(tk // SEL_BLOCK)]
                selm = rows[-1]
                for r in range(tk // SEL_BLOCK - 2, -1, -1):
                    selm = jnp.where(sub_k < (r + 1) * SEL_BLOCK, rows[r], selm)
                valid = ok & (selm > 0.5)
            for p in range(NSA_P):
                h = g * NSA_P + p
                bias = jnp.concatenate([bias_scr[h, d0], bias_scr[h, d1]], axis=0)
                s = _dot(kt, qts[h]) * (NSA_HD ** -0.5) + bias
                l = jnp.where(valid, s, NEG_INF)
                m_old = m_scr[h]
                m_new = jnp.maximum(m_old, jnp.max(l, axis=0, keepdims=True))
                pr = jnp.exp(l - m_new)
                alpha = jnp.exp(m_old - m_new)
                l_scr[h] = alpha * l_scr[h] + jnp.sum(pr, axis=0, keepdims=True)
                acc_scr[h] = alpha * acc_scr[h] + _dot(vtt, pr.astype(BF16))
                m_scr[h] = m_new
        return carry

    lax.fori_loop(k_lo, qi // 2 + 1, body, 0)
    for h in range(NSA_HEADS):
        o_ref[h] = (acc_scr[h] / jnp.maximum(l_scr[h], 1e-30)).T


def _flash_prompt(tab, qn, kv_arr, k_blk, v_blk, sel, t, windowed):
    b, l, _ = qn.shape
    assert l % (2 * t) == 0 and t == LANE
    n_dist = WINDOW // t + 1 if windowed else l // t
    kern = functools.partial(_flash_prompt_kernel, t=t, windowed=windowed, nsb=l // SEL_BLOCK, n_dist=n_dist)
    in_specs = [pl.BlockSpec(memory_space=pltpu.SMEM),
                pl.BlockSpec((None, t, 512), lambda i, j: (i, j, 0)),
                pl.BlockSpec((None, l, LANE), lambda i, j: (i, 0, k_blk)),
                pl.BlockSpec((None, l, LANE), lambda i, j: (i, 0, v_blk))]
    args = [tab, qn, kv_arr, kv_arr]
    if not windowed:
        in_specs += [pl.BlockSpec((None, sel.shape[1], t), lambda i, j: (i, 0, j))]
        args += [sel]
    return pl.pallas_call(
        kern,
        grid=(b, l // t),
        in_specs=in_specs,
        out_specs=pl.BlockSpec((None, NSA_HEADS, t, LANE), lambda i, j: (i, 0, j, 0)),
        out_shape=jax.ShapeDtypeStruct((b, NSA_HEADS, l, LANE), F32),
        scratch_shapes=[pltpu.VMEM((NSA_HEADS, n_dist, t, t), F32),
                        pltpu.VMEM((NSA_HEADS, 1, t), F32), pltpu.VMEM((NSA_HEADS, 1, t), F32),
                        pltpu.VMEM((NSA_HEADS, LANE, t), F32)],
        compiler_params=_cparams(("arbitrary", "arbitrary")),
        name="win_prompt" if windowed else "slc_prompt",
    )(*args)


def _nsa_sample_kernel(pt_ref, tab_ref, tabt_ref, q_ref, new_ref, e_ref, cache_ref, ocmp_ref, o_ref, kv_buf, sem,
                       *, n_pages, past_len, db):
    b = pl.program_id(0)
    slot = b % 2
    nsb = past_len // SEL_BLOCK

    def page_copy(seq, buf, pg):
        return pltpu.make_async_copy(cache_ref.at[pt_ref[seq, pg]],
                                     kv_buf.at[buf, pl.ds(pg * PAGE, PAGE), :], sem.at[buf])

    @pl.when(b == 0)
    def _():
        for pg in range(n_pages):
            page_copy(0, 0, pg).start()

    @pl.when(b + 1 < db)
    def _():
        for pg in range(n_pages):
            page_copy(b + 1, 1 - slot, pg).start()

    q = q_ref[...]
    qpos = past_len + lax.broadcasted_iota(I32, (8, 1), 0)
    dist_p = qpos - lax.broadcasted_iota(I32, (1, past_len), 1)
    bucket_p = _rel_bucket(dist_p)
    dist_n = qpos - (past_len + lax.broadcasted_iota(I32, (1, PAGE), 1))
    ok_n = dist_n >= 0
    bucket_n = _rel_bucket(dist_n)
    knew = new_ref[:, 0:128].astype(BF16)
    vnew = new_ref[:, 128:256].astype(BF16)

    for pg in range(n_pages):
        page_copy(b, slot, pg).wait()

    x3 = kv_buf[slot, :, 0:256].reshape(nsb, SEL_BLOCK, 256)
    kcv = jnp.concatenate([jnp.sum(x3[:, :CMP_BLOCK], axis=1), jnp.sum(x3[:, CMP_BLOCK:], axis=1)],
                          axis=0) * (1.0 / CMP_BLOCK)
    scores = _cmp_attention(tab_ref, tabt_ref, q, kcv, qpos, ocmp_ref)
    k_sel = min(SEL_TOPK, nsb + 1) - 1

    kall = kv_buf[slot, :, 256:384].astype(BF16)
    vall = kv_buf[slot, :, 384:512].astype(BF16)
    for g in range(NSA_GROUPS):
        sel_g = _topk_mask(scores[g], k_sel, 1).astype(BF16)
        qg = jnp.concatenate([_head_q128(q, g * NSA_P + p) for p in range(NSA_P)], axis=0).astype(BF16)
        s_p = _dot_nt(qg, kall) * (NSA_HD ** -0.5)
        s_n = _dot_nt(qg, knew) * (NSA_HD ** -0.5)
        mask_p = _dot(sel_g, e_ref[...]) > 0.5
        pps, pns, dens = [], [], []
        for p in range(NSA_P):
            h = g * NSA_P + p
            rows = slice(8 * p, 8 * p + 8)
            lp = jnp.where(mask_p, s_p[rows] + _bias_lookup(bucket_p, tab_ref, tabt_ref, h), NEG_INF)
            ln = jnp.where(ok_n, s_n[rows] + _bias_lookup(bucket_n, tab_ref, tabt_ref, h), NEG_INF)
            m = jnp.maximum(jnp.max(lp, axis=-1, keepdims=True), jnp.max(ln, axis=-1, keepdims=True))
            pp = jnp.where(mask_p, jnp.exp(lp - m), 0.0)
            pn = jnp.where(ok_n, jnp.exp(ln - m), 0.0)
            dens.append(jnp.sum(pp, axis=-1, keepdims=True) + jnp.sum(pn, axis=-1, keepdims=True))
            pps.append(pp.astype(BF16))
            pns.append(pn.astype(BF16))
        o = _dot(jnp.concatenate(pps, axis=0), vall) + _dot(jnp.concatenate(pns, axis=0), vnew)
        for p in range(NSA_P):
            o_ref[g * NSA_P + p] = o[8 * p:8 * p + 8] / jnp.maximum(dens[p], 1e-30)


def _nsa_sample(page_table, tab, tabt, cache, q8, new_kv, emat, past_len):
    db, n_pages = page_table.shape
    kern = functools.partial(_nsa_sample_kernel, n_pages=n_pages, past_len=past_len, db=db)
    heads = pl.BlockSpec((None, NSA_HEADS, 8, LANE), lambda i, pt: (i, 0, 0, 0))
    grid_spec = pltpu.PrefetchScalarGridSpec(
        num_scalar_prefetch=1,
        grid=(db,),
        in_specs=[pl.BlockSpec(memory_space=pltpu.SMEM),
                  pl.BlockSpec(tabt.shape, lambda i, pt: (0, 0)),
                  pl.BlockSpec((None, 8, 512), lambda i, pt: (i, 0, 0)),
                  pl.BlockSpec((None, PAGE, 256), lambda i, pt: (i, 0, 1)),
                  pl.BlockSpec(emat.shape, lambda i, pt: (0, 0)),
                  pl.BlockSpec(memory_space=pl.ANY)],
        out_specs=[heads, heads],
        scratch_shapes=[pltpu.VMEM((2, n_pages * PAGE, 512), F32), pltpu.SemaphoreType.DMA((2,))],
    )
    return pl.pallas_call(
        kern,
        grid_spec=grid_spec,
        out_shape=[jax.ShapeDtypeStruct((db, NSA_HEADS, 8, LANE), F32)] * 2,
        compiler_params=_cparams(("arbitrary",)),
        name="nsa_sample",
    )(page_table, tab, tabt, q8, new_kv, emat, cache)


def _win_sample_kernel(tab_ref, tabt_ref, q_ref, w_ref, o_ref, *, past_len, wb, lq, nseq):
    n = w_ref.shape[1]
    qpos = past_len + lax.broadcasted_iota(I32, (8, 1), 0)
    j = lax.broadcasted_iota(I32, (1, n), 1)
    kpos = past_len - wb + j
    dist = qpos - kpos
    valid = (dist >= 0) & (dist <= WINDOW) & (kpos >= 0) & (j < wb + lq)
    biases = [_bias_lookup(_rel_bucket(dist), tab_ref, tabt_ref, h) for h in range(NSA_HEADS)]
    for sq in range(nseq):
        q = q_ref[sq]
        kw = w_ref[sq, :, 0:128].astype(BF16)
        vw = w_ref[sq, :, 128:256].astype(BF16)
        for h in range(NSA_HEADS):
            qm = _head_q128(q, h).astype(BF16)
            s = _dot_nt(qm, kw) * (NSA_HD ** -0.5) + biases[h]
            l = jnp.where(valid, s, NEG_INF)
            m = jnp.max(l, axis=-1, keepdims=True)
            pr = jnp.where(valid, jnp.exp(l - m), 0.0)
            pr = pr / jnp.maximum(jnp.sum(pr, axis=-1, keepdims=True), 1e-30)
            o_ref[sq, h] = _dot(pr.astype(BF16), vw)


def _win_sample(tab, tabt, q8, wseq, past_len, wb, lq):
    db, n, _ = wseq.shape
    nseq = 4 if db % 4 == 0 else 1
    kern = functools.partial(_win_sample_kernel, past_len=past_len, wb=wb, lq=lq, nseq=nseq)
    return pl.pallas_call(
        kern,
        grid=(db // nseq,),
        in_specs=[pl.BlockSpec(memory_space=pltpu.SMEM),
                  pl.BlockSpec(tabt.shape, lambda i: (0, 0)),
                  pl.BlockSpec((nseq, 8, 512), lambda i: (i, 0, 0)),
                  pl.BlockSpec((nseq, n, 256), lambda i: (i, 0, 0))],
        out_specs=pl.BlockSpec((nseq, NSA_HEADS, 8, LANE), lambda i: (i, 0, 0, 0)),
        out_shape=jax.ShapeDtypeStruct((db, NSA_HEADS, 8, LANE), F32),
        compiler_params=_cparams(("parallel",)),
        name="win_sample",
    )(tab, tabt, q8, wseq)


def _outproj_kernel(x_ref, og_ref, oc_ref, os_ref, ow_ref, sm_ref, wg_ref, wn_ref, n2_ref,
                    hres_ref, xn_ref):
    gates = jax.nn.sigmoid(sm_ref[...])
    acc = x_ref[...] + _dot(og_ref[...].astype(BF16), wg_ref[...])
    for h in range(NSA_HEADS):
        c = 8 + 3 * h
        on = (gates[:, c:c + 1] * oc_ref[h] + gates[:, c + 1:c + 2] * os_ref[h]
              + gates[:, c + 2:c + 3] * ow_ref[h])
        acc = acc + _dot(on.astype(BF16), wn_ref[h])
    hres_ref[...] = acc
    ms = jnp.mean(acc * acc, axis=-1, keepdims=True)
    xn_ref[...] = acc * lax.rsqrt(ms + EPS) * n2_ref[...]


def _out_proj(x, og, ocmp, oslc, owin, sm, wg, wn, norm2, tm):
    b, l, _ = x.shape
    assert l % tm == 0
    seq = lambda w: pl.BlockSpec((None, tm, w), lambda i, j: (i, j, 0))
    heads = pl.BlockSpec((None, NSA_HEADS, tm, LANE), lambda i, j: (i, 0, j, 0))
    full = lambda a: pl.BlockSpec(a.shape, lambda i, j: (0,) * a.ndim)
    return pl.pallas_call(
        _outproj_kernel,
        grid=(b, l // tm),
        in_specs=[seq(D_MODEL), seq(512), heads, heads, heads, seq(LANE), full(wg), full(wn), full(norm2)],
        out_specs=[seq(D_MODEL), seq(D_MODEL)],
        out_shape=[jax.ShapeDtypeStruct((b, l, D_MODEL), F32)] * 2,
        compiler_params=_cparams(("parallel", "parallel")),
        name="out_proj",
    )(x, og, ocmp, oslc, owin, sm, wg, wn, norm2)


def _peer_topk_kernel(x_ref, wqt_ref, keys_ref, cflat_ref, eid_ref, gate_ref, qt_scr, sv_scr, si_scr, top_scr,
                      *, tm):
    qt_scr[...] = _dot_nt(wqt_ref[...], x_ref[...].astype(BF16))
    rows = _iota_f((PEER_NKEYS, tm), 0)
    cflat = jnp.broadcast_to(cflat_ref[...], (PEER_NCAND, tm))

    def candidates(hh):
        s1 = sv_scr[hh, 0]
        s2 = sv_scr[hh, 1]
        i1 = si_scr[hh, 0] * float(PEER_NKEYS)
        i2 = si_scr[hh, 1]
        cand = [s1[0:1] + s2]
        eidc = [i1[0:1] + i2]
        for a in range(1, 8):
            cand.append(s1[a:a + 1] + s2[0:8])
            eidc.append(i1[a:a + 1] + i2[0:8])
        cand.append(s1[8:16] + s2[0:1])
        eidc.append(i1[8:16] + i2[0:1])
        return jnp.where(cflat >= 0.0, jnp.concatenate(cand, axis=0), REMOVED), jnp.concatenate(eidc, axis=0)

    def head_pair_body(hp, carry):
        for hh in range(2):
            h = 2 * hp + hh
            for c in range(2):
                off = pl.multiple_of(h * (2 * PEER_HALF) + c * PEER_HALF, PEER_HALF)
                qs = qt_scr[pl.ds(off, PEER_HALF), :].astype(BF16)
                s = _dot(keys_ref[h, c], qs)

                def round_body(r, s):
                    m = jnp.max(s, axis=0, keepdims=True)
                    idx = jnp.min(jnp.where(s == m, rows, float(PEER_NKEYS)), axis=0, keepdims=True)
                    sv_scr[hh, c, pl.ds(r, 1), :] = m
                    si_scr[hh, c, pl.ds(r, 1), :] = idx
                    return jnp.where(rows == idx, REMOVED, s)

                lax.fori_loop(0, PEER_TOPK, round_body, s)
        cands, eidcs = zip(*[candidates(hh) for hh in range(2)])

        def round2(r, cands):
            out = []
            for hh, cand in enumerate(cands):
                m = jnp.max(cand, axis=0, keepdims=True)
                f = jnp.min(jnp.where(cand == m, cflat, 1e9), axis=0, keepdims=True)
                hit = cflat == f
                top_scr[hh, pl.ds(r, 1), :] = m
                eid_ref[2 * hp + hh, pl.ds(r, 1), :] = jnp.sum(
                    jnp.where(hit, eidcs[hh], 0.0), axis=0, keepdims=True).astype(I32)
                out.append(jnp.where(hit, REMOVED, cand))
            return tuple(out)

        lax.fori_loop(0, PEER_TOPK, round2, tuple(cands))
        for hh in range(2):
            top = top_scr[hh]
            e = jnp.exp(top - jnp.max(top, axis=0, keepdims=True))
            gate_ref[2 * hp + hh] = e / jnp.sum(e, axis=0, keepdims=True)
        return carry

    lax.fori_loop(0, PEER_HEADS // 2, head_pair_body, 0)


def _peer_cflat():
    rows = [(0, b) for b in range(16)]
    for a in range(1, 8):
        rows += [(a, b) for b in range(8)]
    rows += [(a, 0) for a in range(8, 16)]
    flat = [a * 16 + b if (a + 1) * (b + 1) <= PEER_TOPK else -1 for a, b in rows]
    assert len(flat) == PEER_NCAND
    return jnp.asarray(np.array(flat, np.float32).reshape(PEER_NCAND, 1))


def _peer_topk(xn, wqt, keys, tm):
    t = xn.shape[0]
    assert t % tm == 0
    cflat = _peer_cflat()
    kern = functools.partial(_peer_topk_kernel, tm=tm)
    full = lambda a: pl.BlockSpec(a.shape, lambda i: (0,) * a.ndim)
    out_spec = pl.BlockSpec((PEER_HEADS, PEER_TOPK, tm), lambda i: (0, 0, i))
    return pl.pallas_call(
        kern,
        grid=(t // tm,),
        in_specs=[pl.BlockSpec((tm, D_MODEL), lambda i: (i, 0)), full(wqt), full(keys), full(cflat)],
        out_specs=[out_spec, out_spec],
        out_shape=[jax.ShapeDtypeStruct((PEER_HEADS, PEER_TOPK, t), I32),
                   jax.ShapeDtypeStruct((PEER_HEADS, PEER_TOPK, t), F32)],
        scratch_shapes=[pltpu.VMEM((D_MODEL, tm), F32), pltpu.VMEM((2, 2, PEER_TOPK, tm), F32),
                        pltpu.VMEM((2, 2, PEER_TOPK, tm), F32), pltpu.VMEM((2, PEER_TOPK, tm), F32)],
        compiler_params=_cparams(("parallel",)),
        name="peer_topk",
    )(xn, wqt, keys, cflat)


PEER_GROUP = 2
PEER_SLOTS = 8
PEER_CHUNKS = D_MODEL // LANE


def _peer_expert_kernel(eid_ref, x_ref, gate_ref, hres_ref, uv_ref, y_ref, *scratch, tt):
    bufs = scratch[:PEER_SLOTS]
    sem = scratch[PEER_SLOTS]
    ahead = PEER_SLOTS - PEER_GROUP

    def row_copy(e, slot, k):
        return pltpu.make_async_copy(uv_ref.at[e], bufs[slot].at[pl.ds(k * PEER_CHUNKS, PEER_CHUNKS), :],
                                     sem.at[slot])

    def issue(t, slot, part=None):
        base = t * PEER_SEL
        i, n = (0, 1) if part is None else part
        for k in range(i * PEER_SEL // n, (i + 1) * PEER_SEL // n):
            row_copy(eid_ref[base + k], slot, k).start(priority=k % 2)

    def wait(slot):
        for k in range(PEER_SEL):
            row_copy(0, slot, k).wait()

    def chunk_words(slot, c):
        return bufs[slot][pl.ds(c, PEER_SEL, stride=PEER_CHUNKS), :]

    def evaluate(ts, slots, prefetch):
        for slot in slots:
            wait(slot)
        nparts = 2 * PEER_CHUNKS

        def start_part(i):
            if prefetch:
                for t, slot in zip(ts, slots):
                    issue(t + ahead, (slot + ahead) % PEER_SLOTS, (i, nparts))

        xbs = [x_ref[pl.ds(t, 1), :].astype(BF16).astype(F32) for t in ts]
        accs = [jnp.zeros((PEER_SEL, LANE), F32) for _ in ts]
        for c in range(PEER_CHUNKS):
            start_part(c)
            for j, slot in enumerate(slots):
                uf = pltpu.bitcast(chunk_words(slot, c) & jnp.int32(-65536), F32)
                accs[j] = accs[j] + uf * xbs[j][:, c * LANE:(c + 1) * LANE]
        w2s = []
        for j, t in enumerate(ts):
            act = jnp.sum(accs[j].T, axis=0, keepdims=True)
            w = gate_ref[pl.ds(t, 1), :] * (0.5 * act * (1.0 + lax.erf(act * SQRT_HALF)))
            wb = w.astype(BF16).astype(F32)
            w2s.append(jnp.broadcast_to(wb, (LANE, PEER_SEL)).T)
        outs = [[] for _ in ts]
        for c in range(PEER_CHUNKS):
            start_part(PEER_CHUNKS + c)
            for j, slot in enumerate(slots):
                vf = pltpu.bitcast(chunk_words(slot, c) << 16, F32)
                outs[j].append(jnp.sum(vf * w2s[j], axis=0, keepdims=True))
        for j, t in enumerate(ts):
            y_ref[pl.ds(t, 1), :] = hres_ref[pl.ds(t, 1), :] + jnp.concatenate(outs[j], axis=1)

    for t0 in range(ahead):
        issue(t0, t0)
    n_main = (tt - ahead) // PEER_SLOTS * PEER_SLOTS

    def body(i, carry):
        for r in range(0, PEER_SLOTS, PEER_GROUP):
            slots = list(range(r, r + PEER_GROUP))
            evaluate([i * PEER_SLOTS + s for s in slots], slots, True)
        return carry

    lax.fori_loop(0, n_main // PEER_SLOTS, body, 0)
    for t0 in range(n_main, tt, PEER_GROUP):
        ts = list(range(t0, t0 + PEER_GROUP))
        evaluate(ts, [t % PEER_SLOTS for t in ts], t0 + ahead < tt)


def _pack_expert_rows(u, v):
    bits = lambda a: lax.bitcast_convert_type(a.astype(BF16), jnp.uint16).astype(jnp.uint32)
    words = (bits(u) << 16) | bits(v)
    return lax.bitcast_convert_type(words, I32).reshape(-1, PEER_CHUNKS, LANE)


def _peer_experts(eid_flat, xn, gate, hres, uv, tt):
    t = xn.shape[0]
    assert t % tt == 0 and tt >= PEER_SLOTS
    kern = functools.partial(_peer_expert_kernel, tt=tt)
    row = lambda w: pl.BlockSpec((tt, w), lambda i: (i, 0))
    return pl.pallas_call(
        kern,
        grid=(t // tt,),
        in_specs=[pl.BlockSpec((tt * PEER_SEL,), lambda i: (i,), memory_space=pltpu.SMEM),
                  row(D_MODEL), row(PEER_SEL), row(D_MODEL),
                  pl.BlockSpec(memory_space=pl.ANY)],
        out_specs=row(D_MODEL),
        out_shape=jax.ShapeDtypeStruct((t, D_MODEL), F32),
        scratch_shapes=[pltpu.VMEM((PEER_SEL * PEER_CHUNKS, LANE), I32) for _ in range(PEER_SLOTS)]
        + [pltpu.SemaphoreType.DMA((PEER_SLOTS,))],
        compiler_params=_cparams(("arbitrary",)),
        name="peer_experts",
    )(eid_flat, xn, gate, hres, uv)


def _prep_params(norm1, w_in, gdn_conv_w, gdn_a_log, gdn_dt_bias, gdn_norm, nsa_q_norm, nsa_k_norm,
                 rel_bias, w_o, norm2, peer_wq, peer_subkeys, peer_u, peer_v):
    w = w_in[0]
    p = {}
    p["norm1"] = norm1[0][None]
    p["wm"] = jnp.concatenate([w[:, :OFF_B], w[:, OFF_NQ:OFF_NG]], axis=1).astype(BF16)
    p["ws"] = jnp.concatenate([w[:, OFF_B:OFF_NQ], w[:, OFF_NG:], jnp.zeros((D_MODEL, LANE - 32), F32)],
                              axis=1).astype(BF16)
    li = jnp.arange(LANE)
    p["seg"] = (li[:, None] // NSA_HD == li[None, :] // NSA_HD).astype(BF16)
    p["qg"] = jnp.tile(nsa_q_norm[0], 2)[None]
    p["kg"] = jnp.tile(nsa_k_norm[0], (1, 2))
    p["conv_w"] = gdn_conv_w[0]
    p["al_vec"] = jnp.zeros((1, LANE), F32).at[0, 4:8].set(gdn_a_log[0])
    p["dtb_vec"] = jnp.zeros((1, LANE), F32).at[0, 4:8].set(gdn_dt_bias[0])
    p["gnorm"] = gdn_norm[0][None]
    p["tab"] = rel_bias
    p["tabt"] = jnp.pad(rel_bias.T, ((0, 0), (0, LANE - REL_BUCKETS)))
    wo = w_o[0]
    p["wg"] = wo[:512].astype(BF16)
    wn = jnp.zeros((NSA_HEADS, LANE, D_MODEL), F32)
    for h in range(NSA_HEADS):
        g = h // NSA_P
        wn = wn.at[h, g * NSA_HD:(g + 1) * NSA_HD].set(wo[512 + h * NSA_HD:512 + (h + 1) * NSA_HD])
    p["wn"] = wn.astype(BF16)
    p["norm2"] = norm2[0][None]
    p["wqt"] = peer_wq[0].T.astype(BF16)
    p["keys"] = peer_subkeys[0].astype(BF16)
    p["uv"] = _pack_expert_rows(peer_u[0], peer_v[0])
    return p


def _perm_avg_matrix(n_blocks, n_rows):
    half = n_blocks // 2
    r = jnp.arange(n_blocks)
    blk = jnp.where(r < half, 2 * r, 2 * (r - half) + 1)
    return (jnp.arange(n_rows)[None, :] // CMP_BLOCK == blk[:, None]).astype(BF16)


def _token_mixer_tail(p, x, og, ocmp, oslc, owin, sm, tm_out, tm_topk, tt):
    b, l, _ = x.shape
    hres, xn2 = _out_proj(x, og, ocmp, oslc, owin, sm, p["wg"], p["wn"], p["norm2"], tm_out)
    t = b * l
    xn2 = xn2.reshape(t, D_MODEL)
    eid, gate = _peer_topk(xn2, p["wqt"], p["keys"], tm_topk)
    eid_flat = eid.reshape(PEER_SEL, t).T.reshape(t * PEER_SEL)
    gate_tok = gate.reshape(PEER_SEL, t).T
    y = _peer_experts(eid_flat, xn2, gate_tok, hres.reshape(t, D_MODEL), p["uv"], tt)
    return y.reshape(b, l, D_MODEL)


def kernel(x_prompt, x_sample, cache_nsa_kv, page_table, state_win_kv, state_conv, state_gdn, norm1, w_in, gdn_conv_w, gdn_a_log, gdn_dt_bias, gdn_norm, nsa_q_norm, nsa_k_norm, rel_bias, w_o, norm2, peer_wq, peer_subkeys, peer_u, peer_v):
    assert w_in.shape[0] == 1, "single layer"
    p = _prep_params(norm1, w_in, gdn_conv_w, gdn_a_log, gdn_dt_bias, gdn_norm, nsa_q_norm, nsa_k_norm,
                     rel_bias, w_o, norm2, peer_wq, peer_subkeys, peer_u, peer_v)
    b, l, _ = x_prompt.shape
    db, lq, _ = x_sample.shape
    n_pages = page_table.shape[1]
    past_len = n_pages * PAGE
    wb = state_win_kv.shape[2]
    assert cache_nsa_kv.shape[2] == PAGE and l >= WINDOW and l >= 3 and lq >= 3
    assert lq < CMP_BLOCK and lq <= 8 and wb == WINDOW and past_len >= wb

    tp = b * l
    tm = 256 if tp % 256 == 0 else LANE
    hc, z, sm, qn, kvn, winn = _in_proj(x_prompt.reshape(tp, D_MODEL), p["norm1"], p["wm"], p["ws"],
                                        p["seg"], p["qg"], p["kg"], tm)
    hc3, z3, sm3 = hc.reshape(b, l, GDN_CH), z.reshape(b, l, 512), sm.reshape(b, l, LANE)
    qn3, kvn3, winn3 = qn.reshape(b, l, 512), kvn.reshape(b, l, 512), winn.reshape(b, l, 256)
    og, gdn_p = _gdn(hc3, z3, sm3, p["conv_w"], p["al_vec"], p["dtb_vec"], p["gnorm"],
                     jnp.zeros((b, 3, GDN_CH), F32), jnp.zeros((b, GDN_HEADS, 128, 128), F32),
                     GDN_CHUNK, l)
    nc = l // CMP_BLOCK
    ocmp, sel = _cmp_prompt(p["tab"], p["tabt"], qn3, kvn3, _perm_avg_matrix(nc, l), tm)
    oslc = _flash_prompt(p["tab"], qn3, kvn3, 2, 3, sel, LANE, False)
    owin = _flash_prompt(p["tab"], qn3, winn3, 0, 1, None, LANE, True)
    y_prompt = _token_mixer_tail(p, x_prompt, og, ocmp, oslc, owin, sm3, tm, tm, 512 if tp % 512 == 0 else 64)

    ts = db * lq
    hc_s, z_s, sm_s, qn_s, kvn_s, winn_s = _in_proj(x_sample.reshape(ts, D_MODEL), p["norm1"], p["wm"], p["ws"],
                                                    p["seg"], p["qg"], p["kg"], min(ts, 256))
    pad_rows = lambda a, n: jnp.pad(a.reshape(db, lq, a.shape[-1]), ((0, 0), (0, n - lq), (0, 0)))
    og_s, gdn_s = _gdn(pad_rows(hc_s, GDN_CHUNK), pad_rows(z_s, GDN_CHUNK), pad_rows(sm_s, GDN_CHUNK),
                       p["conv_w"], p["al_vec"], p["dtb_vec"], p["gnorm"], state_conv[0], state_gdn[0],
                       GDN_CHUNK, lq)
    cache3 = cache_nsa_kv[0].reshape(cache_nsa_kv.shape[1], PAGE, 512)
    q8 = pad_rows(qn_s, 8)
    nsb_s = past_len // SEL_BLOCK
    emat_s = (jnp.arange(nsb_s)[:, None] == jnp.arange(past_len)[None, :] // SEL_BLOCK).astype(BF16)
    ocmp_s, oslc_s = _nsa_sample(page_table, p["tab"], p["tabt"], cache3, q8, pad_rows(kvn_s, PAGE), emat_s,
                                 past_len)
    wseq = jnp.concatenate([state_win_kv[0].reshape(db, wb, 256), pad_rows(winn_s, LANE)], axis=1)
    owin_s = _win_sample(p["tab"], p["tabt"], q8, wseq, past_len, wb, lq)
    flat_heads = lambda o: o[:, :, :lq].transpose(1, 0, 2, 3).reshape(1, NSA_HEADS, ts, LANE)
    y_sample = _token_mixer_tail(p, x_sample.reshape(1, ts, D_MODEL), og_s[:, :lq].reshape(1, ts, 512),
                                 flat_heads(ocmp_s), flat_heads(oslc_s), flat_heads(owin_s),
                                 sm_s.reshape(1, ts, LANE), min(ts, 256), min(ts, 256), 64)

    kv_tail = (4, NSA_GROUPS, NSA_HD)
    return (y_prompt,
            y_sample.reshape(db, lq, D_MODEL),
            kvn.reshape((1, b, l) + kv_tail),
            winn3[:, l - WINDOW:].reshape(1, b, WINDOW, 2, NSA_GROUPS, NSA_HD),
            hc3[:, l - 3:][None],
            gdn_p[None],
            kvn_s.reshape((1, db, lq) + kv_tail),
            wseq[:, lq:lq + wb].reshape(1, db, wb, 2, NSA_GROUPS, NSA_HD),
            hc_s.reshape(db, lq, GDN_CH)[:, lq - 3:][None],
            gdn_s[None])
```

```python
import functools
import math

import numpy as np
import jax
import jax.numpy as jnp
from jax import lax
from jax.experimental import pallas as pl
from jax.experimental.pallas import tpu as pltpu

F32 = jnp.float32
BF16 = jnp.bfloat16
I32 = jnp.int32

D_MODEL = 1024
EPS = 1e-6
NEG_INF = -1e30
FORCE_SCORE = 1e4
GDN_HEADS = 4
GDN_DK = 128
GDN_CHUNK = 64
GDN_CH = 1536
NSA_HEADS = 8
NSA_GROUPS = 2
NSA_P = 4
NSA_HD = 64
CMP_BLOCK = 32
SEL_BLOCK = 64
SEL_TOPK = 16
WINDOW = 512
REL_BUCKETS = 32
PAGE = 128
PEER_HEADS = 8
PEER_NKEYS = 128
PEER_TOPK = 16
PEER_HALF = 64
PEER_SEL = PEER_HEADS * PEER_TOPK
PEER_NCAND = 80
OFF_B = 2048
OFF_NQ = 2056
OFF_NG = 3336
MAIN_Z, MAIN_Q, MAIN_KV, MAIN_WIN = 1536, 2048, 2560, 3072
LANE = 128
VMEM_LIMIT = 56 * 1024 * 1024
REMOVED = -3.0e38
SQRT_HALF = 0.7071067811865476


def _cparams(sem):
    return pltpu.CompilerParams(dimension_semantics=sem, vmem_limit_bytes=VMEM_LIMIT)


def _dot(a, b):
    return jnp.dot(a, b, preferred_element_type=F32)


def _dot_nt(a, b):
    return lax.dot_general(a, b, (((1,), (1,)), ((), ())), preferred_element_type=F32)


def _dot_tn(a, b):
    return lax.dot_general(a, b, (((0,), (0,)), ((), ())), preferred_element_type=F32)


def _split2(x):
    hi = x.astype(BF16)
    lo = (x - hi.astype(F32)).astype(BF16)
    return hi, lo


def _split3(x):
    hi = x.astype(BF16)
    r = x - hi.astype(F32)
    mid = r.astype(BF16)
    lo = (r - mid.astype(F32)).astype(BF16)
    return hi, mid, lo


def _dot_exact_lhs(m01, x):
    hi, mid, lo = _split3(x)
    return _dot(m01, hi) + (_dot(m01, mid) + _dot(m01, lo))


def _mm3(a, b):
    ah, al = _split2(a)
    bh, bl = _split2(b)
    return _dot(ah, bh) + (_dot(ah, bl) + _dot(al, bh))


def _iota_f(shape, axis):
    return lax.broadcasted_iota(I32, shape, axis).astype(F32)


def _rel_bucket(dist):
    d = jnp.maximum(dist, 0)
    df = jnp.maximum(d, 1).astype(F32)
    large = 16 + (jnp.log(df / 16.0) / math.log(128.0) * 16.0).astype(I32)
    large = jnp.minimum(large, REL_BUCKETS - 1)
    return jnp.where(d < 16, d, large)


def _bias_from_bucket(bucket, tab_ref, head):
    b = jnp.zeros(bucket.shape, F32)
    for k in range(REL_BUCKETS):
        b = jnp.where(bucket == k, tab_ref[k, head], b)
    return b


def _bias_lookup(bucket, tab_ref, tabt_ref, head):
    rows, width = bucket.shape
    if width % LANE != 0:
        return _bias_from_bucket(bucket, tab_ref, head)
    row = jnp.broadcast_to(tabt_ref[head:head + 1, :], (rows, LANE))
    pieces = [jnp.take_along_axis(row, bucket[:, c:c + LANE], axis=1) for c in range(0, width, LANE)]
    return pieces[0] if len(pieces) == 1 else jnp.concatenate(pieces, axis=1)


def _topk_mask(s, k, axis):
    n = s.shape[axis]
    ids = _iota_f(s.shape, axis)
    sel = jnp.zeros(s.shape, F32)
    for _ in range(k):
        m = jnp.max(s, axis=axis, keepdims=True)
        idx = jnp.min(jnp.where(s == m, ids, float(n)), axis=axis, keepdims=True)
        hit = ids == idx
        sel = jnp.where(hit, 1.0, sel)
        s = jnp.where(hit, REMOVED, s)
    return sel


def _seg_rmsnorm(v, gain, seg):
    sq = v * v
    hi, lo = _split2(sq)
    ssum = _dot(hi, seg) + _dot(lo, seg)
    return v * lax.rsqrt(ssum * (1.0 / NSA_HD) + EPS) * gain


def _inproj_kernel(x_ref, g1_ref, wm_ref, ws_ref, seg_ref, qg_ref, kg_ref,
                   hc_ref, z_ref, sm_ref, q_ref, kv_ref, win_ref):
    x = x_ref[...]
    ms = jnp.mean(x * x, axis=-1, keepdims=True)
    xn = (x * lax.rsqrt(ms + EPS) * g1_ref[...]).astype(BF16)
    h = _dot(xn, wm_ref[...])
    sm_ref[...] = _dot(xn, ws_ref[...])
    hc_ref[...] = h[:, :MAIN_Z]
    z_ref[...] = h[:, MAIN_Z:MAIN_Q]
    seg = seg_ref[...]
    qg = qg_ref[...]
    kg = kg_ref[...]
    piece = lambda off, i: h[:, off + i * LANE:off + (i + 1) * LANE]
    lanes = lambda i: slice(i * LANE, (i + 1) * LANE)
    for i in range(4):
        q_ref[:, lanes(i)] = _seg_rmsnorm(piece(MAIN_Q, i), qg, seg)
    for i, gain in enumerate((kg[0:1], None, kg[1:2], None)):
        v = piece(MAIN_KV, i)
        kv_ref[:, lanes(i)] = v if gain is None else _seg_rmsnorm(v, gain, seg)
    for i, gain in enumerate((kg[2:3], None)):
        v = piece(MAIN_WIN, i)
        win_ref[:, lanes(i)] = v if gain is None else _seg_rmsnorm(v, gain, seg)


def _in_proj(x, norm1, wm, ws, seg, qg, kg, tm):
    t = x.shape[0]
    assert t % tm == 0
    row = lambda w: pl.BlockSpec((tm, w), lambda i: (i, 0))
    full = lambda a: pl.BlockSpec(a.shape, lambda i: (0,) * a.ndim)
    widths = (GDN_CH, 512, LANE, 512, 512, 256)
    return pl.pallas_call(
        _inproj_kernel,
        grid=(t // tm,),
        in_specs=[row(D_MODEL), full(norm1), full(wm), full(ws), full(seg), full(qg), full(kg)],
        out_specs=[row(w) for w in widths],
        out_shape=[jax.ShapeDtypeStruct((t, w), F32) for w in widths],
        compiler_params=_cparams(("parallel",)),
        name="in_proj",
    )(x, norm1, wm, ws, seg, qg, kg)


def _inv_unit_lower(lmat, c):
    ri = lax.broadcasted_iota(I32, lmat.shape, 0)
    ci = lax.broadcasted_iota(I32, lmat.shape, 1)
    eye = jnp.where(ri == ci, 1.0, 0.0).astype(F32)
    n = -lmat
    p = eye + n
    m = _mm3(n, n)
    span = 2
    while True:
        mm = _mm3 if span == 2 else (lambda a, b: _dot(a.astype(BF16), b.astype(BF16)))
        p = p + mm(p, m)
        span *= 2
        if span >= c:
            break
        m = mm(m, m)
    return p


def _gdn_kernel(hc_ref, z_ref, sm_ref, cw_ref, al_ref, dtb_ref, gn_ref, conv0_ref, s0_ref,
                og_ref, sfin_ref, xbuf, s_scr, *, c, l_valid, n_chunks, nseq):
    ci = pl.program_id(1)

    @pl.when(ci == 0)
    def _():
        for sq in range(nseq):
            xbuf[sq, 0:8, :] = jnp.zeros((8, GDN_CH), F32)
            xbuf[sq, 5:8, :] = conv0_ref[sq]
        s_scr[...] = s0_ref[...]

    for sq in range(nseq):
        _gdn_chunk(hc_ref.at[sq], z_ref.at[sq], sm_ref.at[sq], cw_ref, al_ref, dtb_ref, gn_ref,
                   og_ref.at[sq], xbuf.at[sq], s_scr.at[sq], ci, c, l_valid)

    @pl.when(ci == n_chunks - 1)
    def _():
        sfin_ref[...] = s_scr[...]


def _gdn_chunk(hc_ref, z_ref, sm_ref, cw_ref, al_ref, dtb_ref, gn_ref, og_ref, xbuf, s_scr, ci, c, l_valid):
    xbuf[8:8 + c, :] = hc_ref[...]
    w = cw_ref[...]
    y = (xbuf[5:5 + c, :] * w[0:1] + xbuf[6:6 + c, :] * w[1:2]
         + xbuf[7:7 + c, :] * w[2:3] + xbuf[8:8 + c, :] * w[3:4])
    tail = xbuf[5 + c:8 + c, :]
    xbuf[5:8, :] = tail
    y = y * jax.nn.sigmoid(y)

    sm = sm_ref[...]
    rowid = ci * c + lax.broadcasted_iota(I32, (c, 1), 0)
    rvalid = rowid < l_valid
    beta_all = jnp.where(rvalid, jax.nn.sigmoid(sm), 0.0)
    sp_in = sm + dtb_ref[...]
    softplus = jnp.maximum(sp_in, 0.0) + jnp.log1p(jnp.exp(-jnp.abs(sp_in)))
    g_all = jnp.where(rvalid, -jnp.exp(al_ref[...]) * softplus, 0.0)

    ri = lax.broadcasted_iota(I32, (c, c), 0)
    cj = lax.broadcasted_iota(I32, (c, c), 1)
    tri = jnp.where(ri >= cj, 1.0, 0.0).astype(BF16)
    triu = jnp.where(ri <= cj, 1.0, 0.0).astype(BF16)
    g_hi, g_mid, g_lo = _split3(g_all)
    gcum = _dot(tri, g_hi) + (_dot(tri, g_mid) + _dot(tri, g_lo))
    gcum_t = _dot_tn(g_hi, triu) + (_dot_tn(g_mid, triu) + _dot_tn(g_lo, triu))
    gn = gn_ref[...]

    heads = range(GDN_HEADS)
    stack = lambda pieces: jnp.concatenate(pieces, axis=0)

    def l2n(x):
        return jnp.where(rvalid, x * lax.rsqrt(jnp.sum(x * x, axis=-1, keepdims=True) + EPS), 0.0)

    q_s = stack([l2n(y[:, h * 128:(h + 1) * 128]) * (GDN_DK ** -0.5) for h in heads])
    k_s = stack([l2n(y[:, 512 + h * 128:512 + (h + 1) * 128]) for h in heads])
    v_s = stack([jnp.where(rvalid, y[:, 1024 + h * 128:1024 + (h + 1) * 128], 0.0) for h in heads])
    beta_s = stack([beta_all[:, h:h + 1] for h in heads])
    gc_s = stack([gcum[:, 4 + h:5 + h] for h in heads])
    gl_s = stack([jnp.broadcast_to(gcum[c - 1:c, 4 + h:5 + h], (c, 1)) for h in heads])
    gct_s = jnp.concatenate([gcum_t[4 + h:5 + h, :] for h in heads], axis=1)
    r = GDN_HEADS * c
    rr = lax.broadcasted_iota(I32, (r, r), 0)
    rc = lax.broadcasted_iota(I32, (r, r), 1)
    same = (rr // c) == (rc // c)
    causal = same & (rr >= rc)
    strict = same & (rr > rc)
    decay = jnp.where(causal, jnp.exp(jnp.where(causal, gc_s - gct_s, 0.0)), 0.0)
    kb = k_s * beta_s
    ksb = k_s.astype(BF16)
    lmat = jnp.where(strict, _dot_nt(kb.astype(BF16), ksb) * decay, 0.0)
    tmat = _inv_unit_lower(lmat, c).astype(BF16)
    eg = jnp.exp(gc_s)
    u = _dot(tmat, (v_s * beta_s).astype(BF16))
    wmat = _dot(tmat, (kb * eg).astype(BF16)).astype(BF16)
    a_intra = (_dot_nt(q_s.astype(BF16), ksb) * decay).astype(BF16)
    q_dec = (q_s * eg).astype(BF16)
    k_dec = (k_s * jnp.exp(gl_s - gc_s)).astype(BF16)
    rows = lambda a, h: a[h * c:(h + 1) * c]
    sbs = [s_scr[h].astype(BF16) for h in heads]
    v_new = stack([rows(u, h) - _dot(rows(wmat, h), sbs[h]) for h in heads])
    vnb = v_new.astype(BF16)
    o_intra = _dot(a_intra, vnb)
    for h in heads:
        g_last = jnp.exp(gcum[c - 1:c, 4 + h:5 + h])
        s_scr[h] = s_scr[h] * g_last + _dot_tn(rows(k_dec, h), rows(vnb, h))
        o = _dot(rows(q_dec, h), sbs[h]) + rows(o_intra, h)
        o = o * lax.rsqrt(jnp.mean(o * o, axis=-1, keepdims=True) + EPS) * gn
        zh = z_ref[:, h * 128:(h + 1) * 128]
        og_ref[:, h * 128:(h + 1) * 128] = o * (zh * jax.nn.sigmoid(zh))


GDN_SEQS = 2


def _gdn(hc, z, sm, conv_w, al_vec, dtb_vec, gnorm, conv0, s0, c, l_valid):
    b, lp, _ = hc.shape
    nseq = GDN_SEQS if b % GDN_SEQS == 0 else 1
    assert lp % c == 0
    n_chunks = lp // c
    full = lambda a: pl.BlockSpec(a.shape, lambda i, j: (0,) * a.ndim)
    seq = lambda w: pl.BlockSpec((nseq, c, w), lambda i, j: (i, j, 0))
    state = pl.BlockSpec((nseq, GDN_HEADS, 128, 128), lambda i, j: (i, 0, 0, 0))
    kern = functools.partial(_gdn_kernel, c=c, l_valid=l_valid, n_chunks=n_chunks, nseq=nseq)
    return pl.pallas_call(
        kern,
        grid=(b // nseq, n_chunks),
        in_specs=[seq(GDN_CH), seq(512), seq(LANE), full(conv_w), full(al_vec), full(dtb_vec), full(gnorm),
                  pl.BlockSpec((nseq, 3, GDN_CH), lambda i, j: (i, 0, 0)), state],
        out_specs=[seq(512), state],
        out_shape=[jax.ShapeDtypeStruct((b, lp, 512), F32),
                   jax.ShapeDtypeStruct((b, GDN_HEADS, 128, 128), F32)],
        scratch_shapes=[pltpu.VMEM((nseq, c + 8, GDN_CH), F32), pltpu.VMEM((nseq, GDN_HEADS, 128, 128), F32)],
        compiler_params=_cparams(("parallel", "arbitrary")),
        name="gdn",
    )(hc, z, sm, conv_w, al_vec, dtb_vec, gnorm, conv0, s0)


def _head_q128(q, h):
    g = h // NSA_P
    piece = q[:, (h // 2) * LANE:(h // 2 + 1) * LANE]
    lane = lax.broadcasted_iota(I32, piece.shape, 1)
    keep = (lane >= NSA_HD) if h % 2 == 1 else (lane < NSA_HD)
    qm = jnp.where(keep, piece, 0.0)
    if h % 2 != g:
        qm = pltpu.roll(qm, NSA_HD, 1)
    return qm


def _select_scores(imp, qpos, nsb):
    score = imp[:, :nsb] + imp[:, nsb:]
    j = lax.broadcasted_iota(I32, score.shape, 1)
    cur = qpos // SEL_BLOCK
    forced = (j == 0) | (j == cur) | (j == cur - 1)
    future = j * SEL_BLOCK > qpos
    return jnp.where(future, -1.0, jnp.where(forced, FORCE_SCORE, score))


def _cmp_attention(tab_ref, tabt_ref, q, kcv, qpos, ocmp_ref):
    nc = kcv.shape[0]
    nsb = nc // 2
    kc = kcv[:, 0:128].astype(BF16)
    vc = kcv[:, 128:256].astype(BF16)
    lane = lax.broadcasted_iota(I32, (1, nc), 1)
    blk = jnp.where(lane < nsb, 2 * lane, 2 * (lane - nsb) + 1)
    dist = qpos - (blk * CMP_BLOCK + CMP_BLOCK - 1)
    valid = dist >= 0
    bucket = _rel_bucket(dist)
    scores = []
    for g in range(NSA_GROUPS):
        imp = jnp.zeros((q.shape[0], nc), F32)
        for p in range(NSA_P):
            h = g * NSA_P + p
            qm = _head_q128(q, h).astype(BF16)
            logits = _dot_nt(qm, kc) * (NSA_HD ** -0.5) + _bias_lookup(bucket, tab_ref, tabt_ref, h)
            l = jnp.where(valid, logits, NEG_INF)
            m = jnp.max(l, axis=-1, keepdims=True)
            pr = jnp.where(valid, jnp.exp(l - m), 0.0)
            pr = pr / jnp.maximum(jnp.sum(pr, axis=-1, keepdims=True), 1e-30)
            ocmp_ref[h] = _dot(pr.astype(BF16), vc)
            imp = imp + pr
        scores.append(_select_scores(imp, qpos, nsb))
    return scores


def _flash_init(m_scr, l_scr, acc_scr):
    m_scr[...] = jnp.full(m_scr.shape, NEG_INF, F32)
    l_scr[...] = jnp.zeros(l_scr.shape, F32)
    acc_scr[...] = jnp.zeros(acc_scr.shape, F32)


def _cmp_prompt_kernel(tab_ref, tabt_ref, q_ref, kv_ref, mavg_ref, ocmp_ref, sel_ref, kc_scr, *, tq, nsb):
    qi = pl.program_id(1)

    @pl.when(qi == 0)
    def _():
        kc_scr[...] = _dot_exact_lhs(mavg_ref[...], kv_ref[...]) * (1.0 / CMP_BLOCK)

    qpos = qi * tq + lax.broadcasted_iota(I32, (tq, 1), 0)
    scores = _cmp_attention(tab_ref, tabt_ref, q_ref[...], kc_scr[...], qpos, ocmp_ref)
    st = jnp.concatenate(scores, axis=1).T
    k = min(SEL_TOPK, nsb)
    sel_ref[...] = jnp.concatenate([_topk_mask(st[:nsb], k, 0), _topk_mask(st[nsb:], k, 0)], axis=0)


def _cmp_prompt(tab, tabt, qn, kvn, mavg, tq):
    b, l, _ = qn.shape
    nc = l // CMP_BLOCK
    assert l % tq == 0 and l % SEL_BLOCK == 0
    kern = functools.partial(_cmp_prompt_kernel, tq=tq, nsb=nc // 2)
    return pl.pallas_call(
        kern,
        grid=(b, l // tq),
        in_specs=[pl.BlockSpec(memory_space=pltpu.SMEM),
                  pl.BlockSpec(tabt.shape, lambda i, j: (0, 0)),
                  pl.BlockSpec((None, tq, 512), lambda i, j: (i, j, 0)),
                  pl.BlockSpec((None, l, 256), lambda i, j: (i, 0, 0)),
                  pl.BlockSpec(mavg.shape, lambda i, j: (0, 0))],
        out_specs=[pl.BlockSpec((None, NSA_HEADS, tq, LANE), lambda i, j: (i, 0, j, 0)),
                   pl.BlockSpec((None, nc, tq), lambda i, j: (i, 0, j))],
        out_shape=[jax.ShapeDtypeStruct((b, NSA_HEADS, l, LANE), F32),
                   jax.ShapeDtypeStruct((b, nc, l), F32)],
        scratch_shapes=[pltpu.VMEM((nc, 256), F32)],
        compiler_params=_cparams(("parallel", "arbitrary")),
        name="cmp_prompt",
    )(tab, tabt, qn, kvn, mavg)


def _flash_prompt_kernel(tab_ref, q_ref, k_ref, v_ref, *rest, t, windowed, nsb, n_dist):
    if windowed:
        o_ref, bias_scr, m_scr, l_scr, acc_scr = rest
    else:
        sel_ref, o_ref, bias_scr, m_scr, l_scr, acc_scr = rest
    qi = pl.program_id(1)
    lane_i = lax.broadcasted_iota(I32, (1, t), 1)
    sub_j = lax.broadcasted_iota(I32, (t, 1), 0)

    @pl.when((pl.program_id(0) == 0) & (qi == 0))
    def _():
        def build(d, carry):
            bucket = _rel_bucket(d * t + lane_i - sub_j)
            for h in range(NSA_HEADS):
                bias_scr[h, d] = _bias_from_bucket(bucket, tab_ref, h)
            return carry

        lax.fori_loop(0, n_dist, build, 0)

    q = q_ref[...]
    qts = [_head_q128(q, h).T.astype(BF16) for h in range(NSA_HEADS)]
    qpos = qi * t + lane_i
    _flash_init(m_scr, l_scr, acc_scr)
    tk = 2 * t
    sub_k = lax.broadcasted_iota(I32, (tk, 1), 0)
    k_lo = jnp.maximum(qi - WINDOW // t, 0) // 2 if windowed else 0

    def body(ki, carry):
        k0 = pl.multiple_of(ki * tk, tk)
        kt = k_ref[pl.ds(k0, tk), :].astype(BF16)
        vtt = v_ref[pl.ds(k0, tk), :].T.astype(BF16)
        dist = qpos - (k0 + sub_k)
        ok = dist >= 0
        if windowed:
            ok = ok & (dist <= WINDOW)
        d = qi - 2 * ki
        d0 = jnp.minimum(d, n_dist - 1)
        d1 = jnp.clip(d - 1, 0, n_dist - 1)
        for g in range(NSA_GROUPS):
            if windowed:
                valid = ok
            else:
                blk = g * nsb + ki * (tk // SEL_BLOCK)
                rows = [sel_ref[pl.ds(blk + r, 1), :] for r in range(tk // SEL_BLOCK)]
                selm = rows[-1]
                for r in range(tk // SEL_BLOCK - 2, -1, -1):
                    selm = jnp.where(sub_k < (r + 1) * SEL_BLOCK, rows[r], selm)
                valid = ok & (selm > 0.5)
            for p in range(NSA_P):
                h = g * NSA_P + p
                bias = jnp.concatenate([bias_scr[h, d0], bias_scr[h, d1]], axis=0)
                s = _dot(kt, qts[h]) * (NSA_HD ** -0.5) + bias
                l = jnp.where(valid, s, NEG_INF)
                m_old = m_scr[h]
                m_new = jnp.maximum(m_old, jnp.max(l, axis=0, keepdims=True))
                pr = jnp.exp(l - m_new)
                alpha = jnp.exp(m_old - m_new)
                l_scr[h] = alpha * l_scr[h] + jnp.sum(pr, axis=0, keepdims=True)
                acc_scr[h] = alpha * acc_scr[h] + _dot(vtt, pr.astype(BF16))
                m_scr[h] = m_new
        return carry

    lax.fori_loop(k_lo, qi // 2 + 1, body, 0)
    for h in range(NSA_HEADS):
        o_ref[h] = (acc_scr[h] / jnp.maximum(l_scr[h], 1e-30)).T


def _flash_prompt(tab, qn, kv_arr, k_blk, v_blk, sel, t, windowed):
    b, l, _ = qn.shape
    assert l % (2 * t) == 0 and t == LANE
    n_dist = WINDOW // t + 1 if windowed else l // t
    kern = functools.partial(_flash_prompt_kernel, t=t, windowed=windowed, nsb=l // SEL_BLOCK, n_dist=n_dist)
    in_specs = [pl.BlockSpec(memory_space=pltpu.SMEM),
                pl.BlockSpec((None, t, 512), lambda i, j: (i, j, 0)),
                pl.BlockSpec((None, l, LANE), lambda i, j: (i, 0, k_blk)),
                pl.BlockSpec((None, l, LANE), lambda i, j: (i, 0, v_blk))]
    args = [tab, qn, kv_arr, kv_arr]
    if not windowed:
        in_specs += [pl.BlockSpec((None, sel.shape[1], t), lambda i, j: (i, 0, j))]
        args += [sel]
    return pl.pallas_call(
        kern,
        grid=(b, l // t),
        in_specs=in_specs,
        out_specs=pl.BlockSpec((None, NSA_HEADS, t, LANE), lambda i, j: (i, 0, j, 0)),
        out_shape=jax.ShapeDtypeStruct((b, NSA_HEADS, l, LANE), F32),
        scratch_shapes=[pltpu.VMEM((NSA_HEADS, n_dist, t, t), F32),
                        pltpu.VMEM((NSA_HEADS, 1, t), F32), pltpu.VMEM((NSA_HEADS, 1, t), F32),
                        pltpu.VMEM((NSA_HEADS, LANE, t), F32)],
        compiler_params=_cparams(("arbitrary", "arbitrary")),
        name="win_prompt" if windowed else "slc_prompt",
    )(*args)


def _nsa_sample_kernel(pt_ref, tab_ref, tabt_ref, q_ref, new_ref, e_ref, cache_ref, ocmp_ref, o_ref, kv_buf, sem,
                       *, n_pages, past_len, db):
    b = pl.program_id(0)
    slot = b % 2
    nsb = past_len // SEL_BLOCK

    def page_copy(seq, buf, pg):
        return pltpu.make_async_copy(cache_ref.at[pt_ref[seq, pg]],
                                     kv_buf.at[buf, pl.ds(pg * PAGE, PAGE), :], sem.at[buf])

    @pl.when(b == 0)
    def _():
        for pg in range(n_pages):
            page_copy(0, 0, pg).start()

    @pl.when(b + 1 < db)
    def _():
        for pg in range(n_pages):
            page_copy(b + 1, 1 - slot, pg).start()

    q = q_ref[...]
    qpos = past_len + lax.broadcasted_iota(I32, (8, 1), 0)
    dist_p = qpos - lax.broadcasted_iota(I32, (1, past_len), 1)
    bucket_p = _rel_bucket(dist_p)
    dist_n = qpos - (past_len + lax.broadcasted_iota(I32, (1, PAGE), 1))
    ok_n = dist_n >= 0
    bucket_n = _rel_bucket(dist_n)
    knew = new_ref[:, 0:128].astype(BF16)
    vnew = new_ref[:, 128:256].astype(BF16)

    for pg in range(n_pages):
        page_copy(b, slot, pg).wait()

    x3 = kv_buf[slot, :, 0:256].reshape(nsb, SEL_BLOCK, 256)
    kcv = jnp.concatenate([jnp.sum(x3[:, :CMP_BLOCK], axis=1), jnp.sum(x3[:, CMP_BLOCK:], axis=1)],
                          axis=0) * (1.0 / CMP_BLOCK)
    scores = _cmp_attention(tab_ref, tabt_ref, q, kcv, qpos, ocmp_ref)
    k_sel = min(SEL_TOPK, nsb + 1) - 1

    kall = kv_buf[slot, :, 256:384].astype(BF16)
    vall = kv_buf[slot, :, 384:512].astype(BF16)
    for g in range(NSA_GROUPS):
        sel_g = _topk_mask(scores[g], k_sel, 1).astype(BF16)
        qg = jnp.concatenate([_head_q128(q, g * NSA_P + p) for p in range(NSA_P)], axis=0).astype(BF16)
        s_p = _dot_nt(qg, kall) * (NSA_HD ** -0.5)
        s_n = _dot_nt(qg, knew) * (NSA_HD ** -0.5)
        mask_p = _dot(sel_g, e_ref[...]) > 0.5
        pps, pns, dens = [], [], []
        for p in range(NSA_P):
            h = g * NSA_P + p
            rows = slice(8 * p, 8 * p + 8)
            lp = jnp.where(mask_p, s_p[rows] + _bias_lookup(bucket_p, tab_ref, tabt_ref, h), NEG_INF)
            ln = jnp.where(ok_n, s_n[rows] + _bias_lookup(bucket_n, tab_ref, tabt_ref, h), NEG_INF)
            m = jnp.maximum(jnp.max(lp, axis=-1, keepdims=True), jnp.max(ln, axis=-1, keepdims=True))
            pp = jnp.where(mask_p, jnp.exp(lp - m), 0.0)
            pn = jnp.where(ok_n, jnp.exp(ln - m), 0.0)
            dens.append(jnp.sum(pp, axis=-1, keepdims=True) + jnp.sum(pn, axis=-1, keepdims=True))
            pps.append(pp.astype(BF16))
            pns.append(pn.astype(BF16))
        o = _dot(jnp.concatenate(pps, axis=0), vall) + _dot(jnp.concatenate(pns, axis=0), vnew)
        for p in range(NSA_P):
            o_ref[g * NSA_P + p] = o[8 * p:8 * p + 8] / jnp.maximum(dens[p], 1e-30)


def _nsa_sample(page_table, tab, tabt, cache, q8, new_kv, emat, past_len):
    db, n_pages = page_table.shape
    kern = functools.partial(_nsa_sample_kernel, n_pages=n_pages, past_len=past_len, db=db)
    heads = pl.BlockSpec((None, NSA_HEADS, 8, LANE), lambda i, pt: (i, 0, 0, 0))
    grid_spec = pltpu.PrefetchScalarGridSpec(
        num_scalar_prefetch=1,
        grid=(db,),
        in_specs=[pl.BlockSpec(memory_space=pltpu.SMEM),
                  pl.BlockSpec(tabt.shape, lambda i, pt: (0, 0)),
                  pl.BlockSpec((None, 8, 512), lambda i, pt: (i, 0, 0)),
                  pl.BlockSpec((None, PAGE, 256), lambda i, pt: (i, 0, 1)),
                  pl.BlockSpec(emat.shape, lambda i, pt: (0, 0)),
                  pl.BlockSpec(memory_space=pl.ANY)],
        out_specs=[heads, heads],
        scratch_shapes=[pltpu.VMEM((2, n_pages * PAGE, 512), F32), pltpu.SemaphoreType.DMA((2,))],
    )
    return pl.pallas_call(
        kern,
        grid_spec=grid_spec,
        out_shape=[jax.ShapeDtypeStruct((db, NSA_HEADS, 8, LANE), F32)] * 2,
        compiler_params=_cparams(("arbitrary",)),
        name="nsa_sample",
    )(page_table, tab, tabt, q8, new_kv, emat, cache)


def _win_sample_kernel(tab_ref, tabt_ref, q_ref, w_ref, o_ref, *, past_len, wb, lq, nseq):
    n = w_ref.shape[1]
    qpos = past_len + lax.broadcasted_iota(I32, (8, 1), 0)
    j = lax.broadcasted_iota(I32, (1, n), 1)
    kpos = past_len - wb + j
    dist = qpos - kpos
    valid = (dist >= 0) & (dist <= WINDOW) & (kpos >= 0) & (j < wb + lq)
    biases = [_bias_lookup(_rel_bucket(dist), tab_ref, tabt_ref, h) for h in range(NSA_HEADS)]
    for sq in range(nseq):
        q = q_ref[sq]
        kw = w_ref[sq, :, 0:128].astype(BF16)
        vw = w_ref[sq, :, 128:256].astype(BF16)
        for h in range(NSA_HEADS):
            qm = _head_q128(q, h).astype(BF16)
            s = _dot_nt(qm, kw) * (NSA_HD ** -0.5) + biases[h]
            l = jnp.where(valid, s, NEG_INF)
            m = jnp.max(l, axis=-1, keepdims=True)
            pr = jnp.where(valid, jnp.exp(l - m), 0.0)
            pr = pr / jnp.maximum(jnp.sum(pr, axis=-1, keepdims=True), 1e-30)
            o_ref[sq, h] = _dot(pr.astype(BF16), vw)


def _win_sample(tab, tabt, q8, wseq, past_len, wb, lq):
    db, n, _ = wseq.shape
    nseq = 4 if db % 4 == 0 else 1
    kern = functools.partial(_win_sample_kernel, past_len=past_len, wb=wb, lq=lq, nseq=nseq)
    return pl.pallas_call(
        kern,
        grid=(db // nseq,),
        in_specs=[pl.BlockSpec(memory_space=pltpu.SMEM),
                  pl.BlockSpec(tabt.shape, lambda i: (0, 0)),
                  pl.BlockSpec((nseq, 8, 512), lambda i: (i, 0, 0)),
                  pl.BlockSpec((nseq, n, 256), lambda i: (i, 0, 0))],
        out_specs=pl.BlockSpec((nseq, NSA_HEADS, 8, LANE), lambda i: (i, 0, 0, 0)),
        out_shape=jax.ShapeDtypeStruct((db, NSA_HEADS, 8, LANE), F32),
        compiler_params=_cparams(("parallel",)),
        name="win_sample",
    )(tab, tabt, q8, wseq)


def _outproj_kernel(x_ref, og_ref, oc_ref, os_ref, ow_ref, sm_ref, wg_ref, wn_ref, n2_ref,
                    hres_ref, xn_ref):
    gates = jax.nn.sigmoid(sm_ref[...])
    acc = x_ref[...] + _dot(og_ref[...].astype(BF16), wg_ref[...])
    for h in range(NSA_HEADS):
        c = 8 + 3 * h
        on = (gates[:, c:c + 1] * oc_ref[h] + gates[:, c + 1:c + 2] * os_ref[h]
              + gates[:, c + 2:c + 3] * ow_ref[h])
        acc = acc + _dot(on.astype(BF16), wn_ref[h])
    hres_ref[...] = acc
    ms = jnp.mean(acc * acc, axis=-1, keepdims=True)
    xn_ref[...] = acc * lax.rsqrt(ms + EPS) * n2_ref[...]


def _out_proj(x, og, ocmp, oslc, owin, sm, wg, wn, norm2, tm):
    b, l, _ = x.shape
    assert l % tm == 0
    seq = lambda w: pl.BlockSpec((None, tm, w), lambda i, j: (i, j, 0))
    heads = pl.BlockSpec((None, NSA_HEADS, tm, LANE), lambda i, j: (i, 0, j, 0))
    full = lambda a: pl.BlockSpec(a.shape, lambda i, j: (0,) * a.ndim)
    return pl.pallas_call(
        _outproj_kernel,
        grid=(b, l // tm),
        in_specs=[seq(D_MODEL), seq(512), heads, heads, heads, seq(LANE), full(wg), full(wn), full(norm2)],
        out_specs=[seq(D_MODEL), seq(D_MODEL)],
        out_shape=[jax.ShapeDtypeStruct((b, l, D_MODEL), F32)] * 2,
        compiler_params=_cparams(("parallel", "parallel")),
        name="out_proj",
    )(x, og, ocmp, oslc, owin, sm, wg, wn, norm2)


def _peer_topk_kernel(x_ref, wqt_ref, keys_ref, cflat_ref, eid_ref, gate_ref, qt_scr, sv_scr, si_scr, top_scr,
                      *, tm):
    qt_scr[...] = _dot_nt(wqt_ref[...], x_ref[...].astype(BF16))
    rows = _iota_f((PEER_NKEYS, tm), 0)
    cflat = jnp.broadcast_to(cflat_ref[...], (PEER_NCAND, tm))

    def candidates(hh):
        s1 = sv_scr[hh, 0]
        s2 = sv_scr[hh, 1]
        i1 = si_scr[hh, 0] * float(PEER_NKEYS)
        i2 = si_scr[hh, 1]
        cand = [s1[0:1] + s2]
        eidc = [i1[0:1] + i2]
        for a in range(1, 8):
            cand.append(s1[a:a + 1] + s2[0:8])
            eidc.append(i1[a:a + 1] + i2[0:8])
        cand.append(s1[8:16] + s2[0:1])
        eidc.append(i1[8:16] + i2[0:1])
        return jnp.where(cflat >= 0.0, jnp.concatenate(cand, axis=0), REMOVED), jnp.concatenate(eidc, axis=0)

    def head_pair_body(hp, carry):
        for hh in range(2):
            h = 2 * hp + hh
            for c in range(2):
                off = pl.multiple_of(h * (2 * PEER_HALF) + c * PEER_HALF, PEER_HALF)
                qs = qt_scr[pl.ds(off, PEER_HALF), :].astype(BF16)
                s = _dot(keys_ref[h, c], qs)

                def round_body(r, s):
                    m = jnp.max(s, axis=0, keepdims=True)
                    idx = jnp.min(jnp.where(s == m, rows, float(PEER_NKEYS)), axis=0, keepdims=True)
                    sv_scr[hh, c, pl.ds(r, 1), :] = m
                    si_scr[hh, c, pl.ds(r, 1), :] = idx
                    return jnp.where(rows == idx, REMOVED, s)

                lax.fori_loop(0, PEER_TOPK, round_body, s)
        cands, eidcs = zip(*[candidates(hh) for hh in range(2)])

        def round2(r, cands):
            out = []
            for hh, cand in enumerate(cands):
                m = jnp.max(cand, axis=0, keepdims=True)
                f = jnp.min(jnp.where(cand == m, cflat, 1e9), axis=0, keepdims=True)
                hit = cflat == f
                top_scr[hh, pl.ds(r, 1), :] = m
                eid_ref[2 * hp + hh, pl.ds(r, 1), :] = jnp.sum(
                    jnp.where(hit, eidcs[hh], 0.0), axis=0, keepdims=True).astype(I32)
                out.append(jnp.where(hit, REMOVED, cand))
            return tuple(out)

        lax.fori_loop(0, PEER_TOPK, round2, tuple(cands))
        for hh in range(2):
            top = top_scr[hh]
            e = jnp.exp(top - jnp.max(top, axis=0, keepdims=True))
            gate_ref[2 * hp + hh] = e / jnp.sum(e, axis=0, keepdims=True)
        return carry

    lax.fori_loop(0, PEER_HEADS // 2, head_pair_body, 0)


def _peer_cflat():
    rows = [(0, b) for b in range(16)]
    for a in range(1, 8):
        rows += [(a, b) for b in range(8)]
    rows += [(a, 0) for a in range(8, 16)]
    flat = [a * 16 + b if (a + 1) * (b + 1) <= PEER_TOPK else -1 for a, b in rows]
    assert len(flat) == PEER_NCAND
    return jnp.asarray(np.array(flat, np.float32).reshape(PEER_NCAND, 1))


def _peer_topk(xn, wqt, keys, tm):
    t = xn.shape[0]
    assert t % tm == 0
    cflat = _peer_cflat()
    kern = functools.partial(_peer_topk_kernel, tm=tm)
    full = lambda a: pl.BlockSpec(a.shape, lambda i: (0,) * a.ndim)
    out_spec = pl.BlockSpec((PEER_HEADS, PEER_TOPK, tm), lambda i: (0, 0, i))
    return pl.pallas_call(
        kern,
        grid=(t // tm,),
        in_specs=[pl.BlockSpec((tm, D_MODEL), lambda i: (i, 0)), full(wqt), full(keys), full(cflat)],
        out_specs=[out_spec, out_spec],
        out_shape=[jax.ShapeDtypeStruct((PEER_HEADS, PEER_TOPK, t), I32),
                   jax.ShapeDtypeStruct((PEER_HEADS, PEER_TOPK, t), F32)],
        scratch_shapes=[pltpu.VMEM((D_MODEL, tm), F32), pltpu.VMEM((2, 2, PEER_TOPK, tm), F32),
                        pltpu.VMEM((2, 2, PEER_TOPK, tm), F32), pltpu.VMEM((2, PEER_TOPK, tm), F32)],
        compiler_params=_cparams(("parallel",)),
        name="peer_topk",
    )(xn, wqt, keys, cflat)


PEER_GROUP = 2
PEER_SLOTS = 8
PEER_CHUNKS = D_MODEL // LANE


def _peer_expert_kernel(eid_ref, x_ref, gate_ref, hres_ref, uv_ref, y_ref, *scratch, tt):
    bufs = scratch[:PEER_SLOTS]
    sem = scratch[PEER_SLOTS]
    ahead = PEER_SLOTS - PEER_GROUP

    def row_copy(e, slot, k):
        return pltpu.make_async_copy(uv_ref.at[e], bufs[slot].at[pl.ds(k * PEER_CHUNKS, PEER_CHUNKS), :],
                                     sem.at[slot])

    def issue(t, slot, part=None):
        base = t * PEER_SEL
        i, n = (0, 1) if part is None else part
        for k in range(i * PEER_SEL // n, (i + 1) * PEER_SEL // n):
            row_copy(eid_ref[base + k], slot, k).start(priority=k % 2)

    def wait(slot):
        for k in range(PEER_SEL):
            row_copy(0, slot, k).wait()

    def chunk_words(slot, c):
        return bufs[slot][pl.ds(c, PEER_SEL, stride=PEER_CHUNKS), :]

    def evaluate(ts, slots, prefetch):
        for slot in slots:
            wait(slot)
        nparts = 2 * PEER_CHUNKS

        def start_part(i):
            if prefetch:
                for t, slot in zip(ts, slots):
                    issue(t + ahead, (slot + ahead) % PEER_SLOTS, (i, nparts))

        xbs = [x_ref[pl.ds(t, 1), :].astype(BF16).astype(F32) for t in ts]
        accs = [jnp.zeros((PEER_SEL, LANE), F32) for _ in ts]
        for c in range(PEER_CHUNKS):
            start_part(c)
            for j, slot in enumerate(slots):
                uf = pltpu.bitcast(chunk_words(slot, c) & jnp.int32(-65536), F32)
                accs[j] = accs[j] + uf * xbs[j][:, c * LANE:(c + 1) * LANE]
        w2s = []
        for j, t in enumerate(ts):
            act = jnp.sum(accs[j].T, axis=0, keepdims=True)
            w = gate_ref[pl.ds(t, 1), :] * (0.5 * act * (1.0 + lax.erf(act * SQRT_HALF)))
            wb = w.astype(BF16).astype(F32)
            w2s.append(jnp.broadcast_to(wb, (LANE, PEER_SEL)).T)
        outs = [[] for _ in ts]
        for c in range(PEER_CHUNKS):
            start_part(PEER_CHUNKS + c)
            for j, slot in enumerate(slots):
                vf = pltpu.bitcast(chunk_words(slot, c) << 16, F32)
                outs[j].append(jnp.sum(vf * w2s[j], axis=0, keepdims=True))
        for j, t in enumerate(ts):
            y_ref[pl.ds(t, 1), :] = hres_ref[pl.ds(t, 1), :] + jnp.concatenate(outs[j], axis=1)

    for t0 in range(ahead):
        issue(t0, t0)
    n_main = (tt - ahead) // PEER_SLOTS * PEER_SLOTS

    def body(i, carry):
        for r in range(0, PEER_SLOTS, PEER_GROUP):
            slots = list(range(r, r + PEER_GROUP))
            evaluate([i * PEER_SLOTS + s for s in slots], slots, True)
        return carry

    lax.fori_loop(0, n_main // PEER_SLOTS, body, 0)
    for t0 in range(n_main, tt, PEER_GROUP):
        ts = list(range(t0, t0 + PEER_GROUP))
        evaluate(ts, [t % PEER_SLOTS for t in ts], t0 + ahead < tt)


def _pack_expert_rows(u, v):
    bits = lambda a: lax.bitcast_convert_type(a.astype(BF16), jnp.uint16).astype(jnp.uint32)
    words = (bits(u) << 16) | bits(v)
    return lax.bitcast_convert_type(words, I32).reshape(-1, PEER_CHUNKS, LANE)


def _peer_experts(eid_flat, xn, gate, hres, uv, tt):
    t = xn.shape[0]
    assert t % tt == 0 and tt >= PEER_SLOTS
    kern = functools.partial(_peer_expert_kernel, tt=tt)
    row = lambda w: pl.BlockSpec((tt, w), lambda i: (i, 0))
    return pl.pallas_call(
        kern,
        grid=(t // tt,),
        in_specs=[pl.BlockSpec((tt * PEER_SEL,), lambda i: (i,), memory_space=pltpu.SMEM),
                  row(D_MODEL), row(PEER_SEL), row(D_MODEL),
                  pl.BlockSpec(memory_space=pl.ANY)],
        out_specs=row(D_MODEL),
        out_shape=jax.ShapeDtypeStruct((t, D_MODEL), F32),
        scratch_shapes=[pltpu.VMEM((PEER_SEL * PEER_CHUNKS, LANE), I32) for _ in range(PEER_SLOTS)]
        + [pltpu.SemaphoreType.DMA((PEER_SLOTS,))],
        compiler_params=_cparams(("arbitrary",)),
        name="peer_experts",
    )(eid_flat, xn, gate, hres, uv)


def _prep_params(norm1, w_in, gdn_conv_w, gdn_a_log, gdn_dt_bias, gdn_norm, nsa_q_norm, nsa_k_norm,
                 rel_bias, w_o, norm2, peer_wq, peer_subkeys, peer_u, peer_v):
    w = w_in[0]
    p = {}
    p["norm1"] = norm1[0][None]
    p["wm"] = jnp.concatenate([w[:, :OFF_B], w[:, OFF_NQ:OFF_NG]], axis=1).astype(BF16)
    p["ws"] = jnp.concatenate([w[:, OFF_B:OFF_NQ], w[:, OFF_NG:], jnp.zeros((D_MODEL, LANE - 32), F32)],
                              axis=1).astype(BF16)
    li = jnp.arange(LANE)
    p["seg"] = (li[:, None] // NSA_HD == li[None, :] // NSA_HD).astype(BF16)
    p["qg"] = jnp.tile(nsa_q_norm[0], 2)[None]
    p["kg"] = jnp.tile(nsa_k_norm[0], (1, 2))
    p["conv_w"] = gdn_conv_w[0]
    p["al_vec"] = jnp.zeros((1, LANE), F32).at[0, 4:8].set(gdn_a_log[0])
    p["dtb_vec"] = jnp.zeros((1, LANE), F32).at[0, 4:8].set(gdn_dt_bias[0])
    p["gnorm"] = gdn_norm[0][None]
    p["tab"] = rel_bias
    p["tabt"] = jnp.pad(rel_bias.T, ((0, 0), (0, LANE - REL_BUCKETS)))
    wo = w_o[0]
    p["wg"] = wo[:512].astype(BF16)
    wn = jnp.zeros((NSA_HEADS, LANE, D_MODEL), F32)
    for h in range(NSA_HEADS):
        g = h // NSA_P
        wn = wn.at[h, g * NSA_HD:(g + 1) * NSA_HD].set(wo[512 + h * NSA_HD:512 + (h + 1) * NSA_HD])
    p["wn"] = wn.astype(BF16)
    p["norm2"] = norm2[0][None]
    p["wqt"] = peer_wq[0].T.astype(BF16)
    p["keys"] = peer_subkeys[0].astype(BF16)
    p["uv"] = _pack_expert_rows(peer_u[0], peer_v[0])
    return p


def _perm_avg_matrix(n_blocks, n_rows):
    half = n_blocks // 2
    r = jnp.arange(n_blocks)
    blk = jnp.where(r < half, 2 * r, 2 * (r - half) + 1)
    return (jnp.arange(n_rows)[None, :] // CMP_BLOCK == blk[:, None]).astype(BF16)


def _token_mixer_tail(p, x, og, ocmp, oslc, owin, sm, tm_out, tm_topk, tt):
    b, l, _ = x.shape
    hres, xn2 = _out_proj(x, og, ocmp, oslc, owin, sm, p["wg"], p["wn"], p["norm2"], tm_out)
    t = b * l
    xn2 = xn2.reshape(t, D_MODEL)
    eid, gate = _peer_topk(xn2, p["wqt"], p["keys"], tm_topk)
    eid_flat = eid.reshape(PEER_SEL, t).T.reshape(t * PEER_SEL)
    gate_tok = gate.reshape(PEER_SEL, t).T
    y = _peer_experts(eid_flat, xn2, gate_tok, hres.reshape(t, D_MODEL), p["uv"], tt)
    return y.reshape(b, l, D_MODEL)


def kernel(x_prompt, x_sample, cache_nsa_kv, page_table, state_win_kv, state_conv, state_gdn, norm1, w_in, gdn_conv_w, gdn_a_log, gdn_dt_bias, gdn_norm, nsa_q_norm, nsa_k_norm, rel_bias, w_o, norm2, peer_wq, peer_subkeys, peer_u, peer_v):
    assert w_in.shape[0] == 1, "single layer"
    p = _prep_params(norm1, w_in, gdn_conv_w, gdn_a_log, gdn_dt_bias, gdn_norm, nsa_q_norm, nsa_k_norm,
                     rel_bias, w_o, norm2, peer_wq, peer_subkeys, peer_u, peer_v)
    b, l, _ = x_prompt.shape
    db, lq, _ = x_sample.shape
    n_pages = page_table.shape[1]
    past_len = n_pages * PAGE
    wb = state_win_kv.shape[2]
    assert cache_nsa_kv.shape[2] == PAGE and l >= WINDOW and l >= 3 and lq >= 3
    assert lq < CMP_BLOCK and lq <= 8 and wb == WINDOW and past_len >= wb

    tp = b * l
    tm = 256 if tp % 256 == 0 else LANE
    hc, z, sm, qn, kvn, winn = _in_proj(x_prompt.reshape(tp, D_MODEL), p["norm1"], p["wm"], p["ws"],
                                        p["seg"], p["qg"], p["kg"], tm)
    hc3, z3, sm3 = hc.reshape(b, l, GDN_CH), z.reshape(b, l, 512), sm.reshape(b, l, LANE)
    qn3, kvn3, winn3 = qn.reshape(b, l, 512), kvn.reshape(b, l, 512), winn.reshape(b, l, 256)
    og, gdn_p = _gdn(hc3, z3, sm3, p["conv_w"], p["al_vec"], p["dtb_vec"], p["gnorm"],
                     jnp.zeros((b, 3, GDN_CH), F32), jnp.zeros((b, GDN_HEADS, 128, 128), F32),
                     GDN_CHUNK, l)
    nc = l // CMP_BLOCK
    ocmp, sel = _cmp_prompt(p["tab"], p["tabt"], qn3, kvn3, _perm_avg_matrix(nc, l), tm)
    oslc = _flash_prompt(p["tab"], qn3, kvn3, 2, 3, sel, LANE, False)
    owin = _flash_prompt(p["tab"], qn3, winn3, 0, 1, None, LANE, True)
    y_prompt = _token_mixer_tail(p, x_prompt, og, ocmp, oslc, owin, sm3, tm, tm, 512 if tp % 512 == 0 else 64)

    ts = db * lq
    hc_s, z_s, sm_s, qn_s, kvn_s, winn_s = _in_proj(x_sample.reshape(ts, D_MODEL), p["norm1"], p["wm"], p["ws"],
                                                    p["seg"], p["qg"], p["kg"], min(ts, 256))
    pad_rows = lambda a, n: jnp.pad(a.reshape(db, lq, a.shape[-1]), ((0, 0), (0, n - lq), (0, 0)))
    og_s, gdn_s = _gdn(pad_rows(hc_s, GDN_CHUNK), pad_rows(z_s, GDN_CHUNK), pad_rows(sm_s, GDN_CHUNK),
                       p["conv_w"], p["al_vec"], p["dtb_vec"], p["gnorm"], state_conv[0], state_gdn[0],
                       GDN_CHUNK, lq)
    cache3 = cache_nsa_kv[0].reshape(cache_nsa_kv.shape[1], PAGE, 512)
    q8 = pad_rows(qn_s, 8)
    nsb_s = past_len // SEL_BLOCK
    emat_s = (jnp.arange(nsb_s)[:, None] == jnp.arange(past_len)[None, :] // SEL_BLOCK).astype(BF16)
    ocmp_s, oslc_s = _nsa_sample(page_table, p["tab"], p["tabt"], cache3, q8, pad_rows(kvn_s, PAGE), emat_s,
                                 past_len)
    wseq = jnp.concatenate([state_win_kv[0].reshape(db, wb, 256), pad_rows(winn_s, LANE)], axis=1)
    owin_s = _win_sample(p["tab"], p["tabt"], q8, wseq, past_len, wb, lq)
    flat_heads = lambda o: o[:, :, :lq].transpose(1, 0, 2, 3).reshape(1, NSA_HEADS, ts, LANE)
    y_sample = _token_mixer_tail(p, x_sample.reshape(1, ts, D_MODEL), og_s[:, :lq].reshape(1, ts, 512),
                                 flat_heads(ocmp_s), flat_heads(oslc_s), flat_heads(owin_s),
                                 sm_s.reshape(1, ts, LANE), min(ts, 256), min(ts, 256), 64)

    kv_tail = (4, NSA_GROUPS, NSA_HD)
    return (y_prompt,
            y_sample.reshape(db, lq, D_MODEL),
            kvn.reshape((1, b, l) + kv_tail),
            winn3[:, l - WINDOW:].reshape(1, b, WINDOW, 2, NSA_GROUPS, NSA_HD),
            hc3[:, l - 3:][None],
            gdn_p[None],
            kvn_s.reshape((1, db, lq) + kv_tail),
            wseq[:, lq:lq + wb].reshape(1, db, wb, 2, NSA_GROUPS, NSA_HD),
            hc_s.reshape(db, lq, GDN_CH)[:, lq - 3:][None],
            gdn_s[None])
```

```python
import functools
import math

import numpy as np
import jax
import jax.numpy as jnp
from jax import lax
from jax.experimental import pallas as pl
from jax.experimental.pallas import tpu as pltpu

F32 = jnp.float32
BF16 = jnp.bfloat16
I32 = jnp.int32

D_MODEL = 1024
EPS = 1e-6
NEG_INF = -1e30
FORCE_SCORE = 1e4
GDN_HEADS = 4
GDN_DK = 128
GDN_CHUNK = 64
GDN_CH = 1536
NSA_HEADS = 8
NSA_GROUPS = 2
NSA_P = 4
NSA_HD = 64
CMP_BLOCK = 32
SEL_BLOCK = 64
SEL_TOPK = 16
WINDOW = 512
REL_BUCKETS = 32
PAGE = 128
PEER_HEADS = 8
PEER_NKEYS = 128
PEER_TOPK = 16
PEER_HALF = 64
PEER_SEL = PEER_HEADS * PEER_TOPK
PEER_NCAND = 80
OFF_B = 2048
OFF_NQ = 2056
OFF_NG = 3336
MAIN_Z, MAIN_Q, MAIN_KV, MAIN_WIN = 1536, 2048, 2560, 3072
LANE = 128
VMEM_LIMIT = 56 * 1024 * 1024
REMOVED = -3.0e38
SQRT_HALF = 0.7071067811865476


def _cparams(sem):
    return pltpu.CompilerParams(dimension_semantics=sem, vmem_limit_bytes=VMEM_LIMIT)


def _dot(a, b):
    return jnp.dot(a, b, preferred_element_type=F32)


def _dot_nt(a, b):
    return lax.dot_general(a, b, (((1,), (1,)), ((), ())), preferred_element_type=F32)


def _dot_tn(a, b):
    return lax.dot_general(a, b, (((0,), (0,)), ((), ())), preferred_element_type=F32)


def _split2(x):
    hi = x.astype(BF16)
    lo = (x - hi.astype(F32)).astype(BF16)
    return hi, lo


def _split3(x):
    hi = x.astype(BF16)
    r = x - hi.astype(F32)
    mid = r.astype(BF16)
    lo = (r - mid.astype(F32)).astype(BF16)
    return hi, mid, lo


def _dot_exact_lhs(m01, x):
    hi, mid, lo = _split3(x)
    return _dot(m01, hi) + (_dot(m01, mid) + _dot(m01, lo))


def _mm3(a, b):
    ah, al = _split2(a)
    bh, bl = _split2(b)
    return _dot(ah, bh) + (_dot(ah, bl) + _dot(al, bh))


def _iota_f(shape, axis):
    return lax.broadcasted_iota(I32, shape, axis).astype(F32)


def _rel_bucket(dist):
    d = jnp.maximum(dist, 0)
    df = jnp.maximum(d, 1).astype(F32)
    large = 16 + (jnp.log(df / 16.0) / math.log(128.0) * 16.0).astype(I32)
    large = jnp.minimum(large, REL_BUCKETS - 1)
    return jnp.where(d < 16, d, large)


def _bias_from_bucket(bucket, tab_ref, head):
    b = jnp.zeros(bucket.shape, F32)
    for k in range(REL_BUCKETS):
        b = jnp.where(bucket == k, tab_ref[k, head], b)
    return b


def _bias_lookup(bucket, tab_ref, tabt_ref, head):
    rows, width = bucket.shape
    if width % LANE != 0:
        return _bias_from_bucket(bucket, tab_ref, head)
    row = jnp.broadcast_to(tabt_ref[head:head + 1, :], (rows, LANE))
    pieces = [jnp.take_along_axis(row, bucket[:, c:c + LANE], axis=1) for c in range(0, width, LANE)]
    return pieces[0] if len(pieces) == 1 else jnp.concatenate(pieces, axis=1)


def _topk_mask(s, k, axis):
    n = s.shape[axis]
    ids = _iota_f(s.shape, axis)
    sel = jnp.zeros(s.shape, F32)
    for _ in range(k):
        m = jnp.max(s, axis=axis, keepdims=True)
        idx = jnp.min(jnp.where(s == m, ids, float(n)), axis=axis, keepdims=True)
        hit = ids == idx
        sel = jnp.where(hit, 1.0, sel)
        s = jnp.where(hit, REMOVED, s)
    return sel


def _seg_rmsnorm(v, gain, seg):
    sq = v * v
    hi, lo = _split2(sq)
    ssum = _dot(hi, seg) + _dot(lo, seg)
    return v * lax.rsqrt(ssum * (1.0 / NSA_HD) + EPS) * gain


def _inproj_kernel(x_ref, g1_ref, wm_ref, ws_ref, seg_ref, qg_ref, kg_ref,
                   hc_ref, z_ref, sm_ref, q_ref, kv_ref, win_ref):
    x = x_ref[...]
    ms = jnp.mean(x * x, axis=-1, keepdims=True)
    xn = (x * lax.rsqrt(ms + EPS) * g1_ref[...]).astype(BF16)
    h = _dot(xn, wm_ref[...])
    sm_ref[...] = _dot(xn, ws_ref[...])
    hc_ref[...] = h[:, :MAIN_Z]
    z_ref[...] = h[:, MAIN_Z:MAIN_Q]
    seg = seg_ref[...]
    qg = qg_ref[...]
    kg = kg_ref[...]
    piece = lambda off, i: h[:, off + i * LANE:off + (i + 1) * LANE]
    lanes = lambda i: slice(i * LANE, (i + 1) * LANE)
    for i in range(4):
        q_ref[:, lanes(i)] = _seg_rmsnorm(piece(MAIN_Q, i), qg, seg)
    for i, gain in enumerate((kg[0:1], None, kg[1:2], None)):
        v = piece(MAIN_KV, i)
        kv_ref[:, lanes(i)] = v if gain is None else _seg_rmsnorm(v, gain, seg)
    for i, gain in enumerate((kg[2:3], None)):
        v = piece(MAIN_WIN, i)
        win_ref[:, lanes(i)] = v if gain is None else _seg_rmsnorm(v, gain, seg)


def _in_proj(x, norm1, wm, ws, seg, qg, kg, tm):
    t = x.shape[0]
    assert t % tm == 0
    row = lambda w: pl.BlockSpec((tm, w), lambda i: (i, 0))
    full = lambda a: pl.BlockSpec(a.shape, lambda i: (0,) * a.ndim)
    widths = (GDN_CH, 512, LANE, 512, 512, 256)
    return pl.pallas_call(
        _inproj_kernel,
        grid=(t // tm,),
        in_specs=[row(D_MODEL), full(norm1), full(wm), full(ws), full(seg), full(qg), full(kg)],
        out_specs=[row(w) for w in widths],
        out_shape=[jax.ShapeDtypeStruct((t, w), F32) for w in widths],
        compiler_params=_cparams(("parallel",)),
        name="in_proj",
    )(x, norm1, wm, ws, seg, qg, kg)


def _inv_unit_lower(lmat, c):
    ri = lax.broadcasted_iota(I32, lmat.shape, 0)
    ci = lax.broadcasted_iota(I32, lmat.shape, 1)
    eye = jnp.where(ri == ci, 1.0, 0.0).astype(F32)
    n = -lmat
    p = eye + n
    m = _mm3(n, n)
    span = 2
    while True:
        mm = _mm3 if span == 2 else (lambda a, b: _dot(a.astype(BF16), b.astype(BF16)))
        p = p + mm(p, m)
        span *= 2
        if span >= c:
            break
        m = mm(m, m)
    return p


def _gdn_kernel(hc_ref, z_ref, sm_ref, cw_ref, al_ref, dtb_ref, gn_ref, conv0_ref, s0_ref,
                og_ref, sfin_ref, xbuf, s_scr, *, c, l_valid, n_chunks, nseq):
    ci = pl.program_id(1)

    @pl.when(ci == 0)
    def _():
        for sq in range(nseq):
            xbuf[sq, 0:8, :] = jnp.zeros((8, GDN_CH), F32)
            xbuf[sq, 5:8, :] = conv0_ref[sq]
        s_scr[...] = s0_ref[...]

    for sq in range(nseq):
        _gdn_chunk(hc_ref.at[sq], z_ref.at[sq], sm_ref.at[sq], cw_ref, al_ref, dtb_ref, gn_ref,
                   og_ref.at[sq], xbuf.at[sq], s_scr.at[sq], ci, c, l_valid)

    @pl.when(ci == n_chunks - 1)
    def _():
        sfin_ref[...] = s_scr[...]


def _gdn_chunk(hc_ref, z_ref, sm_ref, cw_ref, al_ref, dtb_ref, gn_ref, og_ref, xbuf, s_scr, ci, c, l_valid):
    xbuf[8:8 + c, :] = hc_ref[...]
    w = cw_ref[...]
    y = (xbuf[5:5 + c, :] * w[0:1] + xbuf[6:6 + c, :] * w[1:2]
         + xbuf[7:7 + c, :] * w[2:3] + xbuf[8:8 + c, :] * w[3:4])
    tail = xbuf[5 + c:8 + c, :]
    xbuf[5:8, :] = tail
    y = y * jax.nn.sigmoid(y)

    sm = sm_ref[...]
    rowid = ci * c + lax.broadcasted_iota(I32, (c, 1), 0)
    rvalid = rowid < l_valid
    beta_all = jnp.where(rvalid, jax.nn.sigmoid(sm), 0.0)
    sp_in = sm + dtb_ref[...]
    softplus = jnp.maximum(sp_in, 0.0) + jnp.log1p(jnp.exp(-jnp.abs(sp_in)))
    g_all = jnp.where(rvalid, -jnp.exp(al_ref[...]) * softplus, 0.0)

    ri = lax.broadcasted_iota(I32, (c, c), 0)
    cj = lax.broadcasted_iota(I32, (c, c), 1)
    tri = jnp.where(ri >= cj, 1.0, 0.0).astype(BF16)
    triu = jnp.where(ri <= cj, 1.0, 0.0).astype(BF16)
    g_hi, g_mid, g_lo = _split3(g_all)
    gcum = _dot(tri, g_hi) + (_dot(tri, g_mid) + _dot(tri, g_lo))
    gcum_t = _dot_tn(g_hi, triu) + (_dot_tn(g_mid, triu) + _dot_tn(g_lo, triu))
    gn = gn_ref[...]

    heads = range(GDN_HEADS)
    stack = lambda pieces: jnp.concatenate(pieces, axis=0)

    def l2n(x):
        return jnp.where(rvalid, x * lax.rsqrt(jnp.sum(x * x, axis=-1, keepdims=True) + EPS), 0.0)

    q_s = stack([l2n(y[:, h * 128:(h + 1) * 128]) * (GDN_DK ** -0.5) for h in heads])
    k_s = stack([l2n(y[:, 512 + h * 128:512 + (h + 1) * 128]) for h in heads])
    v_s = stack([jnp.where(rvalid, y[:, 1024 + h * 128:1024 + (h + 1) * 128], 0.0) for h in heads])
    beta_s = stack([beta_all[:, h:h + 1] for h in heads])
    gc_s = stack([gcum[:, 4 + h:5 + h] for h in heads])
    gl_s = stack([jnp.broadcast_to(gcum[c - 1:c, 4 + h:5 + h], (c, 1)) for h in heads])
    gct_s = jnp.concatenate([gcum_t[4 + h:5 + h, :] for h in heads], axis=1)
    r = GDN_HEADS * c
    rr = lax.broadcasted_iota(I32, (r, r), 0)
    rc = lax.broadcasted_iota(I32, (r, r), 1)
    same = (rr // c) == (rc // c)
    causal = same & (rr >= rc)
    strict = same & (rr > rc)
    decay = jnp.where(causal, jnp.exp(jnp.where(causal, gc_s - gct_s, 0.0)), 0.0)
    kb = k_s * beta_s
    ksb = k_s.astype(BF16)
    lmat = jnp.where(strict, _dot_nt(kb.astype(BF16), ksb) * decay, 0.0)
    tmat = _inv_unit_lower(lmat, c).astype(BF16)
    eg = jnp.exp(gc_s)
    u = _dot(tmat, (v_s * beta_s).astype(BF16))
    wmat = _dot(tmat, (kb * eg).astype(BF16)).astype(BF16)
    a_intra = (_dot_nt(q_s.astype(BF16), ksb) * decay).astype(BF16)
    q_dec = (q_s * eg).astype(BF16)
    k_dec = (k_s * jnp.exp(gl_s - gc_s)).astype(BF16)
    rows = lambda a, h: a[h * c:(h + 1) * c]
    sbs = [s_scr[h].astype(BF16) for h in heads]
    v_new = stack([rows(u, h) - _dot(rows(wmat, h), sbs[h]) for h in heads])
    vnb = v_new.astype(BF16)
    o_intra = _dot(a_intra, vnb)
    for h in heads:
        g_last = jnp.exp(gcum[c - 1:c, 4 + h:5 + h])
        s_scr[h] = s_scr[h] * g_last + _dot_tn(rows(k_dec, h), rows(vnb, h))
        o = _dot(rows(q_dec, h), sbs[h]) + rows(o_intra, h)
        o = o * lax.rsqrt(jnp.mean(o * o, axis=-1, keepdims=True) + EPS) * gn
        zh = z_ref[:, h * 128:(h + 1) * 128]
        og_ref[:, h * 128:(h + 1) * 128] = o * (zh * jax.nn.sigmoid(zh))


GDN_SEQS = 2


def _gdn(hc, z, sm, conv_w, al_vec, dtb_vec, gnorm, conv0, s0, c, l_valid):
    b, lp, _ = hc.shape
    nseq = GDN_SEQS if b % GDN_SEQS == 0 else 1
    assert lp % c == 0
    n_chunks = lp // c
    full = lambda a: pl.BlockSpec(a.shape, lambda i, j: (0,) * a.ndim)
    seq = lambda w: pl.BlockSpec((nseq, c, w), lambda i, j: (i, j, 0))
    state = pl.BlockSpec((nseq, GDN_HEADS, 128, 128), lambda i, j: (i, 0, 0, 0))
    kern = functools.partial(_gdn_kernel, c=c, l_valid=l_valid, n_chunks=n_chunks, nseq=nseq)
    return pl.pallas_call(
        kern,
        grid=(b // nseq, n_chunks),
        in_specs=[seq(GDN_CH), seq(512), seq(LANE), full(conv_w), full(al_vec), full(dtb_vec), full(gnorm),
                  pl.BlockSpec((nseq, 3, GDN_CH), lambda i, j: (i, 0, 0)), state],
        out_specs=[seq(512), state],
        out_shape=[jax.ShapeDtypeStruct((b, lp, 512), F32),
                   jax.ShapeDtypeStruct((b, GDN_HEADS, 128, 128), F32)],
        scratch_shapes=[pltpu.VMEM((nseq, c + 8, GDN_CH), F32), pltpu.VMEM((nseq, GDN_HEADS, 128, 128), F32)],
        compiler_params=_cparams(("parallel", "arbitrary")),
        name="gdn",
    )(hc, z, sm, conv_w, al_vec, dtb_vec, gnorm, conv0, s0)


def _head_q128(q, h):
    g = h // NSA_P
    piece = q[:, (h // 2) * LANE:(h // 2 + 1) * LANE]
    lane = lax.broadcasted_iota(I32, piece.shape, 1)
    keep = (lane >= NSA_HD) if h % 2 == 1 else (lane < NSA_HD)
    qm = jnp.where(keep, piece, 0.0)
    if h % 2 != g:
        qm = pltpu.roll(qm, NSA_HD, 1)
    return qm


def _select_scores(imp, qpos, nsb):
    score = imp[:, :nsb] + imp[:, nsb:]
    j = lax.broadcasted_iota(I32, score.shape, 1)
    cur = qpos // SEL_BLOCK
    forced = (j == 0) | (j == cur) | (j == cur - 1)
    future = j * SEL_BLOCK > qpos
    return jnp.where(future, -1.0, jnp.where(forced, FORCE_SCORE, score))


def _cmp_attention(tab_ref, tabt_ref, q, kcv, qpos, ocmp_ref):
    nc = kcv.shape[0]
    nsb = nc // 2
    kc = kcv[:, 0:128].astype(BF16)
    vc = kcv[:, 128:256].astype(BF16)
    lane = lax.broadcasted_iota(I32, (1, nc), 1)
    blk = jnp.where(lane < nsb, 2 * lane, 2 * (lane - nsb) + 1)
    dist = qpos - (blk * CMP_BLOCK + CMP_BLOCK - 1)
    valid = dist >= 0
    bucket = _rel_bucket(dist)
    scores = []
    for g in range(NSA_GROUPS):
        imp = jnp.zeros((q.shape[0], nc), F32)
        for p in range(NSA_P):
            h = g * NSA_P + p
            qm = _head_q128(q, h).astype(BF16)
            logits = _dot_nt(qm, kc) * (NSA_HD ** -0.5) + _bias_lookup(bucket, tab_ref, tabt_ref, h)
            l = jnp.where(valid, logits, NEG_INF)
            m = jnp.max(l, axis=-1, keepdims=True)
            pr = jnp.where(valid, jnp.exp(l - m), 0.0)
            pr = pr / jnp.maximum(jnp.sum(pr, axis=-1, keepdims=True), 1e-30)
            ocmp_ref[h] = _dot(pr.astype(BF16), vc)
            imp = imp + pr
        scores.append(_select_scores(imp, qpos, nsb))
    return scores


def _flash_init(m_scr, l_scr, acc_scr):
    m_scr[...] = jnp.full(m_scr.shape, NEG_INF, F32)
    l_scr[...] = jnp.zeros(l_scr.shape, F32)
    acc_scr[...] = jnp.zeros(acc_scr.shape, F32)


def _cmp_prompt_kernel(tab_ref, tabt_ref, q_ref, kv_ref, mavg_ref, ocmp_ref, sel_ref, kc_scr, *, tq, nsb):
    qi = pl.program_id(1)

    @pl.when(qi == 0)
    def _():
        kc_scr[...] = _dot_exact_lhs(mavg_ref[...], kv_ref[...]) * (1.0 / CMP_BLOCK)

    qpos = qi * tq + lax.broadcasted_iota(I32, (tq, 1), 0)
    scores = _cmp_attention(tab_ref, tabt_ref, q_ref[...], kc_scr[...], qpos, ocmp_ref)
    st = jnp.concatenate(scores, axis=1).T
    k = min(SEL_TOPK, nsb)
    sel_ref[...] = jnp.concatenate([_topk_mask(st[:nsb], k, 0), _topk_mask(st[nsb:], k, 0)], axis=0)


def _cmp_prompt(tab, tabt, qn, kvn, mavg, tq):
    b, l, _ = qn.shape
    nc = l // CMP_BLOCK
    assert l % tq == 0 and l % SEL_BLOCK == 0
    kern = functools.partial(_cmp_prompt_kernel, tq=tq, nsb=nc // 2)
    return pl.pallas_call(
        kern,
        grid=(b, l // tq),
        in_specs=[pl.BlockSpec(memory_space=pltpu.SMEM),
                  pl.BlockSpec(tabt.shape, lambda i, j: (0, 0)),
                  pl.BlockSpec((None, tq, 512), lambda i, j: (i, j, 0)),
                  pl.BlockSpec((None, l, 256), lambda i, j: (i, 0, 0)),
                  pl.BlockSpec(mavg.shape, lambda i, j: (0, 0))],
        out_specs=[pl.BlockSpec((None, NSA_HEADS, tq, LANE), lambda i, j: (i, 0, j, 0)),
                   pl.BlockSpec((None, nc, tq), lambda i, j: (i, 0, j))],
        out_shape=[jax.ShapeDtypeStruct((b, NSA_HEADS, l, LANE), F32),
                   jax.ShapeDtypeStruct((b, nc, l), F32)],
        scratch_shapes=[pltpu.VMEM((nc, 256), F32)],
        compiler_params=_cparams(("parallel", "arbitrary")),
        name="cmp_prompt",
    )(tab, tabt, qn, kvn, mavg)


def _flash_prompt_kernel(tab_ref, q_ref, k_ref, v_ref, *rest, t, windowed, nsb, n_dist):
    if windowed:
        o_ref, bias_scr, m_scr, l_scr, acc_scr = rest
    else:
        sel_ref, o_ref, bias_scr, m_scr, l_scr, acc_scr = rest
    qi = pl.program_id(1)
    lane_i = lax.broadcasted_iota(I32, (1, t), 1)
    sub_j = lax.broadcasted_iota(I32, (t, 1), 0)

    @pl.when((pl.program_id(0) == 0) & (qi == 0))
    def _():
        def build(d, carry):
            bucket = _rel_bucket(d * t + lane_i - sub_j)
            for h in range(NSA_HEADS):
                bias_scr[h, d] = _bias_from_bucket(bucket, tab_ref, h)
            return carry

        lax.fori_loop(0, n_dist, build, 0)

    q = q_ref[...]
    qts = [_head_q128(q, h).T.astype(BF16) for h in range(NSA_HEADS)]
    qpos = qi * t + lane_i
    _flash_init(m_scr, l_scr, acc_scr)
    tk = 2 * t
    sub_k = lax.broadcasted_iota(I32, (tk, 1), 0)
    k_lo = jnp.maximum(qi - WINDOW // t, 0) // 2 if windowed else 0

    def body(ki, carry):
        k0 = pl.multiple_of(ki * tk, tk)
        kt = k_ref[pl.ds(k0, tk), :].astype(BF16)
        vtt = v_ref[pl.ds(k0, tk), :].T.astype(BF16)
        dist = qpos - (k0 + sub_k)
        ok = dist >= 0
        if windowed:
            ok = ok & (dist <= WINDOW)
        d = qi - 2 * ki
        d0 = jnp.minimum(d, n_dist - 1)
        d1 = jnp.clip(d - 1, 0, n_dist - 1)
        for g in range(NSA_GROUPS):
            if windowed:
                valid = ok
            else:
                blk = g * nsb + ki * (tk // SEL_BLOCK)
                rows = [sel_ref[pl.ds(blk + r, 1), :] for r in range(tk // SEL_BLOCK)]
                selm = rows[-1]
                for r in range(tk // SEL_BLOCK - 2, -1, -1):
                    selm = jnp.where(sub_k < (r + 1) * SEL_BLOCK, rows[r], selm)
                valid = ok & (selm > 0.5)
            for p in range(NSA_P):
                h = g * NSA_P + p
                bias = jnp.concatenate([bias_scr[h, d0], bias_scr[h, d1]], axis=0)
                s = _dot(kt, qts[h]) * (NSA_HD ** -0.5) + bias
                l = jnp.where(valid, s, NEG_INF)
                m_old = m_scr[h]
                m_new = jnp.maximum(m_old, jnp.max(l, axis=0, keepdims=True))
                pr = jnp.exp(l - m_new)
                alpha = jnp.exp(m_old - m_new)
                l_scr[h] = alpha * l_scr[h] + jnp.sum(pr, axis=0, keepdims=True)
                acc_scr[h] = alpha * acc_scr[h] + _dot(vtt, pr.astype(BF16))
                m_scr[h] = m_new
        return carry

    lax.fori_loop(k_lo, qi // 2 + 1, body, 0)
    for h in range(NSA_HEADS):
        o_ref[h] = (acc_scr[h] / jnp.maximum(l_scr[h], 1e-30)).T


def _flash_prompt(tab, qn, kv_arr, k_blk, v_blk, sel, t, windowed):
    b, l, _ = qn.shape
    assert l % (2 * t) == 0 and t == LANE
    n_dist = WINDOW // t + 1 if windowed else l // t
    kern = functools.partial(_flash_prompt_kernel, t=t, windowed=windowed, nsb=l // SEL_BLOCK, n_dist=n_dist)
    in_specs = [pl.BlockSpec(memory_space=pltpu.SMEM),
                pl.BlockSpec((None, t, 512), lambda i, j: (i, j, 0)),
                pl.BlockSpec((None, l, LANE), lambda i, j: (i, 0, k_blk)),
                pl.BlockSpec((None, l, LANE), lambda i, j: (i, 0, v_blk))]
    args = [tab, qn, kv_arr, kv_arr]
    if not windowed:
        in_specs += [pl.BlockSpec((None, sel.shape[1], t), lambda i, j: (i, 0, j))]
        args += [sel]
    return pl.pallas_call(
        kern,
        grid=(b, l // t),
        in_specs=in_specs,
        out_specs=pl.BlockSpec((None, NSA_HEADS, t, LANE), lambda i, j: (i, 0, j, 0)),
        out_shape=jax.ShapeDtypeStruct((b, NSA_HEADS, l, LANE), F32),
        scratch_shapes=[pltpu.VMEM((NSA_HEADS, n_dist, t, t), F32),
                        pltpu.VMEM((NSA_HEADS, 1, t), F32), pltpu.VMEM((NSA_HEADS, 1, t), F32),
                        pltpu.VMEM((NSA_HEADS, LANE, t), F32)],
        compiler_params=_cparams(("arbitrary", "arbitrary")),
        name="win_prompt" if windowed else "slc_prompt",
    )(*args)


def _nsa_sample_kernel(pt_ref, tab_ref, tabt_ref, q_ref, new_ref, e_ref, cache_ref, ocmp_ref, o_ref, kv_buf, sem,
                       *, n_pages, past_len, db):
    b = pl.program_id(0)
    slot = b % 2
    nsb = past_len // SEL_BLOCK

    def page_copy(seq, buf, pg):
        return pltpu.make_async_copy(cache_ref.at[pt_ref[seq, pg]],
                                     kv_buf.at[buf, pl.ds(pg * PAGE, PAGE), :], sem.at[buf])

    @pl.when(b == 0)
    def _():
        for pg in range(n_pages):
            page_copy(0, 0, pg).start()

    @pl.when(b + 1 < db)
    def _():
        for pg in range(n_pages):
            page_copy(b + 1, 1 - slot, pg).start()

    q = q_ref[...]
    qpos = past_len + lax.broadcasted_iota(I32, (8, 1), 0)
    dist_p = qpos - lax.broadcasted_iota(I32, (1, past_len), 1)
    bucket_p = _rel_bucket(dist_p)
    dist_n = qpos - (past_len + lax.broadcasted_iota(I32, (1, PAGE), 1))
    ok_n = dist_n >= 0
    bucket_n = _rel_bucket(dist_n)
    knew = new_ref[:, 0:128].astype(BF16)
    vnew = new_ref[:, 128:256].astype(BF16)

    for pg in range(n_pages):
        page_copy(b, slot, pg).wait()

    x3 = kv_buf[slot, :, 0:256].reshape(nsb, SEL_BLOCK, 256)
    kcv = jnp.concatenate([jnp.sum(x3[:, :CMP_BLOCK], axis=1), jnp.sum(x3[:, CMP_BLOCK:], axis=1)],
                          axis=0) * (1.0 / CMP_BLOCK)
    scores = _cmp_attention(tab_ref, tabt_ref, q, kcv, qpos, ocmp_ref)
    k_sel = min(SEL_TOPK, nsb + 1) - 1

    kall = kv_buf[slot, :, 256:384].astype(BF16)
    vall = kv_buf[slot, :, 384:512].astype(BF16)
    for g in range(NSA_GROUPS):
        sel_g = _topk_mask(scores[g], k_sel, 1).astype(BF16)
        qg = jnp.concatenate([_head_q128(q, g * NSA_P + p) for p in range(NSA_P)], axis=0).astype(BF16)
        s_p = _dot_nt(qg, kall) * (NSA_HD ** -0.5)
        s_n = _dot_nt(qg, knew) * (NSA_HD ** -0.5)
        mask_p = _dot(sel_g, e_ref[...]) > 0.5
        pps, pns, dens = [], [], []
        for p in range(NSA_P):
            h = g * NSA_P + p
            rows = slice(8 * p, 8 * p + 8)
            lp = jnp.where(mask_p, s_p[rows] + _bias_lookup(bucket_p, tab_ref, tabt_ref, h), NEG_INF)
            ln = jnp.where(ok_n, s_n[rows] + _bias_lookup(bucket_n, tab_ref, tabt_ref, h), NEG_INF)
            m = jnp.maximum(jnp.max(lp, axis=-1, keepdims=True), jnp.max(ln, axis=-1, keepdims=True))
            pp = jnp.where(mask_p, jnp.exp(lp - m), 0.0)
            pn = jnp.where(ok_n, jnp.exp(ln - m), 0.0)
            dens.append(jnp.sum(pp, axis=-1, keepdims=True) + jnp.sum(pn, axis=-1, keepdims=True))
            pps.append(pp.astype(BF16))
            pns.append(pn.astype(BF16))
        o = _dot(jnp.concatenate(pps, axis=0), vall) + _dot(jnp.concatenate(pns, axis=0), vnew)
        for p in range(NSA_P):
            o_ref[g * NSA_P + p] = o[8 * p:8 * p + 8] / jnp.maximum(dens[p], 1e-30)


def _nsa_sample(page_table, tab, tabt, cache, q8, new_kv, emat, past_len):
    db, n_pages = page_table.shape
    kern = functools.partial(_nsa_sample_kernel, n_pages=n_pages, past_len=past_len, db=db)
    heads = pl.BlockSpec((None, NSA_HEADS, 8, LANE), lambda i, pt: (i, 0, 0, 0))
    grid_spec = pltpu.PrefetchScalarGridSpec(
        num_scalar_prefetch=1,
        grid=(db,),
        in_specs=[pl.BlockSpec(memory_space=pltpu.SMEM),
                  pl.BlockSpec(tabt.shape, lambda i, pt: (0, 0)),
                  pl.BlockSpec((None, 8, 512), lambda i, pt: (i, 0, 0)),
                  pl.BlockSpec((None, PAGE, 256), lambda i, pt: (i, 0, 1)),
                  pl.BlockSpec(emat.shape, lambda i, pt: (0, 0)),
                  pl.BlockSpec(memory_space=pl.ANY)],
        out_specs=[heads, heads],
        scratch_shapes=[pltpu.VMEM((2, n_pages * PAGE, 512), F32), pltpu.SemaphoreType.DMA((2,))],
    )
    return pl.pallas_call(
        kern,
        grid_spec=grid_spec,
        out_shape=[jax.ShapeDtypeStruct((db, NSA_HEADS, 8, LANE), F32)] * 2,
        compiler_params=_cparams(("arbitrary",)),
        name="nsa_sample",
    )(page_table, tab, tabt, q8, new_kv, emat, cache)


def _win_sample_kernel(tab_ref, tabt_ref, q_ref, w_ref, o_ref, *, past_len, wb, lq, nseq):
    n = w_ref.shape[1]
    qpos = past_len + lax.broadcasted_iota(I32, (8, 1), 0)
    j = lax.broadcasted_iota(I32, (1, n), 1)
    kpos = past_len - wb + j
    dist = qpos - kpos
    valid = (dist >= 0) & (dist <= WINDOW) & (kpos >= 0) & (j < wb + lq)
    biases = [_bias_lookup(_rel_bucket(dist), tab_ref, tabt_ref, h) for h in range(NSA_HEADS)]
    for sq in range(nseq):
        q = q_ref[sq]
        kw = w_ref[sq, :, 0:128].astype(BF16)
        vw = w_ref[sq, :, 128:256].astype(BF16)
        for h in range(NSA_HEADS):
            qm = _head_q128(q, h).astype(BF16)
            s = _dot_nt(qm, kw) * (NSA_HD ** -0.5) + biases[h]
            l = jnp.where(valid, s, NEG_INF)
            m = jnp.max(l, axis=-1, keepdims=True)
            pr = jnp.where(valid, jnp.exp(l - m), 0.0)
            pr = pr / jnp.maximum(jnp.sum(pr, axis=-1, keepdims=True), 1e-30)
            o_ref[sq, h] = _dot(pr.astype(BF16), vw)


def _win_sample(tab, tabt, q8, wseq, past_len, wb, lq):
    db, n, _ = wseq.shape
    nseq = 4 if db % 4 == 0 else 1
    kern = functools.partial(_win_sample_kernel, past_len=past_len, wb=wb, lq=lq, nseq=nseq)
    return pl.pallas_call(
        kern,
        grid=(db // nseq,),
        in_specs=[pl.BlockSpec(memory_space=pltpu.SMEM),
                  pl.BlockSpec(tabt.shape, lambda i: (0, 0)),
                  pl.BlockSpec((nseq, 8, 512), lambda i: (i, 0, 0)),
                  pl.BlockSpec((nseq, n, 256), lambda i: (i, 0, 0))],
        out_specs=pl.BlockSpec((nseq, NSA_HEADS, 8, LANE), lambda i: (i, 0, 0, 0)),
        out_shape=jax.ShapeDtypeStruct((db, NSA_HEADS, 8, LANE), F32),
        compiler_params=_cparams(("parallel",)),
        name="win_sample",
    )(tab, tabt, q8, wseq)


def _outproj_kernel(x_ref, og_ref, oc_ref, os_ref, ow_ref, sm_ref, wg_ref, wn_ref, n2_ref,
                    hres_ref, xn_ref):
    gates = jax.nn.sigmoid(sm_ref[...])
    acc = x_ref[...] + _dot(og_ref[...].astype(BF16), wg_ref[...])
    for h in range(NSA_HEADS):
        c = 8 + 3 * h
        on = (gates[:, c:c + 1] * oc_ref[h] + gates[:, c + 1:c + 2] * os_ref[h]
              + gates[:, c + 2:c + 3] * ow_ref[h])
        acc = acc + _dot(on.astype(BF16), wn_ref[h])
    hres_ref[...] = acc
    ms = jnp.mean(acc * acc, axis=-1, keepdims=True)
    xn_ref[...] = acc * lax.rsqrt(ms + EPS) * n2_ref[...]


def _out_proj(x, og, ocmp, oslc, owin, sm, wg, wn, norm2, tm):
    b, l, _ = x.shape
    assert l % tm == 0
    seq = lambda w: pl.BlockSpec((None, tm, w), lambda i, j: (i, j, 0))
    heads = pl.BlockSpec((None, NSA_HEADS, tm, LANE), lambda i, j: (i, 0, j, 0))
    full = lambda a: pl.BlockSpec(a.shape, lambda i, j: (0,) * a.ndim)
    return pl.pallas_call(
        _outproj_kernel,
        grid=(b, l // tm),
        in_specs=[seq(D_MODEL), seq(512), heads, heads, heads, seq(LANE), full(wg), full(wn), full(norm2)],
        out_specs=[seq(D_MODEL), seq(D_MODEL)],
        out_shape=[jax.ShapeDtypeStruct((b, l, D_MODEL), F32)] * 2,
        compiler_params=_cparams(("parallel", "parallel")),
        name="out_proj",
    )(x, og, ocmp, oslc, owin, sm, wg, wn, norm2)


def _peer_topk_kernel(x_ref, wqt_ref, keys_ref, cflat_ref, eid_ref, gate_ref, qt_scr, sv_scr, si_scr, top_scr,
                      *, tm):
    qt_scr[...] = _dot_nt(wqt_ref[...], x_ref[...].astype(BF16))
    rows = _iota_f((PEER_NKEYS, tm), 0)
    cflat = jnp.broadcast_to(cflat_ref[...], (PEER_NCAND, tm))

    def candidates(hh):
        s1 = sv_scr[hh, 0]
        s2 = sv_scr[hh, 1]
        i1 = si_scr[hh, 0] * float(PEER_NKEYS)
        i2 = si_scr[hh, 1]
        cand = [s1[0:1] + s2]
        eidc = [i1[0:1] + i2]
        for a in range(1, 8):
            cand.append(s1[a:a + 1] + s2[0:8])
            eidc.append(i1[a:a + 1] + i2[0:8])
        cand.append(s1[8:16] + s2[0:1])
        eidc.append(i1[8:16] + i2[0:1])
        return jnp.where(cflat >= 0.0, jnp.concatenate(cand, axis=0), REMOVED), jnp.concatenate(eidc, axis=0)

    def head_pair_body(hp, carry):
        for hh in range(2):
            h = 2 * hp + hh
            for c in range(2):
                off = pl.multiple_of(h * (2 * PEER_HALF) + c * PEER_HALF, PEER_HALF)
                qs = qt_scr[pl.ds(off, PEER_HALF), :].astype(BF16)
                s = _dot(keys_ref[h, c], qs)

                def round_body(r, s):
                    m = jnp.max(s, axis=0, keepdims=True)
                    idx = jnp.min(jnp.where(s == m, rows, float(PEER_NKEYS)), axis=0, keepdims=True)
                    sv_scr[hh, c, pl.ds(r, 1), :] = m
                    si_scr[hh, c, pl.ds(r, 1), :] = idx
                    return jnp.where(rows == idx, REMOVED, s)

                lax.fori_loop(0, PEER_TOPK, round_body, s)
        cands, eidcs = zip(*[candidates(hh) for hh in range(2)])

        def round2(r, cands):
            out = []
            for hh, cand in enumerate(cands):
                m = jnp.max(cand, axis=0, keepdims=True)
                f = jnp.min(jnp.where(cand == m, cflat, 1e9), axis=0, keepdims=True)
                hit = cflat == f
                top_scr[hh, pl.ds(r, 1), :] = m
                eid_ref[2 * hp + hh, pl.ds(r, 1), :] = jnp.sum(
                    jnp.where(hit, eidcs[hh], 0.0), axis=0, keepdims=True).astype(I32)
                out.append(jnp.where(hit, REMOVED, cand))
            return tuple(out)

        lax.fori_loop(0, PEER_TOPK, round2, tuple(cands))
        for hh in range(2):
            top = top_scr[hh]
            e = jnp.exp(top - jnp.max(top, axis=0, keepdims=True))
            gate_ref[2 * hp + hh] = e / jnp.sum(e, axis=0, keepdims=True)
        return carry

    lax.fori_loop(0, PEER_HEADS // 2, head_pair_body, 0)


def _peer_cflat():
    rows = [(0, b) for b in range(16)]
    for a in range(1, 8):
        rows += [(a, b) for b in range(8)]
    rows += [(a, 0) for a in range(8, 16)]
    flat = [a * 16 + b if (a + 1) * (b + 1) <= PEER_TOPK else -1 for a, b in rows]
    assert len(flat) == PEER_NCAND
    return jnp.asarray(np.array(flat, np.float32).reshape(PEER_NCAND, 1))


def _peer_topk(xn, wqt, keys, tm):
    t = xn.shape[0]
    assert t % tm == 0
    cflat = _peer_cflat()
    kern = functools.partial(_peer_topk_kernel, tm=tm)
    full = lambda a: pl.BlockSpec(a.shape, lambda i: (0,) * a.ndim)
    out_spec = pl.BlockSpec((PEER_HEADS, PEER_TOPK, tm), lambda i: (0, 0, i))
    return pl.pallas_call(
        kern,
        grid=(t // tm,),
        in_specs=[pl.BlockSpec((tm, D_MODEL), lambda i: (i, 0)), full(wqt), full(keys), full(cflat)],
        out_specs=[out_spec, out_spec],
        out_shape=[jax.ShapeDtypeStruct((PEER_HEADS, PEER_TOPK, t), I32),
                   jax.ShapeDtypeStruct((PEER_HEADS, PEER_TOPK, t), F32)],
        scratch_shapes=[pltpu.VMEM((D_MODEL, tm), F32), pltpu.VMEM((2, 2, PEER_TOPK, tm), F32),
                        pltpu.VMEM((2, 2, PEER_TOPK, tm), F32), pltpu.VMEM((2, PEER_TOPK, tm), F32)],
        compiler_params=_cparams(("parallel",)),
        name="peer_topk",
    )(xn, wqt, keys, cflat)


PEER_GROUP = 2
PEER_SLOTS = 12
PEER_CHUNKS = D_MODEL // LANE


def _peer_expert_kernel(eid_ref, x_ref, gate_ref, hres_ref, uv_ref, y_ref, *scratch, tt):
    bufs = scratch[:PEER_SLOTS]
    sem = scratch[PEER_SLOTS]
    ahead = PEER_SLOTS - PEER_GROUP

    def row_copy(e, slot, k):
        return pltpu.make_async_copy(uv_ref.at[e], bufs[slot].at[pl.ds(k * PEER_CHUNKS, PEER_CHUNKS), :],
                                     sem.at[slot])

    def issue(t, slot, part=None):
        base = t * PEER_SEL
        i, n = (0, 1) if part is None else part
        for k in range(i * PEER_SEL // n, (i + 1) * PEER_SEL // n):
            row_copy(eid_ref[base + k], slot, k).start(priority=k % 2)

    def wait(slot):
        for k in range(PEER_SEL):
            row_copy(0, slot, k).wait()

    def chunk_words(slot, c):
        return bufs[slot][pl.ds(c, PEER_SEL, stride=PEER_CHUNKS), :]

    def evaluate(ts, slots, prefetch):
        for slot in slots:
            wait(slot)
        nparts = 2 * PEER_CHUNKS

        def start_part(i):
            if prefetch:
                for t, slot in zip(ts, slots):
                    issue(t + ahead, (slot + ahead) % PEER_SLOTS, (i, nparts))

        xbs = [x_ref[pl.ds(t, 1), :].astype(BF16).astype(F32) for t in ts]
        accs = [jnp.zeros((PEER_SEL, LANE), F32) for _ in ts]
        for c in range(PEER_CHUNKS):
            start_part(c)
            for j, slot in enumerate(slots):
                uf = pltpu.bitcast(chunk_words(slot, c) & jnp.int32(-65536), F32)
                accs[j] = accs[j] + uf * xbs[j][:, c * LANE:(c + 1) * LANE]
        w2s = []
        for j, t in enumerate(ts):
            act = jnp.sum(accs[j].T, axis=0, keepdims=True)
            w = gate_ref[pl.ds(t, 1), :] * (0.5 * act * (1.0 + lax.erf(act * SQRT_HALF)))
            wb = w.astype(BF16).astype(F32)
            w2s.append(jnp.broadcast_to(wb, (LANE, PEER_SEL)).T)
        outs = [[] for _ in ts]
        for c in range(PEER_CHUNKS):
            start_part(PEER_CHUNKS + c)
            for j, slot in enumerate(slots):
                vf = pltpu.bitcast(chunk_words(slot, c) << 16, F32)
                outs[j].append(jnp.sum(vf * w2s[j], axis=0, keepdims=True))
        for j, t in enumerate(ts):
            y_ref[pl.ds(t, 1), :] = hres_ref[pl.ds(t, 1), :] + jnp.concatenate(outs[j], axis=1)

    for t0 in range(ahead):
        issue(t0, t0)
    n_main = (tt - ahead) // PEER_SLOTS * PEER_SLOTS

    def body(i, carry):
        for r in range(0, PEER_SLOTS, PEER_GROUP):
            slots = list(range(r, r + PEER_GROUP))
            evaluate([i * PEER_SLOTS + s for s in slots], slots, True)
        return carry

    lax.fori_loop(0, n_main // PEER_SLOTS, body, 0)
    for t0 in range(n_main, tt, PEER_GROUP):
        ts = list(range(t0, t0 + PEER_GROUP))
        evaluate(ts, [t % PEER_SLOTS for t in ts], t0 + ahead < tt)


def _pack_expert_rows(u, v):
    bits = lambda a: lax.bitcast_convert_type(a.astype(BF16), jnp.uint16).astype(jnp.uint32)
    words = (bits(u) << 16) | bits(v)
    return lax.bitcast_convert_type(words, I32).reshape(-1, PEER_CHUNKS, LANE)


def _peer_experts(eid_flat, xn, gate, hres, uv, tt):
    t = xn.shape[0]
    assert t % tt == 0 and tt >= PEER_SLOTS
    kern = functools.partial(_peer_expert_kernel, tt=tt)
    row = lambda w: pl.BlockSpec((tt, w), lambda i: (i, 0))
    return pl.pallas_call(
        kern,
        grid=(t // tt,),
        in_specs=[pl.BlockSpec((tt * PEER_SEL,), lambda i: (i,), memory_space=pltpu.SMEM),
                  row(D_MODEL), row(PEER_SEL), row(D_MODEL),
                  pl.BlockSpec(memory_space=pl.ANY)],
        out_specs=row(D_MODEL),
        out_shape=jax.ShapeDtypeStruct((t, D_MODEL), F32),
        scratch_shapes=[pltpu.VMEM((PEER_SEL * PEER_CHUNKS, LANE), I32) for _ in range(PEER_SLOTS)]
        + [pltpu.SemaphoreType.DMA((PEER_SLOTS,))],
        compiler_params=_cparams(("arbitrary",)),
        name="peer_experts",
    )(eid_flat, xn, gate, hres, uv)


def _prep_params(norm1, w_in, gdn_conv_w, gdn_a_log, gdn_dt_bias, gdn_norm, nsa_q_norm, nsa_k_norm,
                 rel_bias, w_o, norm2, peer_wq, peer_subkeys, peer_u, peer_v):
    w = w_in[0]
    p = {}
    p["norm1"] = norm1[0][None]
    p["wm"] = jnp.concatenate([w[:, :OFF_B], w[:, OFF_NQ:OFF_NG]], axis=1).astype(BF16)
    p["ws"] = jnp.concatenate([w[:, OFF_B:OFF_NQ], w[:, OFF_NG:], jnp.zeros((D_MODEL, LANE - 32), F32)],
                              axis=1).astype(BF16)
    li = jnp.arange(LANE)
    p["seg"] = (li[:, None] // NSA_HD == li[None, :] // NSA_HD).astype(BF16)
    p["qg"] = jnp.tile(nsa_q_norm[0], 2)[None]
    p["kg"] = jnp.tile(nsa_k_norm[0], (1, 2))
    p["conv_w"] = gdn_conv_w[0]
    p["al_vec"] = jnp.zeros((1, LANE), F32).at[0, 4:8].set(gdn_a_log[0])
    p["dtb_vec"] = jnp.zeros((1, LANE), F32).at[0, 4:8].set(gdn_dt_bias[0])
    p["gnorm"] = gdn_norm[0][None]
    p["tab"] = rel_bias
    p["tabt"] = jnp.pad(rel_bias.T, ((0, 0), (0, LANE - REL_BUCKETS)))
    wo = w_o[0]
    p["wg"] = wo[:512].astype(BF16)
    wn = jnp.zeros((NSA_HEADS, LANE, D_MODEL), F32)
    for h in range(NSA_HEADS):
        g = h // NSA_P
        wn = wn.at[h, g * NSA_HD:(g + 1) * NSA_HD].set(wo[512 + h * NSA_HD:512 + (h + 1) * NSA_HD])
    p["wn"] = wn.astype(BF16)
    p["norm2"] = norm2[0][None]
    p["wqt"] = peer_wq[0].T.astype(BF16)
    p["keys"] = peer_subkeys[0].astype(BF16)
    p["uv"] = _pack_expert_rows(peer_u[0], peer_v[0])
    return p


def _perm_avg_matrix(n_blocks, n_rows):
    half = n_blocks // 2
    r = jnp.arange(n_blocks)
    blk = jnp.where(r < half, 2 * r, 2 * (r - half) + 1)
    return (jnp.arange(n_rows)[None, :] // CMP_BLOCK == blk[:, None]).astype(BF16)


def _token_mixer_tail(p, x, og, ocmp, oslc, owin, sm, tm_out, tm_topk, tt):
    b, l, _ = x.shape
    hres, xn2 = _out_proj(x, og, ocmp, oslc, owin, sm, p["wg"], p["wn"], p["norm2"], tm_out)
    t = b * l
    xn2 = xn2.reshape(t, D_MODEL)
    eid, gate = _peer_topk(xn2, p["wqt"], p["keys"], tm_topk)
    eid_flat = eid.reshape(PEER_SEL, t).T.reshape(t * PEER_SEL)
    gate_tok = gate.reshape(PEER_SEL, t).T
    y = _peer_experts(eid_flat, xn2, gate_tok, hres.reshape(t, D_MODEL), p["uv"], tt)
    return y.reshape(b, l, D_MODEL)


def kernel(x_prompt, x_sample, cache_nsa_kv, page_table, state_win_kv, state_conv, state_gdn, norm1, w_in, gdn_conv_w, gdn_a_log, gdn_dt_bias, gdn_norm, nsa_q_norm, nsa_k_norm, rel_bias, w_o, norm2, peer_wq, peer_subkeys, peer_u, peer_v):
    assert w_in.shape[0] == 1, "single layer"
    p = _prep_params(norm1, w_in, gdn_conv_w, gdn_a_log, gdn_dt_bias, gdn_norm, nsa_q_norm, nsa_k_norm,
                     rel_bias, w_o, norm2, peer_wq, peer_subkeys, peer_u, peer_v)
    b, l, _ = x_prompt.shape
    db, lq, _ = x_sample.shape
    n_pages = page_table.shape[1]
    past_len = n_pages * PAGE
    wb = state_win_kv.shape[2]
    assert cache_nsa_kv.shape[2] == PAGE and l >= WINDOW and l >= 3 and lq >= 3
    assert lq < CMP_BLOCK and lq <= 8 and wb == WINDOW and past_len >= wb

    tp = b * l
    tm = 256 if tp % 256 == 0 else LANE
    hc, z, sm, qn, kvn, winn = _in_proj(x_prompt.reshape(tp, D_MODEL), p["norm1"], p["wm"], p["ws"],
                                        p["seg"], p["qg"], p["kg"], tm)
    hc3, z3, sm3 = hc.reshape(b, l, GDN_CH), z.reshape(b, l, 512), sm.reshape(b, l, LANE)
    qn3, kvn3, winn3 = qn.reshape(b, l, 512), kvn.reshape(b, l, 512), winn.reshape(b, l, 256)
    og, gdn_p = _gdn(hc3, z3, sm3, p["conv_w"], p["al_vec"], p["dtb_vec"], p["gnorm"],
                     jnp.zeros((b, 3, GDN_CH), F32), jnp.zeros((b, GDN_HEADS, 128, 128), F32),
                     GDN_CHUNK, l)
    nc = l // CMP_BLOCK
    ocmp, sel = _cmp_prompt(p["tab"], p["tabt"], qn3, kvn3, _perm_avg_matrix(nc, l), tm)
    oslc = _flash_prompt(p["tab"], qn3, kvn3, 2, 3, sel, LANE, False)
    owin = _flash_prompt(p["tab"], qn3, winn3, 0, 1, None, LANE, True)
    y_prompt = _token_mixer_tail(p, x_prompt, og, ocmp, oslc, owin, sm3, tm, tm, 512 if tp % 512 == 0 else 64)

    ts = db * lq
    hc_s, z_s, sm_s, qn_s, kvn_s, winn_s = _in_proj(x_sample.reshape(ts, D_MODEL), p["norm1"], p["wm"], p["ws"],
                                                    p["seg"], p["qg"], p["kg"], min(ts, 256))
    pad_rows = lambda a, n: jnp.pad(a.reshape(db, lq, a.shape[-1]), ((0, 0), (0, n - lq), (0, 0)))
    og_s, gdn_s = _gdn(pad_rows(hc_s, GDN_CHUNK), pad_rows(z_s, GDN_CHUNK), pad_rows(sm_s, GDN_CHUNK),
                       p["conv_w"], p["al_vec"], p["dtb_vec"], p["gnorm"], state_conv[0], state_gdn[0],
                       GDN_CHUNK, lq)
    cache3 = cache_nsa_kv[0].reshape(cache_nsa_kv.shape[1], PAGE, 512)
    q8 = pad_rows(qn_s, 8)
    nsb_s = past_len // SEL_BLOCK
    emat_s = (jnp.arange(nsb_s)[:, None] == jnp.arange(past_len)[None, :] // SEL_BLOCK).astype(BF16)
    ocmp_s, oslc_s = _nsa_sample(page_table, p["tab"], p["tabt"], cache3, q8, pad_rows(kvn_s, PAGE), emat_s,
                                 past_len)
    wseq = jnp.concatenate([state_win_kv[0].reshape(db, wb, 256), pad_rows(winn_s, LANE)], axis=1)
    owin_s = _win_sample(p["tab"], p["tabt"], q8, wseq, past_len, wb, lq)
    flat_heads = lambda o: o[:, :, :lq].transpose(1, 0, 2, 3).reshape(1, NSA_HEADS, ts, LANE)
    y_sample = _token_mixer_tail(p, x_sample.reshape(1, ts, D_MODEL), og_s[:, :lq].reshape(1, ts, 512),
                                 flat_heads(ocmp_s), flat_heads(oslc_s), flat_heads(owin_s),
                                 sm_s.reshape(1, ts, LANE), min(ts, 256), min(ts, 256), 64)

    kv_tail = (4, NSA_GROUPS, NSA_HD)
    return (y_prompt,
            y_sample.reshape(db, lq, D_MODEL),
            kvn.reshape((1, b, l) + kv_tail),
            winn3[:, l - WINDOW:].reshape(1, b, WINDOW, 2, NSA_GROUPS, NSA_HD),
            hc3[:, l - 3:][None],
            gdn_p[None],
            kvn_s.reshape((1, db, lq) + kv_tail),
            wseq[:, lq:lq + wb].reshape(1, db, wb, 2, NSA_GROUPS, NSA_HD),
            hc_s.reshape(db, lq, GDN_CH)[:, lq - 3:][None],
            gdn_s[None])
```
